```python
import math
import jax, jax.numpy as jnp
from jax import lax
import numpy as np

D_MODEL = 1024
BATCH = 32
SEQ = 256
DEPTH = 4
DEC_BATCH = 8
DEC_SEQ = 2048
PAST_LEN = 256

GRID_W = 64
EPS = 1e-6
N_EVEN = (DEPTH + 1) // 2
N_ODD = DEPTH // 2
MIX_W = D_MODEL
CHUNK = 64
H_A = 4
DK_A = 64
DV_A = 128
W_A = H_A * DV_A
W_B = MIX_W - W_A
SC_K = 3
H_C = 4
DK_C = 128
DV_C = 128
W_C = H_C * DV_C
QKV_K = 3
W_D = MIX_W - W_C
CF_K = 31
EV_IN = 2 * H_A * DK_A + 2 * W_A + 3 * W_B
OD_IN = 2 * H_C * DK_C + 2 * W_C + 2 * 2 * H_C + 2 * W_D
ROPE_BASE = 10000.0
N_KEYS = 128
N_EXP = N_KEYS * N_KEYS
H_P = 8
D_KEY = 256
TOPK = 16
PEER_BLOCK = 128

kernel_name = 'hybrid_retention_gdn_peer_diffusion_step'


def rmsnorm(x, g):
    xf = x.astype(jnp.float32)
    y = xf * lax.rsqrt(jnp.mean(xf * xf, axis=-1, keepdims=True) + EPS)
    return (y * g.astype(jnp.float32)).astype(x.dtype)


def layernorm(x, g, b):
    xf = x.astype(jnp.float32)
    mu = jnp.mean(xf, axis=-1, keepdims=True)
    xc = xf - mu
    y = xc * lax.rsqrt(jnp.mean(xc * xc, axis=-1, keepdims=True) + EPS)
    return (y * g.astype(jnp.float32) + b.astype(jnp.float32)).astype(x.dtype)


def dwconv(x, w):
    k, ch = w.shape
    return lax.conv_general_dilated(x, w[:, None, :].astype(x.dtype), window_strides=(1,),
                                    padding=[(k // 2, k // 2)],
                                    dimension_numbers=('NWC', 'WIO', 'NWC'),
                                    feature_group_count=ch)


def modulation(cvec, w, b):
    m = jax.nn.silu(cvec) @ w + b
    return jnp.split(m[:, None, :], 6, axis=-1)


def axial_rope(x):
    L, dk = x.shape[1], x.shape[-1]
    rows = L // GRID_W
    row = jnp.repeat(jnp.arange(rows, dtype=jnp.float32), GRID_W)
    col = jnp.tile(jnp.arange(GRID_W, dtype=jnp.float32), rows)
    nf = dk // 4
    freqs = ROPE_BASE ** (-jnp.arange(nf, dtype=jnp.float32) / nf)
    ang = jnp.concatenate([row[:, None] * freqs, col[:, None] * freqs], axis=-1)
    cos = jnp.cos(ang)[None, :, None, :]
    sin = jnp.sin(ang)[None, :, None, :]
    x1, x2 = x[..., : dk // 2], x[..., dk // 2:]
    return jnp.concatenate([x1 * cos - x2 * sin, x2 * cos + x1 * sin], axis=-1)


def retention_scan(q, k, v, log_gamma, s0):
    B, L, H, dk = q.shape
    nc = L // CHUNK
    qc = q.reshape(B, nc, CHUNK, H, dk)
    kc = k.reshape(B, nc, CHUNK, H, dk)
    vc = v.reshape(B, nc, CHUNK, H, -1)
    pos = jnp.arange(CHUNK, dtype=jnp.float32)
    lg = log_gamma[:, None]
    diff = pos[:, None] - pos[None, :]
    dmat = jnp.where(diff >= 0, jnp.exp(lg[:, :, None] * jnp.maximum(diff, 0.0)), 0.0)
    intra = jnp.einsum('bnihd,bnjhd->bnhij', qc, kc) * dmat
    o_intra = jnp.einsum('bnhij,bnjhe->bnihe', intra, vc)
    w_end = jnp.exp(lg * (CHUNK - 1 - pos))
    kv = jnp.einsum('bnjhd,hj,bnjhe->nbhde', kc, w_end, vc)
    g_chunk = jnp.exp(log_gamma * CHUNK)[None, :, None, None]

    def step(S, kv_n):
        return S * g_chunk + kv_n, S

    s_final, s_prev = lax.scan(step, s0, kv)
    w_start = jnp.exp(lg * (pos + 1.0))
    o_cross = jnp.einsum('bnihd,hi,nbhde->bnihe', qc, w_start, s_prev)
    return (o_intra + o_cross).reshape(B, L, H, -1), s_final


def gdn_scan(q, k, v, g, beta, s0):
    B, L, H, dk = q.shape
    dv = v.shape[-1]
    nc = L // CHUNK

    def chunks(t):
        return t.reshape(B, nc, CHUNK, H, -1).transpose(1, 0, 3, 2, 4)

    qc = chunks(q * (dk ** -0.5))
    kc = chunks(k)
    vc = chunks(v)
    bc = beta.reshape(B, nc, CHUNK, H).transpose(1, 0, 3, 2)
    gc = jnp.cumsum(g.reshape(B, nc, CHUNK, H).transpose(1, 0, 3, 2), axis=-1)
    idx = jnp.arange(CHUNK)
    incl = idx[:, None] >= idx[None, :]
    strict = idx[:, None] > idx[None, :]
    diff = gc[..., :, None] - gc[..., None, :]
    decay = jnp.where(incl, jnp.exp(jnp.where(incl, diff, 0.0)), 0.0)
    kb = kc * bc[..., None]
    vb = vc * bc[..., None]
    a_low = jnp.where(strict, jnp.einsum('nbhid,nbhjd->nbhij', kb, kc) * decay, 0.0)
    eye = jnp.eye(CHUNK, dtype=jnp.float32)
    t_inv = lax.linalg.triangular_solve(a_low + eye, jnp.broadcast_to(eye, a_low.shape),
                                        left_side=True, lower=True, unit_diagonal=True)
    u = jnp.einsum('nbhij,nbhje->nbhie', t_inv, vb)
    w = jnp.einsum('nbhij,nbhjd->nbhid', t_inv, kb * jnp.exp(gc)[..., None])
    attn = jnp.where(incl, jnp.einsum('nbhid,nbhjd->nbhij', qc, kc) * decay, 0.0)

    def step(S, xs):
        qn, kn, un, wn, gn, an = xs
        v_new = un - jnp.einsum('bhcd,bhde->bhce', wn, S)
        o = (jnp.einsum('bhcd,bhde->bhce', qn * jnp.exp(gn)[..., None], S)
             + jnp.einsum('bhij,bhje->bhie', an, v_new))
        gl = gn[..., -1]
        S = (S * jnp.exp(gl)[..., None, None]
             + jnp.einsum('bhcd,bhce->bhde', kn * jnp.exp(gl[..., None] - gn)[..., None], v_new))
        return S, o

    s_final, o = lax.scan(step, s0, (qc, kc, u, w, gc, attn))
    return o.transpose(1, 0, 3, 2, 4).reshape(B, L, H, dv), s_final


def even_mixer(h, w_in, w_out, gamma_logit, ret_norm_g, sc_w, s0f, s0b, latent):
    B, L, _ = h.shape
    p = h @ w_in
    q, k, v, g, bg, cg, hb = jnp.split(p, [256, 512, 1024, 1536, 2048, 2560], axis=-1)
    f32 = jnp.float32
    qh = q.reshape(B, L, H_A, DK_A).astype(f32)
    kh = k.reshape(B, L, H_A, DK_A).astype(f32) * (DK_A ** -0.5)
    vh = v.reshape(B, L, H_A, DV_A).astype(f32)
    if latent:
        qh = axial_rope(qh)
        kh = axial_rope(kh)
    log_gamma = jax.nn.log_sigmoid(gamma_logit.astype(f32))
    of, sf = retention_scan(qh, kh, vh, log_gamma[0], s0f)
    ob, sb = retention_scan(qh[:, ::-1], kh[:, ::-1], vh[:, ::-1], log_gamma[1], s0b)
    o = of + ob[:, ::-1]
    o = o * lax.rsqrt(jnp.mean(o * o, axis=-1, keepdims=True) + EPS) * ret_norm_g.astype(f32)
    ya = (jax.nn.silu(g.astype(f32)) * o.reshape(B, L, W_A)).astype(h.dtype)
    yb = bg * dwconv(cg * hb, sc_w)
    return jnp.concatenate([ya, yb], axis=-1) @ w_out, sf, sb


def odd_mixer(h, w_in, w_out, conv_w, a_log, dt_bias, gdn_norm_g, cf_w, cf_b, ln_g, ln_b, s0f, s0b):
    B, L, _ = h.shape
    f32 = jnp.float32
    p = h @ w_in
    qkv, z, a, bb, glu = jnp.split(p, [3 * W_C, 4 * W_C, 4 * W_C + 2 * H_C, 4 * W_C + 4 * H_C], axis=-1)
    qkv = jax.nn.silu(dwconv(qkv, conv_w)).astype(f32)
    q, k, v = jnp.split(qkv, 3, axis=-1)
    q = q.reshape(B, L, H_C, DK_C)
    k = k.reshape(B, L, H_C, DK_C)
    v = v.reshape(B, L, H_C, DV_C)
    q = q * lax.rsqrt(jnp.sum(q * q, axis=-1, keepdims=True) + EPS)
    k = k * lax.rsqrt(jnp.sum(k * k, axis=-1, keepdims=True) + EPS)
    a = a.reshape(B, L, 2, H_C).astype(f32)
    bb = bb.reshape(B, L, 2, H_C).astype(f32)
    g = -jnp.exp(a_log.astype(f32)) * jax.nn.softplus(a + dt_bias.astype(f32))
    beta = jax.nn.sigmoid(bb)
    of, sf = gdn_scan(q, k, v, g[:, :, 0], beta[:, :, 0], s0f)
    ob, sb = gdn_scan(q[:, ::-1], k[:, ::-1], v[:, ::-1], g[:, ::-1, 1], beta[:, ::-1, 1], s0b)
    o = of + ob[:, ::-1]
    o = o * lax.rsqrt(jnp.mean(o * o, axis=-1, keepdims=True) + EPS) * gdn_norm_g.astype(f32)
    zh = jax.nn.silu(z.astype(f32)).reshape(B, L, H_C, DV_C)
    yc = (o * zh).reshape(B, L, W_C).astype(h.dtype)
    ca, cgate = jnp.split(glu, 2, axis=-1)
    hc = ca * jax.nn.sigmoid(cgate)
    hc = dwconv(hc, cf_w) + cf_b
    yd = jax.nn.silu(layernorm(hc, ln_g, ln_b))
    return jnp.concatenate([yc, yd], axis=-1) @ w_out, sf, sb


def peer_ffn(h, wq, keys, u_tab, v_tab):
    B, L, D = h.shape
    T = B * L
    x = h.reshape(T, D)
    q = (x @ wq).reshape(T, H_P, 2, D_KEY // 2)
    s = jnp.einsum('thpd,hpnd->thpn', q, keys)
    s1, i1 = lax.top_k(s[:, :, 0], TOPK)
    s2, i2 = lax.top_k(s[:, :, 1], TOPK)
    cand_s = (s1[..., :, None] + s2[..., None, :]).reshape(T, H_P, TOPK * TOPK)
    cand_i = (i1[..., :, None] * N_KEYS + i2[..., None, :]).reshape(T, H_P, TOPK * TOPK)
    top_s, sel = lax.top_k(cand_s, TOPK)
    eidx = jnp.take_along_axis(cand_i, sel, axis=-1)
    gates = jax.nn.softmax(top_s.astype(jnp.float32), axis=-1).astype(h.dtype)
    nb = T // PEER_BLOCK
    hk = H_P * TOPK

    def block(args):
        xt, et, gt = args
        act = jax.nn.gelu(jnp.einsum('td,tkd->tk', xt, u_tab[et]))
        return jnp.einsum('tk,tkd->td', gt * act, v_tab[et])

    out = lax.map(block, (x.reshape(nb, PEER_BLOCK, D), eidx.reshape(nb, PEER_BLOCK, hk),
                          gates.reshape(nb, PEER_BLOCK, hk)))
    return out.reshape(B, L, D)


def setup_inputs(seed: int = 0) -> dict:
    key = jax.random.key(seed)
    ks = jax.random.split(key, 32)
    f32 = jnp.float32
    d = D_MODEL

    def nrm(k, shape, scale):
        return jax.random.normal(k, shape, f32) * scale

    base = jnp.log(2.0 ** (5.0 + jnp.arange(H_A, dtype=f32)) - 1.0)
    dt = jnp.exp(jax.random.uniform(ks[20], (N_ODD, 2, H_C), f32, math.log(1e-3), math.log(1e-1)))
    return {
        'x_prompt': nrm(ks[0], (BATCH, SEQ, d), 1.0),
        'x_sample': nrm(ks[1], (DEC_BATCH, DEC_SEQ, d), 1.0),
        'state_ret': nrm(ks[2], (DEC_BATCH, N_EVEN, 2, H_A, DK_A, DV_A), 0.1),
        'state_gdn': nrm(ks[3], (DEC_BATCH, N_ODD, 2, H_C, DK_C, DV_C), 0.1),
        'c': nrm(ks[4], (DEC_BATCH, d), 1.0),
        'c_ctx': nrm(ks[5], (d,), 1.0),
        'ada_w': nrm(ks[6], (DEPTH, d, 6 * d), 0.5 * d ** -0.5),
        'ada_b': nrm(ks[7], (DEPTH, 6 * d), 0.02),
        'norm_mix_g': 1.0 + nrm(ks[8], (DEPTH, d), 0.02),
        'norm_ffn_g': 1.0 + nrm(ks[9], (DEPTH, d), 0.02),
        'final_norm_g': 1.0 + nrm(ks[10], (d,), 0.02),
        'ev_w_in': nrm(ks[11], (N_EVEN, d, EV_IN), d ** -0.5),
        'ev_w_out': nrm(ks[12], (N_EVEN, MIX_W, d), MIX_W ** -0.5),
        'ret_gamma_logit': base + nrm(ks[13], (N_EVEN, 2, H_A), 0.01),
        'ret_norm_g': 1.0 + nrm(ks[14], (N_EVEN, H_A, DV_A), 0.02),
        'sc_conv_w': nrm(ks[15], (N_EVEN, SC_K, W_B), SC_K ** -0.5),
        'od_w_in': nrm(ks[16], (N_ODD, d, OD_IN), d ** -0.5),
        'od_w_out': nrm(ks[17], (N_ODD, MIX_W, d), MIX_W ** -0.5),
        'gdn_conv_w': nrm(ks[18], (N_ODD, QKV_K, 3 * W_C), QKV_K ** -0.5),
        'gdn_a_log': jnp.log(jax.random.uniform(ks[19], (N_ODD, 2, H_C), f32, 1.0, 16.0)),
        'gdn_dt_bias': dt + jnp.log(-jnp.expm1(-dt)),
        'gdn_norm_g': 1.0 + nrm(ks[21], (N_ODD, DV_C), 0.02),
        'cf_dw_w': nrm(ks[22], (N_ODD, CF_K, W_D), CF_K ** -0.5),
        'cf_dw_b': nrm(ks[23], (N_ODD, W_D), 0.02),
        'cf_ln_g': 1.0 + nrm(ks[24], (N_ODD, W_D), 0.02),
        'cf_ln_b': nrm(ks[25], (N_ODD, W_D), 0.02),
        'peer_wq': nrm(ks[26], (DEPTH, d, H_P * D_KEY), d ** -0.5),
        'peer_keys': nrm(ks[27], (DEPTH, H_P, 2, N_KEYS, D_KEY // 2), (D_KEY // 2) ** -0.5),
        'peer_u': nrm(ks[28], (DEPTH, N_EXP, d), d ** -0.5),
        'peer_v': nrm(ks[29], (DEPTH, N_EXP, d), 0.1),
    }


def reference(x_prompt, x_sample, state_ret, state_gdn, c, c_ctx, ada_w, ada_b, norm_mix_g, norm_ffn_g,
              final_norm_g, ev_w_in, ev_w_out, ret_gamma_logit, ret_norm_g, sc_conv_w, od_w_in, od_w_out,
              gdn_conv_w, gdn_a_log, gdn_dt_bias, gdn_norm_g, cf_dw_w, cf_dw_b, cf_ln_g, cf_ln_b,
              peer_wq, peer_keys, peer_u, peer_v):
    f32 = jnp.float32
    xp, xs = x_prompt, x_sample
    bp = xp.shape[0]
    ret_new = []
    gdn_new = []
    for l in range(DEPTH):
        sh1p, sc1p, ga1p, sh2p, sc2p, ga2p = modulation(c_ctx[None, :], ada_w[l], ada_b[l])
        sh1s, sc1s, ga1s, sh2s, sc2s, ga2s = modulation(c, ada_w[l], ada_b[l])
        hp = rmsnorm(xp, norm_mix_g[l]) * (1.0 + sc1p) + sh1p
        hs = rmsnorm(xs, norm_mix_g[l]) * (1.0 + sc1s) + sh1s
        i = l // 2
        if l % 2 == 0:
            z0 = jnp.zeros((bp, H_A, DK_A, DV_A), f32)
            yp, sf, sb = even_mixer(hp, ev_w_in[i], ev_w_out[i], ret_gamma_logit[i], ret_norm_g[i],
                                    sc_conv_w[i], z0, z0, False)
            ys, _, _ = even_mixer(hs, ev_w_in[i], ev_w_out[i], ret_gamma_logit[i], ret_norm_g[i],
                                  sc_conv_w[i], state_ret[:, i, 0].astype(f32), state_ret[:, i, 1].astype(f32), True)
            ret_new.append(jnp.stack([sf, sb], axis=1))
        else:
            z0 = jnp.zeros((bp, H_C, DK_C, DV_C), f32)
            yp, sf, sb = odd_mixer(hp, od_w_in[i], od_w_out[i], gdn_conv_w[i], gdn_a_log[i], gdn_dt_bias[i],
                                   gdn_norm_g[i], cf_dw_w[i], cf_dw_b[i], cf_ln_g[i], cf_ln_b[i], z0, z0)
            ys, _, _ = odd_mixer(hs, od_w_in[i], od_w_out[i], gdn_conv_w[i], gdn_a_log[i], gdn_dt_bias[i],
                                 gdn_norm_g[i], cf_dw_w[i], cf_dw_b[i], cf_ln_g[i], cf_ln_b[i],
                                 state_gdn[:, i, 0].astype(f32), state_gdn[:, i, 1].astype(f32))
            gdn_new.append(jnp.stack([sf, sb], axis=1))
        xp = xp + ga1p * yp
        xs = xs + ga1s * ys
        hp = rmsnorm(xp, norm_ffn_g[l]) * (1.0 + sc2p) + sh2p
        hs = rmsnorm(xs, norm_ffn_g[l]) * (1.0 + sc2s) + sh2s
        xp = xp + ga2p * peer_ffn(hp, peer_wq[l], peer_keys[l], peer_u[l], peer_v[l])
        xs = xs + ga2s * peer_ffn(hs, peer_wq[l], peer_keys[l], peer_u[l], peer_v[l])
    y_prompt = rmsnorm(xp, final_norm_g)
    y_sample = rmsnorm(xs, final_norm_g)
    new_state_ret = jnp.stack(ret_new, axis=1).astype(x_prompt.dtype)
    new_state_gdn = jnp.stack(gdn_new, axis=1).astype(x_prompt.dtype)
    return (y_prompt, y_sample, new_state_ret, new_state_gdn)
```

```python
import functools
import math

import jax
import jax.numpy as jnp
from jax import lax
from jax.experimental import pallas as pl
from jax.experimental.pallas import tpu as pltpu

F32 = jnp.float32
BF16 = jnp.bfloat16
HIGHEST = lax.Precision.HIGHEST

EPS = 1e-6
CHUNK = 64
GRID_W = 64
ROPE_BASE = 10000.0
H_A, DK_A, DV_A = 4, 64, 128
H_C, DK_C, DV_C = 4, 128, 128
SC_K, QKV_K, CF_K = 3, 3, 31
N_KEYS, H_P, TOPK = 128, 8, 16
N_MOD_ROWS = 16
LANES = 128
SUBLANES = 8
VMEM_LIMIT = 56 * 1024 * 1024
CONV_TILE = 256
NEG_INF = float("-inf")


def _cparams(sem):
    return pltpu.CompilerParams(dimension_semantics=sem, vmem_limit_bytes=VMEM_LIMIT)


def _bdot(a, b):
    return jnp.dot(a.astype(BF16), b.astype(BF16), preferred_element_type=F32)


def _bdot_nt(a, b):
    return lax.dot_general(a.astype(BF16), b.astype(BF16), (((1,), (1,)), ((), ())),
                           preferred_element_type=F32)


def _bdot_tn(a, b):
    return lax.dot_general(a.astype(BF16), b.astype(BF16), (((0,), (0,)), ((), ())),
                           preferred_element_type=F32)


def _hdot(a, b):
    return jnp.dot(a, b, precision=HIGHEST, preferred_element_type=F32)


def _sigmoid(x):
    return 1.0 / (1.0 + jnp.exp(-x))


def _silu(x):
    return x * _sigmoid(x)


def _softplus(x):
    return jnp.maximum(x, 0.0) + jnp.log1p(jnp.exp(-jnp.abs(x)))


def _log_sigmoid(x):
    return -_softplus(-x)


def _gelu_tanh(x):
    c = math.sqrt(2.0 / math.pi)
    return 0.5 * x * (1.0 + jnp.tanh(c * (x + 0.044715 * (x * x * x))))


def _rms(x, g):
    return x * lax.rsqrt(jnp.mean(x * x, axis=-1, keepdims=True) + EPS) * g


class _Streams:
    def __init__(self, bp, lp, bs, ls):
        self.bp, self.lp, self.bs, self.ls = bp, lp, bs, ls
        self.tp, self.ts = bp * lp, bs * ls
        self.t = self.tp + self.ts

    def mod_row(self, i, tile):
        tiles_p = self.tp // tile
        per_seq = self.ls // tile
        return jnp.where(i < tiles_p, 0, 1 + (i - tiles_p) // per_seq)

    def halo_flags(self, i, tile):
        tiles_p = self.tp // tile
        per_p = self.lp // tile
        per_s = self.ls // tile
        in_p = i < tiles_p
        jp = i % per_p
        js = (i - tiles_p) % per_s
        has_prev = jnp.where(in_p, jp > 0, js > 0)
        has_next = jnp.where(in_p, jp < per_p - 1, js < per_s - 1)
        return has_prev, has_next


def _mod_kernel(c_ref, w_ref, b_ref, o_ref):
    s = _silu(c_ref[...])
    o_ref[0] = _hdot(s, w_ref[0]) + b_ref[0]


def _modulation(cvec, ada_w, ada_b):
    depth, d, d6 = ada_w.shape
    nj = d6 // d
    return pl.pallas_call(
        _mod_kernel,
        grid=(depth, nj),
        in_specs=[pl.BlockSpec((N_MOD_ROWS, d), lambda l, j: (0, 0)),
                  pl.BlockSpec((1, d, d), lambda l, j: (l, 0, j)),
                  pl.BlockSpec((1, 1, d), lambda l, j: (l, 0, j))],
        out_specs=pl.BlockSpec((1, N_MOD_ROWS, d), lambda l, j: (l, 0, j)),
        out_shape=jax.ShapeDtypeStruct((depth, N_MOD_ROWS, d6), F32),
        compiler_params=_cparams(("arbitrary", "arbitrary")),
        name="modulation",
    )(cvec, ada_w, ada_b.reshape(depth, 1, d6))


def _norm_proj_kernel(x_ref, mod_ref, g_ref, w_ref, *o_refs, widths):
    m = mod_ref[0]
    h = _rms(x_ref[...], g_ref[...]) * (1.0 + m[1:2]) + m[0:1]
    p = jnp.dot(h.astype(BF16), w_ref[...], preferred_element_type=F32)
    off = 0
    for o_ref, wd in zip(o_refs, widths):
        o_ref[...] = p[:, off:off + wd]
        off += wd


def _norm_proj(st, x, mod, g, w, widths, tm):
    t, d = x.shape
    n = w.shape[1]
    return pl.pallas_call(
        functools.partial(_norm_proj_kernel, widths=widths),
        grid=(t // tm,),
        in_specs=[pl.BlockSpec((tm, d), lambda i: (i, 0)),
                  pl.BlockSpec((1, 6, d), lambda i: (st.mod_row(i, tm), 0, 0)),
                  pl.BlockSpec((1, d), lambda i: (0, 0)),
                  pl.BlockSpec((d, n), lambda i: (0, 0))],
        out_specs=[pl.BlockSpec((tm, wd), lambda i: (i, 0)) for wd in widths],
        out_shape=[jax.ShapeDtypeStruct((t, wd), F32) for wd in widths],
        compiler_params=_cparams(("arbitrary",)),
        name="norm_proj",
    )(x, mod, g, w)


def _ret_kernel(q_ref, k_ref, v_ref, g_ref, s0_ref, gam_ref, ng_ref, cos_ref, sin_ref,
                ya_ref, sout_ref, qs_ref, ks_ref, o_ref, *, seq_len, latent):
    nc = seq_len // CHUNK
    q = q_ref[...]
    k = k_ref[...] * (DK_A ** -0.5)
    if latent:
        lane = lax.broadcasted_iota(jnp.int32, q.shape, 1)
        first_half = (lane % DK_A) < (DK_A // 2)
        cos = cos_ref[...]
        sin = sin_ref[...]

        def rope(x):
            partner = jnp.where(first_half, pltpu.roll(x, LANES - DK_A // 2, 1),
                                pltpu.roll(x, DK_A // 2, 1))
            return x * cos + partner * sin

        q = rope(q)
        k = rope(k)
    qs_ref[...] = q
    ks_ref[...] = k

    row = lax.broadcasted_iota(jnp.int32, (CHUNK, CHUNK), 0).astype(F32)
    col = lax.broadcasted_iota(jnp.int32, (CHUNK, CHUNK), 1).astype(F32)
    diff = row - col
    for hh in range(2):
        lgf = _log_sigmoid(gam_ref[0, hh])
        lgb = _log_sigmoid(gam_ref[1, hh])
        lgf_c = lgf[:, :CHUNK]
        lgb_c = lgb[:, :CHUNK]
        dcomb = (jnp.where(diff >= 0, jnp.exp(lgf_c * jnp.maximum(diff, 0.0)), 0.0)
                 + jnp.where(diff <= 0, jnp.exp(lgb_c * jnp.maximum(-diff, 0.0)), 0.0))
        wend_f = jnp.exp(lgf_c * (CHUNK - 1.0 - row))
        wstart_f = jnp.exp(lgf_c * (row + 1.0))
        wend_b = jnp.exp(lgb_c * row)
        wstart_b = jnp.exp(lgb_c * (CHUNK - row))
        gch_f = jnp.exp(lgf * float(CHUNK))
        gch_b = jnp.exp(lgb * float(CHUNK))
        qsl = slice(hh * DK_A, (hh + 1) * DK_A)
        vsl = slice(hh * DV_A, (hh + 1) * DV_A)

        def fwd(n, s, qsl=qsl, vsl=vsl, dcomb=dcomb, wstart_f=wstart_f, wend_f=wend_f, gch_f=gch_f):
            r0 = pl.multiple_of(n * CHUNK, CHUNK)
            qn = qs_ref[pl.ds(r0, CHUNK), qsl]
            kn = ks_ref[pl.ds(r0, CHUNK), qsl]
            vn = v_ref[pl.ds(r0, CHUNK), vsl]
            a = _bdot_nt(qn, kn) * dcomb
            o_ref[pl.ds(r0, CHUNK), vsl] = _bdot(a, vn) + _bdot(qn * wstart_f, s)
            return s * gch_f + _bdot_tn(kn * wend_f, vn)

        sout_ref[0, 0, hh] = lax.fori_loop(0, nc, fwd, s0_ref[0, 0, hh])

        def bwd(i, s, qsl=qsl, vsl=vsl, wstart_b=wstart_b, wend_b=wend_b, gch_b=gch_b):
            r0 = pl.multiple_of((nc - 1 - i) * CHUNK, CHUNK)
            qn = qs_ref[pl.ds(r0, CHUNK), qsl]
            kn = ks_ref[pl.ds(r0, CHUNK), qsl]
            vn = v_ref[pl.ds(r0, CHUNK), vsl]
            o_ref[pl.ds(r0, CHUNK), vsl] += _bdot(qn * wstart_b, s)
            return s * gch_b + _bdot_tn(kn * wend_b, vn)

        sout_ref[0, 1, hh] = lax.fori_loop(0, nc, bwd, s0_ref[0, 1, hh])

    for hh in range(2):
        vsl = slice(hh * DV_A, (hh + 1) * DV_A)
        ya_ref[:, vsl] = _silu(g_ref[:, vsl]) * _rms(o_ref[:, vsl], ng_ref[hh])


def _retention(q, k, v, g, s0, gam, ng, cos_t, sin_t, nseq, seq_len, row_off, latent):
    w2 = 2 * DK_A
    v2 = 2 * DV_A
    return pl.pallas_call(
        functools.partial(_ret_kernel, seq_len=seq_len, latent=latent),
        grid=(nseq, H_A // 2),
        in_specs=[pl.BlockSpec((seq_len, w2), lambda b, p: (b + row_off, p)),
                  pl.BlockSpec((seq_len, w2), lambda b, p: (b + row_off, p)),
                  pl.BlockSpec((seq_len, v2), lambda b, p: (b + row_off, p)),
                  pl.BlockSpec((seq_len, v2), lambda b, p: (b + row_off, p)),
                  pl.BlockSpec((1, 2, 2, DK_A, DV_A), lambda b, p: (b, 0, p, 0, 0)),
                  pl.BlockSpec((2, 2, CHUNK, DV_A), lambda b, p: (0, p, 0, 0)),
                  pl.BlockSpec((2, 1, DV_A), lambda b, p: (p, 0, 0)),
                  pl.BlockSpec((seq_len, w2), lambda b, p: (0, 0)),
                  pl.BlockSpec((seq_len, w2), lambda b, p: (0, 0))],
        out_specs=[pl.BlockSpec((seq_len, v2), lambda b, p: (b, p)),
                   pl.BlockSpec((1, 2, 2, DK_A, DV_A), lambda b, p: (b, 0, p, 0, 0))],
        out_shape=[jax.ShapeDtypeStruct((nseq * seq_len, H_A * DV_A), F32),
                   jax.ShapeDtypeStruct((nseq, 2, H_A, DK_A, DV_A), F32)],
        scratch_shapes=[pltpu.VMEM((seq_len, w2), F32), pltpu.VMEM((seq_len, w2), F32),
                        pltpu.VMEM((seq_len, v2), F32)],
        compiler_params=_cparams(("arbitrary", "arbitrary")),
        name="retention",
    )(q, k, v, g, s0, gam, ng, cos_t, sin_t)


def _fill_padded(pad_ref, cur, prev, nxt, has_prev, has_next, halo):
    tile = cur.shape[0]
    pad_ref[0:halo, :] = jnp.where(has_prev, prev, 0.0)
    pad_ref[halo:halo + tile, :] = cur
    pad_ref[halo + tile:halo + tile + halo, :] = jnp.where(has_next, nxt, 0.0)


def _conv_taps(pad_ref, w_ref, ntaps, halo, tile):
    base = halo - ntaps // 2
    acc = w_ref[0:1, :] * pad_ref[base:base + tile, :]
    for kk in range(1, ntaps):
        acc = acc + w_ref[kk:kk + 1, :] * pad_ref[base + kk:base + kk + tile, :]
    return acc


def _halo_specs(width, col_map, halo):
    per = CONV_TILE // halo

    def cur(i, j):
        return (i, col_map(j))

    def prev(i, j):
        return (jnp.maximum(i * per - 1, 0), col_map(j))

    def make_next(nblk):
        def nxt(i, j):
            return (jnp.minimum((i + 1) * per, nblk - 1), col_map(j))
        return nxt

    return cur, prev, make_next


def _sconv_kernel(bg_ref, cg_ref, cgp_ref, cgn_ref, hb_ref, hbp_ref, hbn_ref, w_ref, o_ref, pad_ref, *, st):
    has_prev, has_next = st.halo_flags(pl.program_id(0), CONV_TILE)
    _fill_padded(pad_ref, cg_ref[...] * hb_ref[...], cgp_ref[...] * hbp_ref[...],
                 cgn_ref[...] * hbn_ref[...], has_prev, has_next, SUBLANES)
    o_ref[...] = bg_ref[...] * _conv_taps(pad_ref, w_ref, SC_K, SUBLANES, CONV_TILE)


def _short_gated_conv(st, bg, cg, hb, w):
    t, c = bg.shape
    halo = SUBLANES
    cur, prev, make_next = _halo_specs(c, lambda j: j, halo)
    nxt = make_next(t // halo)
    tile_spec = pl.BlockSpec((CONV_TILE, LANES), cur)
    prev_spec = pl.BlockSpec((halo, LANES), prev)
    next_spec = pl.BlockSpec((halo, LANES), nxt)
    return pl.pallas_call(
        functools.partial(_sconv_kernel, st=st),
        grid=(t // CONV_TILE, c // LANES),
        in_specs=[tile_spec, tile_spec, prev_spec, next_spec, tile_spec, prev_spec, next_spec,
                  pl.BlockSpec((SUBLANES, LANES), lambda i, j: (0, j))],
        out_specs=tile_spec,
        out_shape=jax.ShapeDtypeStruct((t, c), F32),
        scratch_shapes=[pltpu.VMEM((CONV_TILE + 2 * halo, LANES), F32)],
        compiler_params=_cparams(("arbitrary", "arbitrary")),
        name="short_gated_conv",
    )(bg, cg, cg, cg, hb, hb, hb, w)


def _qkv_conv_kernel(x_ref, xp_ref, xn_ref, w_ref, o_ref, pad_ref, *, st):
    has_prev, has_next = st.halo_flags(pl.program_id(0), CONV_TILE)
    _fill_padded(pad_ref, x_ref[...], xp_ref[...], xn_ref[...], has_prev, has_next, SUBLANES)
    s = _silu(_conv_taps(pad_ref, w_ref, QKV_K, SUBLANES, CONV_TILE))
    nrm = s * lax.rsqrt(jnp.sum(s * s, axis=-1, keepdims=True) + EPS)
    j = pl.program_id(1)
    o_ref[...] = jnp.where(j < H_C, nrm * (DK_C ** -0.5), jnp.where(j < 2 * H_C, nrm, s))


def _qkv_conv(st, qkv, w):
    t, c = qkv.shape
    halo = SUBLANES
    cur, prev, make_next = _halo_specs(c, lambda j: j, halo)
    nxt = make_next(t // halo)
    tile_spec = pl.BlockSpec((CONV_TILE, LANES), cur)
    return pl.pallas_call(
        functools.partial(_qkv_conv_kernel, st=st),
        grid=(t // CONV_TILE, c // LANES),
        in_specs=[tile_spec, pl.BlockSpec((halo, LANES), prev), pl.BlockSpec((halo, LANES), nxt),
                  pl.BlockSpec((SUBLANES, LANES), lambda i, j: (0, j))],
        out_specs=tile_spec,
        out_shape=jax.ShapeDtypeStruct((t, c), F32),
        scratch_shapes=[pltpu.VMEM((CONV_TILE + 2 * halo, LANES), F32)],
        compiler_params=_cparams(("arbitrary", "arbitrary")),
        name="qkv_conv",
    )(qkv, qkv, qkv, w)


CF_HALO = 16


def _conformer_kernel(ca_ref, cap_ref, can_ref, cg_ref, cgp_ref, cgn_ref, w_ref, b_ref, lg_ref, lb_ref,
                      o_ref, pad_ref, *, st):
    has_prev, has_next = st.halo_flags(pl.program_id(0), CONV_TILE)
    _fill_padded(pad_ref, ca_ref[...] * _sigmoid(cg_ref[...]), cap_ref[...] * _sigmoid(cgp_ref[...]),
                 can_ref[...] * _sigmoid(cgn_ref[...]), has_prev, has_next, CF_HALO)
    hc = _conv_taps(pad_ref, w_ref, CF_K, CF_HALO, CONV_TILE) + b_ref[...]
    mu = jnp.mean(hc, axis=-1, keepdims=True)
    xc = hc - mu
    y = xc * lax.rsqrt(jnp.mean(xc * xc, axis=-1, keepdims=True) + EPS) * lg_ref[...] + lb_ref[...]
    o_ref[...] = _silu(y)


def _conformer(st, glu, w, b, ln_g, ln_b):
    t, c2 = glu.shape
    c = c2 // 2
    nblk = t // CF_HALO
    per = CONV_TILE // CF_HALO
    vec = pl.BlockSpec((1, c), lambda i: (0, 0))
    return pl.pallas_call(
        functools.partial(_conformer_kernel, st=st),
        grid=(t // CONV_TILE,),
        in_specs=[pl.BlockSpec((CONV_TILE, c), lambda i: (i, 0)),
                  pl.BlockSpec((CF_HALO, c), lambda i: (jnp.maximum(i * per - 1, 0), 0)),
                  pl.BlockSpec((CF_HALO, c), lambda i: (jnp.minimum((i + 1) * per, nblk - 1), 0)),
                  pl.BlockSpec((CONV_TILE, c), lambda i: (i, 1)),
                  pl.BlockSpec((CF_HALO, c), lambda i: (jnp.maximum(i * per - 1, 0), 1)),
                  pl.BlockSpec((CF_HALO, c), lambda i: (jnp.minimum((i + 1) * per, nblk - 1), 1)),
                  pl.BlockSpec((2 * CF_HALO, c), lambda i: (0, 0)), vec, vec, vec],
        out_specs=pl.BlockSpec((CONV_TILE, c), lambda i: (i, 0)),
        out_shape=jax.ShapeDtypeStruct((t, c), F32),
        scratch_shapes=[pltpu.VMEM((CONV_TILE + 2 * CF_HALO, c), F32)],
        compiler_params=_cparams(("arbitrary",)),
        name="conformer_conv",
    )(glu, glu, glu, glu, glu, glu, w, b, ln_g, ln_b)


def _unit_lower_inverse(a):
    eye = (lax.broadcasted_iota(jnp.int32, a.shape, 0) == lax.broadcasted_iota(jnp.int32, a.shape, 1)).astype(F32)
    t = eye - a
    p = a
    steps = int(math.log2(CHUNK)) - 1
    for _ in range(steps):
        p = _hdot(p, p)
        t = t + _hdot(t, p)
    return t


def _gdn_kernel(q_ref, k_ref, v_ref, z_ref, ab_ref, alog_ref, dtb_ref, s0_ref, ng_ref,
                y_ref, sout_ref, g_ref, beta_ref, s_ref, o_ref, *, seq_len):
    nc = seq_len // CHUNK
    h = pl.program_id(1)
    ab = ab_ref[...]
    g_ref[...] = -jnp.exp(alog_ref[...]) * _softplus(ab + dtb_ref[...])
    beta_ref[...] = _sigmoid(ab)

    row = lax.broadcasted_iota(jnp.int32, (CHUNK, CHUNK), 0)
    col = lax.broadcasted_iota(jnp.int32, (CHUNK, CHUNK), 1)
    lane = lax.broadcasted_iota(jnp.int32, (CHUNK, LANES), 1)
    sub8 = lax.broadcasted_iota(jnp.int32, (SUBLANES, LANES), 0)
    lane8 = lax.broadcasted_iota(jnp.int32, (SUBLANES, LANES), 1)

    for d in range(2):
        incl = (row >= col) if d == 0 else (row <= col)
        strict = (row > col) if d == 0 else (row < col)
        tri = incl.astype(F32)
        gcol_idx = d * H_C + h
        bcol_idx = 2 * H_C + d * H_C + h
        pick = ((sub8 == 0) & (lane8 == gcol_idx)).astype(F32)
        last = CHUNK - 1 if d == 0 else 0
        s_ref[...] = s0_ref[0, d, 0]

        def step(i, carry, d=d, incl=incl, strict=strict, tri=tri, gcol_idx=gcol_idx, bcol_idx=bcol_idx,
                 pick=pick, last=last):
            n = i if d == 0 else nc - 1 - i
            r0 = pl.multiple_of(n * CHUNK, CHUNK)
            qc = q_ref[pl.ds(r0, CHUNK), :]
            kc = k_ref[pl.ds(r0, CHUNK), :]
            vc = v_ref[pl.ds(r0, CHUNK), :]
            gcum = _hdot(tri, g_ref[pl.ds(r0, CHUNK), :])
            gcol = jnp.sum(jnp.where(lane == gcol_idx, gcum, 0.0), axis=-1, keepdims=True)
            grow = lax.dot_general(pick, gcum, (((1,), (1,)), ((), ())), precision=HIGHEST,
                                   preferred_element_type=F32)[0:1, :]
            bcol = jnp.sum(jnp.where(lane == bcol_idx, beta_ref[pl.ds(r0, CHUNK), :], 0.0),
                           axis=-1, keepdims=True)
            gl = gcol[last:last + 1, :]
            decay = jnp.where(incl, jnp.exp(jnp.where(incl, gcol - grow, 0.0)), 0.0)
            kb = kc * bcol
            vb = vc * bcol
            a_low = jnp.where(strict, _bdot_nt(kb, kc) * decay, 0.0)
            t_inv = _unit_lower_inverse(a_low)
            u = _bdot(t_inv, vb)
            w = _bdot(t_inv, kb * jnp.exp(gcol))
            attn = jnp.where(incl, _bdot_nt(qc, kc) * decay, 0.0)
            s = s_ref[...]
            v_new = u - _bdot(w, s)
            o = _bdot(qc * jnp.exp(gcol), s) + _bdot(attn, v_new)
            if d == 0:
                o_ref[pl.ds(r0, CHUNK), :] = o
            else:
                o_ref[pl.ds(r0, CHUNK), :] += o
            s_ref[...] = s * jnp.exp(gl) + _bdot_tn(kc * jnp.exp(gl - gcol), v_new)
            return carry

        lax.fori_loop(0, nc, step, 0)
        sout_ref[0, d, 0] = s_ref[...]

    y_ref[...] = _rms(o_ref[...], ng_ref[...]) * _silu(z_ref[...])


def _gdn(qkv_n, z, ab, alog, dtb, s0, ng, nseq, seq_len, row_off):
    blk = lambda off: pl.BlockSpec((seq_len, LANES), lambda b, h: (b + row_off, h + off))
    vec = pl.BlockSpec((1, LANES), lambda b, h: (0, 0))
    st_spec = pl.BlockSpec((1, 2, 1, DK_C, DV_C), lambda b, h: (b, 0, h, 0, 0))
    return pl.pallas_call(
        functools.partial(_gdn_kernel, seq_len=seq_len),
        grid=(nseq, H_C),
        in_specs=[blk(0), blk(H_C), blk(2 * H_C), blk(0),
                  pl.BlockSpec((seq_len, LANES), lambda b, h: (b + row_off, 0)),
                  vec, vec, st_spec, vec],
        out_specs=[pl.BlockSpec((seq_len, LANES), lambda b, h: (b, h)), st_spec],
        out_shape=[jax.ShapeDtypeStruct((nseq * seq_len, H_C * DV_C), F32),
                   jax.ShapeDtypeStruct((nseq, 2, H_C, DK_C, DV_C), F32)],
        scratch_shapes=[pltpu.VMEM((seq_len, LANES), F32), pltpu.VMEM((seq_len, LANES), F32),
                        pltpu.VMEM((DK_C, DV_C), F32), pltpu.VMEM((seq_len, DV_C), F32)],
        compiler_params=_cparams(("arbitrary", "arbitrary")),
        name="gated_deltanet",
    )(qkv_n, qkv_n, qkv_n, z, ab, alog, dtb, s0, ng)


def _out_proj_kernel(ya_ref, yb_ref, x_ref, mod_ref, g_ref, wo_ref, wq_ref, x1_ref, h2_ref, qp_ref):
    half = ya_ref.shape[1]
    m = mod_ref[0]
    y = (jnp.dot(ya_ref[...].astype(BF16), wo_ref[0:half, :], preferred_element_type=F32)
         + jnp.dot(yb_ref[...].astype(BF16), wo_ref[half:, :], preferred_element_type=F32))
    x1 = x_ref[...] + m[2:3] * y
    x1_ref[...] = x1
    h2 = (_rms(x1, g_ref[...]) * (1.0 + m[4:5]) + m[3:4]).astype(BF16)
    h2_ref[...] = h2
    qp_ref[...] = jnp.dot(h2, wq_ref[...], preferred_element_type=F32)


def _out_proj(st, ya, yb, x, mod, g, wo, wq, tm):
    t, d = x.shape
    half = ya.shape[1]
    nq = wq.shape[1]
    return pl.pallas_call(
        _out_proj_kernel,
        grid=(t // tm,),
        in_specs=[pl.BlockSpec((tm, half), lambda i: (i, 0)),
                  pl.BlockSpec((tm, half), lambda i: (i, 0)),
                  pl.BlockSpec((tm, d), lambda i: (i, 0)),
                  pl.BlockSpec((1, 6, d), lambda i: (st.mod_row(i, tm), 0, 0)),
                  pl.BlockSpec((1, d), lambda i: (0, 0)),
                  pl.BlockSpec((2 * half, d), lambda i: (0, 0)),
                  pl.BlockSpec((d, nq), lambda i: (0, 0))],
        out_specs=[pl.BlockSpec((tm, d), lambda i: (i, 0)),
                   pl.BlockSpec((tm, d), lambda i: (i, 0)),
                   pl.BlockSpec((tm, nq), lambda i: (i, 0))],
        out_shape=[jax.ShapeDtypeStruct((t, d), F32), jax.ShapeDtypeStruct((t, d), BF16),
                   jax.ShapeDtypeStruct((t, nq), F32)],
        compiler_params=_cparams(("arbitrary",)),
        name="out_proj",
    )(ya, yb, x, mod, g, wo, wq)


def _top16(s, vals_ref):
    n = s.shape[0]
    iota = lax.broadcasted_iota(jnp.int32, s.shape, 0).astype(F32)
    rank = jnp.full(s.shape, float(TOPK), F32)
    for kk in range(TOPK):
        m = jnp.max(s, axis=0, keepdims=True)
        idx = jnp.min(jnp.where(s == m, iota, float(n)), axis=0, keepdims=True)
        hit = iota == idx
        rank = jnp.where(hit, float(kk), rank)
        vals_ref[kk:kk + 1, :] = m
        s = jnp.where(hit, NEG_INF, s)
    return rank


def _peer_select_kernel(q_ref, keys_ref, e1_ref, cnt_ref, e2_ref, r2_ref, v1_ref, v2_ref, *, tt):
    nk = N_KEYS
    for strip in range(tt // LANES):
        tsl = slice(strip * LANES, (strip + 1) * LANES)
        q1 = q_ref[tsl, 0:nk]
        q2 = q_ref[tsl, nk:2 * nk]
        s1 = _bdot_nt(keys_ref[0, 0], q1)
        s2 = _bdot_nt(keys_ref[0, 1], q2)
        rank1 = _top16(s1, v1_ref)
        rank2 = _top16(s2, v2_ref)
        v1 = v1_ref[...]
        v2 = v2_ref[...]
        ex1 = jnp.exp(v1 - v1[0:1])
        ex2 = jnp.exp(v2 - v2[0:1])
        cand = jnp.concatenate([v1[k1:k1 + 1] + v2 for k1 in range(TOPK)], axis=0)
        prod = jnp.concatenate([ex1[k1:k1 + 1] * ex2 for k1 in range(TOPK)], axis=0)
        ci = lax.broadcasted_iota(jnp.int32, cand.shape, 0)
        viable = ((ci // TOPK) + 1) * ((ci % TOPK) + 1) <= TOPK
        cand = jnp.where(viable, cand, NEG_INF)
        cif = ci.astype(F32)
        sel = jnp.zeros(cand.shape, F32)
        for _ in range(TOPK):
            m = jnp.max(cand, axis=0, keepdims=True)
            idx = jnp.min(jnp.where(cand == m, cif, float(TOPK * TOPK)), axis=0, keepdims=True)
            hit = cif == idx
            sel = jnp.where(hit, 1.0, sel)
            cand = jnp.where(hit, NEG_INF, cand)
        zsum = jnp.sum(sel * prod, axis=0, keepdims=True)
        cnt = jnp.zeros(rank1.shape, F32)
        for k1 in range(TOPK):
            c_k1 = jnp.sum(sel[k1 * TOPK:(k1 + 1) * TOPK], axis=0, keepdims=True)
            cnt = jnp.where(rank1 == float(k1), c_k1, cnt)
        e1_ref[0, :, tsl] = jnp.where(rank1 < float(TOPK), jnp.exp(s1 - v1[0:1]), 0.0) / zsum
        cnt_ref[0, :, tsl] = cnt
        e2_ref[0, :, tsl] = jnp.where(rank2 < float(TOPK), jnp.exp(s2 - v2[0:1]), 0.0)
        r2_ref[0, :, tsl] = rank2


def _peer_select(qp, keys, tt):
    t = qp.shape[0]
    out_spec = pl.BlockSpec((1, N_KEYS, tt), lambda i, h: (h, 0, i))
    out_sds = jax.ShapeDtypeStruct((H_P, N_KEYS, t), F32)
    return pl.pallas_call(
        functools.partial(_peer_select_kernel, tt=tt),
        grid=(t // tt, H_P),
        in_specs=[pl.BlockSpec((tt, 2 * N_KEYS), lambda i, h: (i, h)),
                  pl.BlockSpec((1, 2, N_KEYS, N_KEYS), lambda i, h: (h, 0, 0, 0))],
        out_specs=[out_spec] * 4,
        out_shape=[out_sds] * 4,
        scratch_shapes=[pltpu.VMEM((TOPK, LANES), F32), pltpu.VMEM((TOPK, LANES), F32)],
        compiler_params=_cparams(("arbitrary", "arbitrary")),
        name="peer_select",
    )(qp, keys)


def _peer_dense_kernel(h_ref, u_ref, vt_ref, e1_ref, cnt_ref, e2_ref, r2_ref, x_ref, mod_ref, o_ref,
                       acc_ref, p_ref, *, na):
    j = pl.program_id(1)

    @pl.when(j == 0)
    def _():
        acc_ref[...] = jnp.zeros_like(acc_ref)

    s_t = lax.dot_general(u_ref[...], h_ref[...], (((1,), (1,)), ((), ())),
                          preferred_element_type=F32)
    for aa in range(na):
        a = j * na + aa
        gate = jnp.zeros((N_KEYS, s_t.shape[1]), F32)
        for hh in range(H_P):
            cnt_row = cnt_ref[hh, pl.ds(a, 1), :]
            e1_row = e1_ref[hh, pl.ds(a, 1), :]
            gate = gate + jnp.where(r2_ref[hh] < cnt_row, e2_ref[hh], 0.0) * e1_row
        esl = slice(aa * N_KEYS, (aa + 1) * N_KEYS)
        p_ref[esl, :] = (gate * _gelu_tanh(s_t[esl, :])).astype(BF16)
    acc_ref[...] += jnp.dot(vt_ref[...], p_ref[...], preferred_element_type=F32)

    @pl.when(j == pl.num_programs(1) - 1)
    def _():
        o_ref[...] = x_ref[...] + mod_ref[0][5:6] * acc_ref[...].T


def _peer_dense(st, h2, u, vt, e1, cnt, e2, r2, x, mod, tt, et):
    t, d = x.shape
    n_exp = u.shape[0]
    sel_spec = pl.BlockSpec((H_P, N_KEYS, tt), lambda i, j: (0, 0, i))
    return pl.pallas_call(
        functools.partial(_peer_dense_kernel, na=et // N_KEYS),
        grid=(t // tt, n_exp // et),
        in_specs=[pl.BlockSpec((tt, d), lambda i, j: (i, 0)),
                  pl.BlockSpec((et, d), lambda i, j: (j, 0)),
                  pl.BlockSpec((d, et), lambda i, j: (0, j)),
                  sel_spec, sel_spec, sel_spec, sel_spec,
                  pl.BlockSpec((tt, d), lambda i, j: (i, 0)),
                  pl.BlockSpec((1, 6, d), lambda i, j: (st.mod_row(i, tt), 0, 0))],
        out_specs=pl.BlockSpec((tt, d), lambda i, j: (i, 0)),
        out_shape=jax.ShapeDtypeStruct((t, d), F32),
        scratch_shapes=[pltpu.VMEM((d, tt), F32), pltpu.VMEM((et, tt), BF16)],
        compiler_params=_cparams(("arbitrary", "arbitrary")),
        name="peer_dense",
    )(h2, u, vt, e1, cnt, e2, r2, x, mod)


def _final_norm_kernel(x_ref, g_ref, o_ref):
    o_ref[...] = _rms(x_ref[...], g_ref[...])


def _final_norm(x, g, tm):
    t, d = x.shape
    return pl.pallas_call(
        _final_norm_kernel,
        grid=(t // tm,),
        in_specs=[pl.BlockSpec((tm, d), lambda i: (i, 0)), pl.BlockSpec((1, d), lambda i: (0, 0))],
        out_specs=pl.BlockSpec((tm, d), lambda i: (i, 0)),
        out_shape=jax.ShapeDtypeStruct((t, d), F32),
        compiler_params=_cparams(("arbitrary",)),
        name="final_norm",
    )(x, g)


def _rope_tables(seq_len):
    pos = jnp.arange(seq_len)
    rowp = (pos // GRID_W).astype(F32)
    colp = (pos % GRID_W).astype(F32)
    nf = DK_A // 4
    freqs = ROPE_BASE ** (-jnp.arange(nf, dtype=F32) / nf)
    ang = jnp.concatenate([rowp[:, None] * freqs, colp[:, None] * freqs], axis=-1)
    cos = jnp.cos(ang)
    sin = jnp.sin(ang)
    cos_t = jnp.tile(cos, (1, 4))
    sin_t = jnp.tile(jnp.concatenate([-sin, sin], axis=-1), (1, 2))
    return cos_t, sin_t


def _pad_rows(w, rows):
    return jnp.concatenate([w, jnp.zeros((rows - w.shape[0], w.shape[1]), w.dtype)], axis=0)


def _lane_row(vals):
    flat = vals.reshape(-1).astype(F32)
    return jnp.concatenate([flat, jnp.zeros((LANES - flat.shape[0],), F32)])[None, :]


def kernel(x_prompt, x_sample, state_ret, state_gdn, c, c_ctx, ada_w, ada_b, norm_mix_g, norm_ffn_g,
           final_norm_g, ev_w_in, ev_w_out, ret_gamma_logit, ret_norm_g, sc_conv_w, od_w_in, od_w_out,
           gdn_conv_w, gdn_a_log, gdn_dt_bias, gdn_norm_g, cf_dw_w, cf_dw_b, cf_ln_g, cf_ln_b,
           peer_wq, peer_keys, peer_u, peer_v):
    bp, lp, d = x_prompt.shape
    bs, ls, _ = x_sample.shape
    depth = ada_w.shape[0]
    st = _Streams(bp, lp, bs, ls)
    w_a = H_A * DV_A
    w_b = d - w_a
    w_c = H_C * DV_C
    w_d = d - w_c
    tm = 512
    peer_tt = 512
    peer_et = 512

    x = jnp.concatenate([x_prompt.reshape(st.tp, d), x_sample.reshape(st.ts, d)], axis=0)
    cvec = jnp.concatenate([c_ctx[None, :], c, jnp.zeros((N_MOD_ROWS - 1 - bs, d), F32)], axis=0)
    mods = _modulation(cvec, ada_w, ada_b).reshape(depth, N_MOD_ROWS, 6, d)
    cos_t, sin_t = _rope_tables(ls)
    zero_ret = jnp.zeros((bp, 2, H_A, DK_A, DV_A), F32)
    zero_gdn = jnp.zeros((bp, 2, H_C, DK_C, DV_C), F32)

    ret_new, gdn_new = [], []
    for l in range(depth):
        i = l // 2
        mod = mods[l]
        g1 = norm_mix_g[l][None, :]
        if l % 2 == 0:
            widths = (H_A * DK_A, H_A * DK_A, w_a, w_a, w_b, w_b, w_b)
            q, k, v, g, bg, cg, hb = _norm_proj(st, x, mod, g1, ev_w_in[i].astype(BF16), widths, tm)
            gam = jnp.broadcast_to(ret_gamma_logit[i][:, :, None, None], (2, H_A, CHUNK, DV_A))
            ng = ret_norm_g[i][:, None, :]
            ya_p, s_new = _retention(q, k, v, g, zero_ret, gam, ng, cos_t[:lp], sin_t[:lp],
                                     bp, lp, 0, False)
            ya_s, _ = _retention(q, k, v, g, state_ret[:, i], gam, ng, cos_t, sin_t,
                                 bs, ls, st.tp // ls, True)
            ret_new.append(s_new)
            ya = jnp.concatenate([ya_p, ya_s], axis=0)
            yb = _short_gated_conv(st, bg, cg, hb, _pad_rows(sc_conv_w[i], SUBLANES))
            w_out = ev_w_out[i]
        else:
            n_gate = 2 * 2 * H_C
            w_in = od_w_in[i]
            o_ab = 4 * w_c
            w_main = jnp.concatenate([w_in[:, :o_ab], w_in[:, o_ab + n_gate:],
                                      w_in[:, o_ab:o_ab + n_gate],
                                      jnp.zeros((d, LANES - n_gate), F32)], axis=1).astype(BF16)
            widths = (3 * w_c, w_c, 2 * w_d, LANES)
            qkv, z, glu, ab = _norm_proj(st, x, mod, g1, w_main, widths, tm)
            qkv_n = _qkv_conv(st, qkv, _pad_rows(gdn_conv_w[i], SUBLANES))
            alog = _lane_row(gdn_a_log[i])
            dtb = _lane_row(gdn_dt_bias[i])
            ng = gdn_norm_g[i][None, :]
            yc_p, s_new = _gdn(qkv_n, z, ab, alog, dtb, zero_gdn, ng, bp, lp, 0)
            yc_s, _ = _gdn(qkv_n, z, ab, alog, dtb, state_gdn[:, i], ng, bs, ls, st.tp // ls)
            gdn_new.append(s_new)
            ya = jnp.concatenate([yc_p, yc_s], axis=0)
            yb = _conformer(st, glu, _pad_rows(cf_dw_w[i], 2 * CF_HALO), cf_dw_b[i][None, :],
                            cf_ln_g[i][None, :], cf_ln_b[i][None, :])
            w_out = od_w_out[i]
        x, h2, qp = _out_proj(st, ya, yb, x, mod, norm_ffn_g[l][None, :], w_out.astype(BF16),
                              peer_wq[l].astype(BF16), tm)
        e1, cnt, e2, r2 = _peer_select(qp, peer_keys[l].astype(BF16), 256)
        x = _peer_dense(st, h2, peer_u[l].astype(BF16), peer_v[l].astype(BF16).T, e1, cnt, e2, r2,
                        x, mod, peer_tt, peer_et)

    y = _final_norm(x, final_norm_g[None, :], tm)
    y_prompt = y[:st.tp].reshape(bp, lp, d)
    y_sample = y[st.tp:].reshape(bs, ls, d)
    new_state_ret = jnp.stack(ret_new, axis=1).astype(x_prompt.dtype)
    new_state_gdn = jnp.stack(gdn_new, axis=1).astype(x_prompt.dtype)
    return (y_prompt, y_sample, new_state_ret, new_state_gdn)
```

```python
import functools
import math

import jax
import jax.numpy as jnp
from jax import lax
from jax.experimental import pallas as pl
from jax.experimental.pallas import tpu as pltpu

F32 = jnp.float32
BF16 = jnp.bfloat16
HIGHEST = lax.Precision.HIGHEST

EPS = 1e-6
CHUNK = 64
GRID_W = 64
ROPE_BASE = 10000.0
H_A, DK_A, DV_A = 4, 64, 128
H_C, DK_C, DV_C = 4, 128, 128
SC_K, QKV_K, CF_K = 3, 3, 31
N_KEYS, H_P, TOPK = 128, 8, 16
N_MOD_ROWS = 16
LANES = 128
SUBLANES = 8
VMEM_LIMIT = 56 * 1024 * 1024
CONV_TILE = 256
NEG_INF = float("-inf")


def _cparams(sem):
    return pltpu.CompilerParams(dimension_semantics=sem, vmem_limit_bytes=VMEM_LIMIT)


def _bdot(a, b):
    return jnp.dot(a.astype(BF16), b.astype(BF16), preferred_element_type=F32)


def _bdot_nt(a, b):
    return lax.dot_general(a.astype(BF16), b.astype(BF16), (((1,), (1,)), ((), ())),
                           preferred_element_type=F32)


def _bdot_tn(a, b):
    return lax.dot_general(a.astype(BF16), b.astype(BF16), (((0,), (0,)), ((), ())),
                           preferred_element_type=F32)


def _hdot(a, b):
    return jnp.dot(a, b, precision=HIGHEST, preferred_element_type=F32)


def _sigmoid(x):
    return 1.0 / (1.0 + jnp.exp(-x))


def _silu(x):
    return x * _sigmoid(x)


def _softplus(x):
    return jnp.maximum(x, 0.0) + jnp.log1p(jnp.exp(-jnp.abs(x)))


def _log_sigmoid(x):
    return -_softplus(-x)


def _gelu_tanh(x):
    c = math.sqrt(2.0 / math.pi)
    return 0.5 * x * (1.0 + jnp.tanh(c * (x + 0.044715 * (x * x * x))))


def _rms(x, g):
    return x * lax.rsqrt(jnp.mean(x * x, axis=-1, keepdims=True) + EPS) * g


class _Streams:
    def __init__(self, bp, lp, bs, ls):
        self.bp, self.lp, self.bs, self.ls = bp, lp, bs, ls
        self.tp, self.ts = bp * lp, bs * ls
        self.t = self.tp + self.ts

    def mod_row(self, i, tile):
        tiles_p = self.tp // tile
        per_seq = self.ls // tile
        return jnp.where(i < tiles_p, 0, 1 + (i - tiles_p) // per_seq)

    def halo_flags(self, i, tile):
        tiles_p = self.tp // tile
        per_p = self.lp // tile
        per_s = self.ls // tile
        in_p = i < tiles_p
        jp = i % per_p
        js = (i - tiles_p) % per_s
        has_prev = jnp.where(in_p, jp > 0, js > 0)
        has_next = jnp.where(in_p, jp < per_p - 1, js < per_s - 1)
        return has_prev, has_next


def _mod_kernel(c_ref, w_ref, b_ref, o_ref):
    s = _silu(c_ref[...])
    o_ref[0] = _hdot(s, w_ref[0]) + b_ref[0]


def _modulation(cvec, ada_w, ada_b):
    depth, d, d6 = ada_w.shape
    nj = d6 // d
    return pl.pallas_call(
        _mod_kernel,
        grid=(depth, nj),
        in_specs=[pl.BlockSpec((N_MOD_ROWS, d), lambda l, j: (0, 0)),
                  pl.BlockSpec((1, d, d), lambda l, j: (l, 0, j)),
                  pl.BlockSpec((1, 1, d), lambda l, j: (l, 0, j))],
        out_specs=pl.BlockSpec((1, N_MOD_ROWS, d), lambda l, j: (l, 0, j)),
        out_shape=jax.ShapeDtypeStruct((depth, N_MOD_ROWS, d6), F32),
        compiler_params=_cparams(("arbitrary", "arbitrary")),
        name="modulation",
    )(cvec, ada_w, ada_b.reshape(depth, 1, d6))


def _norm_proj_kernel(x_ref, mod_ref, g_ref, w_ref, *o_refs, widths):
    m = mod_ref[0]
    h = _rms(x_ref[...], g_ref[...]) * (1.0 + m[1:2]) + m[0:1]
    p = jnp.dot(h.astype(BF16), w_ref[...], preferred_element_type=F32)
    off = 0
    for o_ref, wd in zip(o_refs, widths):
        o_ref[...] = p[:, off:off + wd]
        off += wd


def _norm_proj(st, x, mod, g, w, widths, tm):
    t, d = x.shape
    n = w.shape[1]
    return pl.pallas_call(
        functools.partial(_norm_proj_kernel, widths=widths),
        grid=(t // tm,),
        in_specs=[pl.BlockSpec((tm, d), lambda i: (i, 0)),
                  pl.BlockSpec((1, 6, d), lambda i: (st.mod_row(i, tm), 0, 0)),
                  pl.BlockSpec((1, d), lambda i: (0, 0)),
                  pl.BlockSpec((d, n), lambda i: (0, 0))],
        out_specs=[pl.BlockSpec((tm, wd), lambda i: (i, 0)) for wd in widths],
        out_shape=[jax.ShapeDtypeStruct((t, wd), F32) for wd in widths],
        compiler_params=_cparams(("arbitrary",)),
        name="norm_proj",
    )(x, mod, g, w)


def _ret_kernel(q_ref, k_ref, v_ref, g_ref, s0_ref, gam_ref, ng_ref, cos_ref, sin_ref,
                ya_ref, sout_ref, qs_ref, ks_ref, o_ref, *, seq_len, latent):
    nc = seq_len // CHUNK
    q = q_ref[...]
    k = k_ref[...] * (DK_A ** -0.5)
    if latent:
        lane = lax.broadcasted_iota(jnp.int32, q.shape, 1)
        first_half = (lane % DK_A) < (DK_A // 2)
        cos = cos_ref[...]
        sin = sin_ref[...]

        def rope(x):
            partner = jnp.where(first_half, pltpu.roll(x, LANES - DK_A // 2, 1),
                                pltpu.roll(x, DK_A // 2, 1))
            return x * cos + partner * sin

        q = rope(q)
        k = rope(k)
    qs_ref[...] = q
    ks_ref[...] = k

    row = lax.broadcasted_iota(jnp.int32, (CHUNK, CHUNK), 0).astype(F32)
    col = lax.broadcasted_iota(jnp.int32, (CHUNK, CHUNK), 1).astype(F32)
    diff = row - col
    for hh in range(2):
        lgf = _log_sigmoid(gam_ref[0, hh])
        lgb = _log_sigmoid(gam_ref[1, hh])
        lgf_c = lgf[:, :CHUNK]
        lgb_c = lgb[:, :CHUNK]
        dcomb = (jnp.where(diff >= 0, jnp.exp(lgf_c * jnp.maximum(diff, 0.0)), 0.0)
                 + jnp.where(diff <= 0, jnp.exp(lgb_c * jnp.maximum(-diff, 0.0)), 0.0))
        wend_f = jnp.exp(lgf_c * (CHUNK - 1.0 - row))
        wstart_f = jnp.exp(lgf_c * (row + 1.0))
        wend_b = jnp.exp(lgb_c * row)
        wstart_b = jnp.exp(lgb_c * (CHUNK - row))
        gch_f = jnp.exp(lgf * float(CHUNK))
        gch_b = jnp.exp(lgb * float(CHUNK))
        qsl = slice(hh * DK_A, (hh + 1) * DK_A)
        vsl = slice(hh * DV_A, (hh + 1) * DV_A)

        def fwd(n, s, qsl=qsl, vsl=vsl, dcomb=dcomb, wstart_f=wstart_f, wend_f=wend_f, gch_f=gch_f):
            r0 = pl.multiple_of(n * CHUNK, CHUNK)
            qn = qs_ref[pl.ds(r0, CHUNK), qsl]
            kn = ks_ref[pl.ds(r0, CHUNK), qsl]
            vn = v_ref[pl.ds(r0, CHUNK), vsl]
            a = _bdot_nt(qn, kn) * dcomb
            o_ref[pl.ds(r0, CHUNK), vsl] = _bdot(a, vn) + _bdot(qn * wstart_f, s)
            return s * gch_f + _bdot_tn(kn * wend_f, vn)

        sout_ref[0, 0, hh] = lax.fori_loop(0, nc, fwd, s0_ref[0, 0, hh])

        def bwd(i, s, qsl=qsl, vsl=vsl, wstart_b=wstart_b, wend_b=wend_b, gch_b=gch_b):
            r0 = pl.multiple_of((nc - 1 - i) * CHUNK, CHUNK)
            qn = qs_ref[pl.ds(r0, CHUNK), qsl]
            kn = ks_ref[pl.ds(r0, CHUNK), qsl]
            vn = v_ref[pl.ds(r0, CHUNK), vsl]
            o_ref[pl.ds(r0, CHUNK), vsl] += _bdot(qn * wstart_b, s)
            return s * gch_b + _bdot_tn(kn * wend_b, vn)

        sout_ref[0, 1, hh] = lax.fori_loop(0, nc, bwd, s0_ref[0, 1, hh])

    for hh in range(2):
        vsl = slice(hh * DV_A, (hh + 1) * DV_A)
        ya_ref[:, vsl] = _silu(g_ref[:, vsl]) * _rms(o_ref[:, vsl], ng_ref[hh])


def _retention(q, k, v, g, s0, gam, ng, cos_t, sin_t, nseq, seq_len, row_off, latent):
    w2 = 2 * DK_A
    v2 = 2 * DV_A
    return pl.pallas_call(
        functools.partial(_ret_kernel, seq_len=seq_len, latent=latent),
        grid=(nseq, H_A // 2),
        in_specs=[pl.BlockSpec((seq_len, w2), lambda b, p: (b + row_off, p)),
                  pl.BlockSpec((seq_len, w2), lambda b, p: (b + row_off, p)),
                  pl.BlockSpec((seq_len, v2), lambda b, p: (b + row_off, p)),
                  pl.BlockSpec((seq_len, v2), lambda b, p: (b + row_off, p)),
                  pl.BlockSpec((1, 2, 2, DK_A, DV_A), lambda b, p: (b, 0, p, 0, 0)),
                  pl.BlockSpec((2, 2, CHUNK, DV_A), lambda b, p: (0, p, 0, 0)),
                  pl.BlockSpec((2, 1, DV_A), lambda b, p: (p, 0, 0)),
                  pl.BlockSpec((seq_len, w2), lambda b, p: (0, 0)),
                  pl.BlockSpec((seq_len, w2), lambda b, p: (0, 0))],
        out_specs=[pl.BlockSpec((seq_len, v2), lambda b, p: (b, p)),
                   pl.BlockSpec((1, 2, 2, DK_A, DV_A), lambda b, p: (b, 0, p, 0, 0))],
        out_shape=[jax.ShapeDtypeStruct((nseq * seq_len, H_A * DV_A), F32),
                   jax.ShapeDtypeStruct((nseq, 2, H_A, DK_A, DV_A), F32)],
        scratch_shapes=[pltpu.VMEM((seq_len, w2), F32), pltpu.VMEM((seq_len, w2), F32),
                        pltpu.VMEM((seq_len, v2), F32)],
        compiler_params=_cparams(("arbitrary", "arbitrary")),
        name="retention",
    )(q, k, v, g, s0, gam, ng, cos_t, sin_t)


def _fill_padded(pad_ref, cur, prev, nxt, has_prev, has_next, halo):
    tile = cur.shape[0]
    pad_ref[0:halo, :] = jnp.where(has_prev, prev, 0.0)
    pad_ref[halo:halo + tile, :] = cur
    pad_ref[halo + tile:halo + tile + halo, :] = jnp.where(has_next, nxt, 0.0)


def _conv_taps(pad_ref, w_ref, ntaps, halo, tile):
    base = halo - ntaps // 2
    acc = w_ref[0:1, :] * pad_ref[base:base + tile, :]
    for kk in range(1, ntaps):
        acc = acc + w_ref[kk:kk + 1, :] * pad_ref[base + kk:base + kk + tile, :]
    return acc


def _halo_specs(width, col_map, halo):
    per = CONV_TILE // halo

    def cur(i, j):
        return (i, col_map(j))

    def prev(i, j):
        return (jnp.maximum(i * per - 1, 0), col_map(j))

    def make_next(nblk):
        def nxt(i, j):
            return (jnp.minimum((i + 1) * per, nblk - 1), col_map(j))
        return nxt

    return cur, prev, make_next


def _sconv_kernel(bg_ref, cg_ref, cgp_ref, cgn_ref, hb_ref, hbp_ref, hbn_ref, w_ref, o_ref, pad_ref, *, st):
    has_prev, has_next = st.halo_flags(pl.program_id(0), CONV_TILE)
    _fill_padded(pad_ref, cg_ref[...] * hb_ref[...], cgp_ref[...] * hbp_ref[...],
                 cgn_ref[...] * hbn_ref[...], has_prev, has_next, SUBLANES)
    o_ref[...] = bg_ref[...] * _conv_taps(pad_ref, w_ref, SC_K, SUBLANES, CONV_TILE)


def _short_gated_conv(st, bg, cg, hb, w):
    t, c = bg.shape
    halo = SUBLANES
    cur, prev, make_next = _halo_specs(c, lambda j: j, halo)
    nxt = make_next(t // halo)
    tile_spec = pl.BlockSpec((CONV_TILE, LANES), cur)
    prev_spec = pl.BlockSpec((halo, LANES), prev)
    next_spec = pl.BlockSpec((halo, LANES), nxt)
    return pl.pallas_call(
        functools.partial(_sconv_kernel, st=st),
        grid=(t // CONV_TILE, c // LANES),
        in_specs=[tile_spec, tile_spec, prev_spec, next_spec, tile_spec, prev_spec, next_spec,
                  pl.BlockSpec((SUBLANES, LANES), lambda i, j: (0, j))],
        out_specs=tile_spec,
        out_shape=jax.ShapeDtypeStruct((t, c), F32),
        scratch_shapes=[pltpu.VMEM((CONV_TILE + 2 * halo, LANES), F32)],
        compiler_params=_cparams(("arbitrary", "arbitrary")),
        name="short_gated_conv",
    )(bg, cg, cg, cg, hb, hb, hb, w)


def _qkv_conv_kernel(x_ref, xp_ref, xn_ref, w_ref, o_ref, pad_ref, *, st):
    has_prev, has_next = st.halo_flags(pl.program_id(0), CONV_TILE)
    _fill_padded(pad_ref, x_ref[...], xp_ref[...], xn_ref[...], has_prev, has_next, SUBLANES)
    s = _silu(_conv_taps(pad_ref, w_ref, QKV_K, SUBLANES, CONV_TILE))
    nrm = s * lax.rsqrt(jnp.sum(s * s, axis=-1, keepdims=True) + EPS)
    j = pl.program_id(1)
    o_ref[...] = jnp.where(j < H_C, nrm * (DK_C ** -0.5), jnp.where(j < 2 * H_C, nrm, s))


def _qkv_conv(st, qkv, w):
    t, c = qkv.shape
    halo = SUBLANES
    cur, prev, make_next = _halo_specs(c, lambda j: j, halo)
    nxt = make_next(t // halo)
    tile_spec = pl.BlockSpec((CONV_TILE, LANES), cur)
    return pl.pallas_call(
        functools.partial(_qkv_conv_kernel, st=st),
        grid=(t // CONV_TILE, c // LANES),
        in_specs=[tile_spec, pl.BlockSpec((halo, LANES), prev), pl.BlockSpec((halo, LANES), nxt),
                  pl.BlockSpec((SUBLANES, LANES), lambda i, j: (0, j))],
        out_specs=tile_spec,
        out_shape=jax.ShapeDtypeStruct((t, c), F32),
        scratch_shapes=[pltpu.VMEM((CONV_TILE + 2 * halo, LANES), F32)],
        compiler_params=_cparams(("arbitrary", "arbitrary")),
        name="qkv_conv",
    )(qkv, qkv, qkv, w)


CF_HALO = 16


def _conformer_kernel(ca_ref, cap_ref, can_ref, cg_ref, cgp_ref, cgn_ref, w_ref, b_ref, lg_ref, lb_ref,
                      o_ref, pad_ref, *, st):
    has_prev, has_next = st.halo_flags(pl.program_id(0), CONV_TILE)
    _fill_padded(pad_ref, ca_ref[...] * _sigmoid(cg_ref[...]), cap_ref[...] * _sigmoid(cgp_ref[...]),
                 can_ref[...] * _sigmoid(cgn_ref[...]), has_prev, has_next, CF_HALO)
    hc = _conv_taps(pad_ref, w_ref, CF_K, CF_HALO, CONV_TILE) + b_ref[...]
    mu = jnp.mean(hc, axis=-1, keepdims=True)
    xc = hc - mu
    y = xc * lax.rsqrt(jnp.mean(xc * xc, axis=-1, keepdims=True) + EPS) * lg_ref[...] + lb_ref[...]
    o_ref[...] = _silu(y)


def _conformer(st, glu, w, b, ln_g, ln_b):
    t, c2 = glu.shape
    c = c2 // 2
    nblk = t // CF_HALO
    per = CONV_TILE // CF_HALO
    vec = pl.BlockSpec((1, c), lambda i: (0, 0))
    return pl.pallas_call(
        functools.partial(_conformer_kernel, st=st),
        grid=(t // CONV_TILE,),
        in_specs=[pl.BlockSpec((CONV_TILE, c), lambda i: (i, 0)),
                  pl.BlockSpec((CF_HALO, c), lambda i: (jnp.maximum(i * per - 1, 0), 0)),
                  pl.BlockSpec((CF_HALO, c), lambda i: (jnp.minimum((i + 1) * per, nblk - 1), 0)),
                  pl.BlockSpec((CONV_TILE, c), lambda i: (i, 1)),
                  pl.BlockSpec((CF_HALO, c), lambda i: (jnp.maximum(i * per - 1, 0), 1)),
                  pl.BlockSpec((CF_HALO, c), lambda i: (jnp.minimum((i + 1) * per, nblk - 1), 1)),
                  pl.BlockSpec((2 * CF_HALO, c), lambda i: (0, 0)), vec, vec, vec],
        out_specs=pl.BlockSpec((CONV_TILE, c), lambda i: (i, 0)),
        out_shape=jax.ShapeDtypeStruct((t, c), F32),
        scratch_shapes=[pltpu.VMEM((CONV_TILE + 2 * CF_HALO, c), F32)],
        compiler_params=_cparams(("arbitrary",)),
        name="conformer_conv",
    )(glu, glu, glu, glu, glu, glu, w, b, ln_g, ln_b)


GDN_PACK = 4
GDN_GROUPS = 4


def _split_bf16(x):
    hi = x.astype(BF16)
    return hi, (x - hi.astype(F32)).astype(BF16)


def _packed_unit_inverses(groups):
    dot = functools.partial(jnp.dot, preferred_element_type=F32)
    cat = jnp.concatenate
    n = len(groups[0])
    c = CHUNK
    w = n * c
    eye = (lax.broadcasted_iota(jnp.int32, (c, c), 0) == lax.broadcasted_iota(jnp.int32, (c, c), 1)).astype(F32)
    lane_block = lax.broadcasted_iota(jnp.int32, (c, 2 * w), 1) % w // c

    def rhs_of(p):
        hi, lo = _split_bf16(p)
        both = cat([hi, lo], axis=1)
        return hi, lo, cat([jnp.where(lane_block == i, both, jnp.zeros_like(both)) for i in range(n)], axis=0)

    def product(m, r):
        return (r[:m, :w] + r[m:, :w]) + (r[:m, w:] + r[m:, w:])

    ps = [cat(a_list, axis=1) for a_list in groups]
    ts = [cat([eye - a for a in a_list], axis=1) for a_list in groups]
    for g, p in enumerate(ps):
        hi, lo, rhs = rhs_of(p)
        ps[g] = product(c, dot(cat([hi, lo], axis=0), rhs))
    steps = int(math.log2(c)) - 1
    for step in range(steps):
        for g in range(len(groups)):
            p_hi, p_lo, rhs = rhs_of(ps[g])
            t_hi, t_lo = _split_bf16(ts[g])
            if step < steps - 1:
                both = product(2 * c, dot(cat([t_hi, p_hi, t_lo, p_lo], axis=0), rhs))
                ts[g] = ts[g] + both[:c]
                ps[g] = both[c:]
            else:
                ts[g] = ts[g] + product(c, dot(cat([t_hi, t_lo], axis=0), rhs))
    return [[t[:, i * c:(i + 1) * c] for i in range(n)] for t in ts]


def _chunk_cumsum(x, reverse):
    row = lax.broadcasted_iota(jnp.int32, x.shape, 0)
    s = 1
    while s < CHUNK:
        if reverse:
            x = x + jnp.where(row < CHUNK - s, pltpu.roll(x, CHUNK - s, 0), 0.0)
        else:
            x = x + jnp.where(row >= s, pltpu.roll(x, s, 0), 0.0)
        s *= 2
    return x


def _gdn_kernel(q_ref, k_ref, v_ref, z_ref, ab_ref, alog_ref, dtb_ref, s0_ref, ng_ref,
                y_ref, sout_ref, g_ref, beta_ref, u_ref, w_ref, attn_ref, qg_ref, kg_ref, egl_ref, o_ref,
                *, seq_len):
    nc = seq_len // CHUNK
    h = pl.program_id(1)
    row = lax.broadcasted_iota(jnp.int32, (CHUNK, CHUNK), 0)
    col = lax.broadcasted_iota(jnp.int32, (CHUNK, CHUNK), 1)
    lane = lax.broadcasted_iota(jnp.int32, (CHUNK, LANES), 1)
    alog = alog_ref[...]
    dtb = dtb_ref[...]

    def gates(n, carry):
        rows = pl.ds(pl.multiple_of(n * CHUNK, CHUNK), CHUNK)
        ab = ab_ref[rows, :]
        g_all = -jnp.exp(alog) * _softplus(ab + dtb)
        b_all = _sigmoid(ab)
        for d in range(2):
            gsel = jnp.sum(jnp.where(lane == d * H_C + h, g_all, 0.0), axis=-1, keepdims=True)
            bsel = jnp.sum(jnp.where(lane == 2 * H_C + d * H_C + h, b_all, 0.0), axis=-1, keepdims=True)
            g_ref[d, rows, :] = jnp.broadcast_to(gsel, (CHUNK, LANES))
            beta_ref[d, rows, :] = jnp.broadcast_to(bsel, (CHUNK, LANES))
        return carry

    lax.fori_loop(0, nc, gates, 0)

    n_groups = min(GDN_GROUPS, nc // (GDN_PACK // 2))

    def precompute(it, carry):
        groups = [prepare_group(it * n_groups + grp) for grp in range(n_groups)]
        inverses = _packed_unit_inverses([[sysm[0] for sysm in systems] for systems in groups])
        for t_invs, systems in zip(inverses, groups):
            for t_inv, (_, rhs, d, rows) in zip(t_invs, systems):
                uw = _bdot(t_inv, rhs)
                u_ref[d, rows, :] = uw[:, :DV_C]
                w_ref[d, rows, :] = uw[:, DV_C:].astype(BF16)
        return carry

    def prepare_group(pair):
        systems = []
        for jc in range(GDN_PACK // 2):
            n = pair * (GDN_PACK // 2) + jc
            rows = pl.ds(pl.multiple_of(n * CHUNK, CHUNK), CHUNK)
            qc = q_ref[rows, :]
            kc = k_ref[rows, :]
            vc = v_ref[rows, :]
            gbs = [_chunk_cumsum(g_ref[d, rows, :], d == 1) for d in range(2)]
            bbs = [beta_ref[d, rows, :] for d in range(2)]
            kbs = [kc * bb for bb in bbs]
            prod = _bdot_nt(jnp.concatenate(kbs + [qc], axis=0), kc)
            for d in range(2):
                incl = (row >= col) if d == 0 else (row <= col)
                strict = (row > col) if d == 0 else (row < col)
                last = CHUNK - 1 if d == 0 else 0
                gb = gbs[d]
                gr = gb[:, :CHUNK].T
                gl = gb[last:last + 1, :]
                decay = jnp.where(incl, jnp.exp(jnp.where(incl, gb[:, :CHUNK] - gr, 0.0)), 0.0)
                eg = jnp.exp(gb)
                a_low = jnp.where(strict, prod[d * CHUNK:(d + 1) * CHUNK] * decay, 0.0)
                attn_ref[d, rows, :] = jnp.where(incl, prod[2 * CHUNK:] * decay, 0.0).astype(BF16)
                qg_ref[d, rows, :] = (qc * eg).astype(BF16)
                kg_ref[d, rows, :] = kc * jnp.exp(gl - gb)
                egl_ref[d, pl.ds(n, 1), :] = jnp.exp(gl)
                systems.append((a_low, jnp.concatenate([vc * bbs[d], kbs[d] * eg], axis=1), d, rows))
        return systems

    lax.fori_loop(0, nc // (n_groups * GDN_PACK // 2), precompute, 0)

    o_ref[...] = jnp.zeros_like(o_ref)

    def scan(i, carry):
        out = []
        for d, s in enumerate(carry):
            n = i if d == 0 else nc - 1 - i
            rows = pl.ds(pl.multiple_of(n * CHUNK, CHUNK), CHUNK)
            v_new = u_ref[d, rows, :] - _bdot(w_ref[d, rows, :], s)
            o_ref[rows, :] += _bdot(qg_ref[d, rows, :], s) + _bdot(attn_ref[d, rows, :], v_new)
            out.append(s * egl_ref[d, pl.ds(n, 1), :] + _bdot_tn(kg_ref[d, rows, :], v_new))
        return tuple(out)

    s_f, s_b = lax.fori_loop(0, nc, scan, (s0_ref[0, 0, 0], s0_ref[0, 1, 0]))
    sout_ref[0, 0, 0] = s_f
    sout_ref[0, 1, 0] = s_b
    y_ref[...] = _rms(o_ref[...], ng_ref[...]) * _silu(z_ref[...])


def _gdn(qkv_n, z, ab, alog, dtb, s0, ng, nseq, seq_len, row_off):
    blk = lambda off: pl.BlockSpec((seq_len, LANES), lambda b, h: (b + row_off, h + off))
    vec = pl.BlockSpec((1, LANES), lambda b, h: (0, 0))
    st_spec = pl.BlockSpec((1, 2, 1, DK_C, DV_C), lambda b, h: (b, 0, h, 0, 0))
    nc = seq_len // CHUNK
    both = lambda width, dtype: pltpu.VMEM((2, seq_len, width), dtype)
    return pl.pallas_call(
        functools.partial(_gdn_kernel, seq_len=seq_len),
        grid=(nseq, H_C),
        in_specs=[blk(0), blk(H_C), blk(2 * H_C), blk(0),
                  pl.BlockSpec((seq_len, LANES), lambda b, h: (b + row_off, 0)),
                  vec, vec, st_spec, vec],
        out_specs=[pl.BlockSpec((seq_len, LANES), lambda b, h: (b, h)), st_spec],
        out_shape=[jax.ShapeDtypeStruct((nseq * seq_len, H_C * DV_C), F32),
                   jax.ShapeDtypeStruct((nseq, 2, H_C, DK_C, DV_C), F32)],
        scratch_shapes=[both(LANES, F32), both(LANES, F32), both(DV_C, F32), both(DK_C, BF16),
                        both(CHUNK, BF16), both(DK_C, BF16), both(DK_C, F32),
                        pltpu.VMEM((2, max(nc, SUBLANES), LANES), F32), pltpu.VMEM((seq_len, DV_C), F32)],
        compiler_params=_cparams(("arbitrary", "arbitrary")),
        name="gated_deltanet",
    )(qkv_n, qkv_n, qkv_n, z, ab, alog, dtb, s0, ng)


def _out_proj_kernel(ya_ref, yb_ref, x_ref, mod_ref, g_ref, wo_ref, wq_ref, x1_ref, h2_ref, qp_ref):
    half = ya_ref.shape[1]
    m = mod_ref[0]
    y = (jnp.dot(ya_ref[...].astype(BF16), wo_ref[0:half, :], preferred_element_type=F32)
         + jnp.dot(yb_ref[...].astype(BF16), wo_ref[half:, :], preferred_element_type=F32))
    x1 = x_ref[...] + m[2:3] * y
    x1_ref[...] = x1
    h2 = (_rms(x1, g_ref[...]) * (1.0 + m[4:5]) + m[3:4]).astype(BF16)
    h2_ref[...] = h2
    qp_ref[...] = jnp.dot(h2, wq_ref[...], preferred_element_type=F32)


def _out_proj(st, ya, yb, x, mod, g, wo, wq, tm):
    t, d = x.shape
    half = ya.shape[1]
    nq = wq.shape[1]
    return pl.pallas_call(
        _out_proj_kernel,
        grid=(t // tm,),
        in_specs=[pl.BlockSpec((tm, half), lambda i: (i, 0)),
                  pl.BlockSpec((tm, half), lambda i: (i, 0)),
                  pl.BlockSpec((tm, d), lambda i: (i, 0)),
                  pl.BlockSpec((1, 6, d), lambda i: (st.mod_row(i, tm), 0, 0)),
                  pl.BlockSpec((1, d), lambda i: (0, 0)),
                  pl.BlockSpec((2 * half, d), lambda i: (0, 0)),
                  pl.BlockSpec((d, nq), lambda i: (0, 0))],
        out_specs=[pl.BlockSpec((tm, d), lambda i: (i, 0)),
                   pl.BlockSpec((tm, d), lambda i: (i, 0)),
                   pl.BlockSpec((tm, nq), lambda i: (i, 0))],
        out_shape=[jax.ShapeDtypeStruct((t, d), F32), jax.ShapeDtypeStruct((t, d), BF16),
                   jax.ShapeDtypeStruct((t, nq), F32)],
        compiler_params=_cparams(("arbitrary",)),
        name="out_proj",
    )(ya, yb, x, mod, g, wo, wq)


def _candidate_rows():
    groups = ([(0, k2) for k2 in range(16)], [(1, k2) for k2 in range(8)], [(2, k2) for k2 in range(8)],
              [(3, k2) for k2 in range(8)], [(k1, 0) for k1 in range(16)], [(k1, 1) for k1 in range(8)],
              [(k1, 2) for k1 in range(8)])
    rows, seen = [], set()
    for grp in groups:
        for k1, k2 in grp:
            ok = (k1 + 1) * (k2 + 1) <= TOPK and (k1, k2) not in seen
            if ok:
                seen.add((k1, k2))
            rows.append(float(k1 * TOPK + k2) if ok else CAND_INVALID)
    return rows


CAND_INVALID = float(TOPK * TOPK)


def _pack_candidates(a1, a2, op):
    return jnp.concatenate([op(a1[0:1], a2), op(a1[1:2], a2[0:8]), op(a1[2:3], a2[0:8]),
                            op(a1[3:4], a2[0:8]), op(a1, a2[0:1]), op(a1[0:8], a2[1:2]),
                            op(a1[0:8], a2[2:3])], axis=0)


def _top16(s, vals_ref, exact):
    n = s.shape[0]
    iota = lax.broadcasted_iota(jnp.int32, s.shape, 0).astype(F32)
    rank = jnp.full(s.shape, float(TOPK), F32)
    for kk in range(TOPK):
        m = jnp.max(s, axis=0, keepdims=True)
        if exact:
            idx = jnp.min(jnp.where(s == m, iota, float(n)), axis=0, keepdims=True)
            hit = iota == idx
        else:
            hit = s == m
        rank = jnp.where(hit, float(kk), rank)
        vals_ref[kk:kk + 1, :] = m
        s = jnp.where(hit, NEG_INF, s)
    return rank


def _peer_select_kernel(q_ref, keys_ref, cflat_ref, e1_ref, cnt_ref, e2_ref, r2_ref, v1_ref, v2_ref, *, tt):
    nk = N_KEYS
    cflat = cflat_ref[...]
    valid = cflat < CAND_INVALID
    row16 = lax.broadcasted_iota(jnp.int32, (TOPK, LANES), 0)

    def strip_body(si, carry):
        t0 = pl.multiple_of(si * LANES, LANES)
        tsl = pl.ds(t0, LANES)
        s1 = _bdot_nt(keys_ref[0, 0], q_ref[tsl, 0:nk])
        s2 = _bdot_nt(keys_ref[0, 1], q_ref[tsl, nk:2 * nk])

        def compute(exact):
            rank1 = _top16(s1, v1_ref, exact)
            rank2 = _top16(s2, v2_ref, exact)
            v1 = v1_ref[...]
            v2 = v2_ref[...]
            cand = jnp.where(valid, _pack_candidates(v1, v2, jnp.add), NEG_INF)
            prod = _pack_candidates(jnp.exp(v1 - v1[0:1]), jnp.exp(v2 - v2[0:1]), jnp.multiply)
            sel = jnp.zeros(cand.shape, F32)
            for _ in range(TOPK):
                m = jnp.max(cand, axis=0, keepdims=True)
                if exact:
                    idx = jnp.min(jnp.where(cand == m, cflat, CAND_INVALID), axis=0, keepdims=True)
                    hit = cflat == idx
                else:
                    hit = cand == m
                sel = jnp.where(hit, 1.0, sel)
                cand = jnp.where(hit, NEG_INF, cand)
            zsum = jnp.sum(sel * prod, axis=0, keepdims=True)
            c_all = sel[40:56] + jnp.concatenate([sel[56:64] + sel[64:72], jnp.zeros((8, LANES), F32)], axis=0)
            for k1, (lo, hi) in enumerate(((0, 16), (16, 24), (24, 32), (32, 40))):
                c_all = jnp.where(row16 == k1, jnp.sum(sel[lo:hi], axis=0, keepdims=True), c_all)
            cnt = jnp.zeros(rank1.shape, F32)
            for k1 in range(TOPK):
                cnt = jnp.where(rank1 == float(k1), c_all[k1:k1 + 1], cnt)
            in1 = rank1 < float(TOPK)
            in2 = rank2 < float(TOPK)
            e1_ref[0, :, tsl] = jnp.where(in1, jnp.exp(s1 - v1[0:1]), 0.0) / zsum
            cnt_ref[0, :, tsl] = cnt
            e2_ref[:, tsl] = jnp.where(in2, jnp.exp(s2 - v2[0:1]), 0.0).astype(e2_ref.dtype)
            r2_ref[:, tsl] = rank2.astype(r2_ref.dtype)
            n1 = jnp.sum(jnp.where(in1, 1.0, 0.0), axis=0, keepdims=True)
            n2 = jnp.sum(jnp.where(in2, 1.0, 0.0), axis=0, keepdims=True)
            nc = jnp.sum(sel, axis=0, keepdims=True)
            want = float(TOPK)
            return jnp.abs(n1 - want) + jnp.abs(n2 - want) + jnp.abs(nc - want)

        tied = compute(False)

        @pl.when(jnp.max(tied) > 0.0)
        def _():
            compute(True)

        return carry

    lax.fori_loop(0, tt // LANES, strip_body, 0)


def _peer_select(qp, keys, tt):
    t = qp.shape[0]
    rows = _candidate_rows()
    cflat = jnp.broadcast_to(jnp.asarray(rows, F32)[:, None], (len(rows), LANES))
    out_spec = pl.BlockSpec((1, N_KEYS, tt), lambda i, h: (h, 0, i))
    f32_sds = jax.ShapeDtypeStruct((H_P, N_KEYS, t), F32)
    flat_spec = pl.BlockSpec((N_KEYS, tt), lambda i, h: (h, i))
    flat_sds = jax.ShapeDtypeStruct((H_P * N_KEYS, t), F32)
    return pl.pallas_call(
        functools.partial(_peer_select_kernel, tt=tt),
        grid=(t // tt, H_P),
        in_specs=[pl.BlockSpec((tt, 2 * N_KEYS), lambda i, h: (i, h)),
                  pl.BlockSpec((1, 2, N_KEYS, N_KEYS), lambda i, h: (h, 0, 0, 0)),
                  pl.BlockSpec(cflat.shape, lambda i, h: (0, 0))],
        out_specs=[out_spec, out_spec, flat_spec, flat_spec],
        out_shape=[f32_sds, f32_sds, flat_sds, flat_sds],
        scratch_shapes=[pltpu.VMEM((TOPK, LANES), F32), pltpu.VMEM((TOPK, LANES), F32)],
        compiler_params=_cparams(("arbitrary", "arbitrary")),
        name="peer_select",
    )(qp, keys, cflat)


def _peer_dense_kernel(h_ref, u_ref, vt_ref, e1_ref, cnt_ref, e2_ref, r2_ref, x_ref, mod_ref, o_ref,
                       acc_ref, s_ref, p_ref, e2b_ref, r2b_ref, *, na):
    j = pl.program_id(1)
    tt = s_ref.shape[1]

    @pl.when(j == 0)
    def _():
        acc_ref[...] = jnp.zeros_like(acc_ref)
        e2b_ref[...] = e2_ref[...].astype(BF16)
        r2b_ref[...] = r2_ref[...].astype(BF16)

    s_ref[...] = lax.dot_general(u_ref[...], h_ref[...], (((1,), (1,)), ((), ())),
                                 preferred_element_type=F32)

    for aa in range(na):
        rows = slice(aa * N_KEYS, (aa + 1) * N_KEYS)
        for tg in range(tt // LANES):
            tsl = slice(tg * LANES, (tg + 1) * LANES)
            gate = None
            for hh in range(H_P):
                cnt_row = cnt_ref[hh, aa:aa + 1, tsl].astype(BF16)
                e1_row = e1_ref[hh, aa:aa + 1, tsl].astype(BF16)
                hsl = slice(hh * N_KEYS, (hh + 1) * N_KEYS)
                term = jnp.where(r2b_ref[hsl, tsl] < cnt_row, e2b_ref[hsl, tsl], 0.0) * e1_row
                gate = term if gate is None else gate + term
            p_ref[rows, tsl] = gate * _gelu_tanh(s_ref[rows, tsl]).astype(BF16)
    acc_ref[...] += jnp.dot(vt_ref[...], p_ref[...], preferred_element_type=F32)

    @pl.when(j == pl.num_programs(1) - 1)
    def _():
        o_ref[...] = x_ref[...] + mod_ref[0][5:6] * acc_ref[...].T


def _peer_dense(st, h2, u, vt, e1, cnt, e2, r2, x, mod, tt, et):
    t, d = x.shape
    n_exp = u.shape[0]
    na = et // N_KEYS
    row_spec = pl.BlockSpec((H_P, na, tt), lambda i, j: (0, j, i))
    full_spec = pl.BlockSpec((H_P * N_KEYS, tt), lambda i, j: (0, i))
    return pl.pallas_call(
        functools.partial(_peer_dense_kernel, na=na),
        grid=(t // tt, n_exp // et),
        in_specs=[pl.BlockSpec((tt, d), lambda i, j: (i, 0)),
                  pl.BlockSpec((et, d), lambda i, j: (j, 0)),
                  pl.BlockSpec((d, et), lambda i, j: (0, j)),
                  row_spec, row_spec, full_spec, full_spec,
                  pl.BlockSpec((tt, d), lambda i, j: (i, 0)),
                  pl.BlockSpec((1, 6, d), lambda i, j: (st.mod_row(i, tt), 0, 0))],
        out_specs=pl.BlockSpec((tt, d), lambda i, j: (i, 0)),
        out_shape=jax.ShapeDtypeStruct((t, d), F32),
        scratch_shapes=[pltpu.VMEM((d, tt), F32), pltpu.VMEM((et, tt), F32), pltpu.VMEM((et, tt), BF16),
                        pltpu.VMEM((H_P * N_KEYS, tt), BF16), pltpu.VMEM((H_P * N_KEYS, tt), BF16)],
        compiler_params=_cparams(("arbitrary", "arbitrary")),
        name="peer_dense",
    )(h2, u, vt, e1, cnt, e2, r2, x, mod)


def _final_norm_kernel(x_ref, g_ref, o_ref):
    o_ref[...] = _rms(x_ref[...], g_ref[...])


def _final_norm(x, g, tm):
    t, d = x.shape
    return pl.pallas_call(
        _final_norm_kernel,
        grid=(t // tm,),
        in_specs=[pl.BlockSpec((tm, d), lambda i: (i, 0)), pl.BlockSpec((1, d), lambda i: (0, 0))],
        out_specs=pl.BlockSpec((tm, d), lambda i: (i, 0)),
        out_shape=jax.ShapeDtypeStruct((t, d), F32),
        compiler_params=_cparams(("arbitrary",)),
        name="final_norm",
    )(x, g)


def _rope_tables(seq_len):
    pos = jnp.arange(seq_len)
    rowp = (pos // GRID_W).astype(F32)
    colp = (pos % GRID_W).astype(F32)
    nf = DK_A // 4
    freqs = ROPE_BASE ** (-jnp.arange(nf, dtype=F32) / nf)
    ang = jnp.concatenate([rowp[:, None] * freqs, colp[:, None] * freqs], axis=-1)
    cos = jnp.cos(ang)
    sin = jnp.sin(ang)
    cos_t = jnp.tile(cos, (1, 4))
    sin_t = jnp.tile(jnp.concatenate([-sin, sin], axis=-1), (1, 2))
    return cos_t, sin_t


def _pad_rows(w, rows):
    return jnp.concatenate([w, jnp.zeros((rows - w.shape[0], w.shape[1]), w.dtype)], axis=0)


def _lane_row(vals):
    flat = vals.reshape(-1).astype(F32)
    return jnp.concatenate([flat, jnp.zeros((LANES - flat.shape[0],), F32)])[None, :]


def kernel(x_prompt, x_sample, state_ret, state_gdn, c, c_ctx, ada_w, ada_b, norm_mix_g, norm_ffn_g,
           final_norm_g, ev_w_in, ev_w_out, ret_gamma_logit, ret_norm_g, sc_conv_w, od_w_in, od_w_out,
           gdn_conv_w, gdn_a_log, gdn_dt_bias, gdn_norm_g, cf_dw_w, cf_dw_b, cf_ln_g, cf_ln_b,
           peer_wq, peer_keys, peer_u, peer_v):
    bp, lp, d = x_prompt.shape
    bs, ls, _ = x_sample.shape
    depth = ada_w.shape[0]
    st = _Streams(bp, lp, bs, ls)
    w_a = H_A * DV_A
    w_b = d - w_a
    w_c = H_C * DV_C
    w_d = d - w_c
    tm = 512
    peer_tt = 512
    peer_et = 1024
    select_tt = 512

    x = jnp.concatenate([x_prompt.reshape(st.tp, d), x_sample.reshape(st.ts, d)], axis=0)
    cvec = jnp.concatenate([c_ctx[None, :], c, jnp.zeros((N_MOD_ROWS - 1 - bs, d), F32)], axis=0)
    mods = _modulation(cvec, ada_w, ada_b).reshape(depth, N_MOD_ROWS, 6, d)
    cos_t, sin_t = _rope_tables(ls)
    zero_ret = jnp.zeros((bp, 2, H_A, DK_A, DV_A), F32)
    zero_gdn = jnp.zeros((bp, 2, H_C, DK_C, DV_C), F32)

    ret_new, gdn_new = [], []
    for l in range(depth):
        i = l // 2
        mod = mods[l]
        g1 = norm_mix_g[l][None, :]
        if l % 2 == 0:
            widths = (H_A * DK_A, H_A * DK_A, w_a, w_a, w_b, w_b, w_b)
            q, k, v, g, bg, cg, hb = _norm_proj(st, x, mod, g1, ev_w_in[i].astype(BF16), widths, tm)
            gam = jnp.broadcast_to(ret_gamma_logit[i][:, :, None, None], (2, H_A, CHUNK, DV_A))
            ng = ret_norm_g[i][:, None, :]
            ya_p, s_new = _retention(q, k, v, g, zero_ret, gam, ng, cos_t[:lp], sin_t[:lp],
                                     bp, lp, 0, False)
            ya_s, _ = _retention(q, k, v, g, state_ret[:, i], gam, ng, cos_t, sin_t,
                                 bs, ls, st.tp // ls, True)
            ret_new.append(s_new)
            ya = jnp.concatenate([ya_p, ya_s], axis=0)
            yb = _short_gated_conv(st, bg, cg, hb, _pad_rows(sc_conv_w[i], SUBLANES))
            w_out = ev_w_out[i]
        else:
            n_gate = 2 * 2 * H_C
            w_in = od_w_in[i]
            o_ab = 4 * w_c
            w_main = jnp.concatenate([w_in[:, :o_ab], w_in[:, o_ab + n_gate:],
                                      w_in[:, o_ab:o_ab + n_gate],
                                      jnp.zeros((d, LANES - n_gate), F32)], axis=1).astype(BF16)
            widths = (3 * w_c, w_c, 2 * w_d, LANES)
            qkv, z, glu, ab = _norm_proj(st, x, mod, g1, w_main, widths, tm)
            qkv_n = _qkv_conv(st, qkv, _pad_rows(gdn_conv_w[i], SUBLANES))
            alog = _lane_row(gdn_a_log[i])
            dtb = _lane_row(gdn_dt_bias[i])
            ng = gdn_norm_g[i][None, :]
            yc_p, s_new = _gdn(qkv_n, z, ab, alog, dtb, zero_gdn, ng, bp, lp, 0)
            yc_s, _ = _gdn(qkv_n, z, ab, alog, dtb, state_gdn[:, i], ng, bs, ls, st.tp // ls)
            gdn_new.append(s_new)
            ya = jnp.concatenate([yc_p, yc_s], axis=0)
            yb = _conformer(st, glu, _pad_rows(cf_dw_w[i], 2 * CF_HALO), cf_dw_b[i][None, :],
                            cf_ln_g[i][None, :], cf_ln_b[i][None, :])
            w_out = od_w_out[i]
        x, h2, qp = _out_proj(st, ya, yb, x, mod, norm_ffn_g[l][None, :], w_out.astype(BF16),
                              peer_wq[l].astype(BF16), tm)
        e1, cnt, e2, r2 = _peer_select(qp, peer_keys[l].astype(BF16), select_tt)
        x = _peer_dense(st, h2, peer_u[l].astype(BF16), peer_v[l].astype(BF16).T, e1, cnt, e2, r2,
                        x, mod, peer_tt, peer_et)

    y = _final_norm(x, final_norm_g[None, :], tm)
    y_prompt = y[:st.tp].reshape(bp, lp, d)
    y_sample = y[st.tp:].reshape(bs, ls, d)
    new_state_ret = jnp.stack(ret_new, axis=1).astype(x_prompt.dtype)
    new_state_gdn = jnp.stack(gdn_new, axis=1).astype(x_prompt.dtype)
    return (y_prompt, y_sample, new_state_ret, new_state_gdn)
```

```python
import functools
import math

import jax
import jax.numpy as jnp
from jax import lax
from jax.experimental import pallas as pl
from jax.experimental.pallas import tpu as pltpu

F32 = jnp.float32
BF16 = jnp.bfloat16
HIGHEST = lax.Precision.HIGHEST

EPS = 1e-6
CHUNK = 64
GRID_W = 64
ROPE_BASE = 10000.0
H_A, DK_A, DV_A = 4, 64, 128
H_C, DK_C, DV_C = 4, 128, 128
SC_K, QKV_K, CF_K = 3, 3, 31
N_KEYS, H_P, TOPK = 128, 8, 16
N_MOD_ROWS = 16
LANES = 128
SUBLANES = 8
VMEM_LIMIT = 56 * 1024 * 1024
CONV_TILE = 256
NEG_INF = float("-inf")


def _cparams(sem):
    return pltpu.CompilerParams(dimension_semantics=sem, vmem_limit_bytes=VMEM_LIMIT)


def _bdot(a, b):
    return jnp.dot(a.astype(BF16), b.astype(BF16), preferred_element_type=F32)


def _bdot_nt(a, b):
    return lax.dot_general(a.astype(BF16), b.astype(BF16), (((1,), (1,)), ((), ())),
                           preferred_element_type=F32)


def _bdot_tn(a, b):
    return lax.dot_general(a.astype(BF16), b.astype(BF16), (((0,), (0,)), ((), ())),
                           preferred_element_type=F32)


def _hdot(a, b):
    return jnp.dot(a, b, precision=HIGHEST, preferred_element_type=F32)


def _sigmoid(x):
    return 1.0 / (1.0 + jnp.exp(-x))


def _silu(x):
    return x * _sigmoid(x)


def _softplus(x):
    return jnp.maximum(x, 0.0) + jnp.log1p(jnp.exp(-jnp.abs(x)))


def _log_sigmoid(x):
    return -_softplus(-x)


def _gelu_tanh(x):
    c = math.sqrt(2.0 / math.pi)
    return 0.5 * x * (1.0 + jnp.tanh(c * (x + 0.044715 * (x * x * x))))


def _rms(x, g):
    return x * lax.rsqrt(jnp.mean(x * x, axis=-1, keepdims=True) + EPS) * g


class _Streams:
    def __init__(self, bp, lp, bs, ls):
        self.bp, self.lp, self.bs, self.ls = bp, lp, bs, ls
        self.tp, self.ts = bp * lp, bs * ls
        self.t = self.tp + self.ts

    def mod_row(self, i, tile):
        tiles_p = self.tp // tile
        per_seq = self.ls // tile
        return jnp.where(i < tiles_p, 0, 1 + (i - tiles_p) // per_seq)

    def halo_flags(self, i, tile):
        tiles_p = self.tp // tile
        per_p = self.lp // tile
        per_s = self.ls // tile
        in_p = i < tiles_p
        jp = i % per_p
        js = (i - tiles_p) % per_s
        has_prev = jnp.where(in_p, jp > 0, js > 0)
        has_next = jnp.where(in_p, jp < per_p - 1, js < per_s - 1)
        return has_prev, has_next


def _mod_kernel(c_ref, w_ref, b_ref, o_ref):
    s = _silu(c_ref[...])
    o_ref[0] = _hdot(s, w_ref[0]) + b_ref[0]


def _modulation(cvec, ada_w, ada_b):
    depth, d, d6 = ada_w.shape
    nj = d6 // d
    return pl.pallas_call(
        _mod_kernel,
        grid=(depth, nj),
        in_specs=[pl.BlockSpec((N_MOD_ROWS, d), lambda l, j: (0, 0)),
                  pl.BlockSpec((1, d, d), lambda l, j: (l, 0, j)),
                  pl.BlockSpec((1, 1, d), lambda l, j: (l, 0, j))],
        out_specs=pl.BlockSpec((1, N_MOD_ROWS, d), lambda l, j: (l, 0, j)),
        out_shape=jax.ShapeDtypeStruct((depth, N_MOD_ROWS, d6), F32),
        compiler_params=_cparams(("arbitrary", "arbitrary")),
        name="modulation",
    )(cvec, ada_w, ada_b.reshape(depth, 1, d6))


def _norm_proj_kernel(x_ref, mod_ref, g_ref, w_ref, *o_refs, widths):
    m = mod_ref[0]
    h = _rms(x_ref[...], g_ref[...]) * (1.0 + m[1:2]) + m[0:1]
    p = jnp.dot(h.astype(BF16), w_ref[...], preferred_element_type=F32)
    off = 0
    for o_ref, wd in zip(o_refs, widths):
        o_ref[...] = p[:, off:off + wd]
        off += wd


def _norm_proj(st, x, mod, g, w, widths, tm):
    t, d = x.shape
    n = w.shape[1]
    return pl.pallas_call(
        functools.partial(_norm_proj_kernel, widths=widths),
        grid=(t // tm,),
        in_specs=[pl.BlockSpec((tm, d), lambda i: (i, 0)),
                  pl.BlockSpec((1, 6, d), lambda i: (st.mod_row(i, tm), 0, 0)),
                  pl.BlockSpec((1, d), lambda i: (0, 0)),
                  pl.BlockSpec((d, n), lambda i: (0, 0))],
        out_specs=[pl.BlockSpec((tm, wd), lambda i: (i, 0)) for wd in widths],
        out_shape=[jax.ShapeDtypeStruct((t, wd), F32) for wd in widths],
        compiler_params=_cparams(("arbitrary",)),
        name="norm_proj",
    )(x, mod, g, w)


def _ret_kernel(q_ref, k_ref, v_ref, g_ref, s0_ref, gam_ref, ng_ref, cos_ref, sin_ref,
                ya_ref, sout_ref, qs_ref, ks_ref, o_ref, *, seq_len, latent):
    nc = seq_len // CHUNK
    q = q_ref[...]
    k = k_ref[...] * (DK_A ** -0.5)
    if latent:
        lane = lax.broadcasted_iota(jnp.int32, q.shape, 1)
        first_half = (lane % DK_A) < (DK_A // 2)
        cos = cos_ref[...]
        sin = sin_ref[...]

        def rope(x):
            partner = jnp.where(first_half, pltpu.roll(x, LANES - DK_A // 2, 1),
                                pltpu.roll(x, DK_A // 2, 1))
            return x * cos + partner * sin

        q = rope(q)
        k = rope(k)
    qs_ref[...] = q
    ks_ref[...] = k

    row = lax.broadcasted_iota(jnp.int32, (CHUNK, CHUNK), 0).astype(F32)
    col = lax.broadcasted_iota(jnp.int32, (CHUNK, CHUNK), 1).astype(F32)
    diff = row - col
    for hh in range(2):
        lgf = _log_sigmoid(gam_ref[0, hh])
        lgb = _log_sigmoid(gam_ref[1, hh])
        lgf_c = lgf[:, :CHUNK]
        lgb_c = lgb[:, :CHUNK]
        dcomb = (jnp.where(diff >= 0, jnp.exp(lgf_c * jnp.maximum(diff, 0.0)), 0.0)
                 + jnp.where(diff <= 0, jnp.exp(lgb_c * jnp.maximum(-diff, 0.0)), 0.0))
        wend_f = jnp.exp(lgf_c * (CHUNK - 1.0 - row))
        wstart_f = jnp.exp(lgf_c * (row + 1.0))
        wend_b = jnp.exp(lgb_c * row)
        wstart_b = jnp.exp(lgb_c * (CHUNK - row))
        gch_f = jnp.exp(lgf * float(CHUNK))
        gch_b = jnp.exp(lgb * float(CHUNK))
        qsl = slice(hh * DK_A, (hh + 1) * DK_A)
        vsl = slice(hh * DV_A, (hh + 1) * DV_A)

        def fwd(n, s, qsl=qsl, vsl=vsl, dcomb=dcomb, wstart_f=wstart_f, wend_f=wend_f, gch_f=gch_f):
            r0 = pl.multiple_of(n * CHUNK, CHUNK)
            qn = qs_ref[pl.ds(r0, CHUNK), qsl]
            kn = ks_ref[pl.ds(r0, CHUNK), qsl]
            vn = v_ref[pl.ds(r0, CHUNK), vsl]
            a = _bdot_nt(qn, kn) * dcomb
            o_ref[pl.ds(r0, CHUNK), vsl] = _bdot(a, vn) + _bdot(qn * wstart_f, s)
            return s * gch_f + _bdot_tn(kn * wend_f, vn)

        sout_ref[0, 0, hh] = lax.fori_loop(0, nc, fwd, s0_ref[0, 0, hh])

        def bwd(i, s, qsl=qsl, vsl=vsl, wstart_b=wstart_b, wend_b=wend_b, gch_b=gch_b):
            r0 = pl.multiple_of((nc - 1 - i) * CHUNK, CHUNK)
            qn = qs_ref[pl.ds(r0, CHUNK), qsl]
            kn = ks_ref[pl.ds(r0, CHUNK), qsl]
            vn = v_ref[pl.ds(r0, CHUNK), vsl]
            o_ref[pl.ds(r0, CHUNK), vsl] += _bdot(qn * wstart_b, s)
            return s * gch_b + _bdot_tn(kn * wend_b, vn)

        sout_ref[0, 1, hh] = lax.fori_loop(0, nc, bwd, s0_ref[0, 1, hh])

    for hh in range(2):
        vsl = slice(hh * DV_A, (hh + 1) * DV_A)
        ya_ref[:, vsl] = _silu(g_ref[:, vsl]) * _rms(o_ref[:, vsl], ng_ref[hh])


def _retention(q, k, v, g, s0, gam, ng, cos_t, sin_t, nseq, seq_len, row_off, latent):
    w2 = 2 * DK_A
    v2 = 2 * DV_A
    return pl.pallas_call(
        functools.partial(_ret_kernel, seq_len=seq_len, latent=latent),
        grid=(nseq, H_A // 2),
        in_specs=[pl.BlockSpec((seq_len, w2), lambda b, p: (b + row_off, p)),
                  pl.BlockSpec((seq_len, w2), lambda b, p: (b + row_off, p)),
                  pl.BlockSpec((seq_len, v2), lambda b, p: (b + row_off, p)),
                  pl.BlockSpec((seq_len, v2), lambda b, p: (b + row_off, p)),
                  pl.BlockSpec((1, 2, 2, DK_A, DV_A), lambda b, p: (b, 0, p, 0, 0)),
                  pl.BlockSpec((2, 2, CHUNK, DV_A), lambda b, p: (0, p, 0, 0)),
                  pl.BlockSpec((2, 1, DV_A), lambda b, p: (p, 0, 0)),
                  pl.BlockSpec((seq_len, w2), lambda b, p: (0, 0)),
                  pl.BlockSpec((seq_len, w2), lambda b, p: (0, 0))],
        out_specs=[pl.BlockSpec((seq_len, v2), lambda b, p: (b, p)),
                   pl.BlockSpec((1, 2, 2, DK_A, DV_A), lambda b, p: (b, 0, p, 0, 0))],
        out_shape=[jax.ShapeDtypeStruct((nseq * seq_len, H_A * DV_A), F32),
                   jax.ShapeDtypeStruct((nseq, 2, H_A, DK_A, DV_A), F32)],
        scratch_shapes=[pltpu.VMEM((seq_len, w2), F32), pltpu.VMEM((seq_len, w2), F32),
                        pltpu.VMEM((seq_len, v2), F32)],
        compiler_params=_cparams(("arbitrary", "arbitrary")),
        name="retention",
    )(q, k, v, g, s0, gam, ng, cos_t, sin_t)


def _fill_padded(pad_ref, cur, prev, nxt, has_prev, has_next, halo):
    tile = cur.shape[0]
    pad_ref[0:halo, :] = jnp.where(has_prev, prev, 0.0)
    pad_ref[halo:halo + tile, :] = cur
    pad_ref[halo + tile:halo + tile + halo, :] = jnp.where(has_next, nxt, 0.0)


def _conv_taps(pad_ref, w_ref, ntaps, halo, tile):
    base = halo - ntaps // 2
    acc = w_ref[0:1, :] * pad_ref[base:base + tile, :]
    for kk in range(1, ntaps):
        acc = acc + w_ref[kk:kk + 1, :] * pad_ref[base + kk:base + kk + tile, :]
    return acc


def _halo_specs(width, col_map, halo):
    per = CONV_TILE // halo

    def cur(i, j):
        return (i, col_map(j))

    def prev(i, j):
        return (jnp.maximum(i * per - 1, 0), col_map(j))

    def make_next(nblk):
        def nxt(i, j):
            return (jnp.minimum((i + 1) * per, nblk - 1), col_map(j))
        return nxt

    return cur, prev, make_next


def _sconv_kernel(bg_ref, cg_ref, cgp_ref, cgn_ref, hb_ref, hbp_ref, hbn_ref, w_ref, o_ref, pad_ref, *, st):
    has_prev, has_next = st.halo_flags(pl.program_id(0), CONV_TILE)
    _fill_padded(pad_ref, cg_ref[...] * hb_ref[...], cgp_ref[...] * hbp_ref[...],
                 cgn_ref[...] * hbn_ref[...], has_prev, has_next, SUBLANES)
    o_ref[...] = bg_ref[...] * _conv_taps(pad_ref, w_ref, SC_K, SUBLANES, CONV_TILE)


CONV_COLS = 512


def _short_gated_conv(st, bg, cg, hb, w):
    t, c = bg.shape
    halo = SUBLANES
    cur, prev, make_next = _halo_specs(c, lambda j: j, halo)
    nxt = make_next(t // halo)
    tile_spec = pl.BlockSpec((CONV_TILE, CONV_COLS), cur)
    prev_spec = pl.BlockSpec((halo, CONV_COLS), prev)
    next_spec = pl.BlockSpec((halo, CONV_COLS), nxt)
    return pl.pallas_call(
        functools.partial(_sconv_kernel, st=st),
        grid=(t // CONV_TILE, c // CONV_COLS),
        in_specs=[tile_spec, tile_spec, prev_spec, next_spec, tile_spec, prev_spec, next_spec,
                  pl.BlockSpec((SUBLANES, CONV_COLS), lambda i, j: (0, j))],
        out_specs=tile_spec,
        out_shape=jax.ShapeDtypeStruct((t, c), F32),
        scratch_shapes=[pltpu.VMEM((CONV_TILE + 2 * halo, CONV_COLS), F32)],
        compiler_params=_cparams(("arbitrary", "arbitrary")),
        name="short_gated_conv",
    )(bg, cg, cg, cg, hb, hb, hb, w)


def _qkv_conv_kernel(x_ref, xp_ref, xn_ref, w_ref, o_ref, pad_ref, *, st):
    has_prev, has_next = st.halo_flags(pl.program_id(0), CONV_TILE)
    _fill_padded(pad_ref, x_ref[...], xp_ref[...], xn_ref[...], has_prev, has_next, SUBLANES)
    s = _silu(_conv_taps(pad_ref, w_ref, QKV_K, SUBLANES, CONV_TILE))
    j = pl.program_id(1)
    for hh in range(CONV_COLS // DK_C):
        hsl = slice(hh * DK_C, (hh + 1) * DK_C)
        sh = s[:, hsl]
        nrm = sh * lax.rsqrt(jnp.sum(sh * sh, axis=-1, keepdims=True) + EPS)
        o_ref[:, hsl] = jnp.where(j == 0, nrm * (DK_C ** -0.5), jnp.where(j == 1, nrm, sh))


def _qkv_conv(st, qkv, w):
    t, c = qkv.shape
    assert CONV_COLS == H_C * DK_C == H_C * DV_C
    halo = SUBLANES
    cur, prev, make_next = _halo_specs(c, lambda j: j, halo)
    nxt = make_next(t // halo)
    tile_spec = pl.BlockSpec((CONV_TILE, CONV_COLS), cur)
    return pl.pallas_call(
        functools.partial(_qkv_conv_kernel, st=st),
        grid=(t // CONV_TILE, c // CONV_COLS),
        in_specs=[tile_spec, pl.BlockSpec((halo, CONV_COLS), prev), pl.BlockSpec((halo, CONV_COLS), nxt),
                  pl.BlockSpec((SUBLANES, CONV_COLS), lambda i, j: (0, j))],
        out_specs=tile_spec,
        out_shape=jax.ShapeDtypeStruct((t, c), F32),
        scratch_shapes=[pltpu.VMEM((CONV_TILE + 2 * halo, CONV_COLS), F32)],
        compiler_params=_cparams(("arbitrary", "arbitrary")),
        name="qkv_conv",
    )(qkv, qkv, qkv, w)


CF_HALO = 16


def _conformer_kernel(ca_ref, cap_ref, can_ref, cg_ref, cgp_ref, cgn_ref, w_ref, b_ref, lg_ref, lb_ref,
                      o_ref, pad_ref, *, st):
    has_prev, has_next = st.halo_flags(pl.program_id(0), CONV_TILE)
    _fill_padded(pad_ref, ca_ref[...] * _sigmoid(cg_ref[...]), cap_ref[...] * _sigmoid(cgp_ref[...]),
                 can_ref[...] * _sigmoid(cgn_ref[...]), has_prev, has_next, CF_HALO)
    hc = _conv_taps(pad_ref, w_ref, CF_K, CF_HALO, CONV_TILE) + b_ref[...]
    mu = jnp.mean(hc, axis=-1, keepdims=True)
    xc = hc - mu
    y = xc * lax.rsqrt(jnp.mean(xc * xc, axis=-1, keepdims=True) + EPS) * lg_ref[...] + lb_ref[...]
    o_ref[...] = _silu(y)


def _conformer(st, glu, w, b, ln_g, ln_b):
    t, c2 = glu.shape
    c = c2 // 2
    nblk = t // CF_HALO
    per = CONV_TILE // CF_HALO
    vec = pl.BlockSpec((1, c), lambda i: (0, 0))
    return pl.pallas_call(
        functools.partial(_conformer_kernel, st=st),
        grid=(t // CONV_TILE,),
        in_specs=[pl.BlockSpec((CONV_TILE, c), lambda i: (i, 0)),
                  pl.BlockSpec((CF_HALO, c), lambda i: (jnp.maximum(i * per - 1, 0), 0)),
                  pl.BlockSpec((CF_HALO, c), lambda i: (jnp.minimum((i + 1) * per, nblk - 1), 0)),
                  pl.BlockSpec((CONV_TILE, c), lambda i: (i, 1)),
                  pl.BlockSpec((CF_HALO, c), lambda i: (jnp.maximum(i * per - 1, 0), 1)),
                  pl.BlockSpec((CF_HALO, c), lambda i: (jnp.minimum((i + 1) * per, nblk - 1), 1)),
                  pl.BlockSpec((2 * CF_HALO, c), lambda i: (0, 0)), vec, vec, vec],
        out_specs=pl.BlockSpec((CONV_TILE, c), lambda i: (i, 0)),
        out_shape=jax.ShapeDtypeStruct((t, c), F32),
        scratch_shapes=[pltpu.VMEM((CONV_TILE + 2 * CF_HALO, c), F32)],
        compiler_params=_cparams(("arbitrary",)),
        name="conformer_conv",
    )(glu, glu, glu, glu, glu, glu, w, b, ln_g, ln_b)


GDN_PACK = 4
GDN_GROUPS = 4


def _split_bf16(x):
    hi = x.astype(BF16)
    return hi, (x - hi.astype(F32)).astype(BF16)


def _packed_unit_inverses(groups):
    dot = functools.partial(jnp.dot, preferred_element_type=F32)
    cat = jnp.concatenate
    n = len(groups[0])
    c = CHUNK
    w = n * c
    eye = (lax.broadcasted_iota(jnp.int32, (c, c), 0) == lax.broadcasted_iota(jnp.int32, (c, c), 1)).astype(F32)
    lane_block = lax.broadcasted_iota(jnp.int32, (c, 2 * w), 1) % w // c

    def rhs_of(p):
        hi, lo = _split_bf16(p)
        both = cat([hi, lo], axis=1)
        return hi, lo, cat([jnp.where(lane_block == i, both, jnp.zeros_like(both)) for i in range(n)], axis=0)

    def product(m, r):
        return (r[:m, :w] + r[m:, :w]) + (r[:m, w:] + r[m:, w:])

    ps = [cat(a_list, axis=1) for a_list in groups]
    ts = [cat([eye - a for a in a_list], axis=1) for a_list in groups]
    for g, p in enumerate(ps):
        hi, lo, rhs = rhs_of(p)
        ps[g] = product(c, dot(cat([hi, lo], axis=0), rhs))
    steps = int(math.log2(c)) - 1
    for step in range(steps):
        for g in range(len(groups)):
            p_hi, p_lo, rhs = rhs_of(ps[g])
            t_hi, t_lo = _split_bf16(ts[g])
            if step < steps - 1:
                both = product(2 * c, dot(cat([t_hi, p_hi, t_lo, p_lo], axis=0), rhs))
                ts[g] = ts[g] + both[:c]
                ps[g] = both[c:]
            else:
                ts[g] = ts[g] + product(c, dot(cat([t_hi, t_lo], axis=0), rhs))
    return [[t[:, i * c:(i + 1) * c] for i in range(n)] for t in ts]


def _chunk_cumsum(x, reverse):
    row = lax.broadcasted_iota(jnp.int32, x.shape, 0)
    s = 1
    while s < CHUNK:
        if reverse:
            x = x + jnp.where(row < CHUNK - s, pltpu.roll(x, CHUNK - s, 0), 0.0)
        else:
            x = x + jnp.where(row >= s, pltpu.roll(x, s, 0), 0.0)
        s *= 2
    return x


def _gdn_kernel(q_ref, k_ref, v_ref, z_ref, ab_ref, alog_ref, dtb_ref, s0_ref, ng_ref,
                y_ref, sout_ref, g_ref, beta_ref, u_ref, w_ref, attn_ref, qg_ref, kg_ref, egl_ref, o_ref,
                *, seq_len):
    nc = seq_len // CHUNK
    h = pl.program_id(1)
    row = lax.broadcasted_iota(jnp.int32, (CHUNK, CHUNK), 0)
    col = lax.broadcasted_iota(jnp.int32, (CHUNK, CHUNK), 1)
    lane = lax.broadcasted_iota(jnp.int32, (CHUNK, LANES), 1)
    alog = alog_ref[...]
    dtb = dtb_ref[...]

    def gates(n, carry):
        rows = pl.ds(pl.multiple_of(n * CHUNK, CHUNK), CHUNK)
        ab = ab_ref[rows, :]
        g_all = -jnp.exp(alog) * _softplus(ab + dtb)
        b_all = _sigmoid(ab)
        for d in range(2):
            gsel = jnp.sum(jnp.where(lane == d * H_C + h, g_all, 0.0), axis=-1, keepdims=True)
            bsel = jnp.sum(jnp.where(lane == 2 * H_C + d * H_C + h, b_all, 0.0), axis=-1, keepdims=True)
            g_ref[d, rows, :] = jnp.broadcast_to(gsel, (CHUNK, LANES))
            beta_ref[d, rows, :] = jnp.broadcast_to(bsel, (CHUNK, LANES))
        return carry

    lax.fori_loop(0, nc, gates, 0)

    n_groups = min(GDN_GROUPS, nc // (GDN_PACK // 2))

    def precompute(it, carry):
        groups = [prepare_group(it * n_groups + grp) for grp in range(n_groups)]
        inverses = _packed_unit_inverses([[sysm[0] for sysm in systems] for systems in groups])
        for t_invs, systems in zip(inverses, groups):
            for t_inv, (_, rhs, d, rows) in zip(t_invs, systems):
                uw = _bdot(t_inv, rhs)
                u_ref[d, rows, :] = uw[:, :DV_C]
                w_ref[d, rows, :] = uw[:, DV_C:].astype(BF16)
        return carry

    def prepare_group(pair):
        systems = []
        for jc in range(GDN_PACK // 2):
            n = pair * (GDN_PACK // 2) + jc
            rows = pl.ds(pl.multiple_of(n * CHUNK, CHUNK), CHUNK)
            qc = q_ref[rows, :]
            kc = k_ref[rows, :]
            vc = v_ref[rows, :]
            gbs = [_chunk_cumsum(g_ref[d, rows, :], d == 1) for d in range(2)]
            bbs = [beta_ref[d, rows, :] for d in range(2)]
            kbs = [kc * bb for bb in bbs]
            prod = _bdot_nt(jnp.concatenate(kbs + [qc], axis=0), kc)
            for d in range(2):
                incl = (row >= col) if d == 0 else (row <= col)
                strict = (row > col) if d == 0 else (row < col)
                last = CHUNK - 1 if d == 0 else 0
                gb = gbs[d]
                gr = gb[:, :CHUNK].T
                gl = gb[last:last + 1, :]
                decay = jnp.where(incl, jnp.exp(jnp.where(incl, gb[:, :CHUNK] - gr, 0.0)), 0.0)
                eg = jnp.exp(gb)
                a_low = jnp.where(strict, prod[d * CHUNK:(d + 1) * CHUNK] * decay, 0.0)
                attn_ref[d, rows, :] = jnp.where(incl, prod[2 * CHUNK:] * decay, 0.0).astype(BF16)
                qg_ref[d, rows, :] = (qc * eg).astype(BF16)
                kg_ref[d, rows, :] = kc * jnp.exp(gl - gb)
                egl_ref[d, pl.ds(n, 1), :] = jnp.exp(gl)
                systems.append((a_low, jnp.concatenate([vc * bbs[d], kbs[d] * eg], axis=1), d, rows))
        return systems

    lax.fori_loop(0, nc // (n_groups * GDN_PACK // 2), precompute, 0)

    o_ref[...] = jnp.zeros_like(o_ref)

    def scan(i, carry):
        out = []
        for d, s in enumerate(carry):
            n = i if d == 0 else nc - 1 - i
            rows = pl.ds(pl.multiple_of(n * CHUNK, CHUNK), CHUNK)
            v_new = u_ref[d, rows, :] - _bdot(w_ref[d, rows, :], s)
            o_ref[rows, :] += _bdot(qg_ref[d, rows, :], s) + _bdot(attn_ref[d, rows, :], v_new)
            out.append(s * egl_ref[d, pl.ds(n, 1), :] + _bdot_tn(kg_ref[d, rows, :], v_new))
        return tuple(out)

    s_f, s_b = lax.fori_loop(0, nc, scan, (s0_ref[0, 0, 0], s0_ref[0, 1, 0]))
    sout_ref[0, 0, 0] = s_f
    sout_ref[0, 1, 0] = s_b
    y_ref[...] = _rms(o_ref[...], ng_ref[...]) * _silu(z_ref[...])


def _gdn(qkv_n, z, ab, alog, dtb, s0, ng, nseq, seq_len, row_off):
    blk = lambda off: pl.BlockSpec((seq_len, LANES), lambda b, h: (b + row_off, h + off))
    vec = pl.BlockSpec((1, LANES), lambda b, h: (0, 0))
    st_spec = pl.BlockSpec((1, 2, 1, DK_C, DV_C), lambda b, h: (b, 0, h, 0, 0))
    nc = seq_len // CHUNK
    both = lambda width, dtype: pltpu.VMEM((2, seq_len, width), dtype)
    return pl.pallas_call(
        functools.partial(_gdn_kernel, seq_len=seq_len),
        grid=(nseq, H_C),
        in_specs=[blk(0), blk(H_C), blk(2 * H_C), blk(0),
                  pl.BlockSpec((seq_len, LANES), lambda b, h: (b + row_off, 0)),
                  vec, vec, st_spec, vec],
        out_specs=[pl.BlockSpec((seq_len, LANES), lambda b, h: (b, h)), st_spec],
        out_shape=[jax.ShapeDtypeStruct((nseq * seq_len, H_C * DV_C), F32),
                   jax.ShapeDtypeStruct((nseq, 2, H_C, DK_C, DV_C), F32)],
        scratch_shapes=[both(LANES, F32), both(LANES, F32), both(DV_C, F32), both(DK_C, BF16),
                        both(CHUNK, BF16), both(DK_C, BF16), both(DK_C, F32),
                        pltpu.VMEM((2, max(nc, SUBLANES), LANES), F32), pltpu.VMEM((seq_len, DV_C), F32)],
        compiler_params=_cparams(("arbitrary", "arbitrary")),
        name="gated_deltanet",
    )(qkv_n, qkv_n, qkv_n, z, ab, alog, dtb, s0, ng)


def _out_proj_kernel(ya_ref, yb_ref, x_ref, mod_ref, g_ref, wo_ref, wq_ref, x1_ref, h2_ref, qp_ref):
    half = ya_ref.shape[1]
    m = mod_ref[0]
    y = (jnp.dot(ya_ref[...].astype(BF16), wo_ref[0:half, :], preferred_element_type=F32)
         + jnp.dot(yb_ref[...].astype(BF16), wo_ref[half:, :], preferred_element_type=F32))
    x1 = x_ref[...] + m[2:3] * y
    x1_ref[...] = x1
    h2 = (_rms(x1, g_ref[...]) * (1.0 + m[4:5]) + m[3:4]).astype(BF16)
    h2_ref[...] = h2
    qp_ref[...] = jnp.dot(h2, wq_ref[...], preferred_element_type=F32)


def _out_proj(st, ya, yb, x, mod, g, wo, wq, tm):
    t, d = x.shape
    half = ya.shape[1]
    nq = wq.shape[1]
    return pl.pallas_call(
        _out_proj_kernel,
        grid=(t // tm,),
        in_specs=[pl.BlockSpec((tm, half), lambda i: (i, 0)),
                  pl.BlockSpec((tm, half), lambda i: (i, 0)),
                  pl.BlockSpec((tm, d), lambda i: (i, 0)),
                  pl.BlockSpec((1, 6, d), lambda i: (st.mod_row(i, tm), 0, 0)),
                  pl.BlockSpec((1, d), lambda i: (0, 0)),
                  pl.BlockSpec((2 * half, d), lambda i: (0, 0)),
                  pl.BlockSpec((d, nq), lambda i: (0, 0))],
        out_specs=[pl.BlockSpec((tm, d), lambda i: (i, 0)),
                   pl.BlockSpec((tm, d), lambda i: (i, 0)),
                   pl.BlockSpec((tm, nq), lambda i: (i, 0))],
        out_shape=[jax.ShapeDtypeStruct((t, d), F32), jax.ShapeDtypeStruct((t, d), BF16),
                   jax.ShapeDtypeStruct((t, nq), F32)],
        compiler_params=_cparams(("arbitrary",)),
        name="out_proj",
    )(ya, yb, x, mod, g, wo, wq)


def _candidate_rows():
    groups = ([(0, k2) for k2 in range(16)], [(1, k2) for k2 in range(8)], [(2, k2) for k2 in range(8)],
              [(3, k2) for k2 in range(8)], [(k1, 0) for k1 in range(16)], [(k1, 1) for k1 in range(8)],
              [(k1, 2) for k1 in range(8)])
    rows, seen = [], set()
    for grp in groups:
        for k1, k2 in grp:
            ok = (k1 + 1) * (k2 + 1) <= TOPK and (k1, k2) not in seen
            if ok:
                seen.add((k1, k2))
            rows.append(float(k1 * TOPK + k2) if ok else CAND_INVALID)
    return rows


CAND_INVALID = float(TOPK * TOPK)


def _pack_candidates(a1, a2, op):
    return jnp.concatenate([op(a1[0:1], a2), op(a1[1:2], a2[0:8]), op(a1[2:3], a2[0:8]),
                            op(a1[3:4], a2[0:8]), op(a1, a2[0:1]), op(a1[0:8], a2[1:2]),
                            op(a1[0:8], a2[2:3])], axis=0)


def _top16(s, vals_ref, exact):
    n = s.shape[0]
    iota = lax.broadcasted_iota(jnp.int32, s.shape, 0).astype(F32)
    rank = jnp.full(s.shape, float(TOPK), F32)
    for kk in range(TOPK):
        m = jnp.max(s, axis=0, keepdims=True)
        if exact:
            idx = jnp.min(jnp.where(s == m, iota, float(n)), axis=0, keepdims=True)
            hit = iota == idx
        else:
            hit = s == m
        rank = jnp.where(hit, float(kk), rank)
        vals_ref[kk:kk + 1, :] = m
        s = jnp.where(hit, NEG_INF, s)
    return rank


def _peer_select_kernel(q_ref, keys_ref, cflat_ref, e1_ref, cnt_ref, e2_ref, r2_ref, v1_ref, v2_ref, *, tt):
    nk = N_KEYS
    cflat = cflat_ref[...]
    valid = cflat < CAND_INVALID
    row16 = lax.broadcasted_iota(jnp.int32, (TOPK, LANES), 0)

    def strip_body(si, carry):
        t0 = pl.multiple_of(si * LANES, LANES)
        tsl = pl.ds(t0, LANES)
        s1 = _bdot_nt(keys_ref[0, 0], q_ref[tsl, 0:nk])
        s2 = _bdot_nt(keys_ref[0, 1], q_ref[tsl, nk:2 * nk])

        def compute(exact):
            rank1 = _top16(s1, v1_ref, exact)
            rank2 = _top16(s2, v2_ref, exact)
            v1 = v1_ref[...]
            v2 = v2_ref[...]
            cand = jnp.where(valid, _pack_candidates(v1, v2, jnp.add), NEG_INF)
            prod = _pack_candidates(jnp.exp(v1 - v1[0:1]), jnp.exp(v2 - v2[0:1]), jnp.multiply)
            sel = jnp.zeros(cand.shape, F32)
            for _ in range(TOPK):
                m = jnp.max(cand, axis=0, keepdims=True)
                if exact:
                    idx = jnp.min(jnp.where(cand == m, cflat, CAND_INVALID), axis=0, keepdims=True)
                    hit = cflat == idx
                else:
                    hit = cand == m
                sel = jnp.where(hit, 1.0, sel)
                cand = jnp.where(hit, NEG_INF, cand)
            zsum = jnp.sum(sel * prod, axis=0, keepdims=True)
            c_all = sel[40:56] + jnp.concatenate([sel[56:64] + sel[64:72], jnp.zeros((8, LANES), F32)], axis=0)
            for k1, (lo, hi) in enumerate(((0, 16), (16, 24), (24, 32), (32, 40))):
                c_all = jnp.where(row16 == k1, jnp.sum(sel[lo:hi], axis=0, keepdims=True), c_all)
            cnt = jnp.zeros(rank1.shape, F32)
            for k1 in range(TOPK):
                cnt = jnp.where(rank1 == float(k1), c_all[k1:k1 + 1], cnt)
            in1 = rank1 < float(TOPK)
            in2 = rank2 < float(TOPK)
            e1_ref[0, :, tsl] = jnp.where(in1, jnp.exp(s1 - v1[0:1]), 0.0) / zsum
            cnt_ref[0, :, tsl] = cnt
            e2_ref[:, tsl] = jnp.where(in2, jnp.exp(s2 - v2[0:1]), 0.0).astype(e2_ref.dtype)
            r2_ref[:, tsl] = rank2.astype(r2_ref.dtype)
            n1 = jnp.sum(jnp.where(in1, 1.0, 0.0), axis=0, keepdims=True)
            n2 = jnp.sum(jnp.where(in2, 1.0, 0.0), axis=0, keepdims=True)
            nc = jnp.sum(sel, axis=0, keepdims=True)
            want = float(TOPK)
            return jnp.abs(n1 - want) + jnp.abs(n2 - want) + jnp.abs(nc - want)

        tied = compute(False)

        @pl.when(jnp.max(tied) > 0.0)
        def _():
            compute(True)

        return carry

    lax.fori_loop(0, tt // LANES, strip_body, 0)


def _peer_select(qp, keys, tt):
    t = qp.shape[0]
    rows = _candidate_rows()
    cflat = jnp.broadcast_to(jnp.asarray(rows, F32)[:, None], (len(rows), LANES))
    out_spec = pl.BlockSpec((1, N_KEYS, tt), lambda i, h: (h, 0, i))
    f32_sds = jax.ShapeDtypeStruct((H_P, N_KEYS, t), F32)
    flat_spec = pl.BlockSpec((N_KEYS, tt), lambda i, h: (h, i))
    flat_sds = jax.ShapeDtypeStruct((H_P * N_KEYS, t), F32)
    return pl.pallas_call(
        functools.partial(_peer_select_kernel, tt=tt),
        grid=(t // tt, H_P),
        in_specs=[pl.BlockSpec((tt, 2 * N_KEYS), lambda i, h: (i, h)),
                  pl.BlockSpec((1, 2, N_KEYS, N_KEYS), lambda i, h: (h, 0, 0, 0)),
                  pl.BlockSpec(cflat.shape, lambda i, h: (0, 0))],
        out_specs=[out_spec, out_spec, flat_spec, flat_spec],
        out_shape=[f32_sds, f32_sds, flat_sds, flat_sds],
        scratch_shapes=[pltpu.VMEM((TOPK, LANES), F32), pltpu.VMEM((TOPK, LANES), F32)],
        compiler_params=_cparams(("arbitrary", "arbitrary")),
        name="peer_select",
    )(qp, keys, cflat)


PEER_SPLIT = 4


def _peer_dense_kernel(h_ref, *refs, na):
    u_refs = refs[:PEER_SPLIT]
    vt_refs = refs[PEER_SPLIT:2 * PEER_SPLIT]
    (e1_ref, cnt_ref, e2_ref, r2_ref, x_ref, mod_ref, o_ref,
     acc_ref, s_ref, p_ref, e2b_ref, r2b_ref, ht_ref) = refs[2 * PEER_SPLIT:]
    j = pl.program_id(1)
    tt = s_ref.shape[1]

    @pl.when(j == 0)
    def _():
        acc_ref[...] = jnp.zeros_like(acc_ref)
        e2b_ref[...] = e2_ref[...].astype(BF16)
        r2b_ref[...] = r2_ref[...].astype(BF16)
        ht_ref[...] = h_ref[...].astype(F32).T.astype(BF16)

    sub = u_refs[0].shape[0]
    for k, u_ref in enumerate(u_refs):
        s_ref[k * sub:(k + 1) * sub, :] = jnp.dot(u_ref[...], ht_ref[...], preferred_element_type=F32)

    for aa in range(na):
        rows = slice(aa * N_KEYS, (aa + 1) * N_KEYS)
        for tg in range(tt // LANES):
            tsl = slice(tg * LANES, (tg + 1) * LANES)
            gate = None
            for hh in range(H_P):
                cnt_row = cnt_ref[hh, aa:aa + 1, tsl].astype(BF16)
                e1_row = e1_ref[hh, aa:aa + 1, tsl].astype(BF16)
                hsl = slice(hh * N_KEYS, (hh + 1) * N_KEYS)
                term = jnp.where(r2b_ref[hsl, tsl] < cnt_row, e2b_ref[hsl, tsl], 0.0) * e1_row
                gate = term if gate is None else gate + term
            p_ref[rows, tsl] = gate * _gelu_tanh(s_ref[rows, tsl]).astype(BF16)

    dsub = vt_refs[0].shape[0]
    for k, vt_ref in enumerate(vt_refs):
        acc_ref[k * dsub:(k + 1) * dsub, :] += jnp.dot(vt_ref[...], p_ref[...], preferred_element_type=F32)

    @pl.when(j == pl.num_programs(1) - 1)
    def _():
        o_ref[...] = x_ref[...] + mod_ref[0][5:6] * acc_ref[...].T


def _peer_dense(st, h2, u, vt, e1, cnt, e2, r2, x, mod, tt, et):
    t, d = x.shape
    n_exp = u.shape[0]
    na = et // N_KEYS
    u_specs = [pl.BlockSpec((et // PEER_SPLIT, d),
                            functools.partial(lambda i, j, k: (j * PEER_SPLIT + k, 0), k=k))
               for k in range(PEER_SPLIT)]
    vt_specs = [pl.BlockSpec((d // PEER_SPLIT, et), functools.partial(lambda i, j, k: (k, j), k=k))
                for k in range(PEER_SPLIT)]
    row_spec = pl.BlockSpec((H_P, na, tt), lambda i, j: (0, j, i))
    full_spec = pl.BlockSpec((H_P * N_KEYS, tt), lambda i, j: (0, i))
    return pl.pallas_call(
        functools.partial(_peer_dense_kernel, na=na),
        grid=(t // tt, n_exp // et),
        in_specs=[pl.BlockSpec((tt, d), lambda i, j: (i, 0))] + u_specs + vt_specs + [
                  row_spec, row_spec, full_spec, full_spec,
                  pl.BlockSpec((tt, d), lambda i, j: (i, 0)),
                  pl.BlockSpec((1, 6, d), lambda i, j: (st.mod_row(i, tt), 0, 0))],
        out_specs=pl.BlockSpec((tt, d), lambda i, j: (i, 0)),
        out_shape=jax.ShapeDtypeStruct((t, d), F32),
        scratch_shapes=[pltpu.VMEM((d, tt), F32), pltpu.VMEM((et, tt), F32), pltpu.VMEM((et, tt), BF16),
                        pltpu.VMEM((H_P * N_KEYS, tt), BF16), pltpu.VMEM((H_P * N_KEYS, tt), BF16),
                        pltpu.VMEM((d, tt), BF16)],
        compiler_params=_cparams(("arbitrary", "arbitrary")),
        name="peer_dense",
    )(h2, *([u] * PEER_SPLIT), *([vt] * PEER_SPLIT), e1, cnt, e2, r2, x, mod)


def _final_norm_kernel(x_ref, g_ref, o_ref):
    o_ref[...] = _rms(x_ref[...], g_ref[...])


def _final_norm(x, g, tm):
    t, d = x.shape
    return pl.pallas_call(
        _final_norm_kernel,
        grid=(t // tm,),
        in_specs=[pl.BlockSpec((tm, d), lambda i: (i, 0)), pl.BlockSpec((1, d), lambda i: (0, 0))],
        out_specs=pl.BlockSpec((tm, d), lambda i: (i, 0)),
        out_shape=jax.ShapeDtypeStruct((t, d), F32),
        compiler_params=_cparams(("arbitrary",)),
        name="final_norm",
    )(x, g)


def _rope_tables(seq_len):
    pos = jnp.arange(seq_len)
    rowp = (pos // GRID_W).astype(F32)
    colp = (pos % GRID_W).astype(F32)
    nf = DK_A // 4
    freqs = ROPE_BASE ** (-jnp.arange(nf, dtype=F32) / nf)
    ang = jnp.concatenate([rowp[:, None] * freqs, colp[:, None] * freqs], axis=-1)
    cos = jnp.cos(ang)
    sin = jnp.sin(ang)
    cos_t = jnp.tile(cos, (1, 4))
    sin_t = jnp.tile(jnp.concatenate([-sin, sin], axis=-1), (1, 2))
    return cos_t, sin_t


def _pad_rows(w, rows):
    return jnp.concatenate([w, jnp.zeros((rows - w.shape[0], w.shape[1]), w.dtype)], axis=0)


def _lane_row(vals):
    flat = vals.reshape(-1).astype(F32)
    return jnp.concatenate([flat, jnp.zeros((LANES - flat.shape[0],), F32)])[None, :]


def kernel(x_prompt, x_sample, state_ret, state_gdn, c, c_ctx, ada_w, ada_b, norm_mix_g, norm_ffn_g,
           final_norm_g, ev_w_in, ev_w_out, ret_gamma_logit, ret_norm_g, sc_conv_w, od_w_in, od_w_out,
           gdn_conv_w, gdn_a_log, gdn_dt_bias, gdn_norm_g, cf_dw_w, cf_dw_b, cf_ln_g, cf_ln_b,
           peer_wq, peer_keys, peer_u, peer_v):
    bp, lp, d = x_prompt.shape
    bs, ls, _ = x_sample.shape
    depth = ada_w.shape[0]
    st = _Streams(bp, lp, bs, ls)
    w_a = H_A * DV_A
    w_b = d - w_a
    w_c = H_C * DV_C
    w_d = d - w_c
    tm = 512
    peer_tt = 512
    peer_et = 1024
    select_tt = 512

    x = jnp.concatenate([x_prompt.reshape(st.tp, d), x_sample.reshape(st.ts, d)], axis=0)
    cvec = jnp.concatenate([c_ctx[None, :], c, jnp.zeros((N_MOD_ROWS - 1 - bs, d), F32)], axis=0)
    mods = _modulation(cvec, ada_w, ada_b).reshape(depth, N_MOD_ROWS, 6, d)
    cos_t, sin_t = _rope_tables(ls)
    zero_ret = jnp.zeros((bp, 2, H_A, DK_A, DV_A), F32)
    zero_gdn = jnp.zeros((bp, 2, H_C, DK_C, DV_C), F32)

    ret_new, gdn_new = [], []
    for l in range(depth):
        i = l // 2
        mod = mods[l]
        g1 = norm_mix_g[l][None, :]
        if l % 2 == 0:
            widths = (H_A * DK_A, H_A * DK_A, w_a, w_a, w_b, w_b, w_b)
            q, k, v, g, bg, cg, hb = _norm_proj(st, x, mod, g1, ev_w_in[i].astype(BF16), widths, tm)
            gam = jnp.broadcast_to(ret_gamma_logit[i][:, :, None, None], (2, H_A, CHUNK, DV_A))
            ng = ret_norm_g[i][:, None, :]
            ya_p, s_new = _retention(q, k, v, g, zero_ret, gam, ng, cos_t[:lp], sin_t[:lp],
                                     bp, lp, 0, False)
            ya_s, _ = _retention(q, k, v, g, state_ret[:, i], gam, ng, cos_t, sin_t,
                                 bs, ls, st.tp // ls, True)
            ret_new.append(s_new)
            ya = jnp.concatenate([ya_p, ya_s], axis=0)
            yb = _short_gated_conv(st, bg, cg, hb, _pad_rows(sc_conv_w[i], SUBLANES))
            w_out = ev_w_out[i]
        else:
            n_gate = 2 * 2 * H_C
            w_in = od_w_in[i]
            o_ab = 4 * w_c
            w_main = jnp.concatenate([w_in[:, :o_ab], w_in[:, o_ab + n_gate:],
                                      w_in[:, o_ab:o_ab + n_gate],
                                      jnp.zeros((d, LANES - n_gate), F32)], axis=1).astype(BF16)
            widths = (3 * w_c, w_c, 2 * w_d, LANES)
            qkv, z, glu, ab = _norm_proj(st, x, mod, g1, w_main, widths, tm)
            qkv_n = _qkv_conv(st, qkv, _pad_rows(gdn_conv_w[i], SUBLANES))
            alog = _lane_row(gdn_a_log[i])
            dtb = _lane_row(gdn_dt_bias[i])
            ng = gdn_norm_g[i][None, :]
            yc_p, s_new = _gdn(qkv_n, z, ab, alog, dtb, zero_gdn, ng, bp, lp, 0)
            yc_s, _ = _gdn(qkv_n, z, ab, alog, dtb, state_gdn[:, i], ng, bs, ls, st.tp // ls)
            gdn_new.append(s_new)
            ya = jnp.concatenate([yc_p, yc_s], axis=0)
            yb = _conformer(st, glu, _pad_rows(cf_dw_w[i], 2 * CF_HALO), cf_dw_b[i][None, :],
                            cf_ln_g[i][None, :], cf_ln_b[i][None, :])
            w_out = od_w_out[i]
        x, h2, qp = _out_proj(st, ya, yb, x, mod, norm_ffn_g[l][None, :], w_out.astype(BF16),
                              peer_wq[l].astype(BF16), tm)
        e1, cnt, e2, r2 = _peer_select(qp, peer_keys[l].astype(BF16), select_tt)
        x = _peer_dense(st, h2, peer_u[l].astype(BF16), peer_v[l].astype(BF16).T, e1, cnt, e2, r2,
                        x, mod, peer_tt, peer_et)

    y = _final_norm(x, final_norm_g[None, :], tm)
    y_prompt = y[:st.tp].reshape(bp, lp, d)
    y_sample = y[st.tp:].reshape(bs, ls, d)
    new_state_ret = jnp.stack(ret_new, axis=1).astype(x_prompt.dtype)
    new_state_gdn = jnp.stack(gdn_new, axis=1).astype(x_prompt.dtype)
    return (y_prompt, y_sample, new_state_ret, new_state_gdn)
```

```python
import functools
import math

import jax
import jax.numpy as jnp
from jax import lax
from jax.experimental import pallas as pl
from jax.experimental.pallas import tpu as pltpu

F32 = jnp.float32
BF16 = jnp.bfloat16
HIGHEST = lax.Precision.HIGHEST

EPS = 1e-6
CHUNK = 64
GRID_W = 64
ROPE_BASE = 10000.0
H_A, DK_A, DV_A = 4, 64, 128
H_C, DK_C, DV_C = 4, 128, 128
SC_K, QKV_K, CF_K = 3, 3, 31
N_KEYS, H_P, TOPK = 128, 8, 16
N_MOD_ROWS = 16
LANES = 128
SUBLANES = 8
VMEM_LIMIT = 56 * 1024 * 1024
CONV_TILE = 256
NEG_INF = float("-inf")


def _cparams(sem):
    return pltpu.CompilerParams(dimension_semantics=sem, vmem_limit_bytes=VMEM_LIMIT)


def _bdot(a, b):
    return jnp.dot(a.astype(BF16), b.astype(BF16), preferred_element_type=F32)


def _bdot_nt(a, b):
    return lax.dot_general(a.astype(BF16), b.astype(BF16), (((1,), (1,)), ((), ())),
                           preferred_element_type=F32)


def _bdot_tn(a, b):
    return lax.dot_general(a.astype(BF16), b.astype(BF16), (((0,), (0,)), ((), ())),
                           preferred_element_type=F32)


def _hdot(a, b):
    return jnp.dot(a, b, precision=HIGHEST, preferred_element_type=F32)


def _sigmoid(x):
    return 1.0 / (1.0 + jnp.exp(-x))


def _silu(x):
    return x * _sigmoid(x)


def _softplus(x):
    return jnp.maximum(x, 0.0) + jnp.log1p(jnp.exp(-jnp.abs(x)))


def _log_sigmoid(x):
    return -_softplus(-x)


def _gelu_tanh(x):
    c = math.sqrt(2.0 / math.pi)
    return 0.5 * x * (1.0 + jnp.tanh(c * (x + 0.044715 * (x * x * x))))


def _rms(x, g):
    return x * lax.rsqrt(jnp.mean(x * x, axis=-1, keepdims=True) + EPS) * g


class _Streams:
    def __init__(self, bp, lp, bs, ls):
        self.bp, self.lp, self.bs, self.ls = bp, lp, bs, ls
        self.tp, self.ts = bp * lp, bs * ls
        self.t = self.tp + self.ts

    def mod_row(self, i, tile):
        tiles_p = self.tp // tile
        per_seq = self.ls // tile
        return jnp.where(i < tiles_p, 0, 1 + (i - tiles_p) // per_seq)

    def halo_flags(self, i, tile):
        tiles_p = self.tp // tile
        per_p = self.lp // tile
        per_s = self.ls // tile
        in_p = i < tiles_p
        jp = i % per_p
        js = (i - tiles_p) % per_s
        has_prev = jnp.where(in_p, jp > 0, js > 0)
        has_next = jnp.where(in_p, jp < per_p - 1, js < per_s - 1)
        return has_prev, has_next


def _mod_kernel(c_ref, w_ref, b_ref, o_ref):
    s = _silu(c_ref[...])
    o_ref[0] = _hdot(s, w_ref[0]) + b_ref[0]


def _modulation(cvec, ada_w, ada_b):
    depth, d, d6 = ada_w.shape
    nj = d6 // d
    return pl.pallas_call(
        _mod_kernel,
        grid=(depth, nj),
        in_specs=[pl.BlockSpec((N_MOD_ROWS, d), lambda l, j: (0, 0)),
                  pl.BlockSpec((1, d, d), lambda l, j: (l, 0, j)),
                  pl.BlockSpec((1, 1, d), lambda l, j: (l, 0, j))],
        out_specs=pl.BlockSpec((1, N_MOD_ROWS, d), lambda l, j: (l, 0, j)),
        out_shape=jax.ShapeDtypeStruct((depth, N_MOD_ROWS, d6), F32),
        compiler_params=_cparams(("arbitrary", "arbitrary")),
        name="modulation",
    )(cvec, ada_w, ada_b.reshape(depth, 1, d6))


def _norm_proj_kernel(x_ref, mod_ref, g_ref, w_ref, *o_refs, widths):
    m = mod_ref[0]
    h = _rms(x_ref[...], g_ref[...]) * (1.0 + m[1:2]) + m[0:1]
    p = jnp.dot(h.astype(BF16), w_ref[...], preferred_element_type=F32)
    off = 0
    for o_ref, wd in zip(o_refs, widths):
        o_ref[...] = p[:, off:off + wd]
        off += wd


def _norm_proj(st, x, mod, g, w, widths, tm):
    t, d = x.shape
    n = w.shape[1]
    return pl.pallas_call(
        functools.partial(_norm_proj_kernel, widths=widths),
        grid=(t // tm,),
        in_specs=[pl.BlockSpec((tm, d), lambda i: (i, 0)),
                  pl.BlockSpec((1, 6, d), lambda i: (st.mod_row(i, tm), 0, 0)),
                  pl.BlockSpec((1, d), lambda i: (0, 0)),
                  pl.BlockSpec((d, n), lambda i: (0, 0))],
        out_specs=[pl.BlockSpec((tm, wd), lambda i: (i, 0)) for wd in widths],
        out_shape=[jax.ShapeDtypeStruct((t, wd), F32) for wd in widths],
        compiler_params=_cparams(("arbitrary",)),
        name="norm_proj",
    )(x, mod, g, w)


def _ret_kernel(q_ref, k_ref, v_ref, g_ref, s0_ref, gam_ref, ng_ref, cos_ref, sin_ref,
                ya_ref, sout_ref, qs_ref, ks_ref, o_ref, sf_ref, sb_ref, *, seq_len, latent):
    nc = seq_len // CHUNK
    q = q_ref[...]
    k = k_ref[...] * (DK_A ** -0.5)
    if latent:
        lane = lax.broadcasted_iota(jnp.int32, q.shape, 1)
        first_half = (lane % DK_A) < (DK_A // 2)
        cos = cos_ref[...]
        sin = sin_ref[...]

        def rope(x):
            partner = jnp.where(first_half, pltpu.roll(x, LANES - DK_A // 2, 1),
                                pltpu.roll(x, DK_A // 2, 1))
            return x * cos + partner * sin

        q = rope(q)
        k = rope(k)
    qs_ref[...] = q
    ks_ref[...] = k

    cat = jnp.concatenate
    kw = 2 * DK_A
    vw = 2 * DV_A
    row = lax.broadcasted_iota(jnp.int32, (CHUNK, kw), 0).astype(F32)
    lane_k = lax.broadcasted_iota(jnp.int32, (CHUNK, kw), 1)
    lane_v = lax.broadcasted_iota(jnp.int32, (CHUNK, vw), 1)
    head0_k = lane_k < DK_A
    head0_v = lane_v < DV_A
    lg = [[_log_sigmoid(gam_ref[d, hh]) for hh in range(2)] for d in range(2)]
    lgf_k = jnp.where(head0_k, lg[0][0], lg[0][1])
    lgb_k = jnp.where(head0_k, lg[1][0], lg[1][1])
    diff = row - (lane_k % DK_A).astype(F32)
    dcomb = (jnp.where(diff >= 0, jnp.exp(lgf_k * jnp.maximum(diff, 0.0)), 0.0)
             + jnp.where(diff <= 0, jnp.exp(lgb_k * jnp.maximum(-diff, 0.0)), 0.0))
    wend_f = jnp.exp(lgf_k * (CHUNK - 1.0 - row))
    wstart_f = jnp.exp(lgf_k * (row + 1.0))
    wend_b = jnp.exp(lgb_k * row)
    wstart_b = jnp.exp(lgb_k * (CHUNK - row))

    def state_layout(h0, h1):
        zero = jnp.zeros_like(h0)
        return cat([cat([h0, zero], axis=1), cat([zero, h1], axis=1)], axis=0)

    on_diag = state_layout(jnp.ones((DK_A, DV_A), F32), jnp.ones((DK_A, DV_A), F32)) > 0.0
    gch_f = jnp.exp(state_layout(lg[0][0], lg[0][1]) * float(CHUNK))
    gch_b = jnp.exp(state_layout(lg[1][0], lg[1][1]) * float(CHUNK))
    sf_ref[...] = state_layout(s0_ref[0, 0, 0], s0_ref[0, 0, 1])
    sb_ref[...] = state_layout(s0_ref[0, 1, 0], s0_ref[0, 1, 1])
    o_ref[...] = jnp.zeros_like(o_ref)

    def step(i, carry):
        rows = pl.ds(pl.multiple_of(i * CHUNK, CHUNK), CHUNK)
        qn = qs_ref[rows, :]
        kn = ks_ref[rows, :]
        vn = v_ref[rows, :]
        k_bd = cat([jnp.where(head0_k, kn, 0.0), jnp.where(head0_k, 0.0, kn)], axis=0)
        v_bd = cat([jnp.where(head0_v, vn, 0.0), jnp.where(head0_v, 0.0, vn)], axis=0)
        a = _bdot_nt(qn, k_bd) * dcomb
        s_f = sf_ref[...]
        o_ref[rows, :] += _bdot(cat([a, qn * wstart_f], axis=1), cat([v_bd, s_f], axis=0))
        sf_ref[...] = s_f * gch_f + jnp.where(on_diag, _bdot_tn(kn * wend_f, vn), 0.0)
        rows = pl.ds(pl.multiple_of((nc - 1 - i) * CHUNK, CHUNK), CHUNK)
        qn = qs_ref[rows, :]
        kn = ks_ref[rows, :]
        vn = v_ref[rows, :]
        s_b = sb_ref[...]
        o_ref[rows, :] += _bdot(qn * wstart_b, s_b)
        sb_ref[...] = s_b * gch_b + jnp.where(on_diag, _bdot_tn(kn * wend_b, vn), 0.0)
        return carry

    lax.fori_loop(0, nc, step, 0)
    for d, s_ref in enumerate((sf_ref, sb_ref)):
        sout_ref[0, d, 0] = s_ref[:DK_A, :DV_A]
        sout_ref[0, d, 1] = s_ref[DK_A:, DV_A:]

    for hh in range(2):
        vsl = slice(hh * DV_A, (hh + 1) * DV_A)
        ya_ref[:, vsl] = _silu(g_ref[:, vsl]) * _rms(o_ref[:, vsl], ng_ref[hh])


def _retention(q, k, v, g, s0, gam, ng, cos_t, sin_t, nseq, seq_len, row_off, latent):
    w2 = 2 * DK_A
    v2 = 2 * DV_A
    return pl.pallas_call(
        functools.partial(_ret_kernel, seq_len=seq_len, latent=latent),
        grid=(nseq, H_A // 2),
        in_specs=[pl.BlockSpec((seq_len, w2), lambda b, p: (b + row_off, p)),
                  pl.BlockSpec((seq_len, w2), lambda b, p: (b + row_off, p)),
                  pl.BlockSpec((seq_len, v2), lambda b, p: (b + row_off, p)),
                  pl.BlockSpec((seq_len, v2), lambda b, p: (b + row_off, p)),
                  pl.BlockSpec((1, 2, 2, DK_A, DV_A), lambda b, p: (b, 0, p, 0, 0)),
                  pl.BlockSpec((2, 2, CHUNK, DV_A), lambda b, p: (0, p, 0, 0)),
                  pl.BlockSpec((2, 1, DV_A), lambda b, p: (p, 0, 0)),
                  pl.BlockSpec((seq_len, w2), lambda b, p: (0, 0)),
                  pl.BlockSpec((seq_len, w2), lambda b, p: (0, 0))],
        out_specs=[pl.BlockSpec((seq_len, v2), lambda b, p: (b, p)),
                   pl.BlockSpec((1, 2, 2, DK_A, DV_A), lambda b, p: (b, 0, p, 0, 0))],
        out_shape=[jax.ShapeDtypeStruct((nseq * seq_len, H_A * DV_A), F32),
                   jax.ShapeDtypeStruct((nseq, 2, H_A, DK_A, DV_A), F32)],
        scratch_shapes=[pltpu.VMEM((seq_len, w2), F32), pltpu.VMEM((seq_len, w2), F32),
                        pltpu.VMEM((seq_len, v2), F32), pltpu.VMEM((w2, v2), F32), pltpu.VMEM((w2, v2), F32)],
        compiler_params=_cparams(("arbitrary", "arbitrary")),
        name="retention",
    )(q, k, v, g, s0, gam, ng, cos_t, sin_t)


def _fill_padded(pad_ref, cur, prev, nxt, has_prev, has_next, halo):
    tile = cur.shape[0]
    pad_ref[0:halo, :] = jnp.where(has_prev, prev, 0.0)
    pad_ref[halo:halo + tile, :] = cur
    pad_ref[halo + tile:halo + tile + halo, :] = jnp.where(has_next, nxt, 0.0)


def _conv_taps(pad_ref, w_ref, ntaps, halo, tile):
    base = halo - ntaps // 2
    acc = w_ref[0:1, :] * pad_ref[base:base + tile, :]
    for kk in range(1, ntaps):
        acc = acc + w_ref[kk:kk + 1, :] * pad_ref[base + kk:base + kk + tile, :]
    return acc


def _halo_specs(width, col_map, halo):
    per = CONV_TILE // halo

    def cur(i, j):
        return (i, col_map(j))

    def prev(i, j):
        return (jnp.maximum(i * per - 1, 0), col_map(j))

    def make_next(nblk):
        def nxt(i, j):
            return (jnp.minimum((i + 1) * per, nblk - 1), col_map(j))
        return nxt

    return cur, prev, make_next


def _sconv_kernel(bg_ref, cg_ref, cgp_ref, cgn_ref, hb_ref, hbp_ref, hbn_ref, w_ref, o_ref, pad_ref, *, st):
    has_prev, has_next = st.halo_flags(pl.program_id(0), CONV_TILE)
    _fill_padded(pad_ref, cg_ref[...] * hb_ref[...], cgp_ref[...] * hbp_ref[...],
                 cgn_ref[...] * hbn_ref[...], has_prev, has_next, SUBLANES)
    o_ref[...] = bg_ref[...] * _conv_taps(pad_ref, w_ref, SC_K, SUBLANES, CONV_TILE)


CONV_COLS = 512


def _short_gated_conv(st, bg, cg, hb, w):
    t, c = bg.shape
    halo = SUBLANES
    cur, prev, make_next = _halo_specs(c, lambda j: j, halo)
    nxt = make_next(t // halo)
    tile_spec = pl.BlockSpec((CONV_TILE, CONV_COLS), cur)
    prev_spec = pl.BlockSpec((halo, CONV_COLS), prev)
    next_spec = pl.BlockSpec((halo, CONV_COLS), nxt)
    return pl.pallas_call(
        functools.partial(_sconv_kernel, st=st),
        grid=(t // CONV_TILE, c // CONV_COLS),
        in_specs=[tile_spec, tile_spec, prev_spec, next_spec, tile_spec, prev_spec, next_spec,
                  pl.BlockSpec((SUBLANES, CONV_COLS), lambda i, j: (0, j))],
        out_specs=tile_spec,
        out_shape=jax.ShapeDtypeStruct((t, c), F32),
        scratch_shapes=[pltpu.VMEM((CONV_TILE + 2 * halo, CONV_COLS), F32)],
        compiler_params=_cparams(("arbitrary", "arbitrary")),
        name="short_gated_conv",
    )(bg, cg, cg, cg, hb, hb, hb, w)


def _qkv_conv_kernel(x_ref, xp_ref, xn_ref, w_ref, o_ref, pad_ref, *, st):
    has_prev, has_next = st.halo_flags(pl.program_id(0), CONV_TILE)
    _fill_padded(pad_ref, x_ref[...], xp_ref[...], xn_ref[...], has_prev, has_next, SUBLANES)
    s = _silu(_conv_taps(pad_ref, w_ref, QKV_K, SUBLANES, CONV_TILE))
    j = pl.program_id(1)
    for hh in range(CONV_COLS // DK_C):
        hsl = slice(hh * DK_C, (hh + 1) * DK_C)
        sh = s[:, hsl]
        nrm = sh * lax.rsqrt(jnp.sum(sh * sh, axis=-1, keepdims=True) + EPS)
        o_ref[:, hsl] = jnp.where(j == 0, nrm * (DK_C ** -0.5), jnp.where(j == 1, nrm, sh))


def _qkv_conv(st, qkv, w):
    t, c = qkv.shape
    assert CONV_COLS == H_C * DK_C == H_C * DV_C
    halo = SUBLANES
    cur, prev, make_next = _halo_specs(c, lambda j: j, halo)
    nxt = make_next(t // halo)
    tile_spec = pl.BlockSpec((CONV_TILE, CONV_COLS), cur)
    return pl.pallas_call(
        functools.partial(_qkv_conv_kernel, st=st),
        grid=(t // CONV_TILE, c // CONV_COLS),
        in_specs=[tile_spec, pl.BlockSpec((halo, CONV_COLS), prev), pl.BlockSpec((halo, CONV_COLS), nxt),
                  pl.BlockSpec((SUBLANES, CONV_COLS), lambda i, j: (0, j))],
        out_specs=tile_spec,
        out_shape=jax.ShapeDtypeStruct((t, c), F32),
        scratch_shapes=[pltpu.VMEM((CONV_TILE + 2 * halo, CONV_COLS), F32)],
        compiler_params=_cparams(("arbitrary", "arbitrary")),
        name="qkv_conv",
    )(qkv, qkv, qkv, w)


CF_HALO = 16


def _conformer_kernel(ca_ref, cap_ref, can_ref, cg_ref, cgp_ref, cgn_ref, w_ref, b_ref, lg_ref, lb_ref,
                      o_ref, pad_ref, *, st):
    has_prev, has_next = st.halo_flags(pl.program_id(0), CONV_TILE)
    _fill_padded(pad_ref, ca_ref[...] * _sigmoid(cg_ref[...]), cap_ref[...] * _sigmoid(cgp_ref[...]),
                 can_ref[...] * _sigmoid(cgn_ref[...]), has_prev, has_next, CF_HALO)
    hc = _conv_taps(pad_ref, w_ref, CF_K, CF_HALO, CONV_TILE) + b_ref[...]
    mu = jnp.mean(hc, axis=-1, keepdims=True)
    xc = hc - mu
    y = xc * lax.rsqrt(jnp.mean(xc * xc, axis=-1, keepdims=True) + EPS) * lg_ref[...] + lb_ref[...]
    o_ref[...] = _silu(y)


def _conformer(st, glu, w, b, ln_g, ln_b):
    t, c2 = glu.shape
    c = c2 // 2
    nblk = t // CF_HALO
    per = CONV_TILE // CF_HALO
    vec = pl.BlockSpec((1, c), lambda i: (0, 0))
    return pl.pallas_call(
        functools.partial(_conformer_kernel, st=st),
        grid=(t // CONV_TILE,),
        in_specs=[pl.BlockSpec((CONV_TILE, c), lambda i: (i, 0)),
                  pl.BlockSpec((CF_HALO, c), lambda i: (jnp.maximum(i * per - 1, 0), 0)),
                  pl.BlockSpec((CF_HALO, c), lambda i: (jnp.minimum((i + 1) * per, nblk - 1), 0)),
                  pl.BlockSpec((CONV_TILE, c), lambda i: (i, 1)),
                  pl.BlockSpec((CF_HALO, c), lambda i: (jnp.maximum(i * per - 1, 0), 1)),
                  pl.BlockSpec((CF_HALO, c), lambda i: (jnp.minimum((i + 1) * per, nblk - 1), 1)),
                  pl.BlockSpec((2 * CF_HALO, c), lambda i: (0, 0)), vec, vec, vec],
        out_specs=pl.BlockSpec((CONV_TILE, c), lambda i: (i, 0)),
        out_shape=jax.ShapeDtypeStruct((t, c), F32),
        scratch_shapes=[pltpu.VMEM((CONV_TILE + 2 * CF_HALO, c), F32)],
        compiler_params=_cparams(("arbitrary",)),
        name="conformer_conv",
    )(glu, glu, glu, glu, glu, glu, w, b, ln_g, ln_b)


GDN_PACK = 4
GDN_GROUPS = 4


def _split_bf16(x):
    hi = x.astype(BF16)
    return hi, (x - hi.astype(F32)).astype(BF16)


def _packed_unit_inverses(groups):
    dot = functools.partial(jnp.dot, preferred_element_type=F32)
    cat = jnp.concatenate
    n = len(groups[0])
    c = CHUNK
    w = n * c
    eye = (lax.broadcasted_iota(jnp.int32, (c, c), 0) == lax.broadcasted_iota(jnp.int32, (c, c), 1)).astype(F32)
    lane_block = lax.broadcasted_iota(jnp.int32, (c, 2 * w), 1) % w // c

    def rhs_of(p):
        hi, lo = _split_bf16(p)
        both = cat([hi, lo], axis=1)
        return hi, lo, cat([jnp.where(lane_block == i, both, jnp.zeros_like(both)) for i in range(n)], axis=0)

    def product(m, r):
        return (r[:m, :w] + r[m:, :w]) + (r[:m, w:] + r[m:, w:])

    ps = [cat(a_list, axis=1) for a_list in groups]
    ts = [cat([eye - a for a in a_list], axis=1) for a_list in groups]
    for g, p in enumerate(ps):
        hi, lo, rhs = rhs_of(p)
        ps[g] = product(c, dot(cat([hi, lo], axis=0), rhs))
    steps = int(math.log2(c)) - 1
    for step in range(steps):
        for g in range(len(groups)):
            p_hi, p_lo, rhs = rhs_of(ps[g])
            t_hi, t_lo = _split_bf16(ts[g])
            if step < steps - 1:
                both = product(2 * c, dot(cat([t_hi, p_hi, t_lo, p_lo], axis=0), rhs))
                ts[g] = ts[g] + both[:c]
                ps[g] = both[c:]
            else:
                ts[g] = ts[g] + product(c, dot(cat([t_hi, t_lo], axis=0), rhs))
    return [[t[:, i * c:(i + 1) * c] for i in range(n)] for t in ts]


def _chunk_cumsum(x, reverse):
    row = lax.broadcasted_iota(jnp.int32, x.shape, 0)
    s = 1
    while s < CHUNK:
        if reverse:
            x = x + jnp.where(row < CHUNK - s, pltpu.roll(x, CHUNK - s, 0), 0.0)
        else:
            x = x + jnp.where(row >= s, pltpu.roll(x, s, 0), 0.0)
        s *= 2
    return x


def _gdn_kernel(q_ref, k_ref, v_ref, z_ref, ab_ref, alog_ref, dtb_ref, s0_ref, ng_ref,
                y_ref, sout_ref, g_ref, beta_ref, u_ref, w_ref, attn_ref, qg_ref, kg_ref, egl_ref, o_ref,
                *, seq_len):
    nc = seq_len // CHUNK
    h = pl.program_id(1)
    row = lax.broadcasted_iota(jnp.int32, (CHUNK, CHUNK), 0)
    col = lax.broadcasted_iota(jnp.int32, (CHUNK, CHUNK), 1)
    lane = lax.broadcasted_iota(jnp.int32, (CHUNK, LANES), 1)
    alog = alog_ref[...]
    dtb = dtb_ref[...]

    def gates(n, carry):
        rows = pl.ds(pl.multiple_of(n * CHUNK, CHUNK), CHUNK)
        ab = ab_ref[rows, :]
        g_all = -jnp.exp(alog) * _softplus(ab + dtb)
        b_all = _sigmoid(ab)
        for d in range(2):
            gsel = jnp.sum(jnp.where(lane == d * H_C + h, g_all, 0.0), axis=-1, keepdims=True)
            bsel = jnp.sum(jnp.where(lane == 2 * H_C + d * H_C + h, b_all, 0.0), axis=-1, keepdims=True)
            g_ref[d, rows, :] = jnp.broadcast_to(gsel, (CHUNK, LANES))
            beta_ref[d, rows, :] = jnp.broadcast_to(bsel, (CHUNK, LANES))
        return carry

    lax.fori_loop(0, nc, gates, 0)

    n_groups = min(GDN_GROUPS, nc // (GDN_PACK // 2))

    def precompute(it, carry):
        groups = [prepare_group(it * n_groups + grp) for grp in range(n_groups)]
        inverses = _packed_unit_inverses([[sysm[0] for sysm in systems] for systems in groups])
        for t_invs, systems in zip(inverses, groups):
            for t_inv, (_, rhs, d, rows) in zip(t_invs, systems):
                uw = _bdot(t_inv, rhs)
                u_ref[d, rows, :] = uw[:, :DV_C]
                w_ref[d, rows, :] = uw[:, DV_C:].astype(BF16)
        return carry

    def prepare_group(pair):
        systems = []
        for jc in range(GDN_PACK // 2):
            n = pair * (GDN_PACK // 2) + jc
            rows = pl.ds(pl.multiple_of(n * CHUNK, CHUNK), CHUNK)
            qc = q_ref[rows, :]
            kc = k_ref[rows, :]
            vc = v_ref[rows, :]
            gbs = [_chunk_cumsum(g_ref[d, rows, :], d == 1) for d in range(2)]
            bbs = [beta_ref[d, rows, :] for d in range(2)]
            kbs = [kc * bb for bb in bbs]
            prod = _bdot_nt(jnp.concatenate(kbs + [qc], axis=0), kc)
            for d in range(2):
                incl = (row >= col) if d == 0 else (row <= col)
                strict = (row > col) if d == 0 else (row < col)
                last = CHUNK - 1 if d == 0 else 0
                gb = gbs[d]
                gr = gb[:, :CHUNK].T
                gl = gb[last:last + 1, :]
                decay = jnp.where(incl, jnp.exp(jnp.where(incl, gb[:, :CHUNK] - gr, 0.0)), 0.0)
                eg = jnp.exp(gb)
                a_low = jnp.where(strict, prod[d * CHUNK:(d + 1) * CHUNK] * decay, 0.0)
                attn_ref[d, rows, :] = jnp.where(incl, prod[2 * CHUNK:] * decay, 0.0).astype(BF16)
                qg_ref[d, rows, :] = (qc * eg).astype(BF16)
                kg_ref[d, rows, :] = kc * jnp.exp(gl - gb)
                egl_ref[d, pl.ds(n, 1), :] = jnp.exp(gl)
                systems.append((a_low, jnp.concatenate([vc * bbs[d], kbs[d] * eg], axis=1), d, rows))
        return systems

    lax.fori_loop(0, nc // (n_groups * GDN_PACK // 2), precompute, 0)

    o_ref[...] = jnp.zeros_like(o_ref)

    def scan(i, carry):
        out = []
        for d, s in enumerate(carry):
            n = i if d == 0 else nc - 1 - i
            rows = pl.ds(pl.multiple_of(n * CHUNK, CHUNK), CHUNK)
            v_new = u_ref[d, rows, :] - _bdot(w_ref[d, rows, :], s)
            o_ref[rows, :] += _bdot(qg_ref[d, rows, :], s) + _bdot(attn_ref[d, rows, :], v_new)
            out.append(s * egl_ref[d, pl.ds(n, 1), :] + _bdot_tn(kg_ref[d, rows, :], v_new))
        return tuple(out)

    s_f, s_b = lax.fori_loop(0, nc, scan, (s0_ref[0, 0, 0], s0_ref[0, 1, 0]))
    sout_ref[0, 0, 0] = s_f
    sout_ref[0, 1, 0] = s_b
    y_ref[...] = _rms(o_ref[...], ng_ref[...]) * _silu(z_ref[...])


def _gdn(qkv_n, z, ab, alog, dtb, s0, ng, nseq, seq_len, row_off):
    blk = lambda off: pl.BlockSpec((seq_len, LANES), lambda b, h: (b + row_off, h + off))
    vec = pl.BlockSpec((1, LANES), lambda b, h: (0, 0))
    st_spec = pl.BlockSpec((1, 2, 1, DK_C, DV_C), lambda b, h: (b, 0, h, 0, 0))
    nc = seq_len // CHUNK
    both = lambda width, dtype: pltpu.VMEM((2, seq_len, width), dtype)
    return pl.pallas_call(
        functools.partial(_gdn_kernel, seq_len=seq_len),
        grid=(nseq, H_C),
        in_specs=[blk(0), blk(H_C), blk(2 * H_C), blk(0),
                  pl.BlockSpec((seq_len, LANES), lambda b, h: (b + row_off, 0)),
                  vec, vec, st_spec, vec],
        out_specs=[pl.BlockSpec((seq_len, LANES), lambda b, h: (b, h)), st_spec],
        out_shape=[jax.ShapeDtypeStruct((nseq * seq_len, H_C * DV_C), F32),
                   jax.ShapeDtypeStruct((nseq, 2, H_C, DK_C, DV_C), F32)],
        scratch_shapes=[both(LANES, F32), both(LANES, F32), both(DV_C, F32), both(DK_C, BF16),
                        both(CHUNK, BF16), both(DK_C, BF16), both(DK_C, F32),
                        pltpu.VMEM((2, max(nc, SUBLANES), LANES), F32), pltpu.VMEM((seq_len, DV_C), F32)],
        compiler_params=_cparams(("arbitrary", "arbitrary")),
        name="gated_deltanet",
    )(qkv_n, qkv_n, qkv_n, z, ab, alog, dtb, s0, ng)


def _out_proj_kernel(ya_ref, yb_ref, x_ref, mod_ref, g_ref, wo_ref, wq_ref, x1_ref, h2_ref, qp_ref):
    half = ya_ref.shape[1]
    m = mod_ref[0]
    y = (jnp.dot(ya_ref[...].astype(BF16), wo_ref[0:half, :], preferred_element_type=F32)
         + jnp.dot(yb_ref[...].astype(BF16), wo_ref[half:, :], preferred_element_type=F32))
    x1 = x_ref[...] + m[2:3] * y
    x1_ref[...] = x1
    h2 = (_rms(x1, g_ref[...]) * (1.0 + m[4:5]) + m[3:4]).astype(BF16)
    h2_ref[...] = h2
    qp_ref[...] = jnp.dot(h2, wq_ref[...], preferred_element_type=F32)


def _out_proj(st, ya, yb, x, mod, g, wo, wq, tm):
    t, d = x.shape
    half = ya.shape[1]
    nq = wq.shape[1]
    return pl.pallas_call(
        _out_proj_kernel,
        grid=(t // tm,),
        in_specs=[pl.BlockSpec((tm, half), lambda i: (i, 0)),
                  pl.BlockSpec((tm, half), lambda i: (i, 0)),
                  pl.BlockSpec((tm, d), lambda i: (i, 0)),
                  pl.BlockSpec((1, 6, d), lambda i: (st.mod_row(i, tm), 0, 0)),
                  pl.BlockSpec((1, d), lambda i: (0, 0)),
                  pl.BlockSpec((2 * half, d), lambda i: (0, 0)),
                  pl.BlockSpec((d, nq), lambda i: (0, 0))],
        out_specs=[pl.BlockSpec((tm, d), lambda i: (i, 0)),
                   pl.BlockSpec((tm, d), lambda i: (i, 0)),
                   pl.BlockSpec((tm, nq), lambda i: (i, 0))],
        out_shape=[jax.ShapeDtypeStruct((t, d), F32), jax.ShapeDtypeStruct((t, d), BF16),
                   jax.ShapeDtypeStruct((t, nq), F32)],
        compiler_params=_cparams(("arbitrary",)),
        name="out_proj",
    )(ya, yb, x, mod, g, wo, wq)


def _candidate_rows():
    groups = ([(0, k2) for k2 in range(16)], [(1, k2) for k2 in range(8)], [(2, k2) for k2 in range(8)],
              [(3, k2) for k2 in range(8)], [(k1, 0) for k1 in range(16)], [(k1, 1) for k1 in range(8)],
              [(k1, 2) for k1 in range(8)])
    rows, seen = [], set()
    for grp in groups:
        for k1, k2 in grp:
            ok = (k1 + 1) * (k2 + 1) <= TOPK and (k1, k2) not in seen
            if ok:
                seen.add((k1, k2))
            rows.append(float(k1 * TOPK + k2) if ok else CAND_INVALID)
    return rows


CAND_INVALID = float(TOPK * TOPK)
RANK_MARK = 2.0 ** 100
RANK_STEP = 2.0 ** 95


def _pack_candidates(a1, a2, op):
    return jnp.concatenate([op(a1[0:1], a2), op(a1[1:2], a2[0:8]), op(a1[2:3], a2[0:8]),
                            op(a1[3:4], a2[0:8]), op(a1, a2[0:1]), op(a1[0:8], a2[1:2]),
                            op(a1[0:8], a2[2:3])], axis=0)


def _top16(s, vals_ref, exact):
    n = s.shape[0]
    if not exact:
        for kk in range(TOPK):
            m = jnp.max(s, axis=0, keepdims=True)
            vals_ref[kk:kk + 1, :] = m
            s = jnp.where(s == m, -(RANK_MARK + kk * RANK_STEP), s)
        return jnp.where(s < -0.5 * RANK_MARK, (-s - RANK_MARK) * (1.0 / RANK_STEP), float(TOPK))
    iota = lax.broadcasted_iota(jnp.int32, s.shape, 0).astype(F32)
    rank = jnp.full(s.shape, float(TOPK), F32)
    for kk in range(TOPK):
        m = jnp.max(s, axis=0, keepdims=True)
        idx = jnp.min(jnp.where(s == m, iota, float(n)), axis=0, keepdims=True)
        hit = iota == idx
        rank = jnp.where(hit, float(kk), rank)
        vals_ref[kk:kk + 1, :] = m
        s = jnp.where(hit, NEG_INF, s)
    return rank


def _peer_select_kernel(q_ref, keys_ref, cflat_ref, e1_ref, cnt_ref, e2_ref, r2_ref, v1_ref, v2_ref, *, tt):
    nk = N_KEYS
    cflat = cflat_ref[...]
    valid = cflat < CAND_INVALID
    row16 = lax.broadcasted_iota(jnp.int32, (TOPK, LANES), 0)

    def strip_body(si, carry):
        t0 = pl.multiple_of(si * LANES, LANES)
        tsl = pl.ds(t0, LANES)
        s1 = _bdot_nt(keys_ref[0, 0], q_ref[tsl, 0:nk])
        s2 = _bdot_nt(keys_ref[0, 1], q_ref[tsl, nk:2 * nk])

        def compute(exact):
            rank1 = _top16(s1, v1_ref, exact)
            rank2 = _top16(s2, v2_ref, exact)
            v1 = v1_ref[...]
            v2 = v2_ref[...]
            cand = jnp.where(valid, _pack_candidates(v1, v2, jnp.add), NEG_INF)
            prod = _pack_candidates(jnp.exp(v1 - v1[0:1]), jnp.exp(v2 - v2[0:1]), jnp.multiply)
            sel = jnp.zeros(cand.shape, F32)
            for _ in range(TOPK):
                m = jnp.max(cand, axis=0, keepdims=True)
                if exact:
                    idx = jnp.min(jnp.where(cand == m, cflat, CAND_INVALID), axis=0, keepdims=True)
                    hit = cflat == idx
                else:
                    hit = cand == m
                sel = jnp.where(hit, 1.0, sel)
                cand = jnp.where(hit, NEG_INF, cand)
            zsum = jnp.sum(sel * prod, axis=0, keepdims=True)
            c_all = sel[40:56] + jnp.concatenate([sel[56:64] + sel[64:72], jnp.zeros((8, LANES), F32)], axis=0)
            for k1, (lo, hi) in enumerate(((0, 16), (16, 24), (24, 32), (32, 40))):
                c_all = jnp.where(row16 == k1, jnp.sum(sel[lo:hi], axis=0, keepdims=True), c_all)
            cnt = jnp.zeros(rank1.shape, F32)
            for k1 in range(TOPK):
                cnt = jnp.where(rank1 == float(k1), c_all[k1:k1 + 1], cnt)
            in1 = rank1 < float(TOPK)
            in2 = rank2 < float(TOPK)
            e1_ref[0, :, tsl] = jnp.where(in1, jnp.exp(s1 - v1[0:1]), 0.0) / zsum
            cnt_ref[0, :, tsl] = cnt
            e2_ref[:, tsl] = jnp.where(in2, jnp.exp(s2 - v2[0:1]), 0.0).astype(e2_ref.dtype)
            r2_ref[:, tsl] = rank2.astype(r2_ref.dtype)
            n1 = jnp.sum(jnp.where(in1, 1.0, 0.0), axis=0, keepdims=True)
            n2 = jnp.sum(jnp.where(in2, 1.0, 0.0), axis=0, keepdims=True)
            nc = jnp.sum(sel, axis=0, keepdims=True)
            want = float(TOPK)
            return jnp.abs(n1 - want) + jnp.abs(n2 - want) + jnp.abs(nc - want)

        tied = compute(False)

        @pl.when(jnp.max(tied) > 0.0)
        def _():
            compute(True)

        return carry

    lax.fori_loop(0, tt // LANES, strip_body, 0)


def _peer_select(qp, keys, tt):
    t = qp.shape[0]
    rows = _candidate_rows()
    cflat = jnp.broadcast_to(jnp.asarray(rows, F32)[:, None], (len(rows), LANES))
    out_spec = pl.BlockSpec((1, N_KEYS, tt), lambda i, h: (h, 0, i))
    f32_sds = jax.ShapeDtypeStruct((H_P, N_KEYS, t), F32)
    flat_spec = pl.BlockSpec((N_KEYS, tt), lambda i, h: (h, i))
    flat_sds = jax.ShapeDtypeStruct((H_P * N_KEYS, t), F32)
    return pl.pallas_call(
        functools.partial(_peer_select_kernel, tt=tt),
        grid=(t // tt, H_P),
        in_specs=[pl.BlockSpec((tt, 2 * N_KEYS), lambda i, h: (i, h)),
                  pl.BlockSpec((1, 2, N_KEYS, N_KEYS), lambda i, h: (h, 0, 0, 0)),
                  pl.BlockSpec(cflat.shape, lambda i, h: (0, 0))],
        out_specs=[out_spec, out_spec, flat_spec, flat_spec],
        out_shape=[f32_sds, f32_sds, flat_sds, flat_sds],
        scratch_shapes=[pltpu.VMEM((TOPK, LANES), F32), pltpu.VMEM((TOPK, LANES), F32)],
        compiler_params=_cparams(("arbitrary", "arbitrary")),
        name="peer_select",
    )(qp, keys, cflat)


PEER_SPLIT = 1


def _peer_dense_kernel(h_ref, *refs, na):
    u_refs = refs[:PEER_SPLIT]
    vt_refs = refs[PEER_SPLIT:2 * PEER_SPLIT]
    (e1_ref, cnt_ref, e2_ref, r2_ref, x_ref, mod_ref, o_ref,
     acc_ref, s_ref, p_ref, e2b_ref, r2b_ref, ht_ref) = refs[2 * PEER_SPLIT:]
    j = pl.program_id(1)
    tt = s_ref.shape[1]

    @pl.when(j == 0)
    def _():
        acc_ref[...] = jnp.zeros_like(acc_ref)
        e2b_ref[...] = e2_ref[...].astype(BF16)
        r2b_ref[...] = r2_ref[...].astype(BF16)
        ht_ref[...] = h_ref[...].astype(F32).T.astype(BF16)

    sub = u_refs[0].shape[0]
    for k, u_ref in enumerate(u_refs):
        s_ref[k * sub:(k + 1) * sub, :] = jnp.dot(u_ref[...], ht_ref[...], preferred_element_type=F32)

    for aa in range(na):
        rows = slice(aa * N_KEYS, (aa + 1) * N_KEYS)
        for tg in range(tt // LANES):
            tsl = slice(tg * LANES, (tg + 1) * LANES)
            gate = None
            for hh in range(H_P):
                cnt_row = cnt_ref[hh, aa:aa + 1, tsl].astype(BF16)
                e1_row = e1_ref[hh, aa:aa + 1, tsl].astype(BF16)
                hsl = slice(hh * N_KEYS, (hh + 1) * N_KEYS)
                term = jnp.where(r2b_ref[hsl, tsl] < cnt_row, e2b_ref[hsl, tsl], 0.0) * e1_row
                gate = term if gate is None else gate + term
            p_ref[rows, tsl] = gate * _gelu_tanh(s_ref[rows, tsl]).astype(BF16)

    dsub = vt_refs[0].shape[0]
    for k, vt_ref in enumerate(vt_refs):
        acc_ref[k * dsub:(k + 1) * dsub, :] += jnp.dot(vt_ref[...], p_ref[...], preferred_element_type=F32)

    @pl.when(j == pl.num_programs(1) - 1)
    def _():
        o_ref[...] = x_ref[...] + mod_ref[0][5:6] * acc_ref[...].T


def _peer_dense(st, h2, u, vt, e1, cnt, e2, r2, x, mod, tt, et):
    t, d = x.shape
    n_exp = u.shape[0]
    na = et // N_KEYS
    u_specs = [pl.BlockSpec((et // PEER_SPLIT, d),
                            functools.partial(lambda i, j, k: (j * PEER_SPLIT + k, 0), k=k))
               for k in range(PEER_SPLIT)]
    vt_specs = [pl.BlockSpec((d // PEER_SPLIT, et), functools.partial(lambda i, j, k: (k, j), k=k))
                for k in range(PEER_SPLIT)]
    row_spec = pl.BlockSpec((H_P, na, tt), lambda i, j: (0, j, i))
    full_spec = pl.BlockSpec((H_P * N_KEYS, tt), lambda i, j: (0, i))
    return pl.pallas_call(
        functools.partial(_peer_dense_kernel, na=na),
        grid=(t // tt, n_exp // et),
        in_specs=[pl.BlockSpec((tt, d), lambda i, j: (i, 0))] + u_specs + vt_specs + [
                  row_spec, row_spec, full_spec, full_spec,
                  pl.BlockSpec((tt, d), lambda i, j: (i, 0)),
                  pl.BlockSpec((1, 6, d), lambda i, j: (st.mod_row(i, tt), 0, 0))],
        out_specs=pl.BlockSpec((tt, d), lambda i, j: (i, 0)),
        out_shape=jax.ShapeDtypeStruct((t, d), F32),
        scratch_shapes=[pltpu.VMEM((d, tt), F32), pltpu.VMEM((et, tt), F32), pltpu.VMEM((et, tt), BF16),
                        pltpu.VMEM((H_P * N_KEYS, tt), BF16), pltpu.VMEM((H_P * N_KEYS, tt), BF16),
                        pltpu.VMEM((d, tt), BF16)],
        compiler_params=_cparams(("arbitrary", "arbitrary")),
        name="peer_dense",
    )(h2, *([u] * PEER_SPLIT), *([vt] * PEER_SPLIT), e1, cnt, e2, r2, x, mod)


def _final_norm_kernel(x_ref, g_ref, o_ref):
    o_ref[...] = _rms(x_ref[...], g_ref[...])


def _final_norm(x, g, tm):
    t, d = x.shape
    return pl.pallas_call(
        _final_norm_kernel,
        grid=(t // tm,),
        in_specs=[pl.BlockSpec((tm, d), lambda i: (i, 0)), pl.BlockSpec((1, d), lambda i: (0, 0))],
        out_specs=pl.BlockSpec((tm, d), lambda i: (i, 0)),
        out_shape=jax.ShapeDtypeStruct((t, d), F32),
        compiler_params=_cparams(("arbitrary",)),
        name="final_norm",
    )(x, g)


def _rope_tables(seq_len):
    pos = jnp.arange(seq_len)
    rowp = (pos // GRID_W).astype(F32)
    colp = (pos % GRID_W).astype(F32)
    nf = DK_A // 4
    freqs = ROPE_BASE ** (-jnp.arange(nf, dtype=F32) / nf)
    ang = jnp.concatenate([rowp[:, None] * freqs, colp[:, None] * freqs], axis=-1)
    cos = jnp.cos(ang)
    sin = jnp.sin(ang)
    cos_t = jnp.tile(cos, (1, 4))
    sin_t = jnp.tile(jnp.concatenate([-sin, sin], axis=-1), (1, 2))
    return cos_t, sin_t


def _pad_rows(w, rows):
    return jnp.concatenate([w, jnp.zeros((rows - w.shape[0], w.shape[1]), w.dtype)], axis=0)


def _lane_row(vals):
    flat = vals.reshape(-1).astype(F32)
    return jnp.concatenate([flat, jnp.zeros((LANES - flat.shape[0],), F32)])[None, :]


def kernel(x_prompt, x_sample, state_ret, state_gdn, c, c_ctx, ada_w, ada_b, norm_mix_g, norm_ffn_g,
           final_norm_g, ev_w_in, ev_w_out, ret_gamma_logit, ret_norm_g, sc_conv_w, od_w_in, od_w_out,
           gdn_conv_w, gdn_a_log, gdn_dt_bias, gdn_norm_g, cf_dw_w, cf_dw_b, cf_ln_g, cf_ln_b,
           peer_wq, peer_keys, peer_u, peer_v):
    bp, lp, d = x_prompt.shape
    bs, ls, _ = x_sample.shape
    depth = ada_w.shape[0]
    st = _Streams(bp, lp, bs, ls)
    w_a = H_A * DV_A
    w_b = d - w_a
    w_c = H_C * DV_C
    w_d = d - w_c
    tm = 512
    peer_tt = 512
    peer_et = 1024
    select_tt = 512

    x = jnp.concatenate([x_prompt.reshape(st.tp, d), x_sample.reshape(st.ts, d)], axis=0)
    cvec = jnp.concatenate([c_ctx[None, :], c, jnp.zeros((N_MOD_ROWS - 1 - bs, d), F32)], axis=0)
    mods = _modulation(cvec, ada_w, ada_b).reshape(depth, N_MOD_ROWS, 6, d)
    cos_t, sin_t = _rope_tables(ls)
    zero_ret = jnp.zeros((bp, 2, H_A, DK_A, DV_A), F32)
    zero_gdn = jnp.zeros((bp, 2, H_C, DK_C, DV_C), F32)

    ret_new, gdn_new = [], []
    for l in range(depth):
        i = l // 2
        mod = mods[l]
        g1 = norm_mix_g[l][None, :]
        if l % 2 == 0:
            widths = (H_A * DK_A, H_A * DK_A, w_a, w_a, w_b, w_b, w_b)
            q, k, v, g, bg, cg, hb = _norm_proj(st, x, mod, g1, ev_w_in[i].astype(BF16), widths, tm)
            gam = jnp.broadcast_to(ret_gamma_logit[i][:, :, None, None], (2, H_A, CHUNK, DV_A))
            ng = ret_norm_g[i][:, None, :]
            ya_p, s_new = _retention(q, k, v, g, zero_ret, gam, ng, cos_t[:lp], sin_t[:lp],
                                     bp, lp, 0, False)
            ya_s, _ = _retention(q, k, v, g, state_ret[:, i], gam, ng, cos_t, sin_t,
                                 bs, ls, st.tp // ls, True)
            ret_new.append(s_new)
            ya = jnp.concatenate([ya_p, ya_s], axis=0)
            yb = _short_gated_conv(st, bg, cg, hb, _pad_rows(sc_conv_w[i], SUBLANES))
            w_out = ev_w_out[i]
        else:
            n_gate = 2 * 2 * H_C
            w_in = od_w_in[i]
            o_ab = 4 * w_c
            w_main = jnp.concatenate([w_in[:, :o_ab], w_in[:, o_ab + n_gate:],
                                      w_in[:, o_ab:o_ab + n_gate],
                                      jnp.zeros((d, LANES - n_gate), F32)], axis=1).astype(BF16)
            widths = (3 * w_c, w_c, 2 * w_d, LANES)
            qkv, z, glu, ab = _norm_proj(st, x, mod, g1, w_main, widths, tm)
            qkv_n = _qkv_conv(st, qkv, _pad_rows(gdn_conv_w[i], SUBLANES))
            alog = _lane_row(gdn_a_log[i])
            dtb = _lane_row(gdn_dt_bias[i])
            ng = gdn_norm_g[i][None, :]
            yc_p, s_new = _gdn(qkv_n, z, ab, alog, dtb, zero_gdn, ng, bp, lp, 0)
            yc_s, _ = _gdn(qkv_n, z, ab, alog, dtb, state_gdn[:, i], ng, bs, ls, st.tp // ls)
            gdn_new.append(s_new)
            ya = jnp.concatenate([yc_p, yc_s], axis=0)
            yb = _conformer(st, glu, _pad_rows(cf_dw_w[i], 2 * CF_HALO), cf_dw_b[i][None, :],
                            cf_ln_g[i][None, :], cf_ln_b[i][None, :])
            w_out = od_w_out[i]
        x, h2, qp = _out_proj(st, ya, yb, x, mod, norm_ffn_g[l][None, :], w_out.astype(BF16),
                              peer_wq[l].astype(BF16), tm)
        e1, cnt, e2, r2 = _peer_select(qp, peer_keys[l].astype(BF16), select_tt)
        x = _peer_dense(st, h2, peer_u[l].astype(BF16), peer_v[l].astype(BF16).T, e1, cnt, e2, r2,
                        x, mod, peer_tt, peer_et)

    y = _final_norm(x, final_norm_g[None, :], tm)
    y_prompt = y[:st.tp].reshape(bp, lp, d)
    y_sample = y[st.tp:].reshape(bs, ls, d)
    new_state_ret = jnp.stack(ret_new, axis=1).astype(x_prompt.dtype)
    new_state_gdn = jnp.stack(gdn_new, axis=1).astype(x_prompt.dtype)
    return (y_prompt, y_sample, new_state_ret, new_state_gdn)
```

```python
import functools
import math

import jax
import jax.numpy as jnp
from jax import lax
from jax.experimental import pallas as pl
from jax.experimental.pallas import tpu as pltpu

F32 = jnp.float32
BF16 = jnp.bfloat16
HIGHEST = lax.Precision.HIGHEST

EPS = 1e-6
CHUNK = 64
GRID_W = 64
ROPE_BASE = 10000.0
H_A, DK_A, DV_A = 4, 64, 128
H_C, DK_C, DV_C = 4, 128, 128
SC_K, QKV_K, CF_K = 3, 3, 31
N_KEYS, H_P, TOPK = 128, 8, 16
N_MOD_ROWS = 16
LANES = 128
SUBLANES = 8
VMEM_LIMIT = 56 * 1024 * 1024
CONV_TILE = 256
NEG_INF = float("-inf")


def _cparams(sem):
    return pltpu.CompilerParams(dimension_semantics=sem, vmem_limit_bytes=VMEM_LIMIT)


def _bdot(a, b):
    return jnp.dot(a.astype(BF16), b.astype(BF16), preferred_element_type=F32)


def _bdot_nt(a, b):
    return lax.dot_general(a.astype(BF16), b.astype(BF16), (((1,), (1,)), ((), ())),
                           preferred_element_type=F32)


def _bdot_tn(a, b):
    return lax.dot_general(a.astype(BF16), b.astype(BF16), (((0,), (0,)), ((), ())),
                           preferred_element_type=F32)


def _hdot(a, b):
    return jnp.dot(a, b, precision=HIGHEST, preferred_element_type=F32)


def _sigmoid(x):
    return 1.0 / (1.0 + jnp.exp(-x))


def _silu(x):
    return x * _sigmoid(x)


def _softplus(x):
    return jnp.maximum(x, 0.0) + jnp.log1p(jnp.exp(-jnp.abs(x)))


def _log_sigmoid(x):
    return -_softplus(-x)


def _gelu_tanh(x):
    c = math.sqrt(2.0 / math.pi)
    return 0.5 * x * (1.0 + jnp.tanh(c * (x + 0.044715 * (x * x * x))))


def _rms(x, g):
    return x * lax.rsqrt(jnp.mean(x * x, axis=-1, keepdims=True) + EPS) * g


class _Streams:
    def __init__(self, bp, lp, bs, ls):
        self.bp, self.lp, self.bs, self.ls = bp, lp, bs, ls
        self.tp, self.ts = bp * lp, bs * ls
        self.t = self.tp + self.ts

    def mod_row(self, i, tile):
        tiles_p = self.tp // tile
        per_seq = self.ls // tile
        return jnp.where(i < tiles_p, 0, 1 + (i - tiles_p) // per_seq)

    def halo_flags(self, i, tile):
        tiles_p = self.tp // tile
        per_p = self.lp // tile
        per_s = self.ls // tile
        in_p = i < tiles_p
        jp = i % per_p
        js = (i - tiles_p) % per_s
        has_prev = jnp.where(in_p, jp > 0, js > 0)
        has_next = jnp.where(in_p, jp < per_p - 1, js < per_s - 1)
        return has_prev, has_next


def _mod_kernel(c_ref, w_ref, b_ref, o_ref):
    s = _silu(c_ref[...])
    o_ref[0] = _hdot(s, w_ref[0]) + b_ref[0]


def _modulation(cvec, ada_w, ada_b):
    depth, d, d6 = ada_w.shape
    nj = d6 // d
    return pl.pallas_call(
        _mod_kernel,
        grid=(depth, nj),
        in_specs=[pl.BlockSpec((N_MOD_ROWS, d), lambda l, j: (0, 0)),
                  pl.BlockSpec((1, d, d), lambda l, j: (l, 0, j)),
                  pl.BlockSpec((1, 1, d), lambda l, j: (l, 0, j))],
        out_specs=pl.BlockSpec((1, N_MOD_ROWS, d), lambda l, j: (l, 0, j)),
        out_shape=jax.ShapeDtypeStruct((depth, N_MOD_ROWS, d6), F32),
        compiler_params=_cparams(("arbitrary", "arbitrary")),
        name="modulation",
    )(cvec, ada_w, ada_b.reshape(depth, 1, d6))


def _norm_proj_kernel(x_ref, mod_ref, g_ref, w_ref, *o_refs, widths):
    m = mod_ref[0]
    h = _rms(x_ref[...], g_ref[...]) * (1.0 + m[1:2]) + m[0:1]
    p = jnp.dot(h.astype(BF16), w_ref[...], preferred_element_type=F32)
    off = 0
    for o_ref, wd in zip(o_refs, widths):
        o_ref[...] = p[:, off:off + wd]
        off += wd


def _norm_proj(st, x, mod, g, w, widths, tm):
    t, d = x.shape
    n = w.shape[1]
    return pl.pallas_call(
        functools.partial(_norm_proj_kernel, widths=widths),
        grid=(t // tm,),
        in_specs=[pl.BlockSpec((tm, d), lambda i: (i, 0)),
                  pl.BlockSpec((1, 6, d), lambda i: (st.mod_row(i, tm), 0, 0)),
                  pl.BlockSpec((1, d), lambda i: (0, 0)),
                  pl.BlockSpec((d, n), lambda i: (0, 0))],
        out_specs=[pl.BlockSpec((tm, wd), lambda i: (i, 0)) for wd in widths],
        out_shape=[jax.ShapeDtypeStruct((t, wd), F32) for wd in widths],
        compiler_params=_cparams(("arbitrary",)),
        name="norm_proj",
    )(x, mod, g, w)


def _ret_kernel(q_ref, k_ref, v_ref, g_ref, s0_ref, gam_ref, ng_ref, cos_ref, sin_ref,
                ya_ref, sout_ref, qs_ref, ks_ref, o_ref, sf_ref, sb_ref, *, seq_len, latent):
    nc = seq_len // CHUNK
    q = q_ref[...]
    k = k_ref[...] * (DK_A ** -0.5)
    if latent:
        lane = lax.broadcasted_iota(jnp.int32, q.shape, 1)
        first_half = (lane % DK_A) < (DK_A // 2)
        cos = cos_ref[...]
        sin = sin_ref[...]

        def rope(x):
            partner = jnp.where(first_half, pltpu.roll(x, LANES - DK_A // 2, 1),
                                pltpu.roll(x, DK_A // 2, 1))
            return x * cos + partner * sin

        q = rope(q)
        k = rope(k)
    qs_ref[...] = q
    ks_ref[...] = k

    cat = jnp.concatenate
    kw = 2 * DK_A
    vw = 2 * DV_A
    row = lax.broadcasted_iota(jnp.int32, (CHUNK, kw), 0).astype(F32)
    lane_k = lax.broadcasted_iota(jnp.int32, (CHUNK, kw), 1)
    lane_v = lax.broadcasted_iota(jnp.int32, (CHUNK, vw), 1)
    head0_k = lane_k < DK_A
    head0_v = lane_v < DV_A
    lg = [[_log_sigmoid(gam_ref[d, hh]) for hh in range(2)] for d in range(2)]
    lgf_k = jnp.where(head0_k, lg[0][0], lg[0][1])
    lgb_k = jnp.where(head0_k, lg[1][0], lg[1][1])
    diff = row - (lane_k % DK_A).astype(F32)
    dcomb = (jnp.where(diff >= 0, jnp.exp(lgf_k * jnp.maximum(diff, 0.0)), 0.0)
             + jnp.where(diff <= 0, jnp.exp(lgb_k * jnp.maximum(-diff, 0.0)), 0.0))
    wend_f = jnp.exp(lgf_k * (CHUNK - 1.0 - row))
    wstart_f = jnp.exp(lgf_k * (row + 1.0))
    wend_b = jnp.exp(lgb_k * row)
    wstart_b = jnp.exp(lgb_k * (CHUNK - row))

    def state_layout(h0, h1):
        zero = jnp.zeros_like(h0)
        return cat([cat([h0, zero], axis=1), cat([zero, h1], axis=1)], axis=0)

    on_diag = state_layout(jnp.ones((DK_A, DV_A), F32), jnp.ones((DK_A, DV_A), F32)) > 0.0
    gch_f = jnp.exp(state_layout(lg[0][0], lg[0][1]) * float(CHUNK))
    gch_b = jnp.exp(state_layout(lg[1][0], lg[1][1]) * float(CHUNK))
    sf_ref[...] = state_layout(s0_ref[0, 0, 0], s0_ref[0, 0, 1])
    sb_ref[...] = state_layout(s0_ref[0, 1, 0], s0_ref[0, 1, 1])
    o_ref[...] = jnp.zeros_like(o_ref)

    def step(i, carry):
        rows = pl.ds(pl.multiple_of(i * CHUNK, CHUNK), CHUNK)
        qn = qs_ref[rows, :]
        kn = ks_ref[rows, :]
        vn = v_ref[rows, :]
        k_bd = cat([jnp.where(head0_k, kn, 0.0), jnp.where(head0_k, 0.0, kn)], axis=0)
        v_bd = cat([jnp.where(head0_v, vn, 0.0), jnp.where(head0_v, 0.0, vn)], axis=0)
        a = _bdot_nt(qn, k_bd) * dcomb
        s_f = sf_ref[...]
        o_ref[rows, :] += _bdot(cat([a, qn * wstart_f], axis=1), cat([v_bd, s_f], axis=0))
        sf_ref[...] = s_f * gch_f + jnp.where(on_diag, _bdot_tn(kn * wend_f, vn), 0.0)
        rows = pl.ds(pl.multiple_of((nc - 1 - i) * CHUNK, CHUNK), CHUNK)
        qn = qs_ref[rows, :]
        kn = ks_ref[rows, :]
        vn = v_ref[rows, :]
        s_b = sb_ref[...]
        o_ref[rows, :] += _bdot(qn * wstart_b, s_b)
        sb_ref[...] = s_b * gch_b + jnp.where(on_diag, _bdot_tn(kn * wend_b, vn), 0.0)
        return carry

    lax.fori_loop(0, nc, step, 0)
    for d, s_ref in enumerate((sf_ref, sb_ref)):
        sout_ref[0, d, 0] = s_ref[:DK_A, :DV_A]
        sout_ref[0, d, 1] = s_ref[DK_A:, DV_A:]

    for hh in range(2):
        vsl = slice(hh * DV_A, (hh + 1) * DV_A)
        ya_ref[:, vsl] = _silu(g_ref[:, vsl]) * _rms(o_ref[:, vsl], ng_ref[hh])


def _retention(q, k, v, g, s0, gam, ng, cos_t, sin_t, nseq, seq_len, row_off, latent):
    w2 = 2 * DK_A
    v2 = 2 * DV_A
    return pl.pallas_call(
        functools.partial(_ret_kernel, seq_len=seq_len, latent=latent),
        grid=(nseq, H_A // 2),
        in_specs=[pl.BlockSpec((seq_len, w2), lambda b, p: (b + row_off, p)),
                  pl.BlockSpec((seq_len, w2), lambda b, p: (b + row_off, p)),
                  pl.BlockSpec((seq_len, v2), lambda b, p: (b + row_off, p)),
                  pl.BlockSpec((seq_len, v2), lambda b, p: (b + row_off, p)),
                  pl.BlockSpec((1, 2, 2, DK_A, DV_A), lambda b, p: (b, 0, p, 0, 0)),
                  pl.BlockSpec((2, 2, CHUNK, DV_A), lambda b, p: (0, p, 0, 0)),
                  pl.BlockSpec((2, 1, DV_A), lambda b, p: (p, 0, 0)),
                  pl.BlockSpec((seq_len, w2), lambda b, p: (0, 0)),
                  pl.BlockSpec((seq_len, w2), lambda b, p: (0, 0))],
        out_specs=[pl.BlockSpec((seq_len, v2), lambda b, p: (b, p)),
                   pl.BlockSpec((1, 2, 2, DK_A, DV_A), lambda b, p: (b, 0, p, 0, 0))],
        out_shape=[jax.ShapeDtypeStruct((nseq * seq_len, H_A * DV_A), F32),
                   jax.ShapeDtypeStruct((nseq, 2, H_A, DK_A, DV_A), F32)],
        scratch_shapes=[pltpu.VMEM((seq_len, w2), F32), pltpu.VMEM((seq_len, w2), F32),
                        pltpu.VMEM((seq_len, v2), F32), pltpu.VMEM((w2, v2), F32), pltpu.VMEM((w2, v2), F32)],
        compiler_params=_cparams(("arbitrary", "arbitrary")),
        name="retention",
    )(q, k, v, g, s0, gam, ng, cos_t, sin_t)


def _fill_padded(pad_ref, cur, prev, nxt, has_prev, has_next, halo):
    tile = cur.shape[0]
    pad_ref[0:halo, :] = jnp.where(has_prev, prev, 0.0)
    pad_ref[halo:halo + tile, :] = cur
    pad_ref[halo + tile:halo + tile + halo, :] = jnp.where(has_next, nxt, 0.0)


def _conv_taps(pad_ref, w_ref, ntaps, halo, tile):
    base = halo - ntaps // 2
    acc = w_ref[0:1, :] * pad_ref[base:base + tile, :]
    for kk in range(1, ntaps):
        acc = acc + w_ref[kk:kk + 1, :] * pad_ref[base + kk:base + kk + tile, :]
    return acc


def _halo_specs(width, col_map, halo):
    per = CONV_TILE // halo

    def cur(i, j):
        return (i, col_map(j))

    def prev(i, j):
        return (jnp.maximum(i * per - 1, 0), col_map(j))

    def make_next(nblk):
        def nxt(i, j):
            return (jnp.minimum((i + 1) * per, nblk - 1), col_map(j))
        return nxt

    return cur, prev, make_next


def _sconv_kernel(bg_ref, cg_ref, cgp_ref, cgn_ref, hb_ref, hbp_ref, hbn_ref, w_ref, o_ref, pad_ref, *, st):
    has_prev, has_next = st.halo_flags(pl.program_id(0), CONV_TILE)
    _fill_padded(pad_ref, cg_ref[...] * hb_ref[...], cgp_ref[...] * hbp_ref[...],
                 cgn_ref[...] * hbn_ref[...], has_prev, has_next, SUBLANES)
    o_ref[...] = bg_ref[...] * _conv_taps(pad_ref, w_ref, SC_K, SUBLANES, CONV_TILE)


CONV_COLS = 512


def _short_gated_conv(st, bg, cg, hb, w):
    t, c = bg.shape
    halo = SUBLANES
    cur, prev, make_next = _halo_specs(c, lambda j: j, halo)
    nxt = make_next(t // halo)
    tile_spec = pl.BlockSpec((CONV_TILE, CONV_COLS), cur)
    prev_spec = pl.BlockSpec((halo, CONV_COLS), prev)
    next_spec = pl.BlockSpec((halo, CONV_COLS), nxt)
    return pl.pallas_call(
        functools.partial(_sconv_kernel, st=st),
        grid=(t // CONV_TILE, c // CONV_COLS),
        in_specs=[tile_spec, tile_spec, prev_spec, next_spec, tile_spec, prev_spec, next_spec,
                  pl.BlockSpec((SUBLANES, CONV_COLS), lambda i, j: (0, j))],
        out_specs=tile_spec,
        out_shape=jax.ShapeDtypeStruct((t, c), F32),
        scratch_shapes=[pltpu.VMEM((CONV_TILE + 2 * halo, CONV_COLS), F32)],
        compiler_params=_cparams(("arbitrary", "arbitrary")),
        name="short_gated_conv",
    )(bg, cg, cg, cg, hb, hb, hb, w)


def _qkv_conv_kernel(x_ref, xp_ref, xn_ref, w_ref, o_ref, pad_ref, *, st):
    has_prev, has_next = st.halo_flags(pl.program_id(0), CONV_TILE)
    _fill_padded(pad_ref, x_ref[...], xp_ref[...], xn_ref[...], has_prev, has_next, SUBLANES)
    s = _silu(_conv_taps(pad_ref, w_ref, QKV_K, SUBLANES, CONV_TILE))
    j = pl.program_id(1)
    for hh in range(CONV_COLS // DK_C):
        hsl = slice(hh * DK_C, (hh + 1) * DK_C)
        sh = s[:, hsl]
        nrm = sh * lax.rsqrt(jnp.sum(sh * sh, axis=-1, keepdims=True) + EPS)
        o_ref[:, hsl] = jnp.where(j == 0, nrm * (DK_C ** -0.5), jnp.where(j == 1, nrm, sh))


def _qkv_conv(st, qkv, w):
    t, c = qkv.shape
    assert CONV_COLS == H_C * DK_C == H_C * DV_C
    halo = SUBLANES
    cur, prev, make_next = _halo_specs(c, lambda j: j, halo)
    nxt = make_next(t // halo)
    tile_spec = pl.BlockSpec((CONV_TILE, CONV_COLS), cur)
    return pl.pallas_call(
        functools.partial(_qkv_conv_kernel, st=st),
        grid=(t // CONV_TILE, c // CONV_COLS),
        in_specs=[tile_spec, pl.BlockSpec((halo, CONV_COLS), prev), pl.BlockSpec((halo, CONV_COLS), nxt),
                  pl.BlockSpec((SUBLANES, CONV_COLS), lambda i, j: (0, j))],
        out_specs=tile_spec,
        out_shape=jax.ShapeDtypeStruct((t, c), F32),
        scratch_shapes=[pltpu.VMEM((CONV_TILE + 2 * halo, CONV_COLS), F32)],
        compiler_params=_cparams(("arbitrary", "arbitrary")),
        name="qkv_conv",
    )(qkv, qkv, qkv, w)


CF_HALO = 16


def _conformer_kernel(ca_ref, cap_ref, can_ref, cg_ref, cgp_ref, cgn_ref, w_ref, b_ref, lg_ref, lb_ref,
                      o_ref, pad_ref, *, st):
    has_prev, has_next = st.halo_flags(pl.program_id(0), CONV_TILE)
    _fill_padded(pad_ref, ca_ref[...] * _sigmoid(cg_ref[...]), cap_ref[...] * _sigmoid(cgp_ref[...]),
                 can_ref[...] * _sigmoid(cgn_ref[...]), has_prev, has_next, CF_HALO)
    hc = _conv_taps(pad_ref, w_ref, CF_K, CF_HALO, CONV_TILE) + b_ref[...]
    mu = jnp.mean(hc, axis=-1, keepdims=True)
    xc = hc - mu
    y = xc * lax.rsqrt(jnp.mean(xc * xc, axis=-1, keepdims=True) + EPS) * lg_ref[...] + lb_ref[...]
    o_ref[...] = _silu(y)


def _conformer(st, glu, w, b, ln_g, ln_b):
    t, c2 = glu.shape
    c = c2 // 2
    nblk = t // CF_HALO
    per = CONV_TILE // CF_HALO
    vec = pl.BlockSpec((1, c), lambda i: (0, 0))
    return pl.pallas_call(
        functools.partial(_conformer_kernel, st=st),
        grid=(t // CONV_TILE,),
        in_specs=[pl.BlockSpec((CONV_TILE, c), lambda i: (i, 0)),
                  pl.BlockSpec((CF_HALO, c), lambda i: (jnp.maximum(i * per - 1, 0), 0)),
                  pl.BlockSpec((CF_HALO, c), lambda i: (jnp.minimum((i + 1) * per, nblk - 1), 0)),
                  pl.BlockSpec((CONV_TILE, c), lambda i: (i, 1)),
                  pl.BlockSpec((CF_HALO, c), lambda i: (jnp.maximum(i * per - 1, 0), 1)),
                  pl.BlockSpec((CF_HALO, c), lambda i: (jnp.minimum((i + 1) * per, nblk - 1), 1)),
                  pl.BlockSpec((2 * CF_HALO, c), lambda i: (0, 0)), vec, vec, vec],
        out_specs=pl.BlockSpec((CONV_TILE, c), lambda i: (i, 0)),
        out_shape=jax.ShapeDtypeStruct((t, c), F32),
        scratch_shapes=[pltpu.VMEM((CONV_TILE + 2 * CF_HALO, c), F32)],
        compiler_params=_cparams(("arbitrary",)),
        name="conformer_conv",
    )(glu, glu, glu, glu, glu, glu, w, b, ln_g, ln_b)


GDN_PACK = 4
GDN_GROUPS = 4


def _split_bf16(x):
    hi = x.astype(BF16)
    return hi, (x - hi.astype(F32)).astype(BF16)


def _packed_unit_inverses(groups):
    dot = functools.partial(jnp.dot, preferred_element_type=F32)
    cat = jnp.concatenate
    n = len(groups[0])
    c = CHUNK
    w = n * c
    eye = (lax.broadcasted_iota(jnp.int32, (c, c), 0) == lax.broadcasted_iota(jnp.int32, (c, c), 1)).astype(F32)
    lane_block = lax.broadcasted_iota(jnp.int32, (c, 2 * w), 1) % w // c

    def rhs_of(p):
        hi, lo = _split_bf16(p)
        both = cat([hi, lo], axis=1)
        return hi, lo, cat([jnp.where(lane_block == i, both, jnp.zeros_like(both)) for i in range(n)], axis=0)

    def product(m, r):
        return (r[:m, :w] + r[m:, :w]) + (r[:m, w:] + r[m:, w:])

    ps = [cat(a_list, axis=1) for a_list in groups]
    ts = [cat([eye - a for a in a_list], axis=1) for a_list in groups]
    for g, p in enumerate(ps):
        hi, lo, rhs = rhs_of(p)
        ps[g] = product(c, dot(cat([hi, lo], axis=0), rhs))
    steps = int(math.log2(c)) - 1
    for step in range(steps):
        for g in range(len(groups)):
            p_hi, p_lo, rhs = rhs_of(ps[g])
            t_hi, t_lo = _split_bf16(ts[g])
            if step < steps - 1:
                both = product(2 * c, dot(cat([t_hi, p_hi, t_lo, p_lo], axis=0), rhs))
                ts[g] = ts[g] + both[:c]
                ps[g] = both[c:]
            else:
                ts[g] = ts[g] + product(c, dot(cat([t_hi, t_lo], axis=0), rhs))
    return [[t[:, i * c:(i + 1) * c] for i in range(n)] for t in ts]


def _chunk_cumsum(x, reverse):
    row = lax.broadcasted_iota(jnp.int32, x.shape, 0)
    s = 1
    while s < CHUNK:
        if reverse:
            x = x + jnp.where(row < CHUNK - s, pltpu.roll(x, CHUNK - s, 0), 0.0)
        else:
            x = x + jnp.where(row >= s, pltpu.roll(x, s, 0), 0.0)
        s *= 2
    return x


def _gdn_kernel(q_ref, k_ref, v_ref, z_ref, ab_ref, alog_ref, dtb_ref, s0_ref, ng_ref,
                y_ref, sout_ref, g_ref, beta_ref, u_ref, w_ref, attn_ref, qg_ref, kg_ref, egl_ref, o_ref,
                *, seq_len):
    nc = seq_len // CHUNK
    h = pl.program_id(1)
    row = lax.broadcasted_iota(jnp.int32, (CHUNK, CHUNK), 0)
    col = lax.broadcasted_iota(jnp.int32, (CHUNK, CHUNK), 1)
    lane = lax.broadcasted_iota(jnp.int32, (CHUNK, LANES), 1)
    alog = alog_ref[...]
    dtb = dtb_ref[...]

    def gates(n, carry):
        rows = pl.ds(pl.multiple_of(n * CHUNK, CHUNK), CHUNK)
        ab = ab_ref[rows, :]
        g_all = -jnp.exp(alog) * _softplus(ab + dtb)
        b_all = _sigmoid(ab)
        for d in range(2):
            gsel = jnp.sum(jnp.where(lane == d * H_C + h, g_all, 0.0), axis=-1, keepdims=True)
            bsel = jnp.sum(jnp.where(lane == 2 * H_C + d * H_C + h, b_all, 0.0), axis=-1, keepdims=True)
            g_ref[d, rows, :] = jnp.broadcast_to(gsel, (CHUNK, LANES))
            beta_ref[d, rows, :] = jnp.broadcast_to(bsel, (CHUNK, LANES))
        return carry

    lax.fori_loop(0, nc, gates, 0)

    n_groups = min(GDN_GROUPS, nc // (GDN_PACK // 2))

    def precompute(it, carry):
        groups = [prepare_group(it * n_groups + grp) for grp in range(n_groups)]
        inverses = _packed_unit_inverses([[sysm[0] for sysm in systems] for systems in groups])
        for t_invs, systems in zip(inverses, groups):
            for t_inv, (_, rhs, d, rows) in zip(t_invs, systems):
                uw = _bdot(t_inv, rhs)
                u_ref[d, rows, :] = uw[:, :DV_C]
                w_ref[d, rows, :] = uw[:, DV_C:].astype(BF16)
        return carry

    def prepare_group(pair):
        systems = []
        for jc in range(GDN_PACK // 2):
            n = pair * (GDN_PACK // 2) + jc
            rows = pl.ds(pl.multiple_of(n * CHUNK, CHUNK), CHUNK)
            qc = q_ref[rows, :]
            kc = k_ref[rows, :]
            vc = v_ref[rows, :]
            gbs = [_chunk_cumsum(g_ref[d, rows, :], d == 1) for d in range(2)]
            bbs = [beta_ref[d, rows, :] for d in range(2)]
            kbs = [kc * bb for bb in bbs]
            prod = _bdot_nt(jnp.concatenate(kbs + [qc], axis=0), kc)
            for d in range(2):
                incl = (row >= col) if d == 0 else (row <= col)
                strict = (row > col) if d == 0 else (row < col)
                last = CHUNK - 1 if d == 0 else 0
                gb = gbs[d]
                gr = gb[:, :CHUNK].T
                gl = gb[last:last + 1, :]
                decay = jnp.where(incl, jnp.exp(jnp.where(incl, gb[:, :CHUNK] - gr, 0.0)), 0.0)
                eg = jnp.exp(gb)
                a_low = jnp.where(strict, prod[d * CHUNK:(d + 1) * CHUNK] * decay, 0.0)
                attn_ref[d, rows, :] = jnp.where(incl, prod[2 * CHUNK:] * decay, 0.0).astype(BF16)
                qg_ref[d, rows, :] = (qc * eg).astype(BF16)
                kg_ref[d, rows, :] = kc * jnp.exp(gl - gb)
                egl_ref[d, pl.ds(n, 1), :] = jnp.exp(gl)
                systems.append((a_low, jnp.concatenate([vc * bbs[d], kbs[d] * eg], axis=1), d, rows))
        return systems

    lax.fori_loop(0, nc // (n_groups * GDN_PACK // 2), precompute, 0)

    o_ref[...] = jnp.zeros_like(o_ref)

    def scan(i, carry):
        out = []
        for d, s in enumerate(carry):
            n = i if d == 0 else nc - 1 - i
            rows = pl.ds(pl.multiple_of(n * CHUNK, CHUNK), CHUNK)
            v_new = u_ref[d, rows, :] - _bdot(w_ref[d, rows, :], s)
            o_ref[rows, :] += _bdot(qg_ref[d, rows, :], s) + _bdot(attn_ref[d, rows, :], v_new)
            out.append(s * egl_ref[d, pl.ds(n, 1), :] + _bdot_tn(kg_ref[d, rows, :], v_new))
        return tuple(out)

    s_f, s_b = lax.fori_loop(0, nc, scan, (s0_ref[0, 0, 0], s0_ref[0, 1, 0]))
    sout_ref[0, 0, 0] = s_f
    sout_ref[0, 1, 0] = s_b
    y_ref[...] = _rms(o_ref[...], ng_ref[...]) * _silu(z_ref[...])


def _gdn(qkv_n, z, ab, alog, dtb, s0, ng, nseq, seq_len, row_off):
    blk = lambda off: pl.BlockSpec((seq_len, LANES), lambda b, h: (b + row_off, h + off))
    vec = pl.BlockSpec((1, LANES), lambda b, h: (0, 0))
    st_spec = pl.BlockSpec((1, 2, 1, DK_C, DV_C), lambda b, h: (b, 0, h, 0, 0))
    nc = seq_len // CHUNK
    both = lambda width, dtype: pltpu.VMEM((2, seq_len, width), dtype)
    return pl.pallas_call(
        functools.partial(_gdn_kernel, seq_len=seq_len),
        grid=(nseq, H_C),
        in_specs=[blk(0), blk(H_C), blk(2 * H_C), blk(0),
                  pl.BlockSpec((seq_len, LANES), lambda b, h: (b + row_off, 0)),
                  vec, vec, st_spec, vec],
        out_specs=[pl.BlockSpec((seq_len, LANES), lambda b, h: (b, h)), st_spec],
        out_shape=[jax.ShapeDtypeStruct((nseq * seq_len, H_C * DV_C), F32),
                   jax.ShapeDtypeStruct((nseq, 2, H_C, DK_C, DV_C), F32)],
        scratch_shapes=[both(LANES, F32), both(LANES, F32), both(DV_C, F32), both(DK_C, BF16),
                        both(CHUNK, BF16), both(DK_C, BF16), both(DK_C, F32),
                        pltpu.VMEM((2, max(nc, SUBLANES), LANES), F32), pltpu.VMEM((seq_len, DV_C), F32)],
        compiler_params=_cparams(("arbitrary", "arbitrary")),
        name="gated_deltanet",
    )(qkv_n, qkv_n, qkv_n, z, ab, alog, dtb, s0, ng)


def _out_proj_kernel(ya_ref, yb_ref, x_ref, mod_ref, g_ref, wo_ref, wq_ref, x1_ref, h2_ref, qp_ref):
    half = ya_ref.shape[1]
    m = mod_ref[0]
    y = (jnp.dot(ya_ref[...].astype(BF16), wo_ref[0:half, :], preferred_element_type=F32)
         + jnp.dot(yb_ref[...].astype(BF16), wo_ref[half:, :], preferred_element_type=F32))
    x1 = x_ref[...] + m[2:3] * y
    x1_ref[...] = x1
    h2 = (_rms(x1, g_ref[...]) * (1.0 + m[4:5]) + m[3:4]).astype(BF16)
    h2_ref[...] = h2
    qp_ref[...] = jnp.dot(h2, wq_ref[...], preferred_element_type=F32)


def _out_proj(st, ya, yb, x, mod, g, wo, wq, tm):
    t, d = x.shape
    half = ya.shape[1]
    nq = wq.shape[1]
    return pl.pallas_call(
        _out_proj_kernel,
        grid=(t // tm,),
        in_specs=[pl.BlockSpec((tm, half), lambda i: (i, 0)),
                  pl.BlockSpec((tm, half), lambda i: (i, 0)),
                  pl.BlockSpec((tm, d), lambda i: (i, 0)),
                  pl.BlockSpec((1, 6, d), lambda i: (st.mod_row(i, tm), 0, 0)),
                  pl.BlockSpec((1, d), lambda i: (0, 0)),
                  pl.BlockSpec((2 * half, d), lambda i: (0, 0)),
                  pl.BlockSpec((d, nq), lambda i: (0, 0))],
        out_specs=[pl.BlockSpec((tm, d), lambda i: (i, 0)),
                   pl.BlockSpec((tm, d), lambda i: (i, 0)),
                   pl.BlockSpec((tm, nq), lambda i: (i, 0))],
        out_shape=[jax.ShapeDtypeStruct((t, d), F32), jax.ShapeDtypeStruct((t, d), BF16),
                   jax.ShapeDtypeStruct((t, nq), F32)],
        compiler_params=_cparams(("arbitrary",)),
        name="out_proj",
    )(ya, yb, x, mod, g, wo, wq)


def _candidate_rows():
    groups = ([(0, k2) for k2 in range(16)], [(1, k2) for k2 in range(8)], [(2, k2) for k2 in range(8)],
              [(3, k2) for k2 in range(8)], [(k1, 0) for k1 in range(16)], [(k1, 1) for k1 in range(8)],
              [(k1, 2) for k1 in range(8)])
    rows, seen = [], set()
    for grp in groups:
        for k1, k2 in grp:
            ok = (k1 + 1) * (k2 + 1) <= TOPK and (k1, k2) not in seen
            if ok:
                seen.add((k1, k2))
            rows.append(float(k1 * TOPK + k2) if ok else CAND_INVALID)
    return rows


CAND_INVALID = float(TOPK * TOPK)
RANK_MARK = 2.0 ** 100
RANK_STEP = 2.0 ** 95


def _pack_candidates(a1, a2, op):
    return jnp.concatenate([op(a1[0:1], a2), op(a1[1:2], a2[0:8]), op(a1[2:3], a2[0:8]),
                            op(a1[3:4], a2[0:8]), op(a1, a2[0:1]), op(a1[0:8], a2[1:2]),
                            op(a1[0:8], a2[2:3])], axis=0)


def _top16(s, vals_ref, exact):
    n = s.shape[0]
    if not exact:
        for kk in range(TOPK):
            m = jnp.max(s, axis=0, keepdims=True)
            vals_ref[kk:kk + 1, :] = m
            s = jnp.where(s == m, -(RANK_MARK + kk * RANK_STEP), s)
        return jnp.where(s < -0.5 * RANK_MARK, (-s - RANK_MARK) * (1.0 / RANK_STEP), float(TOPK))
    iota = lax.broadcasted_iota(jnp.int32, s.shape, 0).astype(F32)
    rank = jnp.full(s.shape, float(TOPK), F32)
    for kk in range(TOPK):
        m = jnp.max(s, axis=0, keepdims=True)
        idx = jnp.min(jnp.where(s == m, iota, float(n)), axis=0, keepdims=True)
        hit = iota == idx
        rank = jnp.where(hit, float(kk), rank)
        vals_ref[kk:kk + 1, :] = m
        s = jnp.where(hit, NEG_INF, s)
    return rank


def _peer_select_kernel(q_ref, keys_ref, cflat_ref, e1_ref, cnt_ref, e2_ref, r2_ref, v1_ref, v2_ref, *, tt):
    nk = N_KEYS
    cflat = cflat_ref[...]
    valid = cflat < CAND_INVALID
    row16 = lax.broadcasted_iota(jnp.int32, (TOPK, LANES), 0)

    def strip_body(si, carry):
        t0 = pl.multiple_of(si * LANES, LANES)
        tsl = pl.ds(t0, LANES)
        s1 = _bdot_nt(keys_ref[0, 0], q_ref[tsl, 0:nk])
        s2 = _bdot_nt(keys_ref[0, 1], q_ref[tsl, nk:2 * nk])

        def compute(exact):
            rank1 = _top16(s1, v1_ref, exact)
            rank2 = _top16(s2, v2_ref, exact)
            v1 = v1_ref[...]
            v2 = v2_ref[...]
            cand = jnp.where(valid, _pack_candidates(v1, v2, jnp.add), NEG_INF)
            prod = _pack_candidates(jnp.exp(v1 - v1[0:1]), jnp.exp(v2 - v2[0:1]), jnp.multiply)
            sel = jnp.zeros(cand.shape, F32)
            for _ in range(TOPK):
                m = jnp.max(cand, axis=0, keepdims=True)
                if exact:
                    idx = jnp.min(jnp.where(cand == m, cflat, CAND_INVALID), axis=0, keepdims=True)
                    hit = cflat == idx
                else:
                    hit = cand == m
                sel = jnp.where(hit, 1.0, sel)
                cand = jnp.where(hit, NEG_INF, cand)
            zsum = jnp.sum(sel * prod, axis=0, keepdims=True)
            c_all = sel[40:56] + jnp.concatenate([sel[56:64] + sel[64:72], jnp.zeros((8, LANES), F32)], axis=0)
            for k1, (lo, hi) in enumerate(((0, 16), (16, 24), (24, 32), (32, 40))):
                c_all = jnp.where(row16 == k1, jnp.sum(sel[lo:hi], axis=0, keepdims=True), c_all)
            cnt = jnp.zeros(rank1.shape, F32)
            for k1 in range(TOPK):
                cnt = jnp.where(rank1 == float(k1), c_all[k1:k1 + 1], cnt)
            in1 = rank1 < float(TOPK)
            in2 = rank2 < float(TOPK)
            e1_ref[0, :, tsl] = jnp.where(in1, jnp.exp(s1 - v1[0:1]), 0.0) / zsum
            cnt_ref[0, :, tsl] = cnt
            e2_ref[:, tsl] = jnp.where(in2, jnp.exp(s2 - v2[0:1]), 0.0).astype(e2_ref.dtype)
            r2_ref[:, tsl] = rank2.astype(r2_ref.dtype)
            n1 = jnp.sum(jnp.where(in1, 1.0, 0.0), axis=0, keepdims=True)
            n2 = jnp.sum(jnp.where(in2, 1.0, 0.0), axis=0, keepdims=True)
            nc = jnp.sum(sel, axis=0, keepdims=True)
            want = float(TOPK)
            return jnp.abs(n1 - want) + jnp.abs(n2 - want) + jnp.abs(nc - want)

        tied = compute(False)

        @pl.when(jnp.max(tied) > 0.0)
        def _():
            compute(True)

        return carry

    lax.fori_loop(0, tt // LANES, strip_body, 0)


def _peer_select(qp, keys, tt):
    t = qp.shape[0]
    rows = _candidate_rows()
    cflat = jnp.broadcast_to(jnp.asarray(rows, F32)[:, None], (len(rows), LANES))
    out_spec = pl.BlockSpec((1, N_KEYS, tt), lambda i, h: (h, 0, i))
    f32_sds = jax.ShapeDtypeStruct((H_P, N_KEYS, t), F32)
    flat_spec = pl.BlockSpec((N_KEYS, tt), lambda i, h: (h, i))
    flat_sds = jax.ShapeDtypeStruct((H_P * N_KEYS, t), BF16)
    return pl.pallas_call(
        functools.partial(_peer_select_kernel, tt=tt),
        grid=(t // tt, H_P),
        in_specs=[pl.BlockSpec((tt, 2 * N_KEYS), lambda i, h: (i, h)),
                  pl.BlockSpec((1, 2, N_KEYS, N_KEYS), lambda i, h: (h, 0, 0, 0)),
                  pl.BlockSpec(cflat.shape, lambda i, h: (0, 0))],
        out_specs=[out_spec, out_spec, flat_spec, flat_spec],
        out_shape=[f32_sds, f32_sds, flat_sds, flat_sds],
        scratch_shapes=[pltpu.VMEM((TOPK, LANES), F32), pltpu.VMEM((TOPK, LANES), F32)],
        compiler_params=_cparams(("arbitrary", "arbitrary")),
        name="peer_select",
    )(qp, keys, cflat)


PEER_SPLIT = 1


def _peer_dense_kernel(h_ref, *refs, na):
    u_refs = refs[:PEER_SPLIT]
    vt_refs = refs[PEER_SPLIT:2 * PEER_SPLIT]
    (e1_ref, cnt_ref, e2_ref, r2_ref, x_ref, mod_ref, o_ref,
     acc_ref, s_ref, p_ref, e2b_ref, r2b_ref, ht_ref) = refs[2 * PEER_SPLIT:]
    j = pl.program_id(1)
    tt = s_ref.shape[1]

    @pl.when(j == 0)
    def _():
        acc_ref[...] = jnp.zeros_like(acc_ref)
        e2b_ref[...] = e2_ref[...].astype(BF16)
        r2b_ref[...] = r2_ref[...].astype(BF16)
        ht_ref[...] = h_ref[...].astype(F32).T.astype(BF16)

    sub = u_refs[0].shape[0]
    for k, u_ref in enumerate(u_refs):
        s_ref[k * sub:(k + 1) * sub, :] = jnp.dot(u_ref[...], ht_ref[...], preferred_element_type=F32)

    for aa in range(na):
        rows = slice(aa * N_KEYS, (aa + 1) * N_KEYS)
        for tg in range(tt // LANES):
            tsl = slice(tg * LANES, (tg + 1) * LANES)
            gate = None
            for hh in range(H_P):
                cnt_row = cnt_ref[hh, aa:aa + 1, tsl].astype(BF16)
                e1_row = e1_ref[hh, aa:aa + 1, tsl].astype(BF16)
                hsl = slice(hh * N_KEYS, (hh + 1) * N_KEYS)
                term = jnp.where(r2b_ref[hsl, tsl] < cnt_row, e2b_ref[hsl, tsl], 0.0) * e1_row
                gate = term if gate is None else gate + term
            p_ref[rows, tsl] = gate * _gelu_tanh(s_ref[rows, tsl]).astype(BF16)

    dsub = vt_refs[0].shape[0]
    for k, vt_ref in enumerate(vt_refs):
        acc_ref[k * dsub:(k + 1) * dsub, :] += jnp.dot(vt_ref[...], p_ref[...], preferred_element_type=F32)

    @pl.when(j == pl.num_programs(1) - 1)
    def _():
        o_ref[...] = x_ref[...] + mod_ref[0][5:6] * acc_ref[...].T


def _peer_dense_pipelined_kernel(h_ref, u_ref, vt_ref, e1_ref, cnt_ref, e2_ref, r2_ref, x_ref, mod_ref, o_ref,
                                 acc_ref, s_ref, p_ref, e2b_ref, r2b_ref, ht_ref, *, na, n_tiles):
    j = pl.program_id(1)
    tt = s_ref.shape[2]
    cur = j % 2
    prev = 1 - cur

    @pl.when(j == 0)
    def _():
        acc_ref[...] = jnp.zeros_like(acc_ref)
        p_ref[...] = jnp.zeros_like(p_ref)
        s_ref[1] = jnp.zeros(s_ref.shape[1:], F32)
        e2b_ref[...] = e2_ref[...].astype(BF16)
        r2b_ref[...] = r2_ref[...].astype(BF16)
        ht_ref[...] = h_ref[...].astype(F32).T.astype(BF16)

    s_ref[cur] = jnp.dot(u_ref[...], ht_ref[...], preferred_element_type=F32)
    acc_ref[...] += jnp.dot(vt_ref[...], p_ref[cur], preferred_element_type=F32)

    have_prev = j >= 1
    for aa in range(na):
        rows = slice(aa * N_KEYS, (aa + 1) * N_KEYS)
        for tg in range(tt // LANES):
            tsl = slice(tg * LANES, (tg + 1) * LANES)
            gate = None
            for hh in range(H_P):
                cnt_row = cnt_ref[hh, aa:aa + 1, tsl].astype(BF16)
                e1_row = e1_ref[hh, aa:aa + 1, tsl].astype(BF16)
                hsl = slice(hh * N_KEYS, (hh + 1) * N_KEYS)
                term = jnp.where(r2b_ref[hsl, tsl] < cnt_row, e2b_ref[hsl, tsl], 0.0) * e1_row
                gate = term if gate is None else gate + term
            p = gate * _gelu_tanh(s_ref[prev, rows, tsl]).astype(BF16)
            p_ref[prev, rows, tsl] = jnp.where(have_prev, p, jnp.zeros_like(p))

    @pl.when(j == n_tiles + 1)
    def _():
        o_ref[...] = x_ref[...] + mod_ref[0][5:6] * acc_ref[...].T


def _peer_dense_pipelined(st, h2, u, vt, e1, cnt, e2, r2, x, mod, tt, et):
    t, d = x.shape
    n_exp = u.shape[0]
    na = et // N_KEYS
    n_tiles = n_exp // et
    last = n_tiles - 1
    tile2 = lambda j: jnp.clip(j - 1, 0, last)
    row_spec = pl.BlockSpec((H_P, na, tt), lambda i, j: (0, tile2(j), i))
    full_spec = pl.BlockSpec((H_P * N_KEYS, tt), lambda i, j: (0, i))
    return pl.pallas_call(
        functools.partial(_peer_dense_pipelined_kernel, na=na, n_tiles=n_tiles),
        grid=(t // tt, n_tiles + 2),
        in_specs=[pl.BlockSpec((tt, d), lambda i, j: (i, 0)),
                  pl.BlockSpec((et, d), lambda i, j: (jnp.minimum(j, last), 0)),
                  pl.BlockSpec((d, et), lambda i, j: (0, jnp.clip(j - 2, 0, last))),
                  row_spec, row_spec, full_spec, full_spec,
                  pl.BlockSpec((tt, d), lambda i, j: (i, 0)),
                  pl.BlockSpec((1, 6, d), lambda i, j: (st.mod_row(i, tt), 0, 0))],
        out_specs=pl.BlockSpec((tt, d), lambda i, j: (i, 0)),
        out_shape=jax.ShapeDtypeStruct((t, d), F32),
        scratch_shapes=[pltpu.VMEM((d, tt), F32), pltpu.VMEM((2, et, tt), F32), pltpu.VMEM((2, et, tt), BF16),
                        pltpu.VMEM((H_P * N_KEYS, tt), BF16), pltpu.VMEM((H_P * N_KEYS, tt), BF16),
                        pltpu.VMEM((d, tt), BF16)],
        compiler_params=_cparams(("arbitrary", "arbitrary")),
        name="peer_dense_pipelined",
    )(h2, u, vt, e1, cnt, e2, r2, x, mod)


def _peer_dense(st, h2, u, vt, e1, cnt, e2, r2, x, mod, tt, et):
    t, d = x.shape
    n_exp = u.shape[0]
    na = et // N_KEYS
    u_specs = [pl.BlockSpec((et // PEER_SPLIT, d),
                            functools.partial(lambda i, j, k: (j * PEER_SPLIT + k, 0), k=k))
               for k in range(PEER_SPLIT)]
    vt_specs = [pl.BlockSpec((d // PEER_SPLIT, et), functools.partial(lambda i, j, k: (k, j), k=k))
                for k in range(PEER_SPLIT)]
    row_spec = pl.BlockSpec((H_P, na, tt), lambda i, j: (0, j, i))
    full_spec = pl.BlockSpec((H_P * N_KEYS, tt), lambda i, j: (0, i))
    return pl.pallas_call(
        functools.partial(_peer_dense_kernel, na=na),
        grid=(t // tt, n_exp // et),
        in_specs=[pl.BlockSpec((tt, d), lambda i, j: (i, 0))] + u_specs + vt_specs + [
                  row_spec, row_spec, full_spec, full_spec,
                  pl.BlockSpec((tt, d), lambda i, j: (i, 0)),
                  pl.BlockSpec((1, 6, d), lambda i, j: (st.mod_row(i, tt), 0, 0))],
        out_specs=pl.BlockSpec((tt, d), lambda i, j: (i, 0)),
        out_shape=jax.ShapeDtypeStruct((t, d), F32),
        scratch_shapes=[pltpu.VMEM((d, tt), F32), pltpu.VMEM((et, tt), F32), pltpu.VMEM((et, tt), BF16),
                        pltpu.VMEM((H_P * N_KEYS, tt), BF16), pltpu.VMEM((H_P * N_KEYS, tt), BF16),
                        pltpu.VMEM((d, tt), BF16)],
        compiler_params=_cparams(("arbitrary", "arbitrary")),
        name="peer_dense",
    )(h2, *([u] * PEER_SPLIT), *([vt] * PEER_SPLIT), e1, cnt, e2, r2, x, mod)


def _final_norm_kernel(x_ref, g_ref, o_ref):
    o_ref[...] = _rms(x_ref[...], g_ref[...])


def _final_norm(x, g, tm):
    t, d = x.shape
    return pl.pallas_call(
        _final_norm_kernel,
        grid=(t // tm,),
        in_specs=[pl.BlockSpec((tm, d), lambda i: (i, 0)), pl.BlockSpec((1, d), lambda i: (0, 0))],
        out_specs=pl.BlockSpec((tm, d), lambda i: (i, 0)),
        out_shape=jax.ShapeDtypeStruct((t, d), F32),
        compiler_params=_cparams(("arbitrary",)),
        name="final_norm",
    )(x, g)


def _rope_tables(seq_len):
    pos = jnp.arange(seq_len)
    rowp = (pos // GRID_W).astype(F32)
    colp = (pos % GRID_W).astype(F32)
    nf = DK_A // 4
    freqs = ROPE_BASE ** (-jnp.arange(nf, dtype=F32) / nf)
    ang = jnp.concatenate([rowp[:, None] * freqs, colp[:, None] * freqs], axis=-1)
    cos = jnp.cos(ang)
    sin = jnp.sin(ang)
    cos_t = jnp.tile(cos, (1, 4))
    sin_t = jnp.tile(jnp.concatenate([-sin, sin], axis=-1), (1, 2))
    return cos_t, sin_t


def _pad_rows(w, rows):
    return jnp.concatenate([w, jnp.zeros((rows - w.shape[0], w.shape[1]), w.dtype)], axis=0)


def _lane_row(vals):
    flat = vals.reshape(-1).astype(F32)
    return jnp.concatenate([flat, jnp.zeros((LANES - flat.shape[0],), F32)])[None, :]


def kernel(x_prompt, x_sample, state_ret, state_gdn, c, c_ctx, ada_w, ada_b, norm_mix_g, norm_ffn_g,
           final_norm_g, ev_w_in, ev_w_out, ret_gamma_logit, ret_norm_g, sc_conv_w, od_w_in, od_w_out,
           gdn_conv_w, gdn_a_log, gdn_dt_bias, gdn_norm_g, cf_dw_w, cf_dw_b, cf_ln_g, cf_ln_b,
           peer_wq, peer_keys, peer_u, peer_v):
    bp, lp, d = x_prompt.shape
    bs, ls, _ = x_sample.shape
    depth = ada_w.shape[0]
    st = _Streams(bp, lp, bs, ls)
    w_a = H_A * DV_A
    w_b = d - w_a
    w_c = H_C * DV_C
    w_d = d - w_c
    tm = 512
    peer_cfg = ((_peer_dense, 512, 1024), (_peer_dense, 512, 2048), (_peer_dense, 1024, 1024),
                (_peer_dense_pipelined, 512, 1024))
    select_tt = 512

    x = jnp.concatenate([x_prompt.reshape(st.tp, d), x_sample.reshape(st.ts, d)], axis=0)
    cvec = jnp.concatenate([c_ctx[None, :], c, jnp.zeros((N_MOD_ROWS - 1 - bs, d), F32)], axis=0)
    mods = _modulation(cvec, ada_w, ada_b).reshape(depth, N_MOD_ROWS, 6, d)
    cos_t, sin_t = _rope_tables(ls)
    zero_ret = jnp.zeros((bp, 2, H_A, DK_A, DV_A), F32)
    zero_gdn = jnp.zeros((bp, 2, H_C, DK_C, DV_C), F32)

    ret_new, gdn_new = [], []
    for l in range(depth):
        i = l // 2
        mod = mods[l]
        g1 = norm_mix_g[l][None, :]
        if l % 2 == 0:
            widths = (H_A * DK_A, H_A * DK_A, w_a, w_a, w_b, w_b, w_b)
            q, k, v, g, bg, cg, hb = _norm_proj(st, x, mod, g1, ev_w_in[i].astype(BF16), widths, tm)
            gam = jnp.broadcast_to(ret_gamma_logit[i][:, :, None, None], (2, H_A, CHUNK, DV_A))
            ng = ret_norm_g[i][:, None, :]
            ya_p, s_new = _retention(q, k, v, g, zero_ret, gam, ng, cos_t[:lp], sin_t[:lp],
                                     bp, lp, 0, False)
            ya_s, _ = _retention(q, k, v, g, state_ret[:, i], gam, ng, cos_t, sin_t,
                                 bs, ls, st.tp // ls, True)
            ret_new.append(s_new)
            ya = jnp.concatenate([ya_p, ya_s], axis=0)
            yb = _short_gated_conv(st, bg, cg, hb, _pad_rows(sc_conv_w[i], SUBLANES))
            w_out = ev_w_out[i]
        else:
            n_gate = 2 * 2 * H_C
            w_in = od_w_in[i]
            o_ab = 4 * w_c
            w_main = jnp.concatenate([w_in[:, :o_ab], w_in[:, o_ab + n_gate:],
                                      w_in[:, o_ab:o_ab + n_gate],
                                      jnp.zeros((d, LANES - n_gate), F32)], axis=1).astype(BF16)
            widths = (3 * w_c, w_c, 2 * w_d, LANES)
            qkv, z, glu, ab = _norm_proj(st, x, mod, g1, w_main, widths, tm)
            qkv_n = _qkv_conv(st, qkv, _pad_rows(gdn_conv_w[i], SUBLANES))
            alog = _lane_row(gdn_a_log[i])
            dtb = _lane_row(gdn_dt_bias[i])
            ng = gdn_norm_g[i][None, :]
            yc_p, s_new = _gdn(qkv_n, z, ab, alog, dtb, zero_gdn, ng, bp, lp, 0)
            yc_s, _ = _gdn(qkv_n, z, ab, alog, dtb, state_gdn[:, i], ng, bs, ls, st.tp // ls)
            gdn_new.append(s_new)
            ya = jnp.concatenate([yc_p, yc_s], axis=0)
            yb = _conformer(st, glu, _pad_rows(cf_dw_w[i], 2 * CF_HALO), cf_dw_b[i][None, :],
                            cf_ln_g[i][None, :], cf_ln_b[i][None, :])
            w_out = od_w_out[i]
        x, h2, qp = _out_proj(st, ya, yb, x, mod, norm_ffn_g[l][None, :], w_out.astype(BF16),
                              peer_wq[l].astype(BF16), tm)
        e1, cnt, e2, r2 = _peer_select(qp, peer_keys[l].astype(BF16), select_tt)
        dense_fn, peer_tt, peer_et = peer_cfg[l % len(peer_cfg)]
        x = dense_fn(st, h2, peer_u[l].astype(BF16), peer_v[l].astype(BF16).T, e1, cnt, e2, r2,
                     x, mod, peer_tt, peer_et)

    y = _final_norm(x, final_norm_g[None, :], tm)
    y_prompt = y[:st.tp].reshape(bp, lp, d)
    y_sample = y[st.tp:].reshape(bs, ls, d)
    new_state_ret = jnp.stack(ret_new, axis=1).astype(x_prompt.dtype)
    new_state_gdn = jnp.stack(gdn_new, axis=1).astype(x_prompt.dtype)
    return (y_prompt, y_sample, new_state_ret, new_state_gdn)
```

```python
import functools
import math

import jax
import jax.numpy as jnp
from jax import lax
from jax.experimental import pallas as pl
from jax.experimental.pallas import tpu as pltpu

F32 = jnp.float32
BF16 = jnp.bfloat16
HIGHEST = lax.Precision.HIGHEST

EPS = 1e-6
CHUNK = 64
GRID_W = 64
ROPE_BASE = 10000.0
H_A, DK_A, DV_A = 4, 64, 128
H_C, DK_C, DV_C = 4, 128, 128
SC_K, QKV_K, CF_K = 3, 3, 31
N_KEYS, H_P, TOPK = 128, 8, 16
N_MOD_ROWS = 16
LANES = 128
SUBLANES = 8
VMEM_LIMIT = 56 * 1024 * 1024
CONV_TILE = 256
NEG_INF = float("-inf")


def _cparams(sem):
    return pltpu.CompilerParams(dimension_semantics=sem, vmem_limit_bytes=VMEM_LIMIT)


def _bdot(a, b):
    return jnp.dot(a.astype(BF16), b.astype(BF16), preferred_element_type=F32)


def _bdot_nt(a, b):
    return lax.dot_general(a.astype(BF16), b.astype(BF16), (((1,), (1,)), ((), ())),
                           preferred_element_type=F32)


def _bdot_tn(a, b):
    return lax.dot_general(a.astype(BF16), b.astype(BF16), (((0,), (0,)), ((), ())),
                           preferred_element_type=F32)


def _hdot(a, b):
    return jnp.dot(a, b, precision=HIGHEST, preferred_element_type=F32)


def _sigmoid(x):
    return 1.0 / (1.0 + jnp.exp(-x))


def _silu(x):
    return x * _sigmoid(x)


def _softplus(x):
    return jnp.maximum(x, 0.0) + jnp.log1p(jnp.exp(-jnp.abs(x)))


def _log_sigmoid(x):
    return -_softplus(-x)


def _gelu_tanh(x):
    c = math.sqrt(2.0 / math.pi)
    return 0.5 * x * (1.0 + jnp.tanh(c * (x + 0.044715 * (x * x * x))))


def _rms(x, g):
    return x * lax.rsqrt(jnp.mean(x * x, axis=-1, keepdims=True) + EPS) * g


class _Streams:
    def __init__(self, bp, lp, bs, ls):
        self.bp, self.lp, self.bs, self.ls = bp, lp, bs, ls
        self.tp, self.ts = bp * lp, bs * ls
        self.t = self.tp + self.ts

    def mod_row(self, i, tile):
        tiles_p = self.tp // tile
        per_seq = self.ls // tile
        return jnp.where(i < tiles_p, 0, 1 + (i - tiles_p) // per_seq)

    def halo_flags(self, i, tile):
        tiles_p = self.tp // tile
        per_p = self.lp // tile
        per_s = self.ls // tile
        in_p = i < tiles_p
        jp = i % per_p
        js = (i - tiles_p) % per_s
        has_prev = jnp.where(in_p, jp > 0, js > 0)
        has_next = jnp.where(in_p, jp < per_p - 1, js < per_s - 1)
        return has_prev, has_next


def _mod_kernel(c_ref, w_ref, b_ref, o_ref):
    s = _silu(c_ref[...])
    o_ref[0] = _hdot(s, w_ref[0]) + b_ref[0]


def _modulation(cvec, ada_w, ada_b):
    depth, d, d6 = ada_w.shape
    nj = d6 // d
    return pl.pallas_call(
        _mod_kernel,
        grid=(depth, nj),
        in_specs=[pl.BlockSpec((N_MOD_ROWS, d), lambda l, j: (0, 0)),
                  pl.BlockSpec((1, d, d), lambda l, j: (l, 0, j)),
                  pl.BlockSpec((1, 1, d), lambda l, j: (l, 0, j))],
        out_specs=pl.BlockSpec((1, N_MOD_ROWS, d), lambda l, j: (l, 0, j)),
        out_shape=jax.ShapeDtypeStruct((depth, N_MOD_ROWS, d6), F32),
        compiler_params=_cparams(("arbitrary", "arbitrary")),
        name="modulation",
    )(cvec, ada_w, ada_b.reshape(depth, 1, d6))


def _norm_proj_kernel(x_ref, mod_ref, g_ref, w_ref, *o_refs, widths):
    m = mod_ref[0]
    h = _rms(x_ref[...], g_ref[...]) * (1.0 + m[1:2]) + m[0:1]
    p = jnp.dot(h.astype(BF16), w_ref[...], preferred_element_type=F32)
    off = 0
    for o_ref, wd in zip(o_refs, widths):
        o_ref[...] = p[:, off:off + wd]
        off += wd


def _norm_proj(st, x, mod, g, w, widths, tm):
    t, d = x.shape
    n = w.shape[1]
    return pl.pallas_call(
        functools.partial(_norm_proj_kernel, widths=widths),
        grid=(t // tm,),
        in_specs=[pl.BlockSpec((tm, d), lambda i: (i, 0)),
                  pl.BlockSpec((1, 6, d), lambda i: (st.mod_row(i, tm), 0, 0)),
                  pl.BlockSpec((1, d), lambda i: (0, 0)),
                  pl.BlockSpec((d, n), lambda i: (0, 0))],
        out_specs=[pl.BlockSpec((tm, wd), lambda i: (i, 0)) for wd in widths],
        out_shape=[jax.ShapeDtypeStruct((t, wd), F32) for wd in widths],
        compiler_params=_cparams(("arbitrary",)),
        name="norm_proj",
    )(x, mod, g, w)


def _ret_kernel(q_ref, k_ref, v_ref, g_ref, s0_ref, gam_ref, ng_ref, cos_ref, sin_ref,
                ya_ref, sout_ref, qs_ref, ks_ref, o_ref, sf_ref, sb_ref, *, seq_len, latent):
    nc = seq_len // CHUNK
    q = q_ref[...]
    k = k_ref[...] * (DK_A ** -0.5)
    if latent:
        lane = lax.broadcasted_iota(jnp.int32, q.shape, 1)
        first_half = (lane % DK_A) < (DK_A // 2)
        cos = cos_ref[...]
        sin = sin_ref[...]

        def rope(x):
            partner = jnp.where(first_half, pltpu.roll(x, LANES - DK_A // 2, 1),
                                pltpu.roll(x, DK_A // 2, 1))
            return x * cos + partner * sin

        q = rope(q)
        k = rope(k)
    qs_ref[...] = q
    ks_ref[...] = k

    cat = jnp.concatenate
    kw = 2 * DK_A
    vw = 2 * DV_A
    row = lax.broadcasted_iota(jnp.int32, (CHUNK, kw), 0).astype(F32)
    lane_k = lax.broadcasted_iota(jnp.int32, (CHUNK, kw), 1)
    lane_v = lax.broadcasted_iota(jnp.int32, (CHUNK, vw), 1)
    head0_k = lane_k < DK_A
    head0_v = lane_v < DV_A
    lg = [[_log_sigmoid(gam_ref[d, hh]) for hh in range(2)] for d in range(2)]
    lgf_k = jnp.where(head0_k, lg[0][0], lg[0][1])
    lgb_k = jnp.where(head0_k, lg[1][0], lg[1][1])
    diff = row - (lane_k % DK_A).astype(F32)
    dcomb = (jnp.where(diff >= 0, jnp.exp(lgf_k * jnp.maximum(diff, 0.0)), 0.0)
             + jnp.where(diff <= 0, jnp.exp(lgb_k * jnp.maximum(-diff, 0.0)), 0.0))
    wend_f = jnp.exp(lgf_k * (CHUNK - 1.0 - row))
    wstart_f = jnp.exp(lgf_k * (row + 1.0))
    wend_b = jnp.exp(lgb_k * row)
    wstart_b = jnp.exp(lgb_k * (CHUNK - row))

    def state_layout(h0, h1):
        zero = jnp.zeros_like(h0)
        return cat([cat([h0, zero], axis=1), cat([zero, h1], axis=1)], axis=0)

    on_diag = state_layout(jnp.ones((DK_A, DV_A), F32), jnp.ones((DK_A, DV_A), F32)) > 0.0
    gch_f = jnp.exp(state_layout(lg[0][0], lg[0][1]) * float(CHUNK))
    gch_b = jnp.exp(state_layout(lg[1][0], lg[1][1]) * float(CHUNK))
    sf_ref[...] = state_layout(s0_ref[0, 0, 0], s0_ref[0, 0, 1])
    sb_ref[...] = state_layout(s0_ref[0, 1, 0], s0_ref[0, 1, 1])
    o_ref[...] = jnp.zeros_like(o_ref)

    def step(i, carry):
        rows = pl.ds(pl.multiple_of(i * CHUNK, CHUNK), CHUNK)
        qn = qs_ref[rows, :]
        kn = ks_ref[rows, :]
        vn = v_ref[rows, :]
        k_bd = cat([jnp.where(head0_k, kn, 0.0), jnp.where(head0_k, 0.0, kn)], axis=0)
        v_bd = cat([jnp.where(head0_v, vn, 0.0), jnp.where(head0_v, 0.0, vn)], axis=0)
        a = _bdot_nt(qn, k_bd) * dcomb
        s_f = sf_ref[...]
        o_ref[rows, :] += _bdot(cat([a, qn * wstart_f], axis=1), cat([v_bd, s_f], axis=0))
        sf_ref[...] = s_f * gch_f + jnp.where(on_diag, _bdot_tn(kn * wend_f, vn), 0.0)
        rows = pl.ds(pl.multiple_of((nc - 1 - i) * CHUNK, CHUNK), CHUNK)
        qn = qs_ref[rows, :]
        kn = ks_ref[rows, :]
        vn = v_ref[rows, :]
        s_b = sb_ref[...]
        o_ref[rows, :] += _bdot(qn * wstart_b, s_b)
        sb_ref[...] = s_b * gch_b + jnp.where(on_diag, _bdot_tn(kn * wend_b, vn), 0.0)
        return carry

    lax.fori_loop(0, nc, step, 0)
    for d, s_ref in enumerate((sf_ref, sb_ref)):
        sout_ref[0, d, 0] = s_ref[:DK_A, :DV_A]
        sout_ref[0, d, 1] = s_ref[DK_A:, DV_A:]

    for hh in range(2):
        vsl = slice(hh * DV_A, (hh + 1) * DV_A)
        ya_ref[:, vsl] = _silu(g_ref[:, vsl]) * _rms(o_ref[:, vsl], ng_ref[hh])


def _retention(q, k, v, g, s0, gam, ng, cos_t, sin_t, nseq, seq_len, row_off, latent):
    w2 = 2 * DK_A
    v2 = 2 * DV_A
    return pl.pallas_call(
        functools.partial(_ret_kernel, seq_len=seq_len, latent=latent),
        grid=(nseq, H_A // 2),
        in_specs=[pl.BlockSpec((seq_len, w2), lambda b, p: (b + row_off, p)),
                  pl.BlockSpec((seq_len, w2), lambda b, p: (b + row_off, p)),
                  pl.BlockSpec((seq_len, v2), lambda b, p: (b + row_off, p)),
                  pl.BlockSpec((seq_len, v2), lambda b, p: (b + row_off, p)),
                  pl.BlockSpec((1, 2, 2, DK_A, DV_A), lambda b, p: (b, 0, p, 0, 0)),
                  pl.BlockSpec((2, 2, CHUNK, DV_A), lambda b, p: (0, p, 0, 0)),
                  pl.BlockSpec((2, 1, DV_A), lambda b, p: (p, 0, 0)),
                  pl.BlockSpec((seq_len, w2), lambda b, p: (0, 0)),
                  pl.BlockSpec((seq_len, w2), lambda b, p: (0, 0))],
        out_specs=[pl.BlockSpec((seq_len, v2), lambda b, p: (b, p)),
                   pl.BlockSpec((1, 2, 2, DK_A, DV_A), lambda b, p: (b, 0, p, 0, 0))],
        out_shape=[jax.ShapeDtypeStruct((nseq * seq_len, H_A * DV_A), F32),
                   jax.ShapeDtypeStruct((nseq, 2, H_A, DK_A, DV_A), F32)],
        scratch_shapes=[pltpu.VMEM((seq_len, w2), F32), pltpu.VMEM((seq_len, w2), F32),
                        pltpu.VMEM((seq_len, v2), F32), pltpu.VMEM((w2, v2), F32), pltpu.VMEM((w2, v2), F32)],
        compiler_params=_cparams(("arbitrary", "arbitrary")),
        name="retention",
    )(q, k, v, g, s0, gam, ng, cos_t, sin_t)


def _fill_padded(pad_ref, cur, prev, nxt, has_prev, has_next, halo):
    tile = cur.shape[0]
    pad_ref[0:halo, :] = jnp.where(has_prev, prev, 0.0)
    pad_ref[halo:halo + tile, :] = cur
    pad_ref[halo + tile:halo + tile + halo, :] = jnp.where(has_next, nxt, 0.0)


def _conv_taps(pad_ref, w_ref, ntaps, halo, tile):
    base = halo - ntaps // 2
    acc = w_ref[0:1, :] * pad_ref[base:base + tile, :]
    for kk in range(1, ntaps):
        acc = acc + w_ref[kk:kk + 1, :] * pad_ref[base + kk:base + kk + tile, :]
    return acc


def _halo_specs(width, col_map, halo):
    per = CONV_TILE // halo

    def cur(i, j):
        return (i, col_map(j))

    def prev(i, j):
        return (jnp.maximum(i * per - 1, 0), col_map(j))

    def make_next(nblk):
        def nxt(i, j):
            return (jnp.minimum((i + 1) * per, nblk - 1), col_map(j))
        return nxt

    return cur, prev, make_next


def _sconv_kernel(bg_ref, cg_ref, cgp_ref, cgn_ref, hb_ref, hbp_ref, hbn_ref, w_ref, o_ref, pad_ref, *, st):
    has_prev, has_next = st.halo_flags(pl.program_id(0), CONV_TILE)
    _fill_padded(pad_ref, cg_ref[...] * hb_ref[...], cgp_ref[...] * hbp_ref[...],
                 cgn_ref[...] * hbn_ref[...], has_prev, has_next, SUBLANES)
    o_ref[...] = bg_ref[...] * _conv_taps(pad_ref, w_ref, SC_K, SUBLANES, CONV_TILE)


CONV_COLS = 512


def _short_gated_conv(st, bg, cg, hb, w):
    t, c = bg.shape
    halo = SUBLANES
    cur, prev, make_next = _halo_specs(c, lambda j: j, halo)
    nxt = make_next(t // halo)
    tile_spec = pl.BlockSpec((CONV_TILE, CONV_COLS), cur)
    prev_spec = pl.BlockSpec((halo, CONV_COLS), prev)
    next_spec = pl.BlockSpec((halo, CONV_COLS), nxt)
    return pl.pallas_call(
        functools.partial(_sconv_kernel, st=st),
        grid=(t // CONV_TILE, c // CONV_COLS),
        in_specs=[tile_spec, tile_spec, prev_spec, next_spec, tile_spec, prev_spec, next_spec,
                  pl.BlockSpec((SUBLANES, CONV_COLS), lambda i, j: (0, j))],
        out_specs=tile_spec,
        out_shape=jax.ShapeDtypeStruct((t, c), F32),
        scratch_shapes=[pltpu.VMEM((CONV_TILE + 2 * halo, CONV_COLS), F32)],
        compiler_params=_cparams(("arbitrary", "arbitrary")),
        name="short_gated_conv",
    )(bg, cg, cg, cg, hb, hb, hb, w)


def _qkv_conv_kernel(x_ref, xp_ref, xn_ref, w_ref, o_ref, pad_ref, *, st):
    has_prev, has_next = st.halo_flags(pl.program_id(0), CONV_TILE)
    _fill_padded(pad_ref, x_ref[...], xp_ref[...], xn_ref[...], has_prev, has_next, SUBLANES)
    s = _silu(_conv_taps(pad_ref, w_ref, QKV_K, SUBLANES, CONV_TILE))
    j = pl.program_id(1)
    for hh in range(CONV_COLS // DK_C):
        hsl = slice(hh * DK_C, (hh + 1) * DK_C)
        sh = s[:, hsl]
        nrm = sh * lax.rsqrt(jnp.sum(sh * sh, axis=-1, keepdims=True) + EPS)
        o_ref[:, hsl] = jnp.where(j == 0, nrm * (DK_C ** -0.5), jnp.where(j == 1, nrm, sh))


def _qkv_conv(st, qkv, w):
    t, c = qkv.shape
    assert CONV_COLS == H_C * DK_C == H_C * DV_C
    halo = SUBLANES
    cur, prev, make_next = _halo_specs(c, lambda j: j, halo)
    nxt = make_next(t // halo)
    tile_spec = pl.BlockSpec((CONV_TILE, CONV_COLS), cur)
    return pl.pallas_call(
        functools.partial(_qkv_conv_kernel, st=st),
        grid=(t // CONV_TILE, c // CONV_COLS),
        in_specs=[tile_spec, pl.BlockSpec((halo, CONV_COLS), prev), pl.BlockSpec((halo, CONV_COLS), nxt),
                  pl.BlockSpec((SUBLANES, CONV_COLS), lambda i, j: (0, j))],
        out_specs=tile_spec,
        out_shape=jax.ShapeDtypeStruct((t, c), F32),
        scratch_shapes=[pltpu.VMEM((CONV_TILE + 2 * halo, CONV_COLS), F32)],
        compiler_params=_cparams(("arbitrary", "arbitrary")),
        name="qkv_conv",
    )(qkv, qkv, qkv, w)


CF_HALO = 16


def _conformer_kernel(ca_ref, cap_ref, can_ref, cg_ref, cgp_ref, cgn_ref, w_ref, b_ref, lg_ref, lb_ref,
                      o_ref, pad_ref, *, st):
    has_prev, has_next = st.halo_flags(pl.program_id(0), CONV_TILE)
    _fill_padded(pad_ref, ca_ref[...] * _sigmoid(cg_ref[...]), cap_ref[...] * _sigmoid(cgp_ref[...]),
                 can_ref[...] * _sigmoid(cgn_ref[...]), has_prev, has_next, CF_HALO)
    hc = _conv_taps(pad_ref, w_ref, CF_K, CF_HALO, CONV_TILE) + b_ref[...]
    mu = jnp.mean(hc, axis=-1, keepdims=True)
    xc = hc - mu
    y = xc * lax.rsqrt(jnp.mean(xc * xc, axis=-1, keepdims=True) + EPS) * lg_ref[...] + lb_ref[...]
    o_ref[...] = _silu(y)


def _conformer(st, glu, w, b, ln_g, ln_b):
    t, c2 = glu.shape
    c = c2 // 2
    nblk = t // CF_HALO
    per = CONV_TILE // CF_HALO
    vec = pl.BlockSpec((1, c), lambda i: (0, 0))
    return pl.pallas_call(
        functools.partial(_conformer_kernel, st=st),
        grid=(t // CONV_TILE,),
        in_specs=[pl.BlockSpec((CONV_TILE, c), lambda i: (i, 0)),
                  pl.BlockSpec((CF_HALO, c), lambda i: (jnp.maximum(i * per - 1, 0), 0)),
                  pl.BlockSpec((CF_HALO, c), lambda i: (jnp.minimum((i + 1) * per, nblk - 1), 0)),
                  pl.BlockSpec((CONV_TILE, c), lambda i: (i, 1)),
                  pl.BlockSpec((CF_HALO, c), lambda i: (jnp.maximum(i * per - 1, 0), 1)),
                  pl.BlockSpec((CF_HALO, c), lambda i: (jnp.minimum((i + 1) * per, nblk - 1), 1)),
                  pl.BlockSpec((2 * CF_HALO, c), lambda i: (0, 0)), vec, vec, vec],
        out_specs=pl.BlockSpec((CONV_TILE, c), lambda i: (i, 0)),
        out_shape=jax.ShapeDtypeStruct((t, c), F32),
        scratch_shapes=[pltpu.VMEM((CONV_TILE + 2 * CF_HALO, c), F32)],
        compiler_params=_cparams(("arbitrary",)),
        name="conformer_conv",
    )(glu, glu, glu, glu, glu, glu, w, b, ln_g, ln_b)


GDN_PACK = 4
GDN_GROUPS = 4


def _split_bf16(x):
    hi = x.astype(BF16)
    return hi, (x - hi.astype(F32)).astype(BF16)


def _packed_unit_inverses(groups):
    dot = functools.partial(jnp.dot, preferred_element_type=F32)
    cat = jnp.concatenate
    n = len(groups[0])
    c = CHUNK
    w = n * c
    eye = (lax.broadcasted_iota(jnp.int32, (c, c), 0) == lax.broadcasted_iota(jnp.int32, (c, c), 1)).astype(F32)
    lane_block = lax.broadcasted_iota(jnp.int32, (c, 2 * w), 1) % w // c

    def rhs_of(p):
        hi, lo = _split_bf16(p)
        both = cat([hi, lo], axis=1)
        return hi, lo, cat([jnp.where(lane_block == i, both, jnp.zeros_like(both)) for i in range(n)], axis=0)

    def product(m, r):
        return (r[:m, :w] + r[m:, :w]) + (r[:m, w:] + r[m:, w:])

    ps = [cat(a_list, axis=1) for a_list in groups]
    ts = [cat([eye - a for a in a_list], axis=1) for a_list in groups]
    for g, p in enumerate(ps):
        hi, lo, rhs = rhs_of(p)
        ps[g] = product(c, dot(cat([hi, lo], axis=0), rhs))
    steps = int(math.log2(c)) - 1
    for step in range(steps):
        for g in range(len(groups)):
            p_hi, p_lo, rhs = rhs_of(ps[g])
            t_hi, t_lo = _split_bf16(ts[g])
            if step < steps - 1:
                both = product(2 * c, dot(cat([t_hi, p_hi, t_lo, p_lo], axis=0), rhs))
                ts[g] = ts[g] + both[:c]
                ps[g] = both[c:]
            else:
                ts[g] = ts[g] + product(c, dot(cat([t_hi, t_lo], axis=0), rhs))
    return [[t[:, i * c:(i + 1) * c] for i in range(n)] for t in ts]


def _chunk_cumsum(x, reverse):
    row = lax.broadcasted_iota(jnp.int32, x.shape, 0)
    s = 1
    while s < CHUNK:
        if reverse:
            x = x + jnp.where(row < CHUNK - s, pltpu.roll(x, CHUNK - s, 0), 0.0)
        else:
            x = x + jnp.where(row >= s, pltpu.roll(x, s, 0), 0.0)
        s *= 2
    return x


def _gdn_kernel(q_ref, k_ref, v_ref, z_ref, ab_ref, alog_ref, dtb_ref, s0_ref, ng_ref,
                y_ref, sout_ref, g_ref, beta_ref, u_ref, w_ref, attn_ref, qg_ref, kg_ref, egl_ref, o_ref,
                *, seq_len):
    nc = seq_len // CHUNK
    h = pl.program_id(1)
    row = lax.broadcasted_iota(jnp.int32, (CHUNK, CHUNK), 0)
    col = lax.broadcasted_iota(jnp.int32, (CHUNK, CHUNK), 1)
    lane = lax.broadcasted_iota(jnp.int32, (CHUNK, LANES), 1)
    alog = alog_ref[...]
    dtb = dtb_ref[...]

    def gates(n, carry):
        rows = pl.ds(pl.multiple_of(n * CHUNK, CHUNK), CHUNK)
        ab = ab_ref[rows, :]
        g_all = -jnp.exp(alog) * _softplus(ab + dtb)
        b_all = _sigmoid(ab)
        for d in range(2):
            gsel = jnp.sum(jnp.where(lane == d * H_C + h, g_all, 0.0), axis=-1, keepdims=True)
            bsel = jnp.sum(jnp.where(lane == 2 * H_C + d * H_C + h, b_all, 0.0), axis=-1, keepdims=True)
            g_ref[d, rows, :] = jnp.broadcast_to(gsel, (CHUNK, LANES))
            beta_ref[d, rows, :] = jnp.broadcast_to(bsel, (CHUNK, LANES))
        return carry

    lax.fori_loop(0, nc, gates, 0)

    n_groups = min(GDN_GROUPS, nc // (GDN_PACK // 2))

    def precompute(it, carry):
        groups = [prepare_group(it * n_groups + grp) for grp in range(n_groups)]
        inverses = _packed_unit_inverses([[sysm[0] for sysm in systems] for systems in groups])
        for t_invs, systems in zip(inverses, groups):
            for t_inv, (_, rhs, d, rows) in zip(t_invs, systems):
                uw = _bdot(t_inv, rhs)
                u_ref[d, rows, :] = uw[:, :DV_C]
                w_ref[d, rows, :] = uw[:, DV_C:].astype(BF16)
        return carry

    def prepare_group(pair):
        systems = []
        for jc in range(GDN_PACK // 2):
            n = pair * (GDN_PACK // 2) + jc
            rows = pl.ds(pl.multiple_of(n * CHUNK, CHUNK), CHUNK)
            qc = q_ref[rows, :]
            kc = k_ref[rows, :]
            vc = v_ref[rows, :]
            gbs = [_chunk_cumsum(g_ref[d, rows, :], d == 1) for d in range(2)]
            bbs = [beta_ref[d, rows, :] for d in range(2)]
            kbs = [kc * bb for bb in bbs]
            prod = _bdot_nt(jnp.concatenate(kbs + [qc], axis=0), kc)
            for d in range(2):
                incl = (row >= col) if d == 0 else (row <= col)
                strict = (row > col) if d == 0 else (row < col)
                last = CHUNK - 1 if d == 0 else 0
                gb = gbs[d]
                gr = gb[:, :CHUNK].T
                gl = gb[last:last + 1, :]
                decay = jnp.where(incl, jnp.exp(jnp.where(incl, gb[:, :CHUNK] - gr, 0.0)), 0.0)
                eg = jnp.exp(gb)
                a_low = jnp.where(strict, prod[d * CHUNK:(d + 1) * CHUNK] * decay, 0.0)
                attn_ref[d, rows, :] = jnp.where(incl, prod[2 * CHUNK:] * decay, 0.0).astype(BF16)
                qg_ref[d, rows, :] = (qc * eg).astype(BF16)
                kg_ref[d, rows, :] = kc * jnp.exp(gl - gb)
                egl_ref[d, pl.ds(n, 1), :] = jnp.exp(gl)
                systems.append((a_low, jnp.concatenate([vc * bbs[d], kbs[d] * eg], axis=1), d, rows))
        return systems

    lax.fori_loop(0, nc // (n_groups * GDN_PACK // 2), precompute, 0)

    o_ref[...] = jnp.zeros_like(o_ref)

    def scan(i, carry):
        out = []
        for d, s in enumerate(carry):
            n = i if d == 0 else nc - 1 - i
            rows = pl.ds(pl.multiple_of(n * CHUNK, CHUNK), CHUNK)
            v_new = u_ref[d, rows, :] - _bdot(w_ref[d, rows, :], s)
            o_ref[rows, :] += _bdot(qg_ref[d, rows, :], s) + _bdot(attn_ref[d, rows, :], v_new)
            out.append(s * egl_ref[d, pl.ds(n, 1), :] + _bdot_tn(kg_ref[d, rows, :], v_new))
        return tuple(out)

    s_f, s_b = lax.fori_loop(0, nc, scan, (s0_ref[0, 0, 0], s0_ref[0, 1, 0]))
    sout_ref[0, 0, 0] = s_f
    sout_ref[0, 1, 0] = s_b
    y_ref[...] = _rms(o_ref[...], ng_ref[...]) * _silu(z_ref[...])


def _gdn(qkv_n, z, ab, alog, dtb, s0, ng, nseq, seq_len, row_off):
    blk = lambda off: pl.BlockSpec((seq_len, LANES), lambda b, h: (b + row_off, h + off))
    vec = pl.BlockSpec((1, LANES), lambda b, h: (0, 0))
    st_spec = pl.BlockSpec((1, 2, 1, DK_C, DV_C), lambda b, h: (b, 0, h, 0, 0))
    nc = seq_len // CHUNK
    both = lambda width, dtype: pltpu.VMEM((2, seq_len, width), dtype)
    return pl.pallas_call(
        functools.partial(_gdn_kernel, seq_len=seq_len),
        grid=(nseq, H_C),
        in_specs=[blk(0), blk(H_C), blk(2 * H_C), blk(0),
                  pl.BlockSpec((seq_len, LANES), lambda b, h: (b + row_off, 0)),
                  vec, vec, st_spec, vec],
        out_specs=[pl.BlockSpec((seq_len, LANES), lambda b, h: (b, h)), st_spec],
        out_shape=[jax.ShapeDtypeStruct((nseq * seq_len, H_C * DV_C), F32),
                   jax.ShapeDtypeStruct((nseq, 2, H_C, DK_C, DV_C), F32)],
        scratch_shapes=[both(LANES, F32), both(LANES, F32), both(DV_C, F32), both(DK_C, BF16),
                        both(CHUNK, BF16), both(DK_C, BF16), both(DK_C, F32),
                        pltpu.VMEM((2, max(nc, SUBLANES), LANES), F32), pltpu.VMEM((seq_len, DV_C), F32)],
        compiler_params=_cparams(("arbitrary", "arbitrary")),
        name="gated_deltanet",
    )(qkv_n, qkv_n, qkv_n, z, ab, alog, dtb, s0, ng)


def _out_proj_kernel(ya_ref, yb_ref, x_ref, mod_ref, g_ref, wo_ref, wq_ref, x1_ref, h2_ref, qp_ref):
    half = ya_ref.shape[1]
    m = mod_ref[0]
    y = (jnp.dot(ya_ref[...].astype(BF16), wo_ref[0:half, :], preferred_element_type=F32)
         + jnp.dot(yb_ref[...].astype(BF16), wo_ref[half:, :], preferred_element_type=F32))
    x1 = x_ref[...] + m[2:3] * y
    x1_ref[...] = x1
    h2 = (_rms(x1, g_ref[...]) * (1.0 + m[4:5]) + m[3:4]).astype(BF16)
    h2_ref[...] = h2
    qp_ref[...] = jnp.dot(h2, wq_ref[...], preferred_element_type=F32)


def _out_proj(st, ya, yb, x, mod, g, wo, wq, tm):
    t, d = x.shape
    half = ya.shape[1]
    nq = wq.shape[1]
    return pl.pallas_call(
        _out_proj_kernel,
        grid=(t // tm,),
        in_specs=[pl.BlockSpec((tm, half), lambda i: (i, 0)),
                  pl.BlockSpec((tm, half), lambda i: (i, 0)),
                  pl.BlockSpec((tm, d), lambda i: (i, 0)),
                  pl.BlockSpec((1, 6, d), lambda i: (st.mod_row(i, tm), 0, 0)),
                  pl.BlockSpec((1, d), lambda i: (0, 0)),
                  pl.BlockSpec((2 * half, d), lambda i: (0, 0)),
                  pl.BlockSpec((d, nq), lambda i: (0, 0))],
        out_specs=[pl.BlockSpec((tm, d), lambda i: (i, 0)),
                   pl.BlockSpec((tm, d), lambda i: (i, 0)),
                   pl.BlockSpec((tm, nq), lambda i: (i, 0))],
        out_shape=[jax.ShapeDtypeStruct((t, d), F32), jax.ShapeDtypeStruct((t, d), BF16),
                   jax.ShapeDtypeStruct((t, nq), F32)],
        compiler_params=_cparams(("arbitrary",)),
        name="out_proj",
    )(ya, yb, x, mod, g, wo, wq)


def _candidate_rows():
    groups = ([(0, k2) for k2 in range(16)], [(1, k2) for k2 in range(8)], [(2, k2) for k2 in range(8)],
              [(3, k2) for k2 in range(8)], [(k1, 0) for k1 in range(16)], [(k1, 1) for k1 in range(8)],
              [(k1, 2) for k1 in range(8)])
    rows, seen = [], set()
    for grp in groups:
        for k1, k2 in grp:
            ok = (k1 + 1) * (k2 + 1) <= TOPK and (k1, k2) not in seen
            if ok:
                seen.add((k1, k2))
            rows.append(float(k1 * TOPK + k2) if ok else CAND_INVALID)
    return rows


CAND_INVALID = float(TOPK * TOPK)
RANK_MARK = 2.0 ** 100
RANK_STEP = 2.0 ** 95
SELECT_STRIPS = 2


def _pack_candidates(a1, a2, op):
    return jnp.concatenate([op(a1[0:1], a2), op(a1[1:2], a2[0:8]), op(a1[2:3], a2[0:8]),
                            op(a1[3:4], a2[0:8]), op(a1, a2[0:1]), op(a1[0:8], a2[1:2]),
                            op(a1[0:8], a2[2:3])], axis=0)


def _top16(s, vals_ref, exact):
    n = s.shape[0]
    if not exact:
        for kk in range(TOPK):
            m = jnp.max(s, axis=0, keepdims=True)
            vals_ref[kk:kk + 1, :] = m
            s = jnp.where(s == m, -(RANK_MARK + kk * RANK_STEP), s)
        return jnp.where(s < -0.5 * RANK_MARK, (-s - RANK_MARK) * (1.0 / RANK_STEP), float(TOPK))
    iota = lax.broadcasted_iota(jnp.int32, s.shape, 0).astype(F32)
    rank = jnp.full(s.shape, float(TOPK), F32)
    for kk in range(TOPK):
        m = jnp.max(s, axis=0, keepdims=True)
        idx = jnp.min(jnp.where(s == m, iota, float(n)), axis=0, keepdims=True)
        hit = iota == idx
        rank = jnp.where(hit, float(kk), rank)
        vals_ref[kk:kk + 1, :] = m
        s = jnp.where(hit, NEG_INF, s)
    return rank


def _peer_select_kernel(q_ref, keys_ref, cflat_ref, e1_ref, cnt_ref, e2_ref, r2_ref, v1_ref, v2_ref, *, tt):
    nk = N_KEYS
    cflat = cflat_ref[...]
    valid = cflat < CAND_INVALID
    row16 = lax.broadcasted_iota(jnp.int32, (TOPK, LANES), 0)

    def compute(strip, exact):
        s1, s2, tsl, va_ref, vb_ref = strip
        rank1 = _top16(s1, va_ref, exact)
        rank2 = _top16(s2, vb_ref, exact)
        v1 = va_ref[...]
        v2 = vb_ref[...]
        cand = jnp.where(valid, _pack_candidates(v1, v2, jnp.add), NEG_INF)
        prod = _pack_candidates(jnp.exp(v1 - v1[0:1]), jnp.exp(v2 - v2[0:1]), jnp.multiply)
        sel = jnp.zeros(cand.shape, F32)
        for _ in range(TOPK):
            m = jnp.max(cand, axis=0, keepdims=True)
            if exact:
                idx = jnp.min(jnp.where(cand == m, cflat, CAND_INVALID), axis=0, keepdims=True)
                hit = cflat == idx
            else:
                hit = cand == m
            sel = jnp.where(hit, 1.0, sel)
            cand = jnp.where(hit, NEG_INF, cand)
        zsum = jnp.sum(sel * prod, axis=0, keepdims=True)
        c_all = sel[40:56] + jnp.concatenate([sel[56:64] + sel[64:72], jnp.zeros((8, LANES), F32)], axis=0)
        for k1, (lo, hi) in enumerate(((0, 16), (16, 24), (24, 32), (32, 40))):
            c_all = jnp.where(row16 == k1, jnp.sum(sel[lo:hi], axis=0, keepdims=True), c_all)
        cnt = jnp.zeros(rank1.shape, F32)
        for k1 in range(TOPK):
            cnt = jnp.where(rank1 == float(k1), c_all[k1:k1 + 1], cnt)
        in1 = rank1 < float(TOPK)
        in2 = rank2 < float(TOPK)
        e1_ref[0, :, tsl] = jnp.where(in1, jnp.exp(s1 - v1[0:1]), 0.0) / zsum
        cnt_ref[0, :, tsl] = cnt
        e2_ref[:, tsl] = jnp.where(in2, jnp.exp(s2 - v2[0:1]), 0.0).astype(e2_ref.dtype)
        r2_ref[:, tsl] = rank2.astype(r2_ref.dtype)
        n1 = jnp.sum(jnp.where(in1, 1.0, 0.0), axis=0, keepdims=True)
        n2 = jnp.sum(jnp.where(in2, 1.0, 0.0), axis=0, keepdims=True)
        nc = jnp.sum(sel, axis=0, keepdims=True)
        want = float(TOPK)
        return jnp.abs(n1 - want) + jnp.abs(n2 - want) + jnp.abs(nc - want)

    def strip_group(si, carry):
        strips = []
        for sub in range(SELECT_STRIPS):
            t0 = pl.multiple_of((si * SELECT_STRIPS + sub) * LANES, LANES)
            tsl = pl.ds(t0, LANES)
            s1 = _bdot_nt(keys_ref[0, 0], q_ref[tsl, 0:nk])
            s2 = _bdot_nt(keys_ref[0, 1], q_ref[tsl, nk:2 * nk])
            strips.append((s1, s2, tsl, v1_ref.at[sub], v2_ref.at[sub]))
        tied = [compute(strip, False) for strip in strips]
        for strip, bad in zip(strips, tied):
            @pl.when(jnp.max(bad) > 0.0)
            def _(strip=strip):
                compute(strip, True)
        return carry

    lax.fori_loop(0, tt // (SELECT_STRIPS * LANES), strip_group, 0)


def _peer_select(qp, keys, tt):
    t = qp.shape[0]
    rows = _candidate_rows()
    cflat = jnp.broadcast_to(jnp.asarray(rows, F32)[:, None], (len(rows), LANES))
    out_spec = pl.BlockSpec((1, N_KEYS, tt), lambda i, h: (h, 0, i))
    f32_sds = jax.ShapeDtypeStruct((H_P, N_KEYS, t), F32)
    flat_spec = pl.BlockSpec((N_KEYS, tt), lambda i, h: (h, i))
    flat_sds = jax.ShapeDtypeStruct((H_P * N_KEYS, t), BF16)
    return pl.pallas_call(
        functools.partial(_peer_select_kernel, tt=tt),
        grid=(t // tt, H_P),
        in_specs=[pl.BlockSpec((tt, 2 * N_KEYS), lambda i, h: (i, h)),
                  pl.BlockSpec((1, 2, N_KEYS, N_KEYS), lambda i, h: (h, 0, 0, 0)),
                  pl.BlockSpec(cflat.shape, lambda i, h: (0, 0))],
        out_specs=[out_spec, out_spec, flat_spec, flat_spec],
        out_shape=[f32_sds, f32_sds, flat_sds, flat_sds],
        scratch_shapes=[pltpu.VMEM((SELECT_STRIPS, TOPK, LANES), F32),
                        pltpu.VMEM((SELECT_STRIPS, TOPK, LANES), F32)],
        compiler_params=_cparams(("arbitrary", "arbitrary")),
        name="peer_select",
    )(qp, keys, cflat)


PEER_SPLIT = 1


def _peer_dense_kernel(h_ref, *refs, na):
    u_refs = refs[:PEER_SPLIT]
    vt_refs = refs[PEER_SPLIT:2 * PEER_SPLIT]
    (e1_ref, cnt_ref, e2_ref, r2_ref, x_ref, mod_ref, o_ref,
     acc_ref, s_ref, p_ref, e2b_ref, r2b_ref, ht_ref) = refs[2 * PEER_SPLIT:]
    j = pl.program_id(1)
    tt = s_ref.shape[1]

    @pl.when(j == 0)
    def _():
        acc_ref[...] = jnp.zeros_like(acc_ref)
        e2b_ref[...] = e2_ref[...].astype(BF16)
        r2b_ref[...] = r2_ref[...].astype(BF16)
        ht_ref[...] = h_ref[...].astype(F32).T.astype(BF16)

    sub = u_refs[0].shape[0]
    for k, u_ref in enumerate(u_refs):
        s_ref[k * sub:(k + 1) * sub, :] = jnp.dot(u_ref[...], ht_ref[...], preferred_element_type=F32)

    for aa in range(na):
        rows = slice(aa * N_KEYS, (aa + 1) * N_KEYS)
        for tg in range(tt // LANES):
            tsl = slice(tg * LANES, (tg + 1) * LANES)
            gate = None
            for hh in range(H_P):
                cnt_row = cnt_ref[hh, aa:aa + 1, tsl].astype(BF16)
                e1_row = e1_ref[hh, aa:aa + 1, tsl].astype(BF16)
                hsl = slice(hh * N_KEYS, (hh + 1) * N_KEYS)
                term = jnp.where(r2b_ref[hsl, tsl] < cnt_row, e2b_ref[hsl, tsl], 0.0) * e1_row
                gate = term if gate is None else gate + term
            p_ref[rows, tsl] = gate * _gelu_tanh(s_ref[rows, tsl].astype(BF16))

    dsub = vt_refs[0].shape[0]
    for k, vt_ref in enumerate(vt_refs):
        acc_ref[k * dsub:(k + 1) * dsub, :] += jnp.dot(vt_ref[...], p_ref[...], preferred_element_type=F32)

    @pl.when(j == pl.num_programs(1) - 1)
    def _():
        o_ref[...] = x_ref[...] + mod_ref[0][5:6] * acc_ref[...].T


def _peer_dense(st, h2, u, vt, e1, cnt, e2, r2, x, mod, tt, et):
    t, d = x.shape
    n_exp = u.shape[0]
    na = et // N_KEYS
    u_specs = [pl.BlockSpec((et // PEER_SPLIT, d),
                            functools.partial(lambda i, j, k: (j * PEER_SPLIT + k, 0), k=k))
               for k in range(PEER_SPLIT)]
    vt_specs = [pl.BlockSpec((d // PEER_SPLIT, et), functools.partial(lambda i, j, k: (k, j), k=k))
                for k in range(PEER_SPLIT)]
    row_spec = pl.BlockSpec((H_P, na, tt), lambda i, j: (0, j, i))
    full_spec = pl.BlockSpec((H_P * N_KEYS, tt), lambda i, j: (0, i))
    return pl.pallas_call(
        functools.partial(_peer_dense_kernel, na=na),
        grid=(t // tt, n_exp // et),
        in_specs=[pl.BlockSpec((tt, d), lambda i, j: (i, 0))] + u_specs + vt_specs + [
                  row_spec, row_spec, full_spec, full_spec,
                  pl.BlockSpec((tt, d), lambda i, j: (i, 0)),
                  pl.BlockSpec((1, 6, d), lambda i, j: (st.mod_row(i, tt), 0, 0))],
        out_specs=pl.BlockSpec((tt, d), lambda i, j: (i, 0)),
        out_shape=jax.ShapeDtypeStruct((t, d), F32),
        scratch_shapes=[pltpu.VMEM((d, tt), F32), pltpu.VMEM((et, tt), F32), pltpu.VMEM((et, tt), BF16),
                        pltpu.VMEM((H_P * N_KEYS, tt), BF16), pltpu.VMEM((H_P * N_KEYS, tt), BF16),
                        pltpu.VMEM((d, tt), BF16)],
        compiler_params=_cparams(("arbitrary", "arbitrary")),
        name="peer_dense",
    )(h2, *([u] * PEER_SPLIT), *([vt] * PEER_SPLIT), e1, cnt, e2, r2, x, mod)


def _final_norm_kernel(x_ref, g_ref, o_ref):
    o_ref[...] = _rms(x_ref[...], g_ref[...])


def _final_norm(x, g, tm):
    t, d = x.shape
    return pl.pallas_call(
        _final_norm_kernel,
        grid=(t // tm,),
        in_specs=[pl.BlockSpec((tm, d), lambda i: (i, 0)), pl.BlockSpec((1, d), lambda i: (0, 0))],
        out_specs=pl.BlockSpec((tm, d), lambda i: (i, 0)),
        out_shape=jax.ShapeDtypeStruct((t, d), F32),
        compiler_params=_cparams(("arbitrary",)),
        name="final_norm",
    )(x, g)


def _rope_tables(seq_len):
    pos = jnp.arange(seq_len)
    rowp = (pos // GRID_W).astype(F32)
    colp = (pos % GRID_W).astype(F32)
    nf = DK_A // 4
    freqs = ROPE_BASE ** (-jnp.arange(nf, dtype=F32) / nf)
    ang = jnp.concatenate([rowp[:, None] * freqs, colp[:, None] * freqs], axis=-1)
    cos = jnp.cos(ang)
    sin = jnp.sin(ang)
    cos_t = jnp.tile(cos, (1, 4))
    sin_t = jnp.tile(jnp.concatenate([-sin, sin], axis=-1), (1, 2))
    return cos_t, sin_t


def _pad_rows(w, rows):
    return jnp.concatenate([w, jnp.zeros((rows - w.shape[0], w.shape[1]), w.dtype)], axis=0)


def _lane_row(vals):
    flat = vals.reshape(-1).astype(F32)
    return jnp.concatenate([flat, jnp.zeros((LANES - flat.shape[0],), F32)])[None, :]


def kernel(x_prompt, x_sample, state_ret, state_gdn, c, c_ctx, ada_w, ada_b, norm_mix_g, norm_ffn_g,
           final_norm_g, ev_w_in, ev_w_out, ret_gamma_logit, ret_norm_g, sc_conv_w, od_w_in, od_w_out,
           gdn_conv_w, gdn_a_log, gdn_dt_bias, gdn_norm_g, cf_dw_w, cf_dw_b, cf_ln_g, cf_ln_b,
           peer_wq, peer_keys, peer_u, peer_v):
    bp, lp, d = x_prompt.shape
    bs, ls, _ = x_sample.shape
    depth = ada_w.shape[0]
    st = _Streams(bp, lp, bs, ls)
    w_a = H_A * DV_A
    w_b = d - w_a
    w_c = H_C * DV_C
    w_d = d - w_c
    tm = 512
    peer_tt = 512
    peer_et = 2048
    select_tt = 512

    x = jnp.concatenate([x_prompt.reshape(st.tp, d), x_sample.reshape(st.ts, d)], axis=0)
    cvec = jnp.concatenate([c_ctx[None, :], c, jnp.zeros((N_MOD_ROWS - 1 - bs, d), F32)], axis=0)
    mods = _modulation(cvec, ada_w, ada_b).reshape(depth, N_MOD_ROWS, 6, d)
    cos_t, sin_t = _rope_tables(ls)
    zero_ret = jnp.zeros((bp, 2, H_A, DK_A, DV_A), F32)
    zero_gdn = jnp.zeros((bp, 2, H_C, DK_C, DV_C), F32)

    ret_new, gdn_new = [], []
    for l in range(depth):
        i = l // 2
        mod = mods[l]
        g1 = norm_mix_g[l][None, :]
        if l % 2 == 0:
            widths = (H_A * DK_A, H_A * DK_A, w_a, w_a, w_b, w_b, w_b)
            q, k, v, g, bg, cg, hb = _norm_proj(st, x, mod, g1, ev_w_in[i].astype(BF16), widths, tm)
            gam = jnp.broadcast_to(ret_gamma_logit[i][:, :, None, None], (2, H_A, CHUNK, DV_A))
            ng = ret_norm_g[i][:, None, :]
            ya_p, s_new = _retention(q, k, v, g, zero_ret, gam, ng, cos_t[:lp], sin_t[:lp],
                                     bp, lp, 0, False)
            ya_s, _ = _retention(q, k, v, g, state_ret[:, i], gam, ng, cos_t, sin_t,
                                 bs, ls, st.tp // ls, True)
            ret_new.append(s_new)
            ya = jnp.concatenate([ya_p, ya_s], axis=0)
            yb = _short_gated_conv(st, bg, cg, hb, _pad_rows(sc_conv_w[i], SUBLANES))
            w_out = ev_w_out[i]
        else:
            n_gate = 2 * 2 * H_C
            w_in = od_w_in[i]
            o_ab = 4 * w_c
            w_main = jnp.concatenate([w_in[:, :o_ab], w_in[:, o_ab + n_gate:],
                                      w_in[:, o_ab:o_ab + n_gate],
                                      jnp.zeros((d, LANES - n_gate), F32)], axis=1).astype(BF16)
            widths = (3 * w_c, w_c, 2 * w_d, LANES)
            qkv, z, glu, ab = _norm_proj(st, x, mod, g1, w_main, widths, tm)
            qkv_n = _qkv_conv(st, qkv, _pad_rows(gdn_conv_w[i], SUBLANES))
            alog = _lane_row(gdn_a_log[i])
            dtb = _lane_row(gdn_dt_bias[i])
            ng = gdn_norm_g[i][None, :]
            yc_p, s_new = _gdn(qkv_n, z, ab, alog, dtb, zero_gdn, ng, bp, lp, 0)
            yc_s, _ = _gdn(qkv_n, z, ab, alog, dtb, state_gdn[:, i], ng, bs, ls, st.tp // ls)
            gdn_new.append(s_new)
            ya = jnp.concatenate([yc_p, yc_s], axis=0)
            yb = _conformer(st, glu, _pad_rows(cf_dw_w[i], 2 * CF_HALO), cf_dw_b[i][None, :],
                            cf_ln_g[i][None, :], cf_ln_b[i][None, :])
            w_out = od_w_out[i]
        x, h2, qp = _out_proj(st, ya, yb, x, mod, norm_ffn_g[l][None, :], w_out.astype(BF16),
                              peer_wq[l].astype(BF16), tm)
        e1, cnt, e2, r2 = _peer_select(qp, peer_keys[l].astype(BF16), select_tt)
        x = _peer_dense(st, h2, peer_u[l].astype(BF16), peer_v[l].astype(BF16).T, e1, cnt, e2, r2,
                        x, mod, peer_tt, peer_et)

    y = _final_norm(x, final_norm_g[None, :], tm)
    y_prompt = y[:st.tp].reshape(bp, lp, d)
    y_sample = y[st.tp:].reshape(bs, ls, d)
    new_state_ret = jnp.stack(ret_new, axis=1).astype(x_prompt.dtype)
    new_state_gdn = jnp.stack(gdn_new, axis=1).astype(x_prompt.dtype)
    return (y_prompt, y_sample, new_state_ret, new_state_gdn)
```

```python
import functools
import math

import jax
import jax.numpy as jnp
from jax import lax
from jax.experimental import pallas as pl
from jax.experimental.pallas import tpu as pltpu

F32 = jnp.float32
BF16 = jnp.bfloat16
HIGHEST = lax.Precision.HIGHEST

EPS = 1e-6
CHUNK = 64
GRID_W = 64
ROPE_BASE = 10000.0
H_A, DK_A, DV_A = 4, 64, 128
H_C, DK_C, DV_C = 4, 128, 128
SC_K, QKV_K, CF_K = 3, 3, 31
N_KEYS, H_P, TOPK = 128, 8, 16
N_MOD_ROWS = 16
LANES = 128
SUBLANES = 8
VMEM_LIMIT = 56 * 1024 * 1024
CONV_TILE = 256
NEG_INF = float("-inf")


def _cparams(sem):
    return pltpu.CompilerParams(dimension_semantics=sem, vmem_limit_bytes=VMEM_LIMIT)


def _bdot(a, b):
    return jnp.dot(a.astype(BF16), b.astype(BF16), preferred_element_type=F32)


def _bdot_nt(a, b):
    return lax.dot_general(a.astype(BF16), b.astype(BF16), (((1,), (1,)), ((), ())),
                           preferred_element_type=F32)


def _bdot_tn(a, b):
    return lax.dot_general(a.astype(BF16), b.astype(BF16), (((0,), (0,)), ((), ())),
                           preferred_element_type=F32)


def _hdot(a, b):
    return jnp.dot(a, b, precision=HIGHEST, preferred_element_type=F32)


def _sigmoid(x):
    return 1.0 / (1.0 + jnp.exp(-x))


def _silu(x):
    return x * _sigmoid(x)


def _softplus(x):
    return jnp.maximum(x, 0.0) + jnp.log1p(jnp.exp(-jnp.abs(x)))


def _log_sigmoid(x):
    return -_softplus(-x)


def _gelu_tanh(x):
    c = math.sqrt(2.0 / math.pi)
    return 0.5 * x * (1.0 + jnp.tanh(c * (x + 0.044715 * (x * x * x))))


def _rms(x, g):
    return x * lax.rsqrt(jnp.mean(x * x, axis=-1, keepdims=True) + EPS) * g


class _Streams:
    def __init__(self, bp, lp, bs, ls):
        self.bp, self.lp, self.bs, self.ls = bp, lp, bs, ls
        self.tp, self.ts = bp * lp, bs * ls
        self.t = self.tp + self.ts

    def mod_row(self, i, tile):
        tiles_p = self.tp // tile
        per_seq = self.ls // tile
        return jnp.where(i < tiles_p, 0, 1 + (i - tiles_p) // per_seq)

    def halo_flags(self, i, tile):
        tiles_p = self.tp // tile
        per_p = self.lp // tile
        per_s = self.ls // tile
        in_p = i < tiles_p
        jp = i % per_p
        js = (i - tiles_p) % per_s
        has_prev = jnp.where(in_p, jp > 0, js > 0)
        has_next = jnp.where(in_p, jp < per_p - 1, js < per_s - 1)
        return has_prev, has_next


def _mod_kernel(c_ref, w_ref, b_ref, o_ref):
    s = _silu(c_ref[...])
    o_ref[0] = _hdot(s, w_ref[0]) + b_ref[0]


def _modulation(cvec, ada_w, ada_b):
    depth, d, d6 = ada_w.shape
    nj = d6 // d
    return pl.pallas_call(
        _mod_kernel,
        grid=(depth, nj),
        in_specs=[pl.BlockSpec((N_MOD_ROWS, d), lambda l, j: (0, 0)),
                  pl.BlockSpec((1, d, d), lambda l, j: (l, 0, j)),
                  pl.BlockSpec((1, 1, d), lambda l, j: (l, 0, j))],
        out_specs=pl.BlockSpec((1, N_MOD_ROWS, d), lambda l, j: (l, 0, j)),
        out_shape=jax.ShapeDtypeStruct((depth, N_MOD_ROWS, d6), F32),
        compiler_params=_cparams(("arbitrary", "arbitrary")),
        name="modulation",
    )(cvec, ada_w, ada_b.reshape(depth, 1, d6))


def _norm_proj_kernel(x_ref, mod_ref, g_ref, w_ref, *o_refs, widths):
    m = mod_ref[0]
    h = _rms(x_ref[...], g_ref[...]) * (1.0 + m[1:2]) + m[0:1]
    p = jnp.dot(h.astype(BF16), w_ref[...], preferred_element_type=F32)
    off = 0
    for o_ref, wd in zip(o_refs, widths):
        o_ref[...] = p[:, off:off + wd]
        off += wd


def _norm_proj(st, x, mod, g, w, widths, tm):
    t, d = x.shape
    n = w.shape[1]
    return pl.pallas_call(
        functools.partial(_norm_proj_kernel, widths=widths),
        grid=(t // tm,),
        in_specs=[pl.BlockSpec((tm, d), lambda i: (i, 0)),
                  pl.BlockSpec((1, 6, d), lambda i: (st.mod_row(i, tm), 0, 0)),
                  pl.BlockSpec((1, d), lambda i: (0, 0)),
                  pl.BlockSpec((d, n), lambda i: (0, 0))],
        out_specs=[pl.BlockSpec((tm, wd), lambda i: (i, 0)) for wd in widths],
        out_shape=[jax.ShapeDtypeStruct((t, wd), F32) for wd in widths],
        compiler_params=_cparams(("arbitrary",)),
        name="norm_proj",
    )(x, mod, g, w)


def _ret_kernel(q_ref, k_ref, v_ref, g_ref, s0_ref, gam_ref, ng_ref, cos_ref, sin_ref,
                ya_ref, sout_ref, qs_ref, ks_ref, o_ref, sf_ref, sb_ref, *, seq_len, latent):
    nc = seq_len // CHUNK
    q = q_ref[...]
    k = k_ref[...] * (DK_A ** -0.5)
    if latent:
        lane = lax.broadcasted_iota(jnp.int32, q.shape, 1)
        first_half = (lane % DK_A) < (DK_A // 2)
        cos = cos_ref[...]
        sin = sin_ref[...]

        def rope(x):
            partner = jnp.where(first_half, pltpu.roll(x, LANES - DK_A // 2, 1),
                                pltpu.roll(x, DK_A // 2, 1))
            return x * cos + partner * sin

        q = rope(q)
        k = rope(k)
    qs_ref[...] = q
    ks_ref[...] = k

    cat = jnp.concatenate
    kw = 2 * DK_A
    vw = 2 * DV_A
    row = lax.broadcasted_iota(jnp.int32, (CHUNK, kw), 0).astype(F32)
    lane_k = lax.broadcasted_iota(jnp.int32, (CHUNK, kw), 1)
    lane_v = lax.broadcasted_iota(jnp.int32, (CHUNK, vw), 1)
    head0_k = lane_k < DK_A
    head0_v = lane_v < DV_A
    lg = [[_log_sigmoid(gam_ref[d, hh]) for hh in range(2)] for d in range(2)]
    lgf_k = jnp.where(head0_k, lg[0][0], lg[0][1])
    lgb_k = jnp.where(head0_k, lg[1][0], lg[1][1])
    diff = row - (lane_k % DK_A).astype(F32)
    dcomb = (jnp.where(diff >= 0, jnp.exp(lgf_k * jnp.maximum(diff, 0.0)), 0.0)
             + jnp.where(diff <= 0, jnp.exp(lgb_k * jnp.maximum(-diff, 0.0)), 0.0))
    wend_f = jnp.exp(lgf_k * (CHUNK - 1.0 - row))
    wstart_f = jnp.exp(lgf_k * (row + 1.0))
    wend_b = jnp.exp(lgb_k * row)
    wstart_b = jnp.exp(lgb_k * (CHUNK - row))

    def state_layout(h0, h1):
        zero = jnp.zeros_like(h0)
        return cat([cat([h0, zero], axis=1), cat([zero, h1], axis=1)], axis=0)

    on_diag = state_layout(jnp.ones((DK_A, DV_A), F32), jnp.ones((DK_A, DV_A), F32)) > 0.0
    gch_f = jnp.exp(state_layout(lg[0][0], lg[0][1]) * float(CHUNK))
    gch_b = jnp.exp(state_layout(lg[1][0], lg[1][1]) * float(CHUNK))
    sf_ref[...] = state_layout(s0_ref[0, 0, 0], s0_ref[0, 0, 1])
    sb_ref[...] = state_layout(s0_ref[0, 1, 0], s0_ref[0, 1, 1])
    o_ref[...] = jnp.zeros_like(o_ref)

    def step(i, carry):
        rows = pl.ds(pl.multiple_of(i * CHUNK, CHUNK), CHUNK)
        qn = qs_ref[rows, :]
        kn = ks_ref[rows, :]
        vn = v_ref[rows, :]
        k_bd = cat([jnp.where(head0_k, kn, 0.0), jnp.where(head0_k, 0.0, kn)], axis=0)
        v_bd = cat([jnp.where(head0_v, vn, 0.0), jnp.where(head0_v, 0.0, vn)], axis=0)
        a = _bdot_nt(qn, k_bd) * dcomb
        s_f = sf_ref[...]
        o_ref[rows, :] += _bdot(cat([a, qn * wstart_f], axis=1), cat([v_bd, s_f], axis=0))
        sf_ref[...] = s_f * gch_f + jnp.where(on_diag, _bdot_tn(kn * wend_f, vn), 0.0)
        rows = pl.ds(pl.multiple_of((nc - 1 - i) * CHUNK, CHUNK), CHUNK)
        qn = qs_ref[rows, :]
        kn = ks_ref[rows, :]
        vn = v_ref[rows, :]
        s_b = sb_ref[...]
        o_ref[rows, :] += _bdot(qn * wstart_b, s_b)
        sb_ref[...] = s_b * gch_b + jnp.where(on_diag, _bdot_tn(kn * wend_b, vn), 0.0)
        return carry

    lax.fori_loop(0, nc, step, 0)
    for d, s_ref in enumerate((sf_ref, sb_ref)):
        sout_ref[0, d, 0] = s_ref[:DK_A, :DV_A]
        sout_ref[0, d, 1] = s_ref[DK_A:, DV_A:]

    for hh in range(2):
        vsl = slice(hh * DV_A, (hh + 1) * DV_A)
        ya_ref[:, vsl] = _silu(g_ref[:, vsl]) * _rms(o_ref[:, vsl], ng_ref[hh])


def _retention(q, k, v, g, s0, gam, ng, cos_t, sin_t, nseq, seq_len, row_off, latent):
    w2 = 2 * DK_A
    v2 = 2 * DV_A
    return pl.pallas_call(
        functools.partial(_ret_kernel, seq_len=seq_len, latent=latent),
        grid=(nseq, H_A // 2),
        in_specs=[pl.BlockSpec((seq_len, w2), lambda b, p: (b + row_off, p)),
                  pl.BlockSpec((seq_len, w2), lambda b, p: (b + row_off, p)),
                  pl.BlockSpec((seq_len, v2), lambda b, p: (b + row_off, p)),
                  pl.BlockSpec((seq_len, v2), lambda b, p: (b + row_off, p)),
                  pl.BlockSpec((1, 2, 2, DK_A, DV_A), lambda b, p: (b, 0, p, 0, 0)),
                  pl.BlockSpec((2, 2, CHUNK, DV_A), lambda b, p: (0, p, 0, 0)),
                  pl.BlockSpec((2, 1, DV_A), lambda b, p: (p, 0, 0)),
                  pl.BlockSpec((seq_len, w2), lambda b, p: (0, 0)),
                  pl.BlockSpec((seq_len, w2), lambda b, p: (0, 0))],
        out_specs=[pl.BlockSpec((seq_len, v2), lambda b, p: (b, p)),
                   pl.BlockSpec((1, 2, 2, DK_A, DV_A), lambda b, p: (b, 0, p, 0, 0))],
        out_shape=[jax.ShapeDtypeStruct((nseq * seq_len, H_A * DV_A), F32),
                   jax.ShapeDtypeStruct((nseq, 2, H_A, DK_A, DV_A), F32)],
        scratch_shapes=[pltpu.VMEM((seq_len, w2), F32), pltpu.VMEM((seq_len, w2), F32),
                        pltpu.VMEM((seq_len, v2), F32), pltpu.VMEM((w2, v2), F32), pltpu.VMEM((w2, v2), F32)],
        compiler_params=_cparams(("arbitrary", "arbitrary")),
        name="retention",
    )(q, k, v, g, s0, gam, ng, cos_t, sin_t)


def _fill_padded(pad_ref, cur, prev, nxt, has_prev, has_next, halo):
    tile = cur.shape[0]
    pad_ref[0:halo, :] = jnp.where(has_prev, prev, 0.0)
    pad_ref[halo:halo + tile, :] = cur
    pad_ref[halo + tile:halo + tile + halo, :] = jnp.where(has_next, nxt, 0.0)


def _conv_taps(pad_ref, w_ref, ntaps, halo, tile):
    base = halo - ntaps // 2
    acc = w_ref[0:1, :] * pad_ref[base:base + tile, :]
    for kk in range(1, ntaps):
        acc = acc + w_ref[kk:kk + 1, :] * pad_ref[base + kk:base + kk + tile, :]
    return acc


def _halo_specs(width, col_map, halo):
    per = CONV_TILE // halo

    def cur(i, j):
        return (i, col_map(j))

    def prev(i, j):
        return (jnp.maximum(i * per - 1, 0), col_map(j))

    def make_next(nblk):
        def nxt(i, j):
            return (jnp.minimum((i + 1) * per, nblk - 1), col_map(j))
        return nxt

    return cur, prev, make_next


def _sconv_kernel(bg_ref, cg_ref, cgp_ref, cgn_ref, hb_ref, hbp_ref, hbn_ref, w_ref, o_ref, pad_ref, *, st):
    has_prev, has_next = st.halo_flags(pl.program_id(0), CONV_TILE)
    _fill_padded(pad_ref, cg_ref[...] * hb_ref[...], cgp_ref[...] * hbp_ref[...],
                 cgn_ref[...] * hbn_ref[...], has_prev, has_next, SUBLANES)
    o_ref[...] = bg_ref[...] * _conv_taps(pad_ref, w_ref, SC_K, SUBLANES, CONV_TILE)


CONV_COLS = 512


def _short_gated_conv(st, bg, cg, hb, w):
    t, c = bg.shape
    halo = SUBLANES
    cur, prev, make_next = _halo_specs(c, lambda j: j, halo)
    nxt = make_next(t // halo)
    tile_spec = pl.BlockSpec((CONV_TILE, CONV_COLS), cur)
    prev_spec = pl.BlockSpec((halo, CONV_COLS), prev)
    next_spec = pl.BlockSpec((halo, CONV_COLS), nxt)
    return pl.pallas_call(
        functools.partial(_sconv_kernel, st=st),
        grid=(t // CONV_TILE, c // CONV_COLS),
        in_specs=[tile_spec, tile_spec, prev_spec, next_spec, tile_spec, prev_spec, next_spec,
                  pl.BlockSpec((SUBLANES, CONV_COLS), lambda i, j: (0, j))],
        out_specs=tile_spec,
        out_shape=jax.ShapeDtypeStruct((t, c), F32),
        scratch_shapes=[pltpu.VMEM((CONV_TILE + 2 * halo, CONV_COLS), F32)],
        compiler_params=_cparams(("arbitrary", "arbitrary")),
        name="short_gated_conv",
    )(bg, cg, cg, cg, hb, hb, hb, w)


def _qkv_conv_kernel(x_ref, xp_ref, xn_ref, w_ref, o_ref, pad_ref, *, st):
    has_prev, has_next = st.halo_flags(pl.program_id(0), CONV_TILE)
    _fill_padded(pad_ref, x_ref[...], xp_ref[...], xn_ref[...], has_prev, has_next, SUBLANES)
    s = _silu(_conv_taps(pad_ref, w_ref, QKV_K, SUBLANES, CONV_TILE))
    j = pl.program_id(1)
    for hh in range(CONV_COLS // DK_C):
        hsl = slice(hh * DK_C, (hh + 1) * DK_C)
        sh = s[:, hsl]
        nrm = sh * lax.rsqrt(jnp.sum(sh * sh, axis=-1, keepdims=True) + EPS)
        o_ref[:, hsl] = jnp.where(j == 0, nrm * (DK_C ** -0.5), jnp.where(j == 1, nrm, sh))


def _qkv_conv(st, qkv, w):
    t, c = qkv.shape
    assert CONV_COLS == H_C * DK_C == H_C * DV_C
    halo = SUBLANES
    cur, prev, make_next = _halo_specs(c, lambda j: j, halo)
    nxt = make_next(t // halo)
    tile_spec = pl.BlockSpec((CONV_TILE, CONV_COLS), cur)
    return pl.pallas_call(
        functools.partial(_qkv_conv_kernel, st=st),
        grid=(t // CONV_TILE, c // CONV_COLS),
        in_specs=[tile_spec, pl.BlockSpec((halo, CONV_COLS), prev), pl.BlockSpec((halo, CONV_COLS), nxt),
                  pl.BlockSpec((SUBLANES, CONV_COLS), lambda i, j: (0, j))],
        out_specs=tile_spec,
        out_shape=jax.ShapeDtypeStruct((t, c), F32),
        scratch_shapes=[pltpu.VMEM((CONV_TILE + 2 * halo, CONV_COLS), F32)],
        compiler_params=_cparams(("arbitrary", "arbitrary")),
        name="qkv_conv",
    )(qkv, qkv, qkv, w)


CF_HALO = 16


def _conformer_kernel(ca_ref, cap_ref, can_ref, cg_ref, cgp_ref, cgn_ref, w_ref, b_ref, lg_ref, lb_ref,
                      o_ref, pad_ref, *, st):
    has_prev, has_next = st.halo_flags(pl.program_id(0), CONV_TILE)
    _fill_padded(pad_ref, ca_ref[...] * _sigmoid(cg_ref[...]), cap_ref[...] * _sigmoid(cgp_ref[...]),
                 can_ref[...] * _sigmoid(cgn_ref[...]), has_prev, has_next, CF_HALO)
    hc = _conv_taps(pad_ref, w_ref, CF_K, CF_HALO, CONV_TILE) + b_ref[...]
    mu = jnp.mean(hc, axis=-1, keepdims=True)
    xc = hc - mu
    y = xc * lax.rsqrt(jnp.mean(xc * xc, axis=-1, keepdims=True) + EPS) * lg_ref[...] + lb_ref[...]
    o_ref[...] = _silu(y)


def _conformer(st, glu, w, b, ln_g, ln_b):
    t, c2 = glu.shape
    c = c2 // 2
    nblk = t // CF_HALO
    per = CONV_TILE // CF_HALO
    vec = pl.BlockSpec((1, c), lambda i: (0, 0))
    return pl.pallas_call(
        functools.partial(_conformer_kernel, st=st),
        grid=(t // CONV_TILE,),
        in_specs=[pl.BlockSpec((CONV_TILE, c), lambda i: (i, 0)),
                  pl.BlockSpec((CF_HALO, c), lambda i: (jnp.maximum(i * per - 1, 0), 0)),
                  pl.BlockSpec((CF_HALO, c), lambda i: (jnp.minimum((i + 1) * per, nblk - 1), 0)),
                  pl.BlockSpec((CONV_TILE, c), lambda i: (i, 1)),
                  pl.BlockSpec((CF_HALO, c), lambda i: (jnp.maximum(i * per - 1, 0), 1)),
                  pl.BlockSpec((CF_HALO, c), lambda i: (jnp.minimum((i + 1) * per, nblk - 1), 1)),
                  pl.BlockSpec((2 * CF_HALO, c), lambda i: (0, 0)), vec, vec, vec],
        out_specs=pl.BlockSpec((CONV_TILE, c), lambda i: (i, 0)),
        out_shape=jax.ShapeDtypeStruct((t, c), F32),
        scratch_shapes=[pltpu.VMEM((CONV_TILE + 2 * CF_HALO, c), F32)],
        compiler_params=_cparams(("arbitrary",)),
        name="conformer_conv",
    )(glu, glu, glu, glu, glu, glu, w, b, ln_g, ln_b)


GDN_PACK = 4
GDN_GROUPS = 4


def _split_bf16(x):
    hi = x.astype(BF16)
    return hi, (x - hi.astype(F32)).astype(BF16)


def _packed_unit_inverses(groups):
    dot = functools.partial(jnp.dot, preferred_element_type=F32)
    cat = jnp.concatenate
    n = len(groups[0])
    c = CHUNK
    w = n * c
    eye = (lax.broadcasted_iota(jnp.int32, (c, c), 0) == lax.broadcasted_iota(jnp.int32, (c, c), 1)).astype(F32)
    lane_block = lax.broadcasted_iota(jnp.int32, (c, 2 * w), 1) % w // c

    def rhs_of(p):
        hi, lo = _split_bf16(p)
        both = cat([hi, lo], axis=1)
        return hi, lo, cat([jnp.where(lane_block == i, both, jnp.zeros_like(both)) for i in range(n)], axis=0)

    def product(m, r):
        return (r[:m, :w] + r[m:, :w]) + (r[:m, w:] + r[m:, w:])

    ps = [cat(a_list, axis=1) for a_list in groups]
    ts = [cat([eye - a for a in a_list], axis=1) for a_list in groups]
    for g, p in enumerate(ps):
        hi, lo, rhs = rhs_of(p)
        ps[g] = product(c, dot(cat([hi, lo], axis=0), rhs))
    steps = int(math.log2(c)) - 1
    for step in range(steps):
        for g in range(len(groups)):
            p_hi, p_lo, rhs = rhs_of(ps[g])
            t_hi, t_lo = _split_bf16(ts[g])
            if step < steps - 1:
                both = product(2 * c, dot(cat([t_hi, p_hi, t_lo, p_lo], axis=0), rhs))
                ts[g] = ts[g] + both[:c]
                ps[g] = both[c:]
            else:
                ts[g] = ts[g] + product(c, dot(cat([t_hi, t_lo], axis=0), rhs))
    return [[t[:, i * c:(i + 1) * c] for i in range(n)] for t in ts]


def _chunk_cumsum(x, reverse):
    row = lax.broadcasted_iota(jnp.int32, x.shape, 0)
    s = 1
    while s < CHUNK:
        if reverse:
            x = x + jnp.where(row < CHUNK - s, pltpu.roll(x, CHUNK - s, 0), 0.0)
        else:
            x = x + jnp.where(row >= s, pltpu.roll(x, s, 0), 0.0)
        s *= 2
    return x


def _gdn_kernel(q_ref, k_ref, v_ref, z_ref, ab_ref, alog_ref, dtb_ref, s0_ref, ng_ref,
                y_ref, sout_ref, g_ref, beta_ref, u_ref, w_ref, attn_ref, qg_ref, kg_ref, egl_ref, o_ref,
                *, seq_len):
    nc = seq_len // CHUNK
    h = pl.program_id(1)
    row = lax.broadcasted_iota(jnp.int32, (CHUNK, CHUNK), 0)
    col = lax.broadcasted_iota(jnp.int32, (CHUNK, CHUNK), 1)
    lane = lax.broadcasted_iota(jnp.int32, (CHUNK, LANES), 1)
    alog = alog_ref[...]
    dtb = dtb_ref[...]

    def gates(n, carry):
        rows = pl.ds(pl.multiple_of(n * CHUNK, CHUNK), CHUNK)
        ab = ab_ref[rows, :]
        g_all = -jnp.exp(alog) * _softplus(ab + dtb)
        b_all = _sigmoid(ab)
        for d in range(2):
            gsel = jnp.sum(jnp.where(lane == d * H_C + h, g_all, 0.0), axis=-1, keepdims=True)
            bsel = jnp.sum(jnp.where(lane == 2 * H_C + d * H_C + h, b_all, 0.0), axis=-1, keepdims=True)
            g_ref[d, rows, :] = jnp.broadcast_to(gsel, (CHUNK, LANES))
            beta_ref[d, rows, :] = jnp.broadcast_to(bsel, (CHUNK, LANES))
        return carry

    lax.fori_loop(0, nc, gates, 0)

    n_groups = min(GDN_GROUPS, nc // (GDN_PACK // 2))

    def precompute(it, carry):
        groups = [prepare_group(it * n_groups + grp) for grp in range(n_groups)]
        inverses = _packed_unit_inverses([[sysm[0] for sysm in systems] for systems in groups])
        for t_invs, systems in zip(inverses, groups):
            for t_inv, (_, rhs, d, rows) in zip(t_invs, systems):
                uw = _bdot(t_inv, rhs)
                u_ref[d, rows, :] = uw[:, :DV_C]
                w_ref[d, rows, :] = uw[:, DV_C:].astype(BF16)
        return carry

    def prepare_group(pair):
        systems = []
        for jc in range(GDN_PACK // 2):
            n = pair * (GDN_PACK // 2) + jc
            rows = pl.ds(pl.multiple_of(n * CHUNK, CHUNK), CHUNK)
            qc = q_ref[rows, :]
            kc = k_ref[rows, :]
            vc = v_ref[rows, :]
            gbs = [_chunk_cumsum(g_ref[d, rows, :], d == 1) for d in range(2)]
            bbs = [beta_ref[d, rows, :] for d in range(2)]
            kbs = [kc * bb for bb in bbs]
            prod = _bdot_nt(jnp.concatenate(kbs + [qc], axis=0), kc)
            for d in range(2):
                incl = (row >= col) if d == 0 else (row <= col)
                strict = (row > col) if d == 0 else (row < col)
                last = CHUNK - 1 if d == 0 else 0
                gb = gbs[d]
                gr = gb[:, :CHUNK].T
                gl = gb[last:last + 1, :]
                decay = jnp.where(incl, jnp.exp(jnp.where(incl, gb[:, :CHUNK] - gr, 0.0)), 0.0)
                eg = jnp.exp(gb)
                a_low = jnp.where(strict, prod[d * CHUNK:(d + 1) * CHUNK] * decay, 0.0)
                attn_ref[d, rows, :] = jnp.where(incl, prod[2 * CHUNK:] * decay, 0.0).astype(BF16)
                qg_ref[d, rows, :] = (qc * eg).astype(BF16)
                kg_ref[d, rows, :] = kc * jnp.exp(gl - gb)
                egl_ref[d, pl.ds(n, 1), :] = jnp.exp(gl)
                systems.append((a_low, jnp.concatenate([vc * bbs[d], kbs[d] * eg], axis=1), d, rows))
        return systems

    lax.fori_loop(0, nc // (n_groups * GDN_PACK // 2), precompute, 0)

    o_ref[...] = jnp.zeros_like(o_ref)

    def scan(i, carry):
        out = []
        for d, s in enumerate(carry):
            n = i if d == 0 else nc - 1 - i
            rows = pl.ds(pl.multiple_of(n * CHUNK, CHUNK), CHUNK)
            v_new = u_ref[d, rows, :] - _bdot(w_ref[d, rows, :], s)
            o_ref[rows, :] += _bdot(qg_ref[d, rows, :], s) + _bdot(attn_ref[d, rows, :], v_new)
            out.append(s * egl_ref[d, pl.ds(n, 1), :] + _bdot_tn(kg_ref[d, rows, :], v_new))
        return tuple(out)

    s_f, s_b = lax.fori_loop(0, nc, scan, (s0_ref[0, 0, 0], s0_ref[0, 1, 0]))
    sout_ref[0, 0, 0] = s_f
    sout_ref[0, 1, 0] = s_b
    y_ref[...] = _rms(o_ref[...], ng_ref[...]) * _silu(z_ref[...])


def _gdn(qkv_n, z, ab, alog, dtb, s0, ng, nseq, seq_len, row_off):
    blk = lambda off: pl.BlockSpec((seq_len, LANES), lambda b, h: (b + row_off, h + off))
    vec = pl.BlockSpec((1, LANES), lambda b, h: (0, 0))
    st_spec = pl.BlockSpec((1, 2, 1, DK_C, DV_C), lambda b, h: (b, 0, h, 0, 0))
    nc = seq_len // CHUNK
    both = lambda width, dtype: pltpu.VMEM((2, seq_len, width), dtype)
    return pl.pallas_call(
        functools.partial(_gdn_kernel, seq_len=seq_len),
        grid=(nseq, H_C),
        in_specs=[blk(0), blk(H_C), blk(2 * H_C), blk(0),
                  pl.BlockSpec((seq_len, LANES), lambda b, h: (b + row_off, 0)),
                  vec, vec, st_spec, vec],
        out_specs=[pl.BlockSpec((seq_len, LANES), lambda b, h: (b, h)), st_spec],
        out_shape=[jax.ShapeDtypeStruct((nseq * seq_len, H_C * DV_C), F32),
                   jax.ShapeDtypeStruct((nseq, 2, H_C, DK_C, DV_C), F32)],
        scratch_shapes=[both(LANES, F32), both(LANES, F32), both(DV_C, F32), both(DK_C, BF16),
                        both(CHUNK, BF16), both(DK_C, BF16), both(DK_C, F32),
                        pltpu.VMEM((2, max(nc, SUBLANES), LANES), F32), pltpu.VMEM((seq_len, DV_C), F32)],
        compiler_params=_cparams(("arbitrary", "arbitrary")),
        name="gated_deltanet",
    )(qkv_n, qkv_n, qkv_n, z, ab, alog, dtb, s0, ng)


def _out_proj_kernel(ya_ref, yb_ref, x_ref, mod_ref, g_ref, wo_ref, wq_ref, x1_ref, h2_ref, qp_ref):
    half = ya_ref.shape[1]
    m = mod_ref[0]
    y = (jnp.dot(ya_ref[...].astype(BF16), wo_ref[0:half, :], preferred_element_type=F32)
         + jnp.dot(yb_ref[...].astype(BF16), wo_ref[half:, :], preferred_element_type=F32))
    x1 = x_ref[...] + m[2:3] * y
    x1_ref[...] = x1
    h2 = (_rms(x1, g_ref[...]) * (1.0 + m[4:5]) + m[3:4]).astype(BF16)
    h2_ref[...] = h2
    qp_ref[...] = jnp.dot(h2, wq_ref[...], preferred_element_type=F32)


def _out_proj(st, ya, yb, x, mod, g, wo, wq, tm):
    t, d = x.shape
    half = ya.shape[1]
    nq = wq.shape[1]
    return pl.pallas_call(
        _out_proj_kernel,
        grid=(t // tm,),
        in_specs=[pl.BlockSpec((tm, half), lambda i: (i, 0)),
                  pl.BlockSpec((tm, half), lambda i: (i, 0)),
                  pl.BlockSpec((tm, d), lambda i: (i, 0)),
                  pl.BlockSpec((1, 6, d), lambda i: (st.mod_row(i, tm), 0, 0)),
                  pl.BlockSpec((1, d), lambda i: (0, 0)),
                  pl.BlockSpec((2 * half, d), lambda i: (0, 0)),
                  pl.BlockSpec((d, nq), lambda i: (0, 0))],
        out_specs=[pl.BlockSpec((tm, d), lambda i: (i, 0)),
                   pl.BlockSpec((tm, d), lambda i: (i, 0)),
                   pl.BlockSpec((tm, nq), lambda i: (i, 0))],
        out_shape=[jax.ShapeDtypeStruct((t, d), F32), jax.ShapeDtypeStruct((t, d), BF16),
                   jax.ShapeDtypeStruct((t, nq), F32)],
        compiler_params=_cparams(("arbitrary",)),
        name="out_proj",
    )(ya, yb, x, mod, g, wo, wq)


def _candidate_rows():
    groups = ([(0, k2) for k2 in range(16)], [(1, k2) for k2 in range(8)], [(2, k2) for k2 in range(8)],
              [(3, k2) for k2 in range(8)], [(k1, 0) for k1 in range(16)], [(k1, 1) for k1 in range(8)],
              [(k1, 2) for k1 in range(8)])
    rows, seen = [], set()
    for grp in groups:
        for k1, k2 in grp:
            ok = (k1 + 1) * (k2 + 1) <= TOPK and (k1, k2) not in seen
            if ok:
                seen.add((k1, k2))
            rows.append(float(k1 * TOPK + k2) if ok else CAND_INVALID)
    return rows


CAND_INVALID = float(TOPK * TOPK)
RANK_MARK = 2.0 ** 100
RANK_STEP = 2.0 ** 95
SELECT_STRIPS = 2


def _pack_candidates(a1, a2, op):
    return jnp.concatenate([op(a1[0:1], a2), op(a1[1:2], a2[0:8]), op(a1[2:3], a2[0:8]),
                            op(a1[3:4], a2[0:8]), op(a1, a2[0:1]), op(a1[0:8], a2[1:2]),
                            op(a1[0:8], a2[2:3])], axis=0)


def _top16(s, vals_ref, exact):
    n = s.shape[0]
    if not exact:
        for kk in range(TOPK):
            m = jnp.max(s, axis=0, keepdims=True)
            vals_ref[kk:kk + 1, :] = m
            s = jnp.where(s == m, -(RANK_MARK + kk * RANK_STEP), s)
        return jnp.where(s < -0.5 * RANK_MARK, (-s - RANK_MARK) * (1.0 / RANK_STEP), float(TOPK))
    iota = lax.broadcasted_iota(jnp.int32, s.shape, 0).astype(F32)
    rank = jnp.full(s.shape, float(TOPK), F32)
    for kk in range(TOPK):
        m = jnp.max(s, axis=0, keepdims=True)
        idx = jnp.min(jnp.where(s == m, iota, float(n)), axis=0, keepdims=True)
        hit = iota == idx
        rank = jnp.where(hit, float(kk), rank)
        vals_ref[kk:kk + 1, :] = m
        s = jnp.where(hit, NEG_INF, s)
    return rank


def _peer_select_kernel(q_ref, keys_ref, cflat_ref, e1_ref, cnt_ref, e2_ref, r2_ref, v1_ref, v2_ref, *, tt):
    nk = N_KEYS
    cflat = cflat_ref[...]
    valid = cflat < CAND_INVALID
    row16 = lax.broadcasted_iota(jnp.int32, (TOPK, LANES), 0)

    def compute(strip, exact):
        s1, s2, tsl, va_ref, vb_ref = strip
        rank1 = _top16(s1, va_ref, exact)
        rank2 = _top16(s2, vb_ref, exact)
        v1 = va_ref[...]
        v2 = vb_ref[...]
        cand = jnp.where(valid, _pack_candidates(v1, v2, jnp.add), NEG_INF)
        prod = _pack_candidates(jnp.exp(v1 - v1[0:1]), jnp.exp(v2 - v2[0:1]), jnp.multiply)
        sel = jnp.zeros(cand.shape, F32)
        for _ in range(TOPK):
            m = jnp.max(cand, axis=0, keepdims=True)
            if exact:
                idx = jnp.min(jnp.where(cand == m, cflat, CAND_INVALID), axis=0, keepdims=True)
                hit = cflat == idx
            else:
                hit = cand == m
            sel = jnp.where(hit, 1.0, sel)
            cand = jnp.where(hit, NEG_INF, cand)
        zsum = jnp.sum(sel * prod, axis=0, keepdims=True)
        c_all = sel[40:56] + jnp.concatenate([sel[56:64] + sel[64:72], jnp.zeros((8, LANES), F32)], axis=0)
        for k1, (lo, hi) in enumerate(((0, 16), (16, 24), (24, 32), (32, 40))):
            c_all = jnp.where(row16 == k1, jnp.sum(sel[lo:hi], axis=0, keepdims=True), c_all)
        cnt = jnp.zeros(rank1.shape, F32)
        for k1 in range(TOPK):
            cnt = jnp.where(rank1 == float(k1), c_all[k1:k1 + 1], cnt)
        in1 = rank1 < float(TOPK)
        in2 = rank2 < float(TOPK)
        e1_ref[0, :, tsl] = jnp.where(in1, jnp.exp(s1 - v1[0:1]), 0.0) / zsum
        cnt_ref[0, :, tsl] = cnt
        e2_ref[:, tsl] = jnp.where(in2, jnp.exp(s2 - v2[0:1]), 0.0).astype(e2_ref.dtype)
        r2_ref[:, tsl] = rank2.astype(r2_ref.dtype)
        n1 = jnp.sum(jnp.where(in1, 1.0, 0.0), axis=0, keepdims=True)
        n2 = jnp.sum(jnp.where(in2, 1.0, 0.0), axis=0, keepdims=True)
        nc = jnp.sum(sel, axis=0, keepdims=True)
        want = float(TOPK)
        return jnp.abs(n1 - want) + jnp.abs(n2 - want) + jnp.abs(nc - want)

    def strip_group(si, carry):
        strips = []
        for sub in range(SELECT_STRIPS):
            t0 = pl.multiple_of((si * SELECT_STRIPS + sub) * LANES, LANES)
            tsl = pl.ds(t0, LANES)
            s1 = _bdot_nt(keys_ref[0, 0], q_ref[tsl, 0:nk])
            s2 = _bdot_nt(keys_ref[0, 1], q_ref[tsl, nk:2 * nk])
            strips.append((s1, s2, tsl, v1_ref.at[sub], v2_ref.at[sub]))
        tied = [compute(strip, False) for strip in strips]
        for strip, bad in zip(strips, tied):
            @pl.when(jnp.max(bad) > 0.0)
            def _(strip=strip):
                compute(strip, True)
        return carry

    lax.fori_loop(0, tt // (SELECT_STRIPS * LANES), strip_group, 0)


def _peer_select(qp, keys, tt):
    t = qp.shape[0]
    rows = _candidate_rows()
    cflat = jnp.broadcast_to(jnp.asarray(rows, F32)[:, None], (len(rows), LANES))
    out_spec = pl.BlockSpec((1, N_KEYS, tt), lambda i, h: (h, 0, i))
    f32_sds = jax.ShapeDtypeStruct((H_P, N_KEYS, t), F32)
    flat_spec = pl.BlockSpec((N_KEYS, tt), lambda i, h: (h, i))
    flat_sds = jax.ShapeDtypeStruct((H_P * N_KEYS, t), BF16)
    return pl.pallas_call(
        functools.partial(_peer_select_kernel, tt=tt),
        grid=(t // tt, H_P),
        in_specs=[pl.BlockSpec((tt, 2 * N_KEYS), lambda i, h: (i, h)),
                  pl.BlockSpec((1, 2, N_KEYS, N_KEYS), lambda i, h: (h, 0, 0, 0)),
                  pl.BlockSpec(cflat.shape, lambda i, h: (0, 0))],
        out_specs=[out_spec, out_spec, flat_spec, flat_spec],
        out_shape=[f32_sds, f32_sds, flat_sds, flat_sds],
        scratch_shapes=[pltpu.VMEM((SELECT_STRIPS, TOPK, LANES), F32),
                        pltpu.VMEM((SELECT_STRIPS, TOPK, LANES), F32)],
        compiler_params=_cparams(("arbitrary", "arbitrary")),
        name="peer_select",
    )(qp, keys, cflat)


def _peer_dense_kernel(h_ref, u_ref, vt_ref, e1_ref, cnt_ref, e2_ref, r2_ref, x_ref, mod_ref, o_ref,
                       acc_ref, s_ref, p_ref, e2b_ref, r2b_ref, ht_ref, *, na):
    j = pl.program_id(1)
    tt = s_ref.shape[1]

    @pl.when(j == 0)
    def _():
        acc_ref[...] = jnp.zeros_like(acc_ref)
        e2b_ref[...] = e2_ref[...].astype(BF16)
        r2b_ref[...] = r2_ref[...].astype(BF16)
        ht_ref[...] = h_ref[...].astype(F32).T.astype(BF16)

    s_ref[...] = jnp.dot(u_ref[...], ht_ref[...], preferred_element_type=F32)

    for aa in range(na):
        rows = slice(aa * N_KEYS, (aa + 1) * N_KEYS)
        for tg in range(tt // LANES):
            tsl = slice(tg * LANES, (tg + 1) * LANES)
            gate = None
            for hh in range(H_P):
                cnt_row = cnt_ref[hh, aa:aa + 1, tsl].astype(BF16)
                e1_row = e1_ref[hh, aa:aa + 1, tsl].astype(BF16)
                hsl = slice(hh * N_KEYS, (hh + 1) * N_KEYS)
                term = jnp.where(r2b_ref[hsl, tsl] < cnt_row, e2b_ref[hsl, tsl], 0.0) * e1_row
                gate = term if gate is None else gate + term
            p_ref[rows, tsl] = gate * _gelu_tanh(s_ref[rows, tsl].astype(BF16))

    vt = vt_ref[0] if len(vt_ref.shape) == 3 else vt_ref[...]
    acc_ref[...] += jnp.dot(vt, p_ref[...], preferred_element_type=F32)

    @pl.when(j == pl.num_programs(1) - 1)
    def _():
        o_ref[...] = x_ref[...] + mod_ref[0][5:6] * acc_ref[...].T


def _peer_dense(st, h2, u, v, e1, cnt, e2, r2, x, mod, tt, et, v_tiled):
    t, d = x.shape
    n_exp = u.shape[0]
    na = et // N_KEYS
    if v_tiled:
        vt = v.reshape(n_exp // et, et, d).transpose(0, 2, 1)
        vt_spec = pl.BlockSpec((1, d, et), lambda i, j: (j, 0, 0))
    else:
        vt = v.T
        vt_spec = pl.BlockSpec((d, et), lambda i, j: (0, j))
    row_spec = pl.BlockSpec((H_P, na, tt), lambda i, j: (0, j, i))
    full_spec = pl.BlockSpec((H_P * N_KEYS, tt), lambda i, j: (0, i))
    return pl.pallas_call(
        functools.partial(_peer_dense_kernel, na=na),
        grid=(t // tt, n_exp // et),
        in_specs=[pl.BlockSpec((tt, d), lambda i, j: (i, 0)),
                  pl.BlockSpec((et, d), lambda i, j: (j, 0)), vt_spec,
                  row_spec, row_spec, full_spec, full_spec,
                  pl.BlockSpec((tt, d), lambda i, j: (i, 0)),
                  pl.BlockSpec((1, 6, d), lambda i, j: (st.mod_row(i, tt), 0, 0))],
        out_specs=pl.BlockSpec((tt, d), lambda i, j: (i, 0)),
        out_shape=jax.ShapeDtypeStruct((t, d), F32),
        scratch_shapes=[pltpu.VMEM((d, tt), F32), pltpu.VMEM((et, tt), F32), pltpu.VMEM((et, tt), BF16),
                        pltpu.VMEM((H_P * N_KEYS, tt), BF16), pltpu.VMEM((H_P * N_KEYS, tt), BF16),
                        pltpu.VMEM((d, tt), BF16)],
        compiler_params=_cparams(("arbitrary", "arbitrary")),
        name="peer_dense",
    )(h2, u, vt, e1, cnt, e2, r2, x, mod)


def _final_norm_kernel(x_ref, g_ref, o_ref):
    o_ref[...] = _rms(x_ref[...], g_ref[...])


def _final_norm(x, g, tm):
    t, d = x.shape
    return pl.pallas_call(
        _final_norm_kernel,
        grid=(t // tm,),
        in_specs=[pl.BlockSpec((tm, d), lambda i: (i, 0)), pl.BlockSpec((1, d), lambda i: (0, 0))],
        out_specs=pl.BlockSpec((tm, d), lambda i: (i, 0)),
        out_shape=jax.ShapeDtypeStruct((t, d), F32),
        compiler_params=_cparams(("arbitrary",)),
        name="final_norm",
    )(x, g)


def _rope_tables(seq_len):
    pos = jnp.arange(seq_len)
    rowp = (pos // GRID_W).astype(F32)
    colp = (pos % GRID_W).astype(F32)
    nf = DK_A // 4
    freqs = ROPE_BASE ** (-jnp.arange(nf, dtype=F32) / nf)
    ang = jnp.concatenate([rowp[:, None] * freqs, colp[:, None] * freqs], axis=-1)
    cos = jnp.cos(ang)
    sin = jnp.sin(ang)
    cos_t = jnp.tile(cos, (1, 4))
    sin_t = jnp.tile(jnp.concatenate([-sin, sin], axis=-1), (1, 2))
    return cos_t, sin_t


def _pad_rows(w, rows):
    return jnp.concatenate([w, jnp.zeros((rows - w.shape[0], w.shape[1]), w.dtype)], axis=0)


def _lane_row(vals):
    flat = vals.reshape(-1).astype(F32)
    return jnp.concatenate([flat, jnp.zeros((LANES - flat.shape[0],), F32)])[None, :]


def kernel(x_prompt, x_sample, state_ret, state_gdn, c, c_ctx, ada_w, ada_b, norm_mix_g, norm_ffn_g,
           final_norm_g, ev_w_in, ev_w_out, ret_gamma_logit, ret_norm_g, sc_conv_w, od_w_in, od_w_out,
           gdn_conv_w, gdn_a_log, gdn_dt_bias, gdn_norm_g, cf_dw_w, cf_dw_b, cf_ln_g, cf_ln_b,
           peer_wq, peer_keys, peer_u, peer_v):
    bp, lp, d = x_prompt.shape
    bs, ls, _ = x_sample.shape
    depth = ada_w.shape[0]
    st = _Streams(bp, lp, bs, ls)
    w_a = H_A * DV_A
    w_b = d - w_a
    w_c = H_C * DV_C
    w_d = d - w_c
    tm = 512
    peer_cfg = ((512, 2048, False), (1024, 1024, False), (512, 2048, True), (1024, 1024, True))
    select_tt = 512

    x = jnp.concatenate([x_prompt.reshape(st.tp, d), x_sample.reshape(st.ts, d)], axis=0)
    cvec = jnp.concatenate([c_ctx[None, :], c, jnp.zeros((N_MOD_ROWS - 1 - bs, d), F32)], axis=0)
    mods = _modulation(cvec, ada_w, ada_b).reshape(depth, N_MOD_ROWS, 6, d)
    cos_t, sin_t = _rope_tables(ls)
    zero_ret = jnp.zeros((bp, 2, H_A, DK_A, DV_A), F32)
    zero_gdn = jnp.zeros((bp, 2, H_C, DK_C, DV_C), F32)

    ret_new, gdn_new = [], []
    for l in range(depth):
        i = l // 2
        mod = mods[l]
        g1 = norm_mix_g[l][None, :]
        if l % 2 == 0:
            widths = (H_A * DK_A, H_A * DK_A, w_a, w_a, w_b, w_b, w_b)
            q, k, v, g, bg, cg, hb = _norm_proj(st, x, mod, g1, ev_w_in[i].astype(BF16), widths, tm)
            gam = jnp.broadcast_to(ret_gamma_logit[i][:, :, None, None], (2, H_A, CHUNK, DV_A))
            ng = ret_norm_g[i][:, None, :]
            ya_p, s_new = _retention(q, k, v, g, zero_ret, gam, ng, cos_t[:lp], sin_t[:lp],
                                     bp, lp, 0, False)
            ya_s, _ = _retention(q, k, v, g, state_ret[:, i], gam, ng, cos_t, sin_t,
                                 bs, ls, st.tp // ls, True)
            ret_new.append(s_new)
            ya = jnp.concatenate([ya_p, ya_s], axis=0)
            yb = _short_gated_conv(st, bg, cg, hb, _pad_rows(sc_conv_w[i], SUBLANES))
            w_out = ev_w_out[i]
        else:
            n_gate = 2 * 2 * H_C
            w_in = od_w_in[i]
            o_ab = 4 * w_c
            w_main = jnp.concatenate([w_in[:, :o_ab], w_in[:, o_ab + n_gate:],
                                      w_in[:, o_ab:o_ab + n_gate],
                                      jnp.zeros((d, LANES - n_gate), F32)], axis=1).astype(BF16)
            widths = (3 * w_c, w_c, 2 * w_d, LANES)
            qkv, z, glu, ab = _norm_proj(st, x, mod, g1, w_main, widths, tm)
            qkv_n = _qkv_conv(st, qkv, _pad_rows(gdn_conv_w[i], SUBLANES))
            alog = _lane_row(gdn_a_log[i])
            dtb = _lane_row(gdn_dt_bias[i])
            ng = gdn_norm_g[i][None, :]
            yc_p, s_new = _gdn(qkv_n, z, ab, alog, dtb, zero_gdn, ng, bp, lp, 0)
            yc_s, _ = _gdn(qkv_n, z, ab, alog, dtb, state_gdn[:, i], ng, bs, ls, st.tp // ls)
            gdn_new.append(s_new)
            ya = jnp.concatenate([yc_p, yc_s], axis=0)
            yb = _conformer(st, glu, _pad_rows(cf_dw_w[i], 2 * CF_HALO), cf_dw_b[i][None, :],
                            cf_ln_g[i][None, :], cf_ln_b[i][None, :])
            w_out = od_w_out[i]
        x, h2, qp = _out_proj(st, ya, yb, x, mod, norm_ffn_g[l][None, :], w_out.astype(BF16),
                              peer_wq[l].astype(BF16), tm)
        e1, cnt, e2, r2 = _peer_select(qp, peer_keys[l].astype(BF16), select_tt)
        x = _peer_dense(st, h2, peer_u[l].astype(BF16), peer_v[l].astype(BF16), e1, cnt, e2, r2,
                        x, mod, *peer_cfg[l % len(peer_cfg)])

    y = _final_norm(x, final_norm_g[None, :], tm)
    y_prompt = y[:st.tp].reshape(bp, lp, d)
    y_sample = y[st.tp:].reshape(bs, ls, d)
    new_state_ret = jnp.stack(ret_new, axis=1).astype(x_prompt.dtype)
    new_state_gdn = jnp.stack(gdn_new, axis=1).astype(x_prompt.dtype)
    return (y_prompt, y_sample, new_state_ret, new_state_gdn)
```

```python
import functools
import math

import jax
import jax.numpy as jnp
from jax import lax
from jax.experimental import pallas as pl
from jax.experimental.pallas import tpu as pltpu

F32 = jnp.float32
BF16 = jnp.bfloat16
HIGHEST = lax.Precision.HIGHEST

EPS = 1e-6
CHUNK = 64
GRID_W = 64
ROPE_BASE = 10000.0
H_A, DK_A, DV_A = 4, 64, 128
H_C, DK_C, DV_C = 4, 128, 128
SC_K, QKV_K, CF_K = 3, 3, 31
N_KEYS, H_P, TOPK = 128, 8, 16
N_MOD_ROWS = 16
LANES = 128
SUBLANES = 8
VMEM_LIMIT = 56 * 1024 * 1024
CONV_TILE = 256
NEG_INF = float("-inf")


def _cparams(sem):
    return pltpu.CompilerParams(dimension_semantics=sem, vmem_limit_bytes=VMEM_LIMIT)


def _bdot(a, b):
    return jnp.dot(a.astype(BF16), b.astype(BF16), preferred_element_type=F32)


def _bdot_nt(a, b):
    return lax.dot_general(a.astype(BF16), b.astype(BF16), (((1,), (1,)), ((), ())),
                           preferred_element_type=F32)


def _bdot_tn(a, b):
    return lax.dot_general(a.astype(BF16), b.astype(BF16), (((0,), (0,)), ((), ())),
                           preferred_element_type=F32)


def _hdot(a, b):
    return jnp.dot(a, b, precision=HIGHEST, preferred_element_type=F32)


def _sigmoid(x):
    return 1.0 / (1.0 + jnp.exp(-x))


def _silu(x):
    return x * _sigmoid(x)


def _softplus(x):
    return jnp.maximum(x, 0.0) + jnp.log1p(jnp.exp(-jnp.abs(x)))


def _log_sigmoid(x):
    return -_softplus(-x)


def _gelu_tanh(x):
    c = math.sqrt(2.0 / math.pi)
    return 0.5 * x * (1.0 + jnp.tanh(c * (x + 0.044715 * (x * x * x))))


def _rms(x, g):
    return x * lax.rsqrt(jnp.mean(x * x, axis=-1, keepdims=True) + EPS) * g


class _Streams:
    def __init__(self, bp, lp, bs, ls):
        self.bp, self.lp, self.bs, self.ls = bp, lp, bs, ls
        self.tp, self.ts = bp * lp, bs * ls
        self.t = self.tp + self.ts

    def mod_row(self, i, tile):
        tiles_p = self.tp // tile
        per_seq = self.ls // tile
        return jnp.where(i < tiles_p, 0, 1 + (i - tiles_p) // per_seq)

    def halo_flags(self, i, tile):
        tiles_p = self.tp // tile
        per_p = self.lp // tile
        per_s = self.ls // tile
        in_p = i < tiles_p
        jp = i % per_p
        js = (i - tiles_p) % per_s
        has_prev = jnp.where(in_p, jp > 0, js > 0)
        has_next = jnp.where(in_p, jp < per_p - 1, js < per_s - 1)
        return has_prev, has_next


def _mod_kernel(c_ref, w_ref, b_ref, o_ref):
    s = _silu(c_ref[...])
    o_ref[0] = _hdot(s, w_ref[0]) + b_ref[0]


def _modulation(cvec, ada_w, ada_b):
    depth, d, d6 = ada_w.shape
    nj = d6 // d
    return pl.pallas_call(
        _mod_kernel,
        grid=(depth, nj),
        in_specs=[pl.BlockSpec((N_MOD_ROWS, d), lambda l, j: (0, 0)),
                  pl.BlockSpec((1, d, d), lambda l, j: (l, 0, j)),
                  pl.BlockSpec((1, 1, d), lambda l, j: (l, 0, j))],
        out_specs=pl.BlockSpec((1, N_MOD_ROWS, d), lambda l, j: (l, 0, j)),
        out_shape=jax.ShapeDtypeStruct((depth, N_MOD_ROWS, d6), F32),
        compiler_params=_cparams(("arbitrary", "arbitrary")),
        name="modulation",
    )(cvec, ada_w, ada_b.reshape(depth, 1, d6))


def _norm_proj_kernel(x_ref, mod_ref, g_ref, w_ref, *o_refs, widths):
    m = mod_ref[0]
    h = _rms(x_ref[...], g_ref[...]) * (1.0 + m[1:2]) + m[0:1]
    p = jnp.dot(h.astype(BF16), w_ref[...], preferred_element_type=F32)
    off = 0
    for o_ref, wd in zip(o_refs, widths):
        o_ref[...] = p[:, off:off + wd]
        off += wd


def _norm_proj(st, x, mod, g, w, widths, tm):
    t, d = x.shape
    n = w.shape[1]
    return pl.pallas_call(
        functools.partial(_norm_proj_kernel, widths=widths),
        grid=(t // tm,),
        in_specs=[pl.BlockSpec((tm, d), lambda i: (i, 0)),
                  pl.BlockSpec((1, 6, d), lambda i: (st.mod_row(i, tm), 0, 0)),
                  pl.BlockSpec((1, d), lambda i: (0, 0)),
                  pl.BlockSpec((d, n), lambda i: (0, 0))],
        out_specs=[pl.BlockSpec((tm, wd), lambda i: (i, 0)) for wd in widths],
        out_shape=[jax.ShapeDtypeStruct((t, wd), F32) for wd in widths],
        compiler_params=_cparams(("arbitrary",)),
        name="norm_proj",
    )(x, mod, g, w)


def _ret_kernel(q_ref, k_ref, v_ref, g_ref, s0_ref, gam_ref, ng_ref, cos_ref, sin_ref,
                ya_ref, sout_ref, qs_ref, ks_ref, o_ref, sf_ref, sb_ref, *, seq_len, latent):
    nc = seq_len // CHUNK
    q = q_ref[...]
    k = k_ref[...] * (DK_A ** -0.5)
    if latent:
        lane = lax.broadcasted_iota(jnp.int32, q.shape, 1)
        first_half = (lane % DK_A) < (DK_A // 2)
        cos = cos_ref[...]
        sin = sin_ref[...]

        def rope(x):
            partner = jnp.where(first_half, pltpu.roll(x, LANES - DK_A // 2, 1),
                                pltpu.roll(x, DK_A // 2, 1))
            return x * cos + partner * sin

        q = rope(q)
        k = rope(k)
    qs_ref[...] = q
    ks_ref[...] = k

    cat = jnp.concatenate
    kw = 2 * DK_A
    vw = 2 * DV_A
    row = lax.broadcasted_iota(jnp.int32, (CHUNK, kw), 0).astype(F32)
    lane_k = lax.broadcasted_iota(jnp.int32, (CHUNK, kw), 1)
    lane_v = lax.broadcasted_iota(jnp.int32, (CHUNK, vw), 1)
    head0_k = lane_k < DK_A
    head0_v = lane_v < DV_A
    lg = [[_log_sigmoid(gam_ref[d, hh]) for hh in range(2)] for d in range(2)]
    lgf_k = jnp.where(head0_k, lg[0][0], lg[0][1])
    lgb_k = jnp.where(head0_k, lg[1][0], lg[1][1])
    diff = row - (lane_k % DK_A).astype(F32)
    dcomb = (jnp.where(diff >= 0, jnp.exp(lgf_k * jnp.maximum(diff, 0.0)), 0.0)
             + jnp.where(diff <= 0, jnp.exp(lgb_k * jnp.maximum(-diff, 0.0)), 0.0))
    wend_f = jnp.exp(lgf_k * (CHUNK - 1.0 - row))
    wstart_f = jnp.exp(lgf_k * (row + 1.0))
    wend_b = jnp.exp(lgb_k * row)
    wstart_b = jnp.exp(lgb_k * (CHUNK - row))

    def state_layout(h0, h1):
        zero = jnp.zeros_like(h0)
        return cat([cat([h0, zero], axis=1), cat([zero, h1], axis=1)], axis=0)

    on_diag = state_layout(jnp.ones((DK_A, DV_A), F32), jnp.ones((DK_A, DV_A), F32)) > 0.0
    gch_f = jnp.exp(state_layout(lg[0][0], lg[0][1]) * float(CHUNK))
    gch_b = jnp.exp(state_layout(lg[1][0], lg[1][1]) * float(CHUNK))
    sf_ref[...] = state_layout(s0_ref[0, 0, 0], s0_ref[0, 0, 1])
    sb_ref[...] = state_layout(s0_ref[0, 1, 0], s0_ref[0, 1, 1])
    o_ref[...] = jnp.zeros_like(o_ref)

    def step(i, carry):
        rows = pl.ds(pl.multiple_of(i * CHUNK, CHUNK), CHUNK)
        qn = qs_ref[rows, :]
        kn = ks_ref[rows, :]
        vn = v_ref[rows, :]
        k_bd = cat([jnp.where(head0_k, kn, 0.0), jnp.where(head0_k, 0.0, kn)], axis=0)
        v_bd = cat([jnp.where(head0_v, vn, 0.0), jnp.where(head0_v, 0.0, vn)], axis=0)
        a = _bdot_nt(qn, k_bd) * dcomb
        s_f = sf_ref[...]
        o_ref[rows, :] += _bdot(cat([a, qn * wstart_f], axis=1), cat([v_bd, s_f], axis=0))
        sf_ref[...] = s_f * gch_f + jnp.where(on_diag, _bdot_tn(kn * wend_f, vn), 0.0)
        rows = pl.ds(pl.multiple_of((nc - 1 - i) * CHUNK, CHUNK), CHUNK)
        qn = qs_ref[rows, :]
        kn = ks_ref[rows, :]
        vn = v_ref[rows, :]
        s_b = sb_ref[...]
        o_ref[rows, :] += _bdot(qn * wstart_b, s_b)
        sb_ref[...] = s_b * gch_b + jnp.where(on_diag, _bdot_tn(kn * wend_b, vn), 0.0)
        return carry

    lax.fori_loop(0, nc, step, 0)
    for d, s_ref in enumerate((sf_ref, sb_ref)):
        sout_ref[0, d, 0] = s_ref[:DK_A, :DV_A]
        sout_ref[0, d, 1] = s_ref[DK_A:, DV_A:]

    for hh in range(2):
        vsl = slice(hh * DV_A, (hh + 1) * DV_A)
        ya_ref[:, vsl] = _silu(g_ref[:, vsl]) * _rms(o_ref[:, vsl], ng_ref[hh])


def _retention(q, k, v, g, s0, gam, ng, cos_t, sin_t, nseq, seq_len, row_off, latent):
    w2 = 2 * DK_A
    v2 = 2 * DV_A
    return pl.pallas_call(
        functools.partial(_ret_kernel, seq_len=seq_len, latent=latent),
        grid=(nseq, H_A // 2),
        in_specs=[pl.BlockSpec((seq_len, w2), lambda b, p: (b + row_off, p)),
                  pl.BlockSpec((seq_len, w2), lambda b, p: (b + row_off, p)),
                  pl.BlockSpec((seq_len, v2), lambda b, p: (b + row_off, p)),
                  pl.BlockSpec((seq_len, v2), lambda b, p: (b + row_off, p)),
                  pl.BlockSpec((1, 2, 2, DK_A, DV_A), lambda b, p: (b, 0, p, 0, 0)),
                  pl.BlockSpec((2, 2, CHUNK, DV_A), lambda b, p: (0, p, 0, 0)),
                  pl.BlockSpec((2, 1, DV_A), lambda b, p: (p, 0, 0)),
                  pl.BlockSpec((seq_len, w2), lambda b, p: (0, 0)),
                  pl.BlockSpec((seq_len, w2), lambda b, p: (0, 0))],
        out_specs=[pl.BlockSpec((seq_len, v2), lambda b, p: (b, p)),
                   pl.BlockSpec((1, 2, 2, DK_A, DV_A), lambda b, p: (b, 0, p, 0, 0))],
        out_shape=[jax.ShapeDtypeStruct((nseq * seq_len, H_A * DV_A), F32),
                   jax.ShapeDtypeStruct((nseq, 2, H_A, DK_A, DV_A), F32)],
        scratch_shapes=[pltpu.VMEM((seq_len, w2), F32), pltpu.VMEM((seq_len, w2), F32),
                        pltpu.VMEM((seq_len, v2), F32), pltpu.VMEM((w2, v2), F32), pltpu.VMEM((w2, v2), F32)],
        compiler_params=_cparams(("arbitrary", "arbitrary")),
        name="retention",
    )(q, k, v, g, s0, gam, ng, cos_t, sin_t)


def _fill_padded(pad_ref, cur, prev, nxt, has_prev, has_next, halo):
    tile = cur.shape[0]
    pad_ref[0:halo, :] = jnp.where(has_prev, prev, 0.0)
    pad_ref[halo:halo + tile, :] = cur
    pad_ref[halo + tile:halo + tile + halo, :] = jnp.where(has_next, nxt, 0.0)


def _conv_taps(pad_ref, w_ref, ntaps, halo, tile):
    base = halo - ntaps // 2
    acc = w_ref[0:1, :] * pad_ref[base:base + tile, :]
    for kk in range(1, ntaps):
        acc = acc + w_ref[kk:kk + 1, :] * pad_ref[base + kk:base + kk + tile, :]
    return acc


def _halo_specs(width, col_map, halo):
    per = CONV_TILE // halo

    def cur(i, j):
        return (i, col_map(j))

    def prev(i, j):
        return (jnp.maximum(i * per - 1, 0), col_map(j))

    def make_next(nblk):
        def nxt(i, j):
            return (jnp.minimum((i + 1) * per, nblk - 1), col_map(j))
        return nxt

    return cur, prev, make_next


def _sconv_kernel(bg_ref, cg_ref, cgp_ref, cgn_ref, hb_ref, hbp_ref, hbn_ref, w_ref, o_ref, pad_ref, *, st):
    has_prev, has_next = st.halo_flags(pl.program_id(0), CONV_TILE)
    _fill_padded(pad_ref, cg_ref[...] * hb_ref[...], cgp_ref[...] * hbp_ref[...],
                 cgn_ref[...] * hbn_ref[...], has_prev, has_next, SUBLANES)
    o_ref[...] = bg_ref[...] * _conv_taps(pad_ref, w_ref, SC_K, SUBLANES, CONV_TILE)


CONV_COLS = 512


def _short_gated_conv(st, bg, cg, hb, w):
    t, c = bg.shape
    halo = SUBLANES
    cur, prev, make_next = _halo_specs(c, lambda j: j, halo)
    nxt = make_next(t // halo)
    tile_spec = pl.BlockSpec((CONV_TILE, CONV_COLS), cur)
    prev_spec = pl.BlockSpec((halo, CONV_COLS), prev)
    next_spec = pl.BlockSpec((halo, CONV_COLS), nxt)
    return pl.pallas_call(
        functools.partial(_sconv_kernel, st=st),
        grid=(t // CONV_TILE, c // CONV_COLS),
        in_specs=[tile_spec, tile_spec, prev_spec, next_spec, tile_spec, prev_spec, next_spec,
                  pl.BlockSpec((SUBLANES, CONV_COLS), lambda i, j: (0, j))],
        out_specs=tile_spec,
        out_shape=jax.ShapeDtypeStruct((t, c), F32),
        scratch_shapes=[pltpu.VMEM((CONV_TILE + 2 * halo, CONV_COLS), F32)],
        compiler_params=_cparams(("arbitrary", "arbitrary")),
        name="short_gated_conv",
    )(bg, cg, cg, cg, hb, hb, hb, w)


def _qkv_conv_kernel(x_ref, xp_ref, xn_ref, w_ref, o_ref, pad_ref, *, st):
    has_prev, has_next = st.halo_flags(pl.program_id(0), CONV_TILE)
    _fill_padded(pad_ref, x_ref[...], xp_ref[...], xn_ref[...], has_prev, has_next, SUBLANES)
    s = _silu(_conv_taps(pad_ref, w_ref, QKV_K, SUBLANES, CONV_TILE))
    j = pl.program_id(1)
    for hh in range(CONV_COLS // DK_C):
        hsl = slice(hh * DK_C, (hh + 1) * DK_C)
        sh = s[:, hsl]
        nrm = sh * lax.rsqrt(jnp.sum(sh * sh, axis=-1, keepdims=True) + EPS)
        o_ref[:, hsl] = jnp.where(j == 0, nrm * (DK_C ** -0.5), jnp.where(j == 1, nrm, sh))


def _qkv_conv(st, qkv, w):
    t, c = qkv.shape
    assert CONV_COLS == H_C * DK_C == H_C * DV_C
    halo = SUBLANES
    cur, prev, make_next = _halo_specs(c, lambda j: j, halo)
    nxt = make_next(t // halo)
    tile_spec = pl.BlockSpec((CONV_TILE, CONV_COLS), cur)
    return pl.pallas_call(
        functools.partial(_qkv_conv_kernel, st=st),
        grid=(t // CONV_TILE, c // CONV_COLS),
        in_specs=[tile_spec, pl.BlockSpec((halo, CONV_COLS), prev), pl.BlockSpec((halo, CONV_COLS), nxt),
                  pl.BlockSpec((SUBLANES, CONV_COLS), lambda i, j: (0, j))],
        out_specs=tile_spec,
        out_shape=jax.ShapeDtypeStruct((t, c), F32),
        scratch_shapes=[pltpu.VMEM((CONV_TILE + 2 * halo, CONV_COLS), F32)],
        compiler_params=_cparams(("arbitrary", "arbitrary")),
        name="qkv_conv",
    )(qkv, qkv, qkv, w)


CF_HALO = 16


def _conformer_kernel(ca_ref, cap_ref, can_ref, cg_ref, cgp_ref, cgn_ref, w_ref, b_ref, lg_ref, lb_ref,
                      o_ref, pad_ref, *, st):
    has_prev, has_next = st.halo_flags(pl.program_id(0), CONV_TILE)
    _fill_padded(pad_ref, ca_ref[...] * _sigmoid(cg_ref[...]), cap_ref[...] * _sigmoid(cgp_ref[...]),
                 can_ref[...] * _sigmoid(cgn_ref[...]), has_prev, has_next, CF_HALO)
    hc = _conv_taps(pad_ref, w_ref, CF_K, CF_HALO, CONV_TILE) + b_ref[...]
    mu = jnp.mean(hc, axis=-1, keepdims=True)
    xc = hc - mu
    y = xc * lax.rsqrt(jnp.mean(xc * xc, axis=-1, keepdims=True) + EPS) * lg_ref[...] + lb_ref[...]
    o_ref[...] = _silu(y)


def _conformer(st, glu, w, b, ln_g, ln_b):
    t, c2 = glu.shape
    c = c2 // 2
    nblk = t // CF_HALO
    per = CONV_TILE // CF_HALO
    vec = pl.BlockSpec((1, c), lambda i: (0, 0))
    return pl.pallas_call(
        functools.partial(_conformer_kernel, st=st),
        grid=(t // CONV_TILE,),
        in_specs=[pl.BlockSpec((CONV_TILE, c), lambda i: (i, 0)),
                  pl.BlockSpec((CF_HALO, c), lambda i: (jnp.maximum(i * per - 1, 0), 0)),
                  pl.BlockSpec((CF_HALO, c), lambda i: (jnp.minimum((i + 1) * per, nblk - 1), 0)),
                  pl.BlockSpec((CONV_TILE, c), lambda i: (i, 1)),
                  pl.BlockSpec((CF_HALO, c), lambda i: (jnp.maximum(i * per - 1, 0), 1)),
                  pl.BlockSpec((CF_HALO, c), lambda i: (jnp.minimum((i + 1) * per, nblk - 1), 1)),
                  pl.BlockSpec((2 * CF_HALO, c), lambda i: (0, 0)), vec, vec, vec],
        out_specs=pl.BlockSpec((CONV_TILE, c), lambda i: (i, 0)),
        out_shape=jax.ShapeDtypeStruct((t, c), F32),
        scratch_shapes=[pltpu.VMEM((CONV_TILE + 2 * CF_HALO, c), F32)],
        compiler_params=_cparams(("arbitrary",)),
        name="conformer_conv",
    )(glu, glu, glu, glu, glu, glu, w, b, ln_g, ln_b)


GDN_PACK = 4
GDN_GROUPS = 4


def _split_bf16(x):
    hi = x.astype(BF16)
    return hi, (x - hi.astype(F32)).astype(BF16)


def _packed_unit_inverses(groups):
    dot = functools.partial(jnp.dot, preferred_element_type=F32)
    cat = jnp.concatenate
    n = len(groups[0])
    c = CHUNK
    w = n * c
    eye = (lax.broadcasted_iota(jnp.int32, (c, c), 0) == lax.broadcasted_iota(jnp.int32, (c, c), 1)).astype(F32)
    lane_block = lax.broadcasted_iota(jnp.int32, (c, 2 * w), 1) % w // c

    def rhs_of(p):
        hi, lo = _split_bf16(p)
        both = cat([hi, lo], axis=1)
        return hi, lo, cat([jnp.where(lane_block == i, both, jnp.zeros_like(both)) for i in range(n)], axis=0)

    def product(m, r):
        return (r[:m, :w] + r[m:, :w]) + (r[:m, w:] + r[m:, w:])

    ps = [cat(a_list, axis=1) for a_list in groups]
    ts = [cat([eye - a for a in a_list], axis=1) for a_list in groups]
    for g, p in enumerate(ps):
        hi, lo, rhs = rhs_of(p)
        ps[g] = product(c, dot(cat([hi, lo], axis=0), rhs))
    steps = int(math.log2(c)) - 1
    for step in range(steps):
        for g in range(len(groups)):
            p_hi, p_lo, rhs = rhs_of(ps[g])
            t_hi, t_lo = _split_bf16(ts[g])
            if step < steps - 1:
                both = product(2 * c, dot(cat([t_hi, p_hi, t_lo, p_lo], axis=0), rhs))
                ts[g] = ts[g] + both[:c]
                ps[g] = both[c:]
            else:
                ts[g] = ts[g] + product(c, dot(cat([t_hi, t_lo], axis=0), rhs))
    return [[t[:, i * c:(i + 1) * c] for i in range(n)] for t in ts]


def _chunk_cumsum(x, reverse):
    row = lax.broadcasted_iota(jnp.int32, x.shape, 0)
    s = 1
    while s < CHUNK:
        if reverse:
            x = x + jnp.where(row < CHUNK - s, pltpu.roll(x, CHUNK - s, 0), 0.0)
        else:
            x = x + jnp.where(row >= s, pltpu.roll(x, s, 0), 0.0)
        s *= 2
    return x


def _gdn_kernel(q_ref, k_ref, v_ref, z_ref, ab_ref, alog_ref, dtb_ref, s0_ref, ng_ref,
                y_ref, sout_ref, g_ref, beta_ref, u_ref, wq_ref, ak_ref, egl_ref, o_ref, *, seq_len):
    nc = seq_len // CHUNK
    h = pl.program_id(1)
    row = lax.broadcasted_iota(jnp.int32, (CHUNK, CHUNK), 0)
    col = lax.broadcasted_iota(jnp.int32, (CHUNK, CHUNK), 1)
    lane = lax.broadcasted_iota(jnp.int32, (CHUNK, LANES), 1)
    alog = alog_ref[...]
    dtb = dtb_ref[...]

    def gates(n, carry):
        rows = pl.ds(pl.multiple_of(n * CHUNK, CHUNK), CHUNK)
        ab = ab_ref[rows, :]
        g_all = -jnp.exp(alog) * _softplus(ab + dtb)
        b_all = _sigmoid(ab)
        for d in range(2):
            gsel = jnp.sum(jnp.where(lane == d * H_C + h, g_all, 0.0), axis=-1, keepdims=True)
            bsel = jnp.sum(jnp.where(lane == 2 * H_C + d * H_C + h, b_all, 0.0), axis=-1, keepdims=True)
            g_ref[d, rows, :] = jnp.broadcast_to(gsel, (CHUNK, LANES))
            beta_ref[d, rows, :] = jnp.broadcast_to(bsel, (CHUNK, LANES))
        return carry

    lax.fori_loop(0, nc, gates, 0)

    n_groups = min(GDN_GROUPS, nc // (GDN_PACK // 2))

    def precompute(it, carry):
        groups = [prepare_group(it * n_groups + grp) for grp in range(n_groups)]
        inverses = _packed_unit_inverses([[sysm[0] for sysm in systems] for systems in groups])
        for t_invs, systems in zip(inverses, groups):
            for t_inv, (_, rhs, d, n) in zip(t_invs, systems):
                uw = _bdot(t_inv, rhs)
                u_ref[d, pl.ds(pl.multiple_of(n * CHUNK, CHUNK), CHUNK), :] = uw[:, :DV_C]
                wq_ref[d, pl.ds(pl.multiple_of(n * 2 * CHUNK, CHUNK), CHUNK), :] = uw[:, DV_C:].astype(BF16)
        return carry

    def prepare_group(pair):
        systems = []
        for jc in range(GDN_PACK // 2):
            n = pair * (GDN_PACK // 2) + jc
            rows = pl.ds(pl.multiple_of(n * CHUNK, CHUNK), CHUNK)
            qc = q_ref[rows, :]
            kc = k_ref[rows, :]
            vc = v_ref[rows, :]
            gbs = [_chunk_cumsum(g_ref[d, rows, :], d == 1) for d in range(2)]
            bbs = [beta_ref[d, rows, :] for d in range(2)]
            kbs = [kc * bb for bb in bbs]
            prod = _bdot_nt(jnp.concatenate(kbs + [qc], axis=0), kc)
            for d in range(2):
                incl = (row >= col) if d == 0 else (row <= col)
                strict = (row > col) if d == 0 else (row < col)
                last = CHUNK - 1 if d == 0 else 0
                gb = gbs[d]
                gr = gb[:, :CHUNK].T
                gl = gb[last:last + 1, :]
                decay = jnp.where(incl, jnp.exp(jnp.where(incl, gb[:, :CHUNK] - gr, 0.0)), 0.0)
                eg = jnp.exp(gb)
                a_low = jnp.where(strict, prod[d * CHUNK:(d + 1) * CHUNK] * decay, 0.0)
                attn = jnp.where(incl, prod[2 * CHUNK:] * decay, 0.0)
                kg_t = (kc * jnp.exp(gl - gb)).T
                ak_ref[d, pl.ds(pl.multiple_of(n * 3 * CHUNK, CHUNK), 3 * CHUNK), :] = jnp.concatenate(
                    [attn, kg_t], axis=0).astype(BF16)
                wq_ref[d, pl.ds(pl.multiple_of(n * 2 * CHUNK + CHUNK, CHUNK), CHUNK), :] = (qc * eg).astype(BF16)
                egl_ref[d, pl.ds(n, 1), :] = jnp.exp(gl)
                systems.append((a_low, jnp.concatenate([vc * bbs[d], kbs[d] * eg], axis=1), d, n))
        return systems

    lax.fori_loop(0, nc // (n_groups * GDN_PACK // 2), precompute, 0)

    o_ref[...] = jnp.zeros_like(o_ref)

    def scan(i, carry):
        out = []
        for d, s in enumerate(carry):
            n = i if d == 0 else nc - 1 - i
            rows = pl.ds(pl.multiple_of(n * CHUNK, CHUNK), CHUNK)
            on_s = _bdot(wq_ref[d, pl.ds(pl.multiple_of(n * 2 * CHUNK, CHUNK), 2 * CHUNK), :], s)
            v_new = u_ref[d, rows, :] - on_s[:CHUNK]
            on_v = _bdot(ak_ref[d, pl.ds(pl.multiple_of(n * 3 * CHUNK, CHUNK), 3 * CHUNK), :], v_new)
            o_ref[rows, :] += on_s[CHUNK:] + on_v[:CHUNK]
            out.append(s * egl_ref[d, pl.ds(n, 1), :] + on_v[CHUNK:])
        return tuple(out)

    s_f, s_b = lax.fori_loop(0, nc, scan, (s0_ref[0, 0, 0], s0_ref[0, 1, 0]))
    sout_ref[0, 0, 0] = s_f
    sout_ref[0, 1, 0] = s_b
    y_ref[...] = _rms(o_ref[...], ng_ref[...]) * _silu(z_ref[...])


def _gdn(qkv_n, z, ab, alog, dtb, s0, ng, nseq, seq_len, row_off):
    blk = lambda off: pl.BlockSpec((seq_len, LANES), lambda b, h: (b + row_off, h + off))
    vec = pl.BlockSpec((1, LANES), lambda b, h: (0, 0))
    st_spec = pl.BlockSpec((1, 2, 1, DK_C, DV_C), lambda b, h: (b, 0, h, 0, 0))
    nc = seq_len // CHUNK
    both = lambda width, dtype: pltpu.VMEM((2, seq_len, width), dtype)
    return pl.pallas_call(
        functools.partial(_gdn_kernel, seq_len=seq_len),
        grid=(nseq, H_C),
        in_specs=[blk(0), blk(H_C), blk(2 * H_C), blk(0),
                  pl.BlockSpec((seq_len, LANES), lambda b, h: (b + row_off, 0)),
                  vec, vec, st_spec, vec],
        out_specs=[pl.BlockSpec((seq_len, LANES), lambda b, h: (b, h)), st_spec],
        out_shape=[jax.ShapeDtypeStruct((nseq * seq_len, H_C * DV_C), F32),
                   jax.ShapeDtypeStruct((nseq, 2, H_C, DK_C, DV_C), F32)],
        scratch_shapes=[both(LANES, F32), both(LANES, F32), both(DV_C, F32),
                        pltpu.VMEM((2, 2 * seq_len, DK_C), BF16), pltpu.VMEM((2, 3 * seq_len, CHUNK), BF16),
                        pltpu.VMEM((2, max(nc, SUBLANES), LANES), F32), pltpu.VMEM((seq_len, DV_C), F32)],
        compiler_params=_cparams(("arbitrary", "arbitrary")),
        name="gated_deltanet",
    )(qkv_n, qkv_n, qkv_n, z, ab, alog, dtb, s0, ng)


def _out_proj_kernel(ya_ref, yb_ref, x_ref, mod_ref, g_ref, wo_ref, wq_ref, x1_ref, h2_ref, qp_ref):
    half = ya_ref.shape[1]
    m = mod_ref[0]
    y = (jnp.dot(ya_ref[...].astype(BF16), wo_ref[0:half, :], preferred_element_type=F32)
         + jnp.dot(yb_ref[...].astype(BF16), wo_ref[half:, :], preferred_element_type=F32))
    x1 = x_ref[...] + m[2:3] * y
    x1_ref[...] = x1
    h2 = (_rms(x1, g_ref[...]) * (1.0 + m[4:5]) + m[3:4]).astype(BF16)
    h2_ref[...] = h2
    qp_ref[...] = jnp.dot(h2, wq_ref[...], preferred_element_type=F32)


def _out_proj(st, ya, yb, x, mod, g, wo, wq, tm):
    t, d = x.shape
    half = ya.shape[1]
    nq = wq.shape[1]
    return pl.pallas_call(
        _out_proj_kernel,
        grid=(t // tm,),
        in_specs=[pl.BlockSpec((tm, half), lambda i: (i, 0)),
                  pl.BlockSpec((tm, half), lambda i: (i, 0)),
                  pl.BlockSpec((tm, d), lambda i: (i, 0)),
                  pl.BlockSpec((1, 6, d), lambda i: (st.mod_row(i, tm), 0, 0)),
                  pl.BlockSpec((1, d), lambda i: (0, 0)),
                  pl.BlockSpec((2 * half, d), lambda i: (0, 0)),
                  pl.BlockSpec((d, nq), lambda i: (0, 0))],
        out_specs=[pl.BlockSpec((tm, d), lambda i: (i, 0)),
                   pl.BlockSpec((tm, d), lambda i: (i, 0)),
                   pl.BlockSpec((tm, nq), lambda i: (i, 0))],
        out_shape=[jax.ShapeDtypeStruct((t, d), F32), jax.ShapeDtypeStruct((t, d), BF16),
                   jax.ShapeDtypeStruct((t, nq), F32)],
        compiler_params=_cparams(("arbitrary",)),
        name="out_proj",
    )(ya, yb, x, mod, g, wo, wq)


def _candidate_rows():
    groups = ([(0, k2) for k2 in range(16)], [(1, k2) for k2 in range(8)], [(2, k2) for k2 in range(8)],
              [(3, k2) for k2 in range(8)], [(k1, 0) for k1 in range(16)], [(k1, 1) for k1 in range(8)],
              [(k1, 2) for k1 in range(8)])
    rows, seen = [], set()
    for grp in groups:
        for k1, k2 in grp:
            ok = (k1 + 1) * (k2 + 1) <= TOPK and (k1, k2) not in seen
            if ok:
                seen.add((k1, k2))
            rows.append(float(k1 * TOPK + k2) if ok else CAND_INVALID)
    return rows


CAND_INVALID = float(TOPK * TOPK)
RANK_MARK = 2.0 ** 100
RANK_STEP = 2.0 ** 95
SELECT_STRIPS = 2


def _pack_candidates(a1, a2, op):
    return jnp.concatenate([op(a1[0:1], a2), op(a1[1:2], a2[0:8]), op(a1[2:3], a2[0:8]),
                            op(a1[3:4], a2[0:8]), op(a1, a2[0:1]), op(a1[0:8], a2[1:2]),
                            op(a1[0:8], a2[2:3])], axis=0)


def _top16(s, vals_ref, exact):
    n = s.shape[0]
    if not exact:
        for kk in range(TOPK):
            m = jnp.max(s, axis=0, keepdims=True)
            vals_ref[kk:kk + 1, :] = m
            s = jnp.where(s == m, -(RANK_MARK + kk * RANK_STEP), s)
        return jnp.where(s < -0.5 * RANK_MARK, (-s - RANK_MARK) * (1.0 / RANK_STEP), float(TOPK))
    iota = lax.broadcasted_iota(jnp.int32, s.shape, 0).astype(F32)
    rank = jnp.full(s.shape, float(TOPK), F32)
    for kk in range(TOPK):
        m = jnp.max(s, axis=0, keepdims=True)
        idx = jnp.min(jnp.where(s == m, iota, float(n)), axis=0, keepdims=True)
        hit = iota == idx
        rank = jnp.where(hit, float(kk), rank)
        vals_ref[kk:kk + 1, :] = m
        s = jnp.where(hit, NEG_INF, s)
    return rank


def _peer_select_kernel(q_ref, keys_ref, cflat_ref, e1_ref, cnt_ref, e2_ref, r2_ref, v1_ref, v2_ref, *, tt):
    nk = N_KEYS
    cflat = cflat_ref[...]
    valid = cflat < CAND_INVALID
    row16 = lax.broadcasted_iota(jnp.int32, (TOPK, LANES), 0)

    def compute(strip, exact):
        s1, s2, tsl, va_ref, vb_ref = strip
        rank1 = _top16(s1, va_ref, exact)
        rank2 = _top16(s2, vb_ref, exact)
        v1 = va_ref[...]
        v2 = vb_ref[...]
        cand = jnp.where(valid, _pack_candidates(v1, v2, jnp.add), NEG_INF)
        prod = _pack_candidates(jnp.exp(v1 - v1[0:1]), jnp.exp(v2 - v2[0:1]), jnp.multiply)
        sel = jnp.zeros(cand.shape, F32)
        for _ in range(TOPK):
            m = jnp.max(cand, axis=0, keepdims=True)
            if exact:
                idx = jnp.min(jnp.where(cand == m, cflat, CAND_INVALID), axis=0, keepdims=True)
                hit = cflat == idx
            else:
                hit = cand == m
            sel = jnp.where(hit, 1.0, sel)
            cand = jnp.where(hit, NEG_INF, cand)
        zsum = jnp.sum(sel * prod, axis=0, keepdims=True)
        c_all = sel[40:56] + jnp.concatenate([sel[56:64] + sel[64:72], jnp.zeros((8, LANES), F32)], axis=0)
        for k1, (lo, hi) in enumerate(((0, 16), (16, 24), (24, 32), (32, 40))):
            c_all = jnp.where(row16 == k1, jnp.sum(sel[lo:hi], axis=0, keepdims=True), c_all)
        cnt = jnp.zeros(rank1.shape, F32)
        for k1 in range(TOPK):
            cnt = jnp.where(rank1 == float(k1), c_all[k1:k1 + 1], cnt)
        in1 = rank1 < float(TOPK)
        in2 = rank2 < float(TOPK)
        e1_ref[0, :, tsl] = jnp.where(in1, jnp.exp(s1 - v1[0:1]), 0.0) / zsum
        cnt_ref[0, :, tsl] = cnt
        e2_ref[:, tsl] = jnp.where(in2, jnp.exp(s2 - v2[0:1]), 0.0).astype(e2_ref.dtype)
        r2_ref[:, tsl] = rank2.astype(r2_ref.dtype)
        n1 = jnp.sum(jnp.where(in1, 1.0, 0.0), axis=0, keepdims=True)
        n2 = jnp.sum(jnp.where(in2, 1.0, 0.0), axis=0, keepdims=True)
        nc = jnp.sum(sel, axis=0, keepdims=True)
        want = float(TOPK)
        return jnp.abs(n1 - want) + jnp.abs(n2 - want) + jnp.abs(nc - want)

    def strip_group(si, carry):
        strips = []
        for sub in range(SELECT_STRIPS):
            t0 = pl.multiple_of((si * SELECT_STRIPS + sub) * LANES, LANES)
            tsl = pl.ds(t0, LANES)
            s1 = _bdot_nt(keys_ref[0, 0], q_ref[tsl, 0:nk])
            s2 = _bdot_nt(keys_ref[0, 1], q_ref[tsl, nk:2 * nk])
            strips.append((s1, s2, tsl, v1_ref.at[sub], v2_ref.at[sub]))
        tied = [compute(strip, False) for strip in strips]
        for strip, bad in zip(strips, tied):
            @pl.when(jnp.max(bad) > 0.0)
            def _(strip=strip):
                compute(strip, True)
        return carry

    lax.fori_loop(0, tt // (SELECT_STRIPS * LANES), strip_group, 0)


def _peer_select(qp, keys, tt):
    t = qp.shape[0]
    rows = _candidate_rows()
    cflat = jnp.broadcast_to(jnp.asarray(rows, F32)[:, None], (len(rows), LANES))
    out_spec = pl.BlockSpec((1, N_KEYS, tt), lambda i, h: (h, 0, i))
    f32_sds = jax.ShapeDtypeStruct((H_P, N_KEYS, t), F32)
    flat_spec = pl.BlockSpec((N_KEYS, tt), lambda i, h: (h, i))
    flat_sds = jax.ShapeDtypeStruct((H_P * N_KEYS, t), BF16)
    return pl.pallas_call(
        functools.partial(_peer_select_kernel, tt=tt),
        grid=(t // tt, H_P),
        in_specs=[pl.BlockSpec((tt, 2 * N_KEYS), lambda i, h: (i, h)),
                  pl.BlockSpec((1, 2, N_KEYS, N_KEYS), lambda i, h: (h, 0, 0, 0)),
                  pl.BlockSpec(cflat.shape, lambda i, h: (0, 0))],
        out_specs=[out_spec, out_spec, flat_spec, flat_spec],
        out_shape=[f32_sds, f32_sds, flat_sds, flat_sds],
        scratch_shapes=[pltpu.VMEM((SELECT_STRIPS, TOPK, LANES), F32),
                        pltpu.VMEM((SELECT_STRIPS, TOPK, LANES), F32)],
        compiler_params=_cparams(("arbitrary", "arbitrary")),
        name="peer_select",
    )(qp, keys, cflat)


def _peer_dense_kernel(h_ref, u_ref, vt_ref, e1_ref, cnt_ref, e2_ref, r2_ref, x_ref, mod_ref, o_ref,
                       acc_ref, s_ref, p_ref, e2b_ref, r2b_ref, ht_ref, *, na):
    j = pl.program_id(1)
    tt = s_ref.shape[1]

    @pl.when(j == 0)
    def _():
        acc_ref[...] = jnp.zeros_like(acc_ref)
        e2b_ref[...] = e2_ref[...].astype(BF16)
        r2b_ref[...] = r2_ref[...].astype(BF16)
        ht_ref[...] = h_ref[...].astype(F32).T.astype(BF16)

    s_ref[...] = jnp.dot(u_ref[...], ht_ref[...], preferred_element_type=F32)

    for aa in range(na):
        rows = slice(aa * N_KEYS, (aa + 1) * N_KEYS)
        for tg in range(tt // LANES):
            tsl = slice(tg * LANES, (tg + 1) * LANES)
            gate = None
            for hh in range(H_P):
                cnt_row = cnt_ref[hh, aa:aa + 1, tsl].astype(BF16)
                e1_row = e1_ref[hh, aa:aa + 1, tsl].astype(BF16)
                hsl = slice(hh * N_KEYS, (hh + 1) * N_KEYS)
                term = jnp.where(r2b_ref[hsl, tsl] < cnt_row, e2b_ref[hsl, tsl], 0.0) * e1_row
                gate = term if gate is None else gate + term
            p_ref[rows, tsl] = gate * _gelu_tanh(s_ref[rows, tsl].astype(BF16))

    vt = vt_ref[0] if len(vt_ref.shape) == 3 else vt_ref[...]
    acc_ref[...] += jnp.dot(vt, p_ref[...], preferred_element_type=F32)

    @pl.when(j == pl.num_programs(1) - 1)
    def _():
        o_ref[...] = x_ref[...] + mod_ref[0][5:6] * acc_ref[...].T


def _peer_dense(st, h2, u, v, e1, cnt, e2, r2, x, mod, tt, et, v_tiled):
    t, d = x.shape
    n_exp = u.shape[0]
    na = et // N_KEYS
    if v_tiled:
        vt = v.reshape(n_exp // et, et, d).transpose(0, 2, 1)
        vt_spec = pl.BlockSpec((1, d, et), lambda i, j: (j, 0, 0))
    else:
        vt = v.T
        vt_spec = pl.BlockSpec((d, et), lambda i, j: (0, j))
    row_spec = pl.BlockSpec((H_P, na, tt), lambda i, j: (0, j, i))
    full_spec = pl.BlockSpec((H_P * N_KEYS, tt), lambda i, j: (0, i))
    return pl.pallas_call(
        functools.partial(_peer_dense_kernel, na=na),
        grid=(t // tt, n_exp // et),
        in_specs=[pl.BlockSpec((tt, d), lambda i, j: (i, 0)),
                  pl.BlockSpec((et, d), lambda i, j: (j, 0)), vt_spec,
                  row_spec, row_spec, full_spec, full_spec,
                  pl.BlockSpec((tt, d), lambda i, j: (i, 0)),
                  pl.BlockSpec((1, 6, d), lambda i, j: (st.mod_row(i, tt), 0, 0))],
        out_specs=pl.BlockSpec((tt, d), lambda i, j: (i, 0)),
        out_shape=jax.ShapeDtypeStruct((t, d), F32),
        scratch_shapes=[pltpu.VMEM((d, tt), F32), pltpu.VMEM((et, tt), F32), pltpu.VMEM((et, tt), BF16),
                        pltpu.VMEM((H_P * N_KEYS, tt), BF16), pltpu.VMEM((H_P * N_KEYS, tt), BF16),
                        pltpu.VMEM((d, tt), BF16)],
        compiler_params=_cparams(("arbitrary", "arbitrary")),
        name="peer_dense",
    )(h2, u, vt, e1, cnt, e2, r2, x, mod)


def _final_norm_kernel(x_ref, g_ref, o_ref):
    o_ref[...] = _rms(x_ref[...], g_ref[...])


def _final_norm(x, g, tm):
    t, d = x.shape
    return pl.pallas_call(
        _final_norm_kernel,
        grid=(t // tm,),
        in_specs=[pl.BlockSpec((tm, d), lambda i: (i, 0)), pl.BlockSpec((1, d), lambda i: (0, 0))],
        out_specs=pl.BlockSpec((tm, d), lambda i: (i, 0)),
        out_shape=jax.ShapeDtypeStruct((t, d), F32),
        compiler_params=_cparams(("arbitrary",)),
        name="final_norm",
    )(x, g)


def _rope_tables(seq_len):
    pos = jnp.arange(seq_len)
    rowp = (pos // GRID_W).astype(F32)
    colp = (pos % GRID_W).astype(F32)
    nf = DK_A // 4
    freqs = ROPE_BASE ** (-jnp.arange(nf, dtype=F32) / nf)
    ang = jnp.concatenate([rowp[:, None] * freqs, colp[:, None] * freqs], axis=-1)
    cos = jnp.cos(ang)
    sin = jnp.sin(ang)
    cos_t = jnp.tile(cos, (1, 4))
    sin_t = jnp.tile(jnp.concatenate([-sin, sin], axis=-1), (1, 2))
    return cos_t, sin_t


def _pad_rows(w, rows):
    return jnp.concatenate([w, jnp.zeros((rows - w.shape[0], w.shape[1]), w.dtype)], axis=0)


def _lane_row(vals):
    flat = vals.reshape(-1).astype(F32)
    return jnp.concatenate([flat, jnp.zeros((LANES - flat.shape[0],), F32)])[None, :]


def kernel(x_prompt, x_sample, state_ret, state_gdn, c, c_ctx, ada_w, ada_b, norm_mix_g, norm_ffn_g,
           final_norm_g, ev_w_in, ev_w_out, ret_gamma_logit, ret_norm_g, sc_conv_w, od_w_in, od_w_out,
           gdn_conv_w, gdn_a_log, gdn_dt_bias, gdn_norm_g, cf_dw_w, cf_dw_b, cf_ln_g, cf_ln_b,
           peer_wq, peer_keys, peer_u, peer_v):
    bp, lp, d = x_prompt.shape
    bs, ls, _ = x_sample.shape
    depth = ada_w.shape[0]
    st = _Streams(bp, lp, bs, ls)
    w_a = H_A * DV_A
    w_b = d - w_a
    w_c = H_C * DV_C
    w_d = d - w_c
    tm = 512
    peer_tt = 512
    peer_et = 2048
    select_tt = 512

    x = jnp.concatenate([x_prompt.reshape(st.tp, d), x_sample.reshape(st.ts, d)], axis=0)
    cvec = jnp.concatenate([c_ctx[None, :], c, jnp.zeros((N_MOD_ROWS - 1 - bs, d), F32)], axis=0)
    mods = _modulation(cvec, ada_w, ada_b).reshape(depth, N_MOD_ROWS, 6, d)
    cos_t, sin_t = _rope_tables(ls)
    zero_ret = jnp.zeros((bp, 2, H_A, DK_A, DV_A), F32)
    zero_gdn = jnp.zeros((bp, 2, H_C, DK_C, DV_C), F32)

    ret_new, gdn_new = [], []
    for l in range(depth):
        i = l // 2
        mod = mods[l]
        g1 = norm_mix_g[l][None, :]
        if l % 2 == 0:
            widths = (H_A * DK_A, H_A * DK_A, w_a, w_a, w_b, w_b, w_b)
            q, k, v, g, bg, cg, hb = _norm_proj(st, x, mod, g1, ev_w_in[i].astype(BF16), widths, tm)
            gam = jnp.broadcast_to(ret_gamma_logit[i][:, :, None, None], (2, H_A, CHUNK, DV_A))
            ng = ret_norm_g[i][:, None, :]
            ya_p, s_new = _retention(q, k, v, g, zero_ret, gam, ng, cos_t[:lp], sin_t[:lp],
                                     bp, lp, 0, False)
            ya_s, _ = _retention(q, k, v, g, state_ret[:, i], gam, ng, cos_t, sin_t,
                                 bs, ls, st.tp // ls, True)
            ret_new.append(s_new)
            ya = jnp.concatenate([ya_p, ya_s], axis=0)
            yb = _short_gated_conv(st, bg, cg, hb, _pad_rows(sc_conv_w[i], SUBLANES))
            w_out = ev_w_out[i]
        else:
            n_gate = 2 * 2 * H_C
            w_in = od_w_in[i]
            o_ab = 4 * w_c
            w_main = jnp.concatenate([w_in[:, :o_ab], w_in[:, o_ab + n_gate:],
                                      w_in[:, o_ab:o_ab + n_gate],
                                      jnp.zeros((d, LANES - n_gate), F32)], axis=1).astype(BF16)
            widths = (3 * w_c, w_c, 2 * w_d, LANES)
            qkv, z, glu, ab = _norm_proj(st, x, mod, g1, w_main, widths, tm)
            qkv_n = _qkv_conv(st, qkv, _pad_rows(gdn_conv_w[i], SUBLANES))
            alog = _lane_row(gdn_a_log[i])
            dtb = _lane_row(gdn_dt_bias[i])
            ng = gdn_norm_g[i][None, :]
            yc_p, s_new = _gdn(qkv_n, z, ab, alog, dtb, zero_gdn, ng, bp, lp, 0)
            yc_s, _ = _gdn(qkv_n, z, ab, alog, dtb, state_gdn[:, i], ng, bs, ls, st.tp // ls)
            gdn_new.append(s_new)
            ya = jnp.concatenate([yc_p, yc_s], axis=0)
            yb = _conformer(st, glu, _pad_rows(cf_dw_w[i], 2 * CF_HALO), cf_dw_b[i][None, :],
                            cf_ln_g[i][None, :], cf_ln_b[i][None, :])
            w_out = od_w_out[i]
        x, h2, qp = _out_proj(st, ya, yb, x, mod, norm_ffn_g[l][None, :], w_out.astype(BF16),
                              peer_wq[l].astype(BF16), tm)
        e1, cnt, e2, r2 = _peer_select(qp, peer_keys[l].astype(BF16), select_tt)
        x = _peer_dense(st, h2, peer_u[l].astype(BF16), peer_v[l].astype(BF16), e1, cnt, e2, r2,
                        x, mod, *((256, 4096) if l == 1 else (peer_tt, peer_et)), False)

    y = _final_norm(x, final_norm_g[None, :], tm)
    y_prompt = y[:st.tp].reshape(bp, lp, d)
    y_sample = y[st.tp:].reshape(bs, ls, d)
    new_state_ret = jnp.stack(ret_new, axis=1).astype(x_prompt.dtype)
    new_state_gdn = jnp.stack(gdn_new, axis=1).astype(x_prompt.dtype)
    return (y_prompt, y_sample, new_state_ret, new_state_gdn)
```

```python
import functools
import math

import jax
import jax.numpy as jnp
from jax import lax
from jax.experimental import pallas as pl
from jax.experimental.pallas import tpu as pltpu

F32 = jnp.float32
BF16 = jnp.bfloat16
HIGHEST = lax.Precision.HIGHEST

EPS = 1e-6
CHUNK = 64
GRID_W = 64
ROPE_BASE = 10000.0
H_A, DK_A, DV_A = 4, 64, 128
H_C, DK_C, DV_C = 4, 128, 128
SC_K, QKV_K, CF_K = 3, 3, 31
N_KEYS, H_P, TOPK = 128, 8, 16
N_MOD_ROWS = 16
LANES = 128
SUBLANES = 8
VMEM_LIMIT = 56 * 1024 * 1024
CONV_TILE = 256
TOKEN_TILE = 512
PEER_TOKEN_TILE = 512
PEER_EXPERT_TILE = 2048
NEG_INF = float("-inf")


def _cparams(sem):
    return pltpu.CompilerParams(dimension_semantics=sem, vmem_limit_bytes=VMEM_LIMIT)


def _bdot(a, b):
    return jnp.dot(a.astype(BF16), b.astype(BF16), preferred_element_type=F32)


def _bdot_nt(a, b):
    return lax.dot_general(a.astype(BF16), b.astype(BF16), (((1,), (1,)), ((), ())),
                           preferred_element_type=F32)


def _bdot_tn(a, b):
    return lax.dot_general(a.astype(BF16), b.astype(BF16), (((0,), (0,)), ((), ())),
                           preferred_element_type=F32)


def _hdot(a, b):
    return jnp.dot(a, b, precision=HIGHEST, preferred_element_type=F32)


def _sigmoid(x):
    return 1.0 / (1.0 + jnp.exp(-x))


def _silu(x):
    return x * _sigmoid(x)


def _softplus(x):
    return jnp.maximum(x, 0.0) + jnp.log1p(jnp.exp(-jnp.abs(x)))


def _log_sigmoid(x):
    return -_softplus(-x)


def _gelu_tanh(x):
    c = math.sqrt(2.0 / math.pi)
    return 0.5 * x * (1.0 + jnp.tanh(c * (x + 0.044715 * (x * x * x))))


def _rms(x, g):
    return x * lax.rsqrt(jnp.mean(x * x, axis=-1, keepdims=True) + EPS) * g


class _Streams:
    def __init__(self, bp, lp, bs, ls):
        self.bp, self.lp, self.bs, self.ls = bp, lp, bs, ls
        self.tp, self.ts = bp * lp, bs * ls
        self.t = self.tp + self.ts

    def mod_row(self, i, tile):
        tiles_p = self.tp // tile
        per_seq = self.ls // tile
        return jnp.where(i < tiles_p, 0, 1 + (i - tiles_p) // per_seq)

    def halo_flags(self, i, tile):
        tiles_p = self.tp // tile
        per_p = self.lp // tile
        per_s = self.ls // tile
        in_p = i < tiles_p
        jp = i % per_p
        js = (i - tiles_p) % per_s
        has_prev = jnp.where(in_p, jp > 0, js > 0)
        has_next = jnp.where(in_p, jp < per_p - 1, js < per_s - 1)
        return has_prev, has_next


def _mod_kernel(c_ref, w_ref, b_ref, o_ref):
    s = _silu(c_ref[...])
    o_ref[0] = _hdot(s, w_ref[0]) + b_ref[0]


def _modulation(cvec, ada_w, ada_b):
    depth, d, d6 = ada_w.shape
    nj = d6 // d
    return pl.pallas_call(
        _mod_kernel,
        grid=(depth, nj),
        in_specs=[pl.BlockSpec((N_MOD_ROWS, d), lambda l, j: (0, 0)),
                  pl.BlockSpec((1, d, d), lambda l, j: (l, 0, j)),
                  pl.BlockSpec((1, 1, d), lambda l, j: (l, 0, j))],
        out_specs=pl.BlockSpec((1, N_MOD_ROWS, d), lambda l, j: (l, 0, j)),
        out_shape=jax.ShapeDtypeStruct((depth, N_MOD_ROWS, d6), F32),
        compiler_params=_cparams(("arbitrary", "arbitrary")),
        name="modulation",
    )(cvec, ada_w, ada_b.reshape(depth, 1, d6))


def _norm_proj_kernel(x_ref, mod_ref, g_ref, w_ref, *o_refs, widths):
    m = mod_ref[0]
    h = _rms(x_ref[...], g_ref[...]) * (1.0 + m[1:2]) + m[0:1]
    p = jnp.dot(h.astype(BF16), w_ref[...], preferred_element_type=F32)
    off = 0
    for o_ref, wd in zip(o_refs, widths):
        o_ref[...] = p[:, off:off + wd]
        off += wd


def _norm_proj(st, x, mod, g, w, widths, tm):
    t, d = x.shape
    n = w.shape[1]
    return pl.pallas_call(
        functools.partial(_norm_proj_kernel, widths=widths),
        grid=(t // tm,),
        in_specs=[pl.BlockSpec((tm, d), lambda i: (i, 0)),
                  pl.BlockSpec((1, 6, d), lambda i: (st.mod_row(i, tm), 0, 0)),
                  pl.BlockSpec((1, d), lambda i: (0, 0)),
                  pl.BlockSpec((d, n), lambda i: (0, 0))],
        out_specs=[pl.BlockSpec((tm, wd), lambda i: (i, 0)) for wd in widths],
        out_shape=[jax.ShapeDtypeStruct((t, wd), F32) for wd in widths],
        compiler_params=_cparams(("arbitrary",)),
        name="norm_proj",
    )(x, mod, g, w)


def _ret_kernel(q_ref, k_ref, v_ref, g_ref, s0_ref, gam_ref, ng_ref, cos_ref, sin_ref,
                ya_ref, sout_ref, qs_ref, ks_ref, o_ref, sf_ref, sb_ref, *, seq_len, latent):
    nc = seq_len // CHUNK
    q = q_ref[...]
    k = k_ref[...] * (DK_A ** -0.5)
    if latent:
        lane = lax.broadcasted_iota(jnp.int32, q.shape, 1)
        first_half = (lane % DK_A) < (DK_A // 2)
        cos = cos_ref[...]
        sin = sin_ref[...]

        def rope(x):
            partner = jnp.where(first_half, pltpu.roll(x, LANES - DK_A // 2, 1),
                                pltpu.roll(x, DK_A // 2, 1))
            return x * cos + partner * sin

        q = rope(q)
        k = rope(k)
    qs_ref[...] = q
    ks_ref[...] = k

    cat = jnp.concatenate
    kw = 2 * DK_A
    vw = 2 * DV_A
    row = lax.broadcasted_iota(jnp.int32, (CHUNK, kw), 0).astype(F32)
    lane_k = lax.broadcasted_iota(jnp.int32, (CHUNK, kw), 1)
    lane_v = lax.broadcasted_iota(jnp.int32, (CHUNK, vw), 1)
    head0_k = lane_k < DK_A
    head0_v = lane_v < DV_A
    lg = [[_log_sigmoid(gam_ref[d, hh]) for hh in range(2)] for d in range(2)]
    lgf_k = jnp.where(head0_k, lg[0][0], lg[0][1])
    lgb_k = jnp.where(head0_k, lg[1][0], lg[1][1])
    diff = row - (lane_k % DK_A).astype(F32)
    dcomb = (jnp.where(diff >= 0, jnp.exp(lgf_k * jnp.maximum(diff, 0.0)), 0.0)
             + jnp.where(diff <= 0, jnp.exp(lgb_k * jnp.maximum(-diff, 0.0)), 0.0))
    wend_f = jnp.exp(lgf_k * (CHUNK - 1.0 - row))
    wstart_f = jnp.exp(lgf_k * (row + 1.0))
    wend_b = jnp.exp(lgb_k * row)
    wstart_b = jnp.exp(lgb_k * (CHUNK - row))

    def state_layout(h0, h1):
        zero = jnp.zeros_like(h0)
        return cat([cat([h0, zero], axis=1), cat([zero, h1], axis=1)], axis=0)

    on_diag = state_layout(jnp.ones((DK_A, DV_A), F32), jnp.ones((DK_A, DV_A), F32)) > 0.0
    gch_f = jnp.exp(state_layout(lg[0][0], lg[0][1]) * float(CHUNK))
    gch_b = jnp.exp(state_layout(lg[1][0], lg[1][1]) * float(CHUNK))
    sf_ref[...] = state_layout(s0_ref[0, 0, 0], s0_ref[0, 0, 1])
    sb_ref[...] = state_layout(s0_ref[0, 1, 0], s0_ref[0, 1, 1])
    o_ref[...] = jnp.zeros_like(o_ref)

    def step(i, carry):
        rows = pl.ds(pl.multiple_of(i * CHUNK, CHUNK), CHUNK)
        qn = qs_ref[rows, :]
        kn = ks_ref[rows, :]
        vn = v_ref[rows, :]
        k_bd = cat([jnp.where(head0_k, kn, 0.0), jnp.where(head0_k, 0.0, kn)], axis=0)
        v_bd = cat([jnp.where(head0_v, vn, 0.0), jnp.where(head0_v, 0.0, vn)], axis=0)
        a = _bdot_nt(qn, k_bd) * dcomb
        s_f = sf_ref[...]
        o_ref[rows, :] += _bdot(cat([a, qn * wstart_f], axis=1), cat([v_bd, s_f], axis=0))
        sf_ref[...] = s_f * gch_f + jnp.where(on_diag, _bdot_tn(kn * wend_f, vn), 0.0)
        rows = pl.ds(pl.multiple_of((nc - 1 - i) * CHUNK, CHUNK), CHUNK)
        qn = qs_ref[rows, :]
        kn = ks_ref[rows, :]
        vn = v_ref[rows, :]
        s_b = sb_ref[...]
        o_ref[rows, :] += _bdot(qn * wstart_b, s_b)
        sb_ref[...] = s_b * gch_b + jnp.where(on_diag, _bdot_tn(kn * wend_b, vn), 0.0)
        return carry

    lax.fori_loop(0, nc, step, 0)
    for d, s_ref in enumerate((sf_ref, sb_ref)):
        sout_ref[0, d, 0] = s_ref[:DK_A, :DV_A]
        sout_ref[0, d, 1] = s_ref[DK_A:, DV_A:]

    for hh in range(2):
        vsl = slice(hh * DV_A, (hh + 1) * DV_A)
        ya_ref[:, vsl] = _silu(g_ref[:, vsl]) * _rms(o_ref[:, vsl], ng_ref[hh])


def _retention(q, k, v, g, s0, gam, ng, cos_t, sin_t, nseq, seq_len, row_off, latent):
    w2 = 2 * DK_A
    v2 = 2 * DV_A
    return pl.pallas_call(
        functools.partial(_ret_kernel, seq_len=seq_len, latent=latent),
        grid=(nseq, H_A // 2),
        in_specs=[pl.BlockSpec((seq_len, w2), lambda b, p: (b + row_off, p)),
                  pl.BlockSpec((seq_len, w2), lambda b, p: (b + row_off, p)),
                  pl.BlockSpec((seq_len, v2), lambda b, p: (b + row_off, p)),
                  pl.BlockSpec((seq_len, v2), lambda b, p: (b + row_off, p)),
                  pl.BlockSpec((1, 2, 2, DK_A, DV_A), lambda b, p: (b, 0, p, 0, 0)),
                  pl.BlockSpec((2, 2, CHUNK, DV_A), lambda b, p: (0, p, 0, 0)),
                  pl.BlockSpec((2, 1, DV_A), lambda b, p: (p, 0, 0)),
                  pl.BlockSpec((seq_len, w2), lambda b, p: (0, 0)),
                  pl.BlockSpec((seq_len, w2), lambda b, p: (0, 0))],
        out_specs=[pl.BlockSpec((seq_len, v2), lambda b, p: (b, p)),
                   pl.BlockSpec((1, 2, 2, DK_A, DV_A), lambda b, p: (b, 0, p, 0, 0))],
        out_shape=[jax.ShapeDtypeStruct((nseq * seq_len, H_A * DV_A), F32),
                   jax.ShapeDtypeStruct((nseq, 2, H_A, DK_A, DV_A), F32)],
        scratch_shapes=[pltpu.VMEM((seq_len, w2), F32), pltpu.VMEM((seq_len, w2), F32),
                        pltpu.VMEM((seq_len, v2), F32), pltpu.VMEM((w2, v2), F32), pltpu.VMEM((w2, v2), F32)],
        compiler_params=_cparams(("arbitrary", "arbitrary")),
        name="retention",
    )(q, k, v, g, s0, gam, ng, cos_t, sin_t)


def _fill_padded(pad_ref, cur, prev, nxt, has_prev, has_next, halo):
    tile = cur.shape[0]
    pad_ref[0:halo, :] = jnp.where(has_prev, prev, 0.0)
    pad_ref[halo:halo + tile, :] = cur
    pad_ref[halo + tile:halo + tile + halo, :] = jnp.where(has_next, nxt, 0.0)


def _conv_taps(pad_ref, w_ref, ntaps, halo, tile):
    base = halo - ntaps // 2
    acc = w_ref[0:1, :] * pad_ref[base:base + tile, :]
    for kk in range(1, ntaps):
        acc = acc + w_ref[kk:kk + 1, :] * pad_ref[base + kk:base + kk + tile, :]
    return acc


def _halo_specs(width, col_map, halo):
    per = CONV_TILE // halo

    def cur(i, j):
        return (i, col_map(j))

    def prev(i, j):
        return (jnp.maximum(i * per - 1, 0), col_map(j))

    def make_next(nblk):
        def nxt(i, j):
            return (jnp.minimum((i + 1) * per, nblk - 1), col_map(j))
        return nxt

    return cur, prev, make_next


def _sconv_kernel(bg_ref, cg_ref, cgp_ref, cgn_ref, hb_ref, hbp_ref, hbn_ref, w_ref, o_ref, pad_ref, *, st):
    has_prev, has_next = st.halo_flags(pl.program_id(0), CONV_TILE)
    _fill_padded(pad_ref, cg_ref[...] * hb_ref[...], cgp_ref[...] * hbp_ref[...],
                 cgn_ref[...] * hbn_ref[...], has_prev, has_next, SUBLANES)
    o_ref[...] = bg_ref[...] * _conv_taps(pad_ref, w_ref, SC_K, SUBLANES, CONV_TILE)


CONV_COLS = 512


def _short_gated_conv(st, bg, cg, hb, w):
    t, c = bg.shape
    halo = SUBLANES
    cur, prev, make_next = _halo_specs(c, lambda j: j, halo)
    nxt = make_next(t // halo)
    tile_spec = pl.BlockSpec((CONV_TILE, CONV_COLS), cur)
    prev_spec = pl.BlockSpec((halo, CONV_COLS), prev)
    next_spec = pl.BlockSpec((halo, CONV_COLS), nxt)
    return pl.pallas_call(
        functools.partial(_sconv_kernel, st=st),
        grid=(t // CONV_TILE, c // CONV_COLS),
        in_specs=[tile_spec, tile_spec, prev_spec, next_spec, tile_spec, prev_spec, next_spec,
                  pl.BlockSpec((SUBLANES, CONV_COLS), lambda i, j: (0, j))],
        out_specs=tile_spec,
        out_shape=jax.ShapeDtypeStruct((t, c), F32),
        scratch_shapes=[pltpu.VMEM((CONV_TILE + 2 * halo, CONV_COLS), F32)],
        compiler_params=_cparams(("arbitrary", "arbitrary")),
        name="short_gated_conv",
    )(bg, cg, cg, cg, hb, hb, hb, w)


def _qkv_conv_kernel(x_ref, xp_ref, xn_ref, w_ref, o_ref, pad_ref, *, st):
    has_prev, has_next = st.halo_flags(pl.program_id(0), CONV_TILE)
    _fill_padded(pad_ref, x_ref[...], xp_ref[...], xn_ref[...], has_prev, has_next, SUBLANES)
    s = _silu(_conv_taps(pad_ref, w_ref, QKV_K, SUBLANES, CONV_TILE))
    j = pl.program_id(1)
    for hh in range(CONV_COLS // DK_C):
        hsl = slice(hh * DK_C, (hh + 1) * DK_C)
        sh = s[:, hsl]
        nrm = sh * lax.rsqrt(jnp.sum(sh * sh, axis=-1, keepdims=True) + EPS)
        o_ref[:, hsl] = jnp.where(j == 0, nrm * (DK_C ** -0.5), jnp.where(j == 1, nrm, sh))


def _qkv_conv(st, qkv, w):
    t, c = qkv.shape
    assert CONV_COLS == H_C * DK_C == H_C * DV_C
    halo = SUBLANES
    cur, prev, make_next = _halo_specs(c, lambda j: j, halo)
    nxt = make_next(t // halo)
    tile_spec = pl.BlockSpec((CONV_TILE, CONV_COLS), cur)
    return pl.pallas_call(
        functools.partial(_qkv_conv_kernel, st=st),
        grid=(t // CONV_TILE, c // CONV_COLS),
        in_specs=[tile_spec, pl.BlockSpec((halo, CONV_COLS), prev), pl.BlockSpec((halo, CONV_COLS), nxt),
                  pl.BlockSpec((SUBLANES, CONV_COLS), lambda i, j: (0, j))],
        out_specs=tile_spec,
        out_shape=jax.ShapeDtypeStruct((t, c), F32),
        scratch_shapes=[pltpu.VMEM((CONV_TILE + 2 * halo, CONV_COLS), F32)],
        compiler_params=_cparams(("arbitrary", "arbitrary")),
        name="qkv_conv",
    )(qkv, qkv, qkv, w)


CF_HALO = 16


def _conformer_kernel(ca_ref, cap_ref, can_ref, cg_ref, cgp_ref, cgn_ref, w_ref, b_ref, lg_ref, lb_ref,
                      o_ref, pad_ref, *, st):
    has_prev, has_next = st.halo_flags(pl.program_id(0), CONV_TILE)
    _fill_padded(pad_ref, ca_ref[...] * _sigmoid(cg_ref[...]), cap_ref[...] * _sigmoid(cgp_ref[...]),
                 can_ref[...] * _sigmoid(cgn_ref[...]), has_prev, has_next, CF_HALO)
    hc = _conv_taps(pad_ref, w_ref, CF_K, CF_HALO, CONV_TILE) + b_ref[...]
    mu = jnp.mean(hc, axis=-1, keepdims=True)
    xc = hc - mu
    y = xc * lax.rsqrt(jnp.mean(xc * xc, axis=-1, keepdims=True) + EPS) * lg_ref[...] + lb_ref[...]
    o_ref[...] = _silu(y)


def _conformer(st, glu, w, b, ln_g, ln_b):
    t, c2 = glu.shape
    c = c2 // 2
    nblk = t // CF_HALO
    per = CONV_TILE // CF_HALO
    vec = pl.BlockSpec((1, c), lambda i: (0, 0))
    return pl.pallas_call(
        functools.partial(_conformer_kernel, st=st),
        grid=(t // CONV_TILE,),
        in_specs=[pl.BlockSpec((CONV_TILE, c), lambda i: (i, 0)),
                  pl.BlockSpec((CF_HALO, c), lambda i: (jnp.maximum(i * per - 1, 0), 0)),
                  pl.BlockSpec((CF_HALO, c), lambda i: (jnp.minimum((i + 1) * per, nblk - 1), 0)),
                  pl.BlockSpec((CONV_TILE, c), lambda i: (i, 1)),
                  pl.BlockSpec((CF_HALO, c), lambda i: (jnp.maximum(i * per - 1, 0), 1)),
                  pl.BlockSpec((CF_HALO, c), lambda i: (jnp.minimum((i + 1) * per, nblk - 1), 1)),
                  pl.BlockSpec((2 * CF_HALO, c), lambda i: (0, 0)), vec, vec, vec],
        out_specs=pl.BlockSpec((CONV_TILE, c), lambda i: (i, 0)),
        out_shape=jax.ShapeDtypeStruct((t, c), F32),
        scratch_shapes=[pltpu.VMEM((CONV_TILE + 2 * CF_HALO, c), F32)],
        compiler_params=_cparams(("arbitrary",)),
        name="conformer_conv",
    )(glu, glu, glu, glu, glu, glu, w, b, ln_g, ln_b)


GDN_PACK = 4
GDN_GROUPS = 4


def _split_bf16(x):
    hi = x.astype(BF16)
    return hi, (x - hi.astype(F32)).astype(BF16)


def _packed_unit_inverses(groups):
    dot = functools.partial(jnp.dot, preferred_element_type=F32)
    cat = jnp.concatenate
    n = len(groups[0])
    c = CHUNK
    w = n * c
    eye = (lax.broadcasted_iota(jnp.int32, (c, c), 0) == lax.broadcasted_iota(jnp.int32, (c, c), 1)).astype(F32)
    lane_block = lax.broadcasted_iota(jnp.int32, (c, 2 * w), 1) % w // c

    def rhs_of(p):
        hi, lo = _split_bf16(p)
        both = cat([hi, lo], axis=1)
        return hi, lo, cat([jnp.where(lane_block == i, both, jnp.zeros_like(both)) for i in range(n)], axis=0)

    def product(m, r):
        return (r[:m, :w] + r[m:, :w]) + (r[:m, w:] + r[m:, w:])

    ps = [cat(a_list, axis=1) for a_list in groups]
    ts = [cat([eye - a for a in a_list], axis=1) for a_list in groups]
    for g, p in enumerate(ps):
        hi, lo, rhs = rhs_of(p)
        ps[g] = product(c, dot(cat([hi, lo], axis=0), rhs))
    steps = int(math.log2(c)) - 1
    for step in range(steps):
        for g in range(len(groups)):
            p_hi, p_lo, rhs = rhs_of(ps[g])
            t_hi, t_lo = _split_bf16(ts[g])
            if step < steps - 1:
                both = product(2 * c, dot(cat([t_hi, p_hi, t_lo, p_lo], axis=0), rhs))
                ts[g] = ts[g] + both[:c]
                ps[g] = both[c:]
            else:
                ts[g] = ts[g] + product(c, dot(cat([t_hi, t_lo], axis=0), rhs))
    return [[t[:, i * c:(i + 1) * c] for i in range(n)] for t in ts]


def _chunk_cumsum(x, reverse):
    row = lax.broadcasted_iota(jnp.int32, x.shape, 0)
    s = 1
    while s < CHUNK:
        if reverse:
            x = x + jnp.where(row < CHUNK - s, pltpu.roll(x, CHUNK - s, 0), 0.0)
        else:
            x = x + jnp.where(row >= s, pltpu.roll(x, s, 0), 0.0)
        s *= 2
    return x


def _gdn_kernel(q_ref, k_ref, v_ref, z_ref, ab_ref, alog_ref, dtb_ref, s0_ref, ng_ref,
                y_ref, sout_ref, g_ref, beta_ref, u_ref, wq_ref, ak_ref, egl_ref, o_ref, *, seq_len):
    nc = seq_len // CHUNK
    h = pl.program_id(1)
    row = lax.broadcasted_iota(jnp.int32, (CHUNK, CHUNK), 0)
    col = lax.broadcasted_iota(jnp.int32, (CHUNK, CHUNK), 1)
    lane = lax.broadcasted_iota(jnp.int32, (CHUNK, LANES), 1)
    alog = alog_ref[...]
    dtb = dtb_ref[...]

    def gates(n, carry):
        rows = pl.ds(pl.multiple_of(n * CHUNK, CHUNK), CHUNK)
        ab = ab_ref[rows, :]
        g_all = -jnp.exp(alog) * _softplus(ab + dtb)
        b_all = _sigmoid(ab)
        for d in range(2):
            gsel = jnp.sum(jnp.where(lane == d * H_C + h, g_all, 0.0), axis=-1, keepdims=True)
            bsel = jnp.sum(jnp.where(lane == 2 * H_C + d * H_C + h, b_all, 0.0), axis=-1, keepdims=True)
            g_ref[d, rows, :] = jnp.broadcast_to(gsel, (CHUNK, LANES))
            beta_ref[d, rows, :] = jnp.broadcast_to(bsel, (CHUNK, LANES))
        return carry

    lax.fori_loop(0, nc, gates, 0)

    n_groups = min(GDN_GROUPS, nc // (GDN_PACK // 2))

    def precompute(it, carry):
        groups = [prepare_group(it * n_groups + grp) for grp in range(n_groups)]
        inverses = _packed_unit_inverses([[sysm[0] for sysm in systems] for systems in groups])
        for t_invs, systems in zip(inverses, groups):
            for t_inv, (_, rhs, d, n) in zip(t_invs, systems):
                uw = _bdot(t_inv, rhs)
                u_ref[d, pl.ds(pl.multiple_of(n * CHUNK, CHUNK), CHUNK), :] = uw[:, :DV_C]
                wq_ref[d, pl.ds(pl.multiple_of(n * 2 * CHUNK, CHUNK), CHUNK), :] = uw[:, DV_C:].astype(BF16)
        return carry

    def prepare_group(pair):
        systems = []
        for jc in range(GDN_PACK // 2):
            n = pair * (GDN_PACK // 2) + jc
            rows = pl.ds(pl.multiple_of(n * CHUNK, CHUNK), CHUNK)
            qc = q_ref[rows, :]
            kc = k_ref[rows, :]
            vc = v_ref[rows, :]
            gbs = [_chunk_cumsum(g_ref[d, rows, :], d == 1) for d in range(2)]
            bbs = [beta_ref[d, rows, :] for d in range(2)]
            kbs = [kc * bb for bb in bbs]
            prod = _bdot_nt(jnp.concatenate(kbs + [qc], axis=0), kc)
            for d in range(2):
                incl = (row >= col) if d == 0 else (row <= col)
                strict = (row > col) if d == 0 else (row < col)
                last = CHUNK - 1 if d == 0 else 0
                gb = gbs[d]
                gr = gb[:, :CHUNK].T
                gl = gb[last:last + 1, :]
                decay = jnp.where(incl, jnp.exp(jnp.where(incl, gb[:, :CHUNK] - gr, 0.0)), 0.0)
                eg = jnp.exp(gb)
                a_low = jnp.where(strict, prod[d * CHUNK:(d + 1) * CHUNK] * decay, 0.0)
                attn = jnp.where(incl, prod[2 * CHUNK:] * decay, 0.0)
                kg_t = (kc * jnp.exp(gl - gb)).T
                ak_ref[d, pl.ds(pl.multiple_of(n * 3 * CHUNK, CHUNK), 3 * CHUNK), :] = jnp.concatenate(
                    [attn, kg_t], axis=0).astype(BF16)
                wq_ref[d, pl.ds(pl.multiple_of(n * 2 * CHUNK + CHUNK, CHUNK), CHUNK), :] = (qc * eg).astype(BF16)
                egl_ref[d, pl.ds(n, 1), :] = jnp.exp(gl)
                systems.append((a_low, jnp.concatenate([vc * bbs[d], kbs[d] * eg], axis=1), d, n))
        return systems

    lax.fori_loop(0, nc // (n_groups * GDN_PACK // 2), precompute, 0)

    o_ref[...] = jnp.zeros_like(o_ref)

    def scan(i, carry):
        out = []
        for d, s in enumerate(carry):
            n = i if d == 0 else nc - 1 - i
            rows = pl.ds(pl.multiple_of(n * CHUNK, CHUNK), CHUNK)
            on_s = _bdot(wq_ref[d, pl.ds(pl.multiple_of(n * 2 * CHUNK, CHUNK), 2 * CHUNK), :], s)
            v_new = u_ref[d, rows, :] - on_s[:CHUNK]
            on_v = _bdot(ak_ref[d, pl.ds(pl.multiple_of(n * 3 * CHUNK, CHUNK), 3 * CHUNK), :], v_new)
            o_ref[rows, :] += on_s[CHUNK:] + on_v[:CHUNK]
            out.append(s * egl_ref[d, pl.ds(n, 1), :] + on_v[CHUNK:])
        return tuple(out)

    s_f, s_b = lax.fori_loop(0, nc, scan, (s0_ref[0, 0, 0], s0_ref[0, 1, 0]))
    sout_ref[0, 0, 0] = s_f
    sout_ref[0, 1, 0] = s_b
    y_ref[...] = _rms(o_ref[...], ng_ref[...]) * _silu(z_ref[...])


def _gdn(qkv_n, z, ab, alog, dtb, s0, ng, nseq, seq_len, row_off):
    blk = lambda off: pl.BlockSpec((seq_len, LANES), lambda b, h: (b + row_off, h + off))
    vec = pl.BlockSpec((1, LANES), lambda b, h: (0, 0))
    st_spec = pl.BlockSpec((1, 2, 1, DK_C, DV_C), lambda b, h: (b, 0, h, 0, 0))
    nc = seq_len // CHUNK
    both = lambda width, dtype: pltpu.VMEM((2, seq_len, width), dtype)
    return pl.pallas_call(
        functools.partial(_gdn_kernel, seq_len=seq_len),
        grid=(nseq, H_C),
        in_specs=[blk(0), blk(H_C), blk(2 * H_C), blk(0),
                  pl.BlockSpec((seq_len, LANES), lambda b, h: (b + row_off, 0)),
                  vec, vec, st_spec, vec],
        out_specs=[pl.BlockSpec((seq_len, LANES), lambda b, h: (b, h)), st_spec],
        out_shape=[jax.ShapeDtypeStruct((nseq * seq_len, H_C * DV_C), F32),
                   jax.ShapeDtypeStruct((nseq, 2, H_C, DK_C, DV_C), F32)],
        scratch_shapes=[both(LANES, F32), both(LANES, F32), both(DV_C, F32),
                        pltpu.VMEM((2, 2 * seq_len, DK_C), BF16), pltpu.VMEM((2, 3 * seq_len, CHUNK), BF16),
                        pltpu.VMEM((2, max(nc, SUBLANES), LANES), F32), pltpu.VMEM((seq_len, DV_C), F32)],
        compiler_params=_cparams(("arbitrary", "arbitrary")),
        name="gated_deltanet",
    )(qkv_n, qkv_n, qkv_n, z, ab, alog, dtb, s0, ng)


def _out_proj_kernel(ya_ref, yb_ref, x_ref, mod_ref, g_ref, wo_ref, wq_ref, x1_ref, h2_ref, qp_ref):
    half = ya_ref.shape[1]
    m = mod_ref[0]
    y = (jnp.dot(ya_ref[...].astype(BF16), wo_ref[0:half, :], preferred_element_type=F32)
         + jnp.dot(yb_ref[...].astype(BF16), wo_ref[half:, :], preferred_element_type=F32))
    x1 = x_ref[...] + m[2:3] * y
    x1_ref[...] = x1
    h2 = (_rms(x1, g_ref[...]) * (1.0 + m[4:5]) + m[3:4]).astype(BF16)
    h2_ref[...] = h2
    qp_ref[...] = jnp.dot(h2, wq_ref[...], preferred_element_type=F32)


def _out_proj(st, ya, yb, x, mod, g, wo, wq, tm):
    t, d = x.shape
    half = ya.shape[1]
    nq = wq.shape[1]
    return pl.pallas_call(
        _out_proj_kernel,
        grid=(t // tm,),
        in_specs=[pl.BlockSpec((tm, half), lambda i: (i, 0)),
                  pl.BlockSpec((tm, half), lambda i: (i, 0)),
                  pl.BlockSpec((tm, d), lambda i: (i, 0)),
                  pl.BlockSpec((1, 6, d), lambda i: (st.mod_row(i, tm), 0, 0)),
                  pl.BlockSpec((1, d), lambda i: (0, 0)),
                  pl.BlockSpec((2 * half, d), lambda i: (0, 0)),
                  pl.BlockSpec((d, nq), lambda i: (0, 0))],
        out_specs=[pl.BlockSpec((tm, d), lambda i: (i, 0)),
                   pl.BlockSpec((tm, d), lambda i: (i, 0)),
                   pl.BlockSpec((tm, nq), lambda i: (i, 0))],
        out_shape=[jax.ShapeDtypeStruct((t, d), F32), jax.ShapeDtypeStruct((t, d), BF16),
                   jax.ShapeDtypeStruct((t, nq), F32)],
        compiler_params=_cparams(("arbitrary",)),
        name="out_proj",
    )(ya, yb, x, mod, g, wo, wq)


def _candidate_rows():
    groups = ([(0, k2) for k2 in range(16)], [(1, k2) for k2 in range(8)], [(2, k2) for k2 in range(8)],
              [(3, k2) for k2 in range(8)], [(k1, 0) for k1 in range(16)], [(k1, 1) for k1 in range(8)],
              [(k1, 2) for k1 in range(8)])
    rows, seen = [], set()
    for grp in groups:
        for k1, k2 in grp:
            ok = (k1 + 1) * (k2 + 1) <= TOPK and (k1, k2) not in seen
            if ok:
                seen.add((k1, k2))
            rows.append(float(k1 * TOPK + k2) if ok else CAND_INVALID)
    return rows


CAND_INVALID = float(TOPK * TOPK)
RANK_MARK = 2.0 ** 100
RANK_STEP = 2.0 ** 95
SELECT_STRIPS = 2


def _pack_candidates(a1, a2, op):
    return jnp.concatenate([op(a1[0:1], a2), op(a1[1:2], a2[0:8]), op(a1[2:3], a2[0:8]),
                            op(a1[3:4], a2[0:8]), op(a1, a2[0:1]), op(a1[0:8], a2[1:2]),
                            op(a1[0:8], a2[2:3])], axis=0)


def _top16(s, vals_ref, exact):
    n = s.shape[0]
    if not exact:
        for kk in range(TOPK):
            m = jnp.max(s, axis=0, keepdims=True)
            vals_ref[kk:kk + 1, :] = m
            s = jnp.where(s == m, -(RANK_MARK + kk * RANK_STEP), s)
        return jnp.where(s < -0.5 * RANK_MARK, (-s - RANK_MARK) * (1.0 / RANK_STEP), float(TOPK))
    iota = lax.broadcasted_iota(jnp.int32, s.shape, 0).astype(F32)
    rank = jnp.full(s.shape, float(TOPK), F32)
    for kk in range(TOPK):
        m = jnp.max(s, axis=0, keepdims=True)
        idx = jnp.min(jnp.where(s == m, iota, float(n)), axis=0, keepdims=True)
        hit = iota == idx
        rank = jnp.where(hit, float(kk), rank)
        vals_ref[kk:kk + 1, :] = m
        s = jnp.where(hit, NEG_INF, s)
    return rank


def _peer_select_kernel(q_ref, keys_ref, cflat_ref, e1_ref, cnt_ref, e2_ref, r2_ref, v1_ref, v2_ref, *, tt):
    nk = N_KEYS
    cflat = cflat_ref[...]
    valid = cflat < CAND_INVALID
    row16 = lax.broadcasted_iota(jnp.int32, (TOPK, LANES), 0)

    def compute(strip, exact):
        s1, s2, tsl, va_ref, vb_ref = strip
        rank1 = _top16(s1, va_ref, exact)
        rank2 = _top16(s2, vb_ref, exact)
        v1 = va_ref[...]
        v2 = vb_ref[...]
        cand = jnp.where(valid, _pack_candidates(v1, v2, jnp.add), NEG_INF)
        prod = _pack_candidates(jnp.exp(v1 - v1[0:1]), jnp.exp(v2 - v2[0:1]), jnp.multiply)
        sel = jnp.zeros(cand.shape, F32)
        for _ in range(TOPK):
            m = jnp.max(cand, axis=0, keepdims=True)
            if exact:
                idx = jnp.min(jnp.where(cand == m, cflat, CAND_INVALID), axis=0, keepdims=True)
                hit = cflat == idx
            else:
                hit = cand == m
            sel = jnp.where(hit, 1.0, sel)
            cand = jnp.where(hit, NEG_INF, cand)
        zsum = jnp.sum(sel * prod, axis=0, keepdims=True)
        c_all = sel[40:56] + jnp.concatenate([sel[56:64] + sel[64:72], jnp.zeros((8, LANES), F32)], axis=0)
        for k1, (lo, hi) in enumerate(((0, 16), (16, 24), (24, 32), (32, 40))):
            c_all = jnp.where(row16 == k1, jnp.sum(sel[lo:hi], axis=0, keepdims=True), c_all)
        cnt = jnp.zeros(rank1.shape, F32)
        for k1 in range(TOPK):
            cnt = jnp.where(rank1 == float(k1), c_all[k1:k1 + 1], cnt)
        in1 = rank1 < float(TOPK)
        in2 = rank2 < float(TOPK)
        e1_ref[0, :, tsl] = jnp.where(in1, jnp.exp(s1 - v1[0:1]), 0.0) / zsum
        cnt_ref[0, :, tsl] = cnt
        e2_ref[:, tsl] = jnp.where(in2, jnp.exp(s2 - v2[0:1]), 0.0).astype(e2_ref.dtype)
        r2_ref[:, tsl] = rank2.astype(r2_ref.dtype)
        n1 = jnp.sum(jnp.where(in1, 1.0, 0.0), axis=0, keepdims=True)
        n2 = jnp.sum(jnp.where(in2, 1.0, 0.0), axis=0, keepdims=True)
        nc = jnp.sum(sel, axis=0, keepdims=True)
        want = float(TOPK)
        return jnp.abs(n1 - want) + jnp.abs(n2 - want) + jnp.abs(nc - want)

    def strip_group(si, carry):
        strips = []
        for sub in range(SELECT_STRIPS):
            t0 = pl.multiple_of((si * SELECT_STRIPS + sub) * LANES, LANES)
            tsl = pl.ds(t0, LANES)
            s1 = _bdot_nt(keys_ref[0, 0], q_ref[tsl, 0:nk])
            s2 = _bdot_nt(keys_ref[0, 1], q_ref[tsl, nk:2 * nk])
            strips.append((s1, s2, tsl, v1_ref.at[sub], v2_ref.at[sub]))
        tied = [compute(strip, False) for strip in strips]
        for strip, bad in zip(strips, tied):
            @pl.when(jnp.max(bad) > 0.0)
            def _(strip=strip):
                compute(strip, True)
        return carry

    lax.fori_loop(0, tt // (SELECT_STRIPS * LANES), strip_group, 0)


def _peer_select(qp, keys, tt):
    t = qp.shape[0]
    rows = _candidate_rows()
    cflat = jnp.broadcast_to(jnp.asarray(rows, F32)[:, None], (len(rows), LANES))
    out_spec = pl.BlockSpec((1, N_KEYS, tt), lambda i, h: (h, 0, i))
    f32_sds = jax.ShapeDtypeStruct((H_P, N_KEYS, t), F32)
    flat_spec = pl.BlockSpec((N_KEYS, tt), lambda i, h: (h, i))
    flat_sds = jax.ShapeDtypeStruct((H_P * N_KEYS, t), BF16)
    return pl.pallas_call(
        functools.partial(_peer_select_kernel, tt=tt),
        grid=(t // tt, H_P),
        in_specs=[pl.BlockSpec((tt, 2 * N_KEYS), lambda i, h: (i, h)),
                  pl.BlockSpec((1, 2, N_KEYS, N_KEYS), lambda i, h: (h, 0, 0, 0)),
                  pl.BlockSpec(cflat.shape, lambda i, h: (0, 0))],
        out_specs=[out_spec, out_spec, flat_spec, flat_spec],
        out_shape=[f32_sds, f32_sds, flat_sds, flat_sds],
        scratch_shapes=[pltpu.VMEM((SELECT_STRIPS, TOPK, LANES), F32),
                        pltpu.VMEM((SELECT_STRIPS, TOPK, LANES), F32)],
        compiler_params=_cparams(("arbitrary", "arbitrary")),
        name="peer_select",
    )(qp, keys, cflat)


def _peer_dense_kernel(h_ref, u_ref, vt_ref, e1_ref, cnt_ref, e2_ref, r2_ref, x_ref, mod_ref, o_ref,
                       acc_ref, s_ref, p_ref, e2b_ref, r2b_ref, ht_ref, *, na):
    j = pl.program_id(1)
    tt = s_ref.shape[1]

    @pl.when(j == 0)
    def _():
        acc_ref[...] = jnp.zeros_like(acc_ref)
        e2b_ref[...] = e2_ref[...].astype(BF16)
        r2b_ref[...] = r2_ref[...].astype(BF16)
        ht_ref[...] = h_ref[...].astype(F32).T.astype(BF16)

    s_ref[...] = jnp.dot(u_ref[...], ht_ref[...], preferred_element_type=F32)

    for aa in range(na):
        rows = slice(aa * N_KEYS, (aa + 1) * N_KEYS)
        for tg in range(tt // LANES):
            tsl = slice(tg * LANES, (tg + 1) * LANES)
            gate = None
            for hh in range(H_P):
                cnt_row = cnt_ref[hh, aa:aa + 1, tsl].astype(BF16)
                e1_row = e1_ref[hh, aa:aa + 1, tsl].astype(BF16)
                hsl = slice(hh * N_KEYS, (hh + 1) * N_KEYS)
                term = jnp.where(r2b_ref[hsl, tsl] < cnt_row, e2b_ref[hsl, tsl], 0.0) * e1_row
                gate = term if gate is None else gate + term
            p_ref[rows, tsl] = gate * _gelu_tanh(s_ref[rows, tsl].astype(BF16))

    acc_ref[...] += jnp.dot(vt_ref[...], p_ref[...], preferred_element_type=F32)

    @pl.when(j == pl.num_programs(1) - 1)
    def _():
        o_ref[...] = x_ref[...] + mod_ref[0][5:6] * acc_ref[...].T


def _peer_dense(st, h2, u, vt, e1, cnt, e2, r2, x, mod, tt, et):
    t, d = x.shape
    n_exp = u.shape[0]
    na = et // N_KEYS
    row_spec = pl.BlockSpec((H_P, na, tt), lambda i, j: (0, j, i))
    full_spec = pl.BlockSpec((H_P * N_KEYS, tt), lambda i, j: (0, i))
    return pl.pallas_call(
        functools.partial(_peer_dense_kernel, na=na),
        grid=(t // tt, n_exp // et),
        in_specs=[pl.BlockSpec((tt, d), lambda i, j: (i, 0)),
                  pl.BlockSpec((et, d), lambda i, j: (j, 0)),
                  pl.BlockSpec((d, et), lambda i, j: (0, j)),
                  row_spec, row_spec, full_spec, full_spec,
                  pl.BlockSpec((tt, d), lambda i, j: (i, 0)),
                  pl.BlockSpec((1, 6, d), lambda i, j: (st.mod_row(i, tt), 0, 0))],
        out_specs=pl.BlockSpec((tt, d), lambda i, j: (i, 0)),
        out_shape=jax.ShapeDtypeStruct((t, d), F32),
        scratch_shapes=[pltpu.VMEM((d, tt), F32), pltpu.VMEM((et, tt), F32), pltpu.VMEM((et, tt), BF16),
                        pltpu.VMEM((H_P * N_KEYS, tt), BF16), pltpu.VMEM((H_P * N_KEYS, tt), BF16),
                        pltpu.VMEM((d, tt), BF16)],
        compiler_params=_cparams(("arbitrary", "arbitrary")),
        name="peer_dense",
    )(h2, u, vt, e1, cnt, e2, r2, x, mod)


def _final_norm_kernel(x_ref, g_ref, o_ref):
    o_ref[...] = _rms(x_ref[...], g_ref[...])


def _final_norm(x, g, tm):
    t, d = x.shape
    return pl.pallas_call(
        _final_norm_kernel,
        grid=(t // tm,),
        in_specs=[pl.BlockSpec((tm, d), lambda i: (i, 0)), pl.BlockSpec((1, d), lambda i: (0, 0))],
        out_specs=pl.BlockSpec((tm, d), lambda i: (i, 0)),
        out_shape=jax.ShapeDtypeStruct((t, d), F32),
        compiler_params=_cparams(("arbitrary",)),
        name="final_norm",
    )(x, g)


def _rope_tables(seq_len):
    pos = jnp.arange(seq_len)
    rowp = (pos // GRID_W).astype(F32)
    colp = (pos % GRID_W).astype(F32)
    nf = DK_A // 4
    freqs = ROPE_BASE ** (-jnp.arange(nf, dtype=F32) / nf)
    ang = jnp.concatenate([rowp[:, None] * freqs, colp[:, None] * freqs], axis=-1)
    cos = jnp.cos(ang)
    sin = jnp.sin(ang)
    cos_t = jnp.tile(cos, (1, 4))
    sin_t = jnp.tile(jnp.concatenate([-sin, sin], axis=-1), (1, 2))
    return cos_t, sin_t


def _pad_rows(w, rows):
    return jnp.concatenate([w, jnp.zeros((rows - w.shape[0], w.shape[1]), w.dtype)], axis=0)


def _lane_row(vals):
    flat = vals.reshape(-1).astype(F32)
    return jnp.concatenate([flat, jnp.zeros((LANES - flat.shape[0],), F32)])[None, :]


def kernel(x_prompt, x_sample, state_ret, state_gdn, c, c_ctx, ada_w, ada_b, norm_mix_g, norm_ffn_g,
           final_norm_g, ev_w_in, ev_w_out, ret_gamma_logit, ret_norm_g, sc_conv_w, od_w_in, od_w_out,
           gdn_conv_w, gdn_a_log, gdn_dt_bias, gdn_norm_g, cf_dw_w, cf_dw_b, cf_ln_g, cf_ln_b,
           peer_wq, peer_keys, peer_u, peer_v):
    bp, lp, d = x_prompt.shape
    bs, ls, _ = x_sample.shape
    depth = ada_w.shape[0]
    st = _Streams(bp, lp, bs, ls)
    w_a = H_A * DV_A
    w_b = d - w_a
    w_c = H_C * DV_C
    w_d = d - w_c
    tm = TOKEN_TILE
    select_tt = TOKEN_TILE

    x = jnp.concatenate([x_prompt.reshape(st.tp, d), x_sample.reshape(st.ts, d)], axis=0)
    cvec = jnp.concatenate([c_ctx[None, :], c, jnp.zeros((N_MOD_ROWS - 1 - bs, d), F32)], axis=0)
    mods = _modulation(cvec, ada_w, ada_b).reshape(depth, N_MOD_ROWS, 6, d)
    cos_t, sin_t = _rope_tables(ls)
    zero_ret = jnp.zeros((bp, 2, H_A, DK_A, DV_A), F32)
    zero_gdn = jnp.zeros((bp, 2, H_C, DK_C, DV_C), F32)

    ret_new, gdn_new = [], []
    for l in range(depth):
        i = l // 2
        mod = mods[l]
        g1 = norm_mix_g[l][None, :]
        if l % 2 == 0:
            widths = (H_A * DK_A, H_A * DK_A, w_a, w_a, w_b, w_b, w_b)
            q, k, v, g, bg, cg, hb = _norm_proj(st, x, mod, g1, ev_w_in[i].astype(BF16), widths, tm)
            gam = jnp.broadcast_to(ret_gamma_logit[i][:, :, None, None], (2, H_A, CHUNK, DV_A))
            ng = ret_norm_g[i][:, None, :]
            ya_p, s_new = _retention(q, k, v, g, zero_ret, gam, ng, cos_t[:lp], sin_t[:lp],
                                     bp, lp, 0, False)
            ya_s, _ = _retention(q, k, v, g, state_ret[:, i], gam, ng, cos_t, sin_t,
                                 bs, ls, st.tp // ls, True)
            ret_new.append(s_new)
            ya = jnp.concatenate([ya_p, ya_s], axis=0)
            yb = _short_gated_conv(st, bg, cg, hb, _pad_rows(sc_conv_w[i], SUBLANES))
            w_out = ev_w_out[i]
        else:
            n_gate = 2 * 2 * H_C
            w_in = od_w_in[i]
            o_ab = 4 * w_c
            w_main = jnp.concatenate([w_in[:, :o_ab], w_in[:, o_ab + n_gate:],
                                      w_in[:, o_ab:o_ab + n_gate],
                                      jnp.zeros((d, LANES - n_gate), F32)], axis=1).astype(BF16)
            widths = (3 * w_c, w_c, 2 * w_d, LANES)
            qkv, z, glu, ab = _norm_proj(st, x, mod, g1, w_main, widths, tm)
            qkv_n = _qkv_conv(st, qkv, _pad_rows(gdn_conv_w[i], SUBLANES))
            alog = _lane_row(gdn_a_log[i])
            dtb = _lane_row(gdn_dt_bias[i])
            ng = gdn_norm_g[i][None, :]
            yc_p, s_new = _gdn(qkv_n, z, ab, alog, dtb, zero_gdn, ng, bp, lp, 0)
            yc_s, _ = _gdn(qkv_n, z, ab, alog, dtb, state_gdn[:, i], ng, bs, ls, st.tp // ls)
            gdn_new.append(s_new)
            ya = jnp.concatenate([yc_p, yc_s], axis=0)
            yb = _conformer(st, glu, _pad_rows(cf_dw_w[i], 2 * CF_HALO), cf_dw_b[i][None, :],
                            cf_ln_g[i][None, :], cf_ln_b[i][None, :])
            w_out = od_w_out[i]
        x, h2, qp = _out_proj(st, ya, yb, x, mod, norm_ffn_g[l][None, :], w_out.astype(BF16),
                              peer_wq[l].astype(BF16), tm)
        e1, cnt, e2, r2 = _peer_select(qp, peer_keys[l].astype(BF16), select_tt)
        x = _peer_dense(st, h2, peer_u[l].astype(BF16), peer_v[l].astype(BF16).T, e1, cnt, e2, r2,
                        x, mod, PEER_TOKEN_TILE, PEER_EXPERT_TILE)

    y = _final_norm(x, final_norm_g[None, :], tm)
    y_prompt = y[:st.tp].reshape(bp, lp, d)
    y_sample = y[st.tp:].reshape(bs, ls, d)
    new_state_ret = jnp.stack(ret_new, axis=1).astype(x_prompt.dtype)
    new_state_gdn = jnp.stack(gdn_new, axis=1).astype(x_prompt.dtype)
    return (y_prompt, y_sample, new_state_ret, new_state_gdn)
```

```python
import functools
import math

import jax
import jax.numpy as jnp
from jax import lax
from jax.experimental import pallas as pl
from jax.experimental.pallas import tpu as pltpu

F32 = jnp.float32
BF16 = jnp.bfloat16
HIGHEST = lax.Precision.HIGHEST

EPS = 1e-6
CHUNK = 64
GRID_W = 64
ROPE_BASE = 10000.0
H_A, DK_A, DV_A = 4, 64, 128
H_C, DK_C, DV_C = 4, 128, 128
SC_K, QKV_K, CF_K = 3, 3, 31
N_KEYS, H_P, TOPK = 128, 8, 16
N_MOD_ROWS = 16
LANES = 128
SUBLANES = 8
VMEM_LIMIT = 56 * 1024 * 1024
CONV_TILE = 256
TOKEN_TILE = 512
PEER_TOKEN_TILE = 512
PEER_EXPERT_TILE = 2048
NEG_INF = float("-inf")


def _cparams(sem):
    return pltpu.CompilerParams(dimension_semantics=sem, vmem_limit_bytes=VMEM_LIMIT)


def _bdot(a, b):
    return jnp.dot(a.astype(BF16), b.astype(BF16), preferred_element_type=F32)


def _bdot_nt(a, b):
    return lax.dot_general(a.astype(BF16), b.astype(BF16), (((1,), (1,)), ((), ())),
                           preferred_element_type=F32)


def _bdot_tn(a, b):
    return lax.dot_general(a.astype(BF16), b.astype(BF16), (((0,), (0,)), ((), ())),
                           preferred_element_type=F32)


def _hdot(a, b):
    return jnp.dot(a, b, precision=HIGHEST, preferred_element_type=F32)


def _sigmoid(x):
    return 1.0 / (1.0 + jnp.exp(-x))


def _silu(x):
    return x * _sigmoid(x)


def _softplus(x):
    return jnp.maximum(x, 0.0) + jnp.log1p(jnp.exp(-jnp.abs(x)))


def _log_sigmoid(x):
    return -_softplus(-x)


def _gelu_tanh(x):
    c = math.sqrt(2.0 / math.pi)
    return 0.5 * x * (1.0 + jnp.tanh(c * (x + 0.044715 * (x * x * x))))


def _rms(x, g):
    return x * lax.rsqrt(jnp.mean(x * x, axis=-1, keepdims=True) + EPS) * g


class _Streams:
    def __init__(self, bp, lp, bs, ls):
        self.bp, self.lp, self.bs, self.ls = bp, lp, bs, ls
        self.tp, self.ts = bp * lp, bs * ls
        self.t = self.tp + self.ts

    def mod_row(self, i, tile):
        tiles_p = self.tp // tile
        per_seq = self.ls // tile
        return jnp.where(i < tiles_p, 0, 1 + (i - tiles_p) // per_seq)

    def halo_flags(self, i, tile):
        tiles_p = self.tp // tile
        per_p = self.lp // tile
        per_s = self.ls // tile
        in_p = i < tiles_p
        jp = i % per_p
        js = (i - tiles_p) % per_s
        has_prev = jnp.where(in_p, jp > 0, js > 0)
        has_next = jnp.where(in_p, jp < per_p - 1, js < per_s - 1)
        return has_prev, has_next


def _mod_kernel(c_ref, w_ref, b_ref, o_ref):
    s = _silu(c_ref[...])
    o_ref[0] = _hdot(s, w_ref[0]) + b_ref[0]


def _modulation(cvec, ada_w, ada_b):
    depth, d, d6 = ada_w.shape
    nj = d6 // d
    return pl.pallas_call(
        _mod_kernel,
        grid=(depth, nj),
        in_specs=[pl.BlockSpec((N_MOD_ROWS, d), lambda l, j: (0, 0)),
                  pl.BlockSpec((1, d, d), lambda l, j: (l, 0, j)),
                  pl.BlockSpec((1, 1, d), lambda l, j: (l, 0, j))],
        out_specs=pl.BlockSpec((1, N_MOD_ROWS, d), lambda l, j: (l, 0, j)),
        out_shape=jax.ShapeDtypeStruct((depth, N_MOD_ROWS, d6), F32),
        compiler_params=_cparams(("arbitrary", "arbitrary")),
        name="modulation",
    )(cvec, ada_w, ada_b.reshape(depth, 1, d6))


def _norm_proj_kernel(x_ref, mod_ref, g_ref, w_ref, *o_refs, widths):
    m = mod_ref[0]
    h = _rms(x_ref[...], g_ref[...]) * (1.0 + m[1:2]) + m[0:1]
    p = jnp.dot(h.astype(BF16), w_ref[...], preferred_element_type=F32)
    off = 0
    for o_ref, wd in zip(o_refs, widths):
        o_ref[...] = p[:, off:off + wd]
        off += wd


def _norm_proj(st, x, mod, g, w, widths, tm):
    t, d = x.shape
    n = w.shape[1]
    return pl.pallas_call(
        functools.partial(_norm_proj_kernel, widths=widths),
        grid=(t // tm,),
        in_specs=[pl.BlockSpec((tm, d), lambda i: (i, 0)),
                  pl.BlockSpec((1, 6, d), lambda i: (st.mod_row(i, tm), 0, 0)),
                  pl.BlockSpec((1, d), lambda i: (0, 0)),
                  pl.BlockSpec((d, n), lambda i: (0, 0))],
        out_specs=[pl.BlockSpec((tm, wd), lambda i: (i, 0)) for wd in widths],
        out_shape=[jax.ShapeDtypeStruct((t, wd), F32) for wd in widths],
        compiler_params=_cparams(("arbitrary",)),
        name="norm_proj",
    )(x, mod, g, w)


def _ret_kernel(q_ref, k_ref, v_ref, g_ref, s0_ref, gam_ref, ng_ref, cos_ref, sin_ref,
                ya_ref, sout_ref, qs_ref, ks_ref, o_ref, sf_ref, sb_ref, *, seq_len, latent):
    nc = seq_len // CHUNK
    q = q_ref[...]
    k = k_ref[...] * (DK_A ** -0.5)
    if latent:
        lane = lax.broadcasted_iota(jnp.int32, q.shape, 1)
        first_half = (lane % DK_A) < (DK_A // 2)
        cos = cos_ref[...]
        sin = sin_ref[...]

        def rope(x):
            partner = jnp.where(first_half, pltpu.roll(x, LANES - DK_A // 2, 1),
                                pltpu.roll(x, DK_A // 2, 1))
            return x * cos + partner * sin

        q = rope(q)
        k = rope(k)
    qs_ref[...] = q
    ks_ref[...] = k

    cat = jnp.concatenate
    kw = 2 * DK_A
    vw = 2 * DV_A
    row = lax.broadcasted_iota(jnp.int32, (CHUNK, kw), 0).astype(F32)
    lane_k = lax.broadcasted_iota(jnp.int32, (CHUNK, kw), 1)
    lane_v = lax.broadcasted_iota(jnp.int32, (CHUNK, vw), 1)
    head0_k = lane_k < DK_A
    head0_v = lane_v < DV_A
    lg = [[_log_sigmoid(gam_ref[d, hh]) for hh in range(2)] for d in range(2)]
    lgf_k = jnp.where(head0_k, lg[0][0], lg[0][1])
    lgb_k = jnp.where(head0_k, lg[1][0], lg[1][1])
    diff = row - (lane_k % DK_A).astype(F32)
    dcomb = (jnp.where(diff >= 0, jnp.exp(lgf_k * jnp.maximum(diff, 0.0)), 0.0)
             + jnp.where(diff <= 0, jnp.exp(lgb_k * jnp.maximum(-diff, 0.0)), 0.0))
    wend_f = jnp.exp(lgf_k * (CHUNK - 1.0 - row))
    wstart_f = jnp.exp(lgf_k * (row + 1.0))
    wend_b = jnp.exp(lgb_k * row)
    wstart_b = jnp.exp(lgb_k * (CHUNK - row))

    def state_layout(h0, h1):
        zero = jnp.zeros_like(h0)
        return cat([cat([h0, zero], axis=1), cat([zero, h1], axis=1)], axis=0)

    on_diag = state_layout(jnp.ones((DK_A, DV_A), F32), jnp.ones((DK_A, DV_A), F32)) > 0.0
    gch_f = jnp.exp(state_layout(lg[0][0], lg[0][1]) * float(CHUNK))
    gch_b = jnp.exp(state_layout(lg[1][0], lg[1][1]) * float(CHUNK))
    sf_ref[...] = state_layout(s0_ref[0, 0, 0], s0_ref[0, 0, 1])
    sb_ref[...] = state_layout(s0_ref[0, 1, 0], s0_ref[0, 1, 1])
    o_ref[...] = jnp.zeros_like(o_ref)

    def step(i, carry):
        rows = pl.ds(pl.multiple_of(i * CHUNK, CHUNK), CHUNK)
        qn = qs_ref[rows, :]
        kn = ks_ref[rows, :]
        vn = v_ref[rows, :]
        k_bd = cat([jnp.where(head0_k, kn, 0.0), jnp.where(head0_k, 0.0, kn)], axis=0)
        v_bd = cat([jnp.where(head0_v, vn, 0.0), jnp.where(head0_v, 0.0, vn)], axis=0)
        a = _bdot_nt(qn, k_bd) * dcomb
        s_f = sf_ref[...]
        o_ref[rows, :] += _bdot(cat([a, qn * wstart_f], axis=1), cat([v_bd, s_f], axis=0))
        sf_ref[...] = s_f * gch_f + jnp.where(on_diag, _bdot_tn(kn * wend_f, vn), 0.0)
        rows = pl.ds(pl.multiple_of((nc - 1 - i) * CHUNK, CHUNK), CHUNK)
        qn = qs_ref[rows, :]
        kn = ks_ref[rows, :]
        vn = v_ref[rows, :]
        s_b = sb_ref[...]
        o_ref[rows, :] += _bdot(qn * wstart_b, s_b)
        sb_ref[...] = s_b * gch_b + jnp.where(on_diag, _bdot_tn(kn * wend_b, vn), 0.0)
        return carry

    lax.fori_loop(0, nc, step, 0)
    for d, s_ref in enumerate((sf_ref, sb_ref)):
        sout_ref[0, d, 0] = s_ref[:DK_A, :DV_A]
        sout_ref[0, d, 1] = s_ref[DK_A:, DV_A:]

    for hh in range(2):
        vsl = slice(hh * DV_A, (hh + 1) * DV_A)
        ya_ref[:, vsl] = _silu(g_ref[:, vsl]) * _rms(o_ref[:, vsl], ng_ref[hh])


def _retention(q, k, v, g, s0, gam, ng, cos_t, sin_t, nseq, seq_len, row_off, latent):
    w2 = 2 * DK_A
    v2 = 2 * DV_A
    return pl.pallas_call(
        functools.partial(_ret_kernel, seq_len=seq_len, latent=latent),
        grid=(nseq, H_A // 2),
        in_specs=[pl.BlockSpec((seq_len, w2), lambda b, p: (b + row_off, p)),
                  pl.BlockSpec((seq_len, w2), lambda b, p: (b + row_off, p)),
                  pl.BlockSpec((seq_len, v2), lambda b, p: (b + row_off, p)),
                  pl.BlockSpec((seq_len, v2), lambda b, p: (b + row_off, p)),
                  pl.BlockSpec((1, 2, 2, DK_A, DV_A), lambda b, p: (b, 0, p, 0, 0)),
                  pl.BlockSpec((2, 2, CHUNK, DV_A), lambda b, p: (0, p, 0, 0)),
                  pl.BlockSpec((2, 1, DV_A), lambda b, p: (p, 0, 0)),
                  pl.BlockSpec((seq_len, w2), lambda b, p: (0, 0)),
                  pl.BlockSpec((seq_len, w2), lambda b, p: (0, 0))],
        out_specs=[pl.BlockSpec((seq_len, v2), lambda b, p: (b, p)),
                   pl.BlockSpec((1, 2, 2, DK_A, DV_A), lambda b, p: (b, 0, p, 0, 0))],
        out_shape=[jax.ShapeDtypeStruct((nseq * seq_len, H_A * DV_A), F32),
                   jax.ShapeDtypeStruct((nseq, 2, H_A, DK_A, DV_A), F32)],
        scratch_shapes=[pltpu.VMEM((seq_len, w2), F32), pltpu.VMEM((seq_len, w2), F32),
                        pltpu.VMEM((seq_len, v2), F32), pltpu.VMEM((w2, v2), F32), pltpu.VMEM((w2, v2), F32)],
        compiler_params=_cparams(("arbitrary", "arbitrary")),
        name="retention",
    )(q, k, v, g, s0, gam, ng, cos_t, sin_t)


def _fill_padded(pad_ref, cur, prev, nxt, has_prev, has_next, halo):
    tile = cur.shape[0]
    pad_ref[0:halo, :] = jnp.where(has_prev, prev, 0.0)
    pad_ref[halo:halo + tile, :] = cur
    pad_ref[halo + tile:halo + tile + halo, :] = jnp.where(has_next, nxt, 0.0)


def _conv_taps(pad_ref, w_ref, ntaps, halo, tile):
    base = halo - ntaps // 2
    acc = w_ref[0:1, :] * pad_ref[base:base + tile, :]
    for kk in range(1, ntaps):
        acc = acc + w_ref[kk:kk + 1, :] * pad_ref[base + kk:base + kk + tile, :]
    return acc


def _halo_specs(width, col_map, halo):
    per = CONV_TILE // halo

    def cur(i, j):
        return (i, col_map(j))

    def prev(i, j):
        return (jnp.maximum(i * per - 1, 0), col_map(j))

    def make_next(nblk):
        def nxt(i, j):
            return (jnp.minimum((i + 1) * per, nblk - 1), col_map(j))
        return nxt

    return cur, prev, make_next


def _sconv_kernel(bg_ref, cg_ref, cgp_ref, cgn_ref, hb_ref, hbp_ref, hbn_ref, w_ref, o_ref, pad_ref, *, st):
    has_prev, has_next = st.halo_flags(pl.program_id(0), CONV_TILE)
    _fill_padded(pad_ref, cg_ref[...] * hb_ref[...], cgp_ref[...] * hbp_ref[...],
                 cgn_ref[...] * hbn_ref[...], has_prev, has_next, SUBLANES)
    o_ref[...] = bg_ref[...] * _conv_taps(pad_ref, w_ref, SC_K, SUBLANES, CONV_TILE)


CONV_COLS = 512


def _short_gated_conv(st, bg, cg, hb, w):
    t, c = bg.shape
    halo = SUBLANES
    cur, prev, make_next = _halo_specs(c, lambda j: j, halo)
    nxt = make_next(t // halo)
    tile_spec = pl.BlockSpec((CONV_TILE, CONV_COLS), cur)
    prev_spec = pl.BlockSpec((halo, CONV_COLS), prev)
    next_spec = pl.BlockSpec((halo, CONV_COLS), nxt)
    return pl.pallas_call(
        functools.partial(_sconv_kernel, st=st),
        grid=(t // CONV_TILE, c // CONV_COLS),
        in_specs=[tile_spec, tile_spec, prev_spec, next_spec, tile_spec, prev_spec, next_spec,
                  pl.BlockSpec((SUBLANES, CONV_COLS), lambda i, j: (0, j))],
        out_specs=tile_spec,
        out_shape=jax.ShapeDtypeStruct((t, c), F32),
        scratch_shapes=[pltpu.VMEM((CONV_TILE + 2 * halo, CONV_COLS), F32)],
        compiler_params=_cparams(("arbitrary", "arbitrary")),
        name="short_gated_conv",
    )(bg, cg, cg, cg, hb, hb, hb, w)


def _qkv_conv_kernel(x_ref, xp_ref, xn_ref, w_ref, o_ref, pad_ref, *, st):
    has_prev, has_next = st.halo_flags(pl.program_id(0), CONV_TILE)
    _fill_padded(pad_ref, x_ref[...], xp_ref[...], xn_ref[...], has_prev, has_next, SUBLANES)
    s = _silu(_conv_taps(pad_ref, w_ref, QKV_K, SUBLANES, CONV_TILE))
    j = pl.program_id(1)
    for hh in range(CONV_COLS // DK_C):
        hsl = slice(hh * DK_C, (hh + 1) * DK_C)
        sh = s[:, hsl]
        nrm = sh * lax.rsqrt(jnp.sum(sh * sh, axis=-1, keepdims=True) + EPS)
        o_ref[:, hsl] = jnp.where(j == 0, nrm * (DK_C ** -0.5), jnp.where(j == 1, nrm, sh))


def _qkv_conv(st, qkv, w):
    t, c = qkv.shape
    assert CONV_COLS == H_C * DK_C == H_C * DV_C
    halo = SUBLANES
    cur, prev, make_next = _halo_specs(c, lambda j: j, halo)
    nxt = make_next(t // halo)
    tile_spec = pl.BlockSpec((CONV_TILE, CONV_COLS), cur)
    return pl.pallas_call(
        functools.partial(_qkv_conv_kernel, st=st),
        grid=(t // CONV_TILE, c // CONV_COLS),
        in_specs=[tile_spec, pl.BlockSpec((halo, CONV_COLS), prev), pl.BlockSpec((halo, CONV_COLS), nxt),
                  pl.BlockSpec((SUBLANES, CONV_COLS), lambda i, j: (0, j))],
        out_specs=tile_spec,
        out_shape=jax.ShapeDtypeStruct((t, c), F32),
        scratch_shapes=[pltpu.VMEM((CONV_TILE + 2 * halo, CONV_COLS), F32)],
        compiler_params=_cparams(("arbitrary", "arbitrary")),
        name="qkv_conv",
    )(qkv, qkv, qkv, w)


CF_HALO = 16


def _conformer_kernel(ca_ref, cap_ref, can_ref, cg_ref, cgp_ref, cgn_ref, w_ref, b_ref, lg_ref, lb_ref,
                      o_ref, pad_ref, *, st):
    has_prev, has_next = st.halo_flags(pl.program_id(0), CONV_TILE)
    _fill_padded(pad_ref, ca_ref[...] * _sigmoid(cg_ref[...]), cap_ref[...] * _sigmoid(cgp_ref[...]),
                 can_ref[...] * _sigmoid(cgn_ref[...]), has_prev, has_next, CF_HALO)
    hc = _conv_taps(pad_ref, w_ref, CF_K, CF_HALO, CONV_TILE) + b_ref[...]
    mu = jnp.mean(hc, axis=-1, keepdims=True)
    xc = hc - mu
    y = xc * lax.rsqrt(jnp.mean(xc * xc, axis=-1, keepdims=True) + EPS) * lg_ref[...] + lb_ref[...]
    o_ref[...] = _silu(y)


def _conformer(st, glu, w, b, ln_g, ln_b):
    t, c2 = glu.shape
    c = c2 // 2
    nblk = t // CF_HALO
    per = CONV_TILE // CF_HALO
    vec = pl.BlockSpec((1, c), lambda i: (0, 0))
    return pl.pallas_call(
        functools.partial(_conformer_kernel, st=st),
        grid=(t // CONV_TILE,),
        in_specs=[pl.BlockSpec((CONV_TILE, c), lambda i: (i, 0)),
                  pl.BlockSpec((CF_HALO, c), lambda i: (jnp.maximum(i * per - 1, 0), 0)),
                  pl.BlockSpec((CF_HALO, c), lambda i: (jnp.minimum((i + 1) * per, nblk - 1), 0)),
                  pl.BlockSpec((CONV_TILE, c), lambda i: (i, 1)),
                  pl.BlockSpec((CF_HALO, c), lambda i: (jnp.maximum(i * per - 1, 0), 1)),
                  pl.BlockSpec((CF_HALO, c), lambda i: (jnp.minimum((i + 1) * per, nblk - 1), 1)),
                  pl.BlockSpec((2 * CF_HALO, c), lambda i: (0, 0)), vec, vec, vec],
        out_specs=pl.BlockSpec((CONV_TILE, c), lambda i: (i, 0)),
        out_shape=jax.ShapeDtypeStruct((t, c), F32),
        scratch_shapes=[pltpu.VMEM((CONV_TILE + 2 * CF_HALO, c), F32)],
        compiler_params=_cparams(("arbitrary",)),
        name="conformer_conv",
    )(glu, glu, glu, glu, glu, glu, w, b, ln_g, ln_b)


GDN_PACK = 4
GDN_GROUPS = 4


def _split_bf16(x):
    hi = x.astype(BF16)
    return hi, (x - hi.astype(F32)).astype(BF16)


def _packed_unit_inverses(groups):
    dot = functools.partial(jnp.dot, preferred_element_type=F32)
    cat = jnp.concatenate
    n = len(groups[0])
    c = CHUNK
    w = n * c
    eye = (lax.broadcasted_iota(jnp.int32, (c, c), 0) == lax.broadcasted_iota(jnp.int32, (c, c), 1)).astype(F32)
    lane_block = lax.broadcasted_iota(jnp.int32, (c, 2 * w), 1) % w // c

    def rhs_of(p):
        hi, lo = _split_bf16(p)
        both = cat([hi, lo], axis=1)
        return hi, lo, cat([jnp.where(lane_block == i, both, jnp.zeros_like(both)) for i in range(n)], axis=0)

    def product(m, r):
        return (r[:m, :w] + r[m:, :w]) + (r[:m, w:] + r[m:, w:])

    ps = [cat(a_list, axis=1) for a_list in groups]
    ts = [cat([eye - a for a in a_list], axis=1) for a_list in groups]
    for g, p in enumerate(ps):
        hi, lo, rhs = rhs_of(p)
        ps[g] = product(c, dot(cat([hi, lo], axis=0), rhs))
    steps = int(math.log2(c)) - 1
    for step in range(steps):
        for g in range(len(groups)):
            p_hi, p_lo, rhs = rhs_of(ps[g])
            t_hi, t_lo = _split_bf16(ts[g])
            if step < steps - 1:
                both = product(2 * c, dot(cat([t_hi, p_hi, t_lo, p_lo], axis=0), rhs))
                ts[g] = ts[g] + both[:c]
                ps[g] = both[c:]
            else:
                ts[g] = ts[g] + product(c, dot(cat([t_hi, t_lo], axis=0), rhs))
    return [[t[:, i * c:(i + 1) * c] for i in range(n)] for t in ts]


def _chunk_cumsum(x, reverse):
    row = lax.broadcasted_iota(jnp.int32, x.shape, 0)
    s = 1
    while s < CHUNK:
        if reverse:
            x = x + jnp.where(row < CHUNK - s, pltpu.roll(x, CHUNK - s, 0), 0.0)
        else:
            x = x + jnp.where(row >= s, pltpu.roll(x, s, 0), 0.0)
        s *= 2
    return x


def _gdn_kernel(q_ref, k_ref, v_ref, z_ref, ab_ref, alog_ref, dtb_ref, s0_ref, ng_ref,
                y_ref, sout_ref, g_ref, beta_ref, u_ref, wq_ref, ak_ref, egl_ref, o_ref, *, seq_len):
    nc = seq_len // CHUNK
    h = pl.program_id(1)
    row = lax.broadcasted_iota(jnp.int32, (CHUNK, CHUNK), 0)
    col = lax.broadcasted_iota(jnp.int32, (CHUNK, CHUNK), 1)
    lane = lax.broadcasted_iota(jnp.int32, (CHUNK, LANES), 1)
    alog = alog_ref[...]
    dtb = dtb_ref[...]

    def gates(n, carry):
        rows = pl.ds(pl.multiple_of(n * CHUNK, CHUNK), CHUNK)
        ab = ab_ref[rows, :]
        g_all = -jnp.exp(alog) * _softplus(ab + dtb)
        b_all = _sigmoid(ab)
        for d in range(2):
            gsel = jnp.sum(jnp.where(lane == d * H_C + h, g_all, 0.0), axis=-1, keepdims=True)
            bsel = jnp.sum(jnp.where(lane == 2 * H_C + d * H_C + h, b_all, 0.0), axis=-1, keepdims=True)
            g_ref[d, rows, :] = jnp.broadcast_to(gsel, (CHUNK, LANES))
            beta_ref[d, rows, :] = jnp.broadcast_to(bsel, (CHUNK, LANES))
        return carry

    lax.fori_loop(0, nc, gates, 0)

    n_groups = min(GDN_GROUPS, nc // (GDN_PACK // 2))

    def precompute(it, carry):
        groups = [prepare_group(it * n_groups + grp) for grp in range(n_groups)]
        inverses = _packed_unit_inverses([[sysm[0] for sysm in systems] for systems in groups])
        for t_invs, systems in zip(inverses, groups):
            for t_inv, (_, rhs, d, n) in zip(t_invs, systems):
                uw = _bdot(t_inv, rhs)
                u_ref[d, pl.ds(pl.multiple_of(n * CHUNK, CHUNK), CHUNK), :] = uw[:, :DV_C]
                wq_ref[d, pl.ds(pl.multiple_of(n * 2 * CHUNK, CHUNK), CHUNK), :] = uw[:, DV_C:].astype(BF16)
        return carry

    def prepare_group(pair):
        systems = []
        for jc in range(GDN_PACK // 2):
            n = pair * (GDN_PACK // 2) + jc
            rows = pl.ds(pl.multiple_of(n * CHUNK, CHUNK), CHUNK)
            qc = q_ref[rows, :]
            kc = k_ref[rows, :]
            vc = v_ref[rows, :]
            gbs = [_chunk_cumsum(g_ref[d, rows, :], d == 1) for d in range(2)]
            bbs = [beta_ref[d, rows, :] for d in range(2)]
            kbs = [kc * bb for bb in bbs]
            prod = _bdot_nt(jnp.concatenate(kbs + [qc], axis=0), kc)
            for d in range(2):
                incl = (row >= col) if d == 0 else (row <= col)
                strict = (row > col) if d == 0 else (row < col)
                last = CHUNK - 1 if d == 0 else 0
                gb = gbs[d]
                gr = gb[:, :CHUNK].T
                gl = gb[last:last + 1, :]
                decay = jnp.where(incl, jnp.exp(jnp.where(incl, gb[:, :CHUNK] - gr, 0.0)), 0.0)
                eg = jnp.exp(gb)
                a_low = jnp.where(strict, prod[d * CHUNK:(d + 1) * CHUNK] * decay, 0.0)
                attn = jnp.where(incl, prod[2 * CHUNK:] * decay, 0.0)
                kg_t = (kc * jnp.exp(gl - gb)).T
                ak_ref[d, pl.ds(pl.multiple_of(n * 3 * CHUNK, CHUNK), 3 * CHUNK), :] = jnp.concatenate(
                    [attn, kg_t], axis=0).astype(BF16)
                wq_ref[d, pl.ds(pl.multiple_of(n * 2 * CHUNK + CHUNK, CHUNK), CHUNK), :] = (qc * eg).astype(BF16)
                egl_ref[d, pl.ds(n, 1), :] = jnp.exp(gl)
                systems.append((a_low, jnp.concatenate([vc * bbs[d], kbs[d] * eg], axis=1), d, n))
        return systems

    lax.fori_loop(0, nc // (n_groups * GDN_PACK // 2), precompute, 0)

    o_ref[...] = jnp.zeros_like(o_ref)

    def scan(i, carry):
        out = []
        for d, s in enumerate(carry):
            n = i if d == 0 else nc - 1 - i
            rows = pl.ds(pl.multiple_of(n * CHUNK, CHUNK), CHUNK)
            on_s = _bdot(wq_ref[d, pl.ds(pl.multiple_of(n * 2 * CHUNK, CHUNK), 2 * CHUNK), :], s)
            v_new = u_ref[d, rows, :] - on_s[:CHUNK]
            on_v = _bdot(ak_ref[d, pl.ds(pl.multiple_of(n * 3 * CHUNK, CHUNK), 3 * CHUNK), :], v_new)
            o_ref[rows, :] += on_s[CHUNK:] + on_v[:CHUNK]
            out.append(s * egl_ref[d, pl.ds(n, 1), :] + on_v[CHUNK:])
        return tuple(out)

    s_f, s_b = lax.fori_loop(0, nc, scan, (s0_ref[0, 0, 0], s0_ref[0, 1, 0]))
    sout_ref[0, 0, 0] = s_f
    sout_ref[0, 1, 0] = s_b
    y_ref[...] = _rms(o_ref[...], ng_ref[...]) * _silu(z_ref[...])


def _gdn(qkv_n, z, ab, alog, dtb, s0, ng, nseq, seq_len, row_off):
    blk = lambda off: pl.BlockSpec((seq_len, LANES), lambda b, h: (b + row_off, h + off))
    vec = pl.BlockSpec((1, LANES), lambda b, h: (0, 0))
    st_spec = pl.BlockSpec((1, 2, 1, DK_C, DV_C), lambda b, h: (b, 0, h, 0, 0))
    nc = seq_len // CHUNK
    both = lambda width, dtype: pltpu.VMEM((2, seq_len, width), dtype)
    return pl.pallas_call(
        functools.partial(_gdn_kernel, seq_len=seq_len),
        grid=(nseq, H_C),
        in_specs=[blk(0), blk(H_C), blk(2 * H_C), blk(0),
                  pl.BlockSpec((seq_len, LANES), lambda b, h: (b + row_off, 0)),
                  vec, vec, st_spec, vec],
        out_specs=[pl.BlockSpec((seq_len, LANES), lambda b, h: (b, h)), st_spec],
        out_shape=[jax.ShapeDtypeStruct((nseq * seq_len, H_C * DV_C), F32),
                   jax.ShapeDtypeStruct((nseq, 2, H_C, DK_C, DV_C), F32)],
        scratch_shapes=[both(LANES, F32), both(LANES, F32), both(DV_C, F32),
                        pltpu.VMEM((2, 2 * seq_len, DK_C), BF16), pltpu.VMEM((2, 3 * seq_len, CHUNK), BF16),
                        pltpu.VMEM((2, max(nc, SUBLANES), LANES), F32), pltpu.VMEM((seq_len, DV_C), F32)],
        compiler_params=_cparams(("arbitrary", "arbitrary")),
        name="gated_deltanet",
    )(qkv_n, qkv_n, qkv_n, z, ab, alog, dtb, s0, ng)


def _out_proj_kernel(ya_ref, yb_ref, x_ref, mod_ref, g_ref, wo_ref, wq_ref, x1_ref, h2_ref, qp_ref):
    half = ya_ref.shape[1]
    m = mod_ref[0]
    y = (jnp.dot(ya_ref[...].astype(BF16), wo_ref[0:half, :], preferred_element_type=F32)
         + jnp.dot(yb_ref[...].astype(BF16), wo_ref[half:, :], preferred_element_type=F32))
    x1 = x_ref[...] + m[2:3] * y
    x1_ref[...] = x1
    h2 = (_rms(x1, g_ref[...]) * (1.0 + m[4:5]) + m[3:4]).astype(BF16)
    h2_ref[...] = h2
    qp_ref[...] = jnp.dot(h2, wq_ref[...], preferred_element_type=F32)


def _out_proj(st, ya, yb, x, mod, g, wo, wq, tm):
    t, d = x.shape
    half = ya.shape[1]
    nq = wq.shape[1]
    return pl.pallas_call(
        _out_proj_kernel,
        grid=(t // tm,),
        in_specs=[pl.BlockSpec((tm, half), lambda i: (i, 0)),
                  pl.BlockSpec((tm, half), lambda i: (i, 0)),
                  pl.BlockSpec((tm, d), lambda i: (i, 0)),
                  pl.BlockSpec((1, 6, d), lambda i: (st.mod_row(i, tm), 0, 0)),
                  pl.BlockSpec((1, d), lambda i: (0, 0)),
                  pl.BlockSpec((2 * half, d), lambda i: (0, 0)),
                  pl.BlockSpec((d, nq), lambda i: (0, 0))],
        out_specs=[pl.BlockSpec((tm, d), lambda i: (i, 0)),
                   pl.BlockSpec((tm, d), lambda i: (i, 0)),
                   pl.BlockSpec((tm, nq), lambda i: (i, 0))],
        out_shape=[jax.ShapeDtypeStruct((t, d), F32), jax.ShapeDtypeStruct((t, d), BF16),
                   jax.ShapeDtypeStruct((t, nq), F32)],
        compiler_params=_cparams(("arbitrary",)),
        name="out_proj",
    )(ya, yb, x, mod, g, wo, wq)


def _candidate_rows():
    groups = ([(0, k2) for k2 in range(16)], [(1, k2) for k2 in range(8)], [(2, k2) for k2 in range(8)],
              [(3, k2) for k2 in range(8)], [(k1, 0) for k1 in range(16)], [(k1, 1) for k1 in range(8)],
              [(k1, 2) for k1 in range(8)])
    rows, seen = [], set()
    for grp in groups:
        for k1, k2 in grp:
            ok = (k1 + 1) * (k2 + 1) <= TOPK and (k1, k2) not in seen
            if ok:
                seen.add((k1, k2))
            rows.append(float(k1 * TOPK + k2) if ok else CAND_INVALID)
    return rows


CAND_INVALID = float(TOPK * TOPK)
RANK_MARK = 2.0 ** 100
RANK_STEP = 2.0 ** 95
SELECT_STRIPS = 4


def _pack_candidates(a1, a2, op):
    return jnp.concatenate([op(a1[0:1], a2), op(a1[1:2], a2[0:8]), op(a1[2:3], a2[0:8]),
                            op(a1[3:4], a2[0:8]), op(a1, a2[0:1]), op(a1[0:8], a2[1:2]),
                            op(a1[0:8], a2[2:3])], axis=0)


def _top16(s, vals_ref, exact):
    n = s.shape[0]
    if not exact:
        for kk in range(TOPK):
            m = jnp.max(s, axis=0, keepdims=True)
            vals_ref[kk:kk + 1, :] = m
            s = jnp.where(s == m, -(RANK_MARK + kk * RANK_STEP), s)
        return jnp.where(s < -0.5 * RANK_MARK, (-s - RANK_MARK) * (1.0 / RANK_STEP), float(TOPK))
    iota = lax.broadcasted_iota(jnp.int32, s.shape, 0).astype(F32)
    rank = jnp.full(s.shape, float(TOPK), F32)
    for kk in range(TOPK):
        m = jnp.max(s, axis=0, keepdims=True)
        idx = jnp.min(jnp.where(s == m, iota, float(n)), axis=0, keepdims=True)
        hit = iota == idx
        rank = jnp.where(hit, float(kk), rank)
        vals_ref[kk:kk + 1, :] = m
        s = jnp.where(hit, NEG_INF, s)
    return rank


def _peer_select_kernel(q_ref, keys_ref, cflat_ref, e1_ref, cnt_ref, e2_ref, r2_ref, v1_ref, v2_ref, *, tt):
    nk = N_KEYS
    cflat = cflat_ref[...]
    valid = cflat < CAND_INVALID
    row16 = lax.broadcasted_iota(jnp.int32, (TOPK, LANES), 0)

    def compute(strip, exact):
        s1, s2, tsl, va_ref, vb_ref = strip
        rank1 = _top16(s1, va_ref, exact)
        rank2 = _top16(s2, vb_ref, exact)
        v1 = va_ref[...]
        v2 = vb_ref[...]
        cand = jnp.where(valid, _pack_candidates(v1, v2, jnp.add), NEG_INF)
        prod = _pack_candidates(jnp.exp(v1 - v1[0:1]), jnp.exp(v2 - v2[0:1]), jnp.multiply)
        sel = jnp.zeros(cand.shape, F32)
        for _ in range(TOPK):
            m = jnp.max(cand, axis=0, keepdims=True)
            if exact:
                idx = jnp.min(jnp.where(cand == m, cflat, CAND_INVALID), axis=0, keepdims=True)
                hit = cflat == idx
            else:
                hit = cand == m
            sel = jnp.where(hit, 1.0, sel)
            cand = jnp.where(hit, NEG_INF, cand)
        zsum = jnp.sum(sel * prod, axis=0, keepdims=True)
        c_all = sel[40:56] + jnp.concatenate([sel[56:64] + sel[64:72], jnp.zeros((8, LANES), F32)], axis=0)
        for k1, (lo, hi) in enumerate(((0, 16), (16, 24), (24, 32), (32, 40))):
            c_all = jnp.where(row16 == k1, jnp.sum(sel[lo:hi], axis=0, keepdims=True), c_all)
        rank1_b = rank1.astype(BF16)
        c_all_b = c_all.astype(BF16)
        cnt = jnp.zeros(rank1.shape, BF16)
        for k1 in range(TOPK):
            cnt = jnp.where(rank1_b == float(k1), c_all_b[k1:k1 + 1], cnt)
        cnt = cnt.astype(F32)
        in1 = rank1 < float(TOPK)
        in2 = rank2 < float(TOPK)
        e1_ref[0, :, tsl] = jnp.where(in1, jnp.exp(s1 - v1[0:1]), 0.0) / zsum
        cnt_ref[0, :, tsl] = cnt
        e2_ref[:, tsl] = jnp.where(in2, jnp.exp(s2 - v2[0:1]), 0.0).astype(e2_ref.dtype)
        r2_ref[:, tsl] = rank2.astype(r2_ref.dtype)
        n1 = jnp.sum(jnp.where(in1, 1.0, 0.0), axis=0, keepdims=True)
        n2 = jnp.sum(jnp.where(in2, 1.0, 0.0), axis=0, keepdims=True)
        nc = jnp.sum(sel, axis=0, keepdims=True)
        want = float(TOPK)
        return jnp.abs(n1 - want) + jnp.abs(n2 - want) + jnp.abs(nc - want)

    def strip_group(si, carry):
        strips = []
        for sub in range(SELECT_STRIPS):
            t0 = pl.multiple_of((si * SELECT_STRIPS + sub) * LANES, LANES)
            tsl = pl.ds(t0, LANES)
            s1 = _bdot_nt(keys_ref[0, 0], q_ref[tsl, 0:nk])
            s2 = _bdot_nt(keys_ref[0, 1], q_ref[tsl, nk:2 * nk])
            strips.append((s1, s2, tsl, v1_ref.at[sub], v2_ref.at[sub]))
        tied = [compute(strip, False) for strip in strips]
        for strip, bad in zip(strips, tied):
            @pl.when(jnp.max(bad) > 0.0)
            def _(strip=strip):
                compute(strip, True)
        return carry

    lax.fori_loop(0, tt // (SELECT_STRIPS * LANES), strip_group, 0)


def _peer_select(qp, keys, tt):
    t = qp.shape[0]
    rows = _candidate_rows()
    cflat = jnp.broadcast_to(jnp.asarray(rows, F32)[:, None], (len(rows), LANES))
    out_spec = pl.BlockSpec((1, N_KEYS, tt), lambda i, h: (h, 0, i))
    f32_sds = jax.ShapeDtypeStruct((H_P, N_KEYS, t), F32)
    flat_spec = pl.BlockSpec((N_KEYS, tt), lambda i, h: (h, i))
    flat_sds = jax.ShapeDtypeStruct((H_P * N_KEYS, t), BF16)
    return pl.pallas_call(
        functools.partial(_peer_select_kernel, tt=tt),
        grid=(t // tt, H_P),
        in_specs=[pl.BlockSpec((tt, 2 * N_KEYS), lambda i, h: (i, h)),
                  pl.BlockSpec((1, 2, N_KEYS, N_KEYS), lambda i, h: (h, 0, 0, 0)),
                  pl.BlockSpec(cflat.shape, lambda i, h: (0, 0))],
        out_specs=[out_spec, out_spec, flat_spec, flat_spec],
        out_shape=[f32_sds, f32_sds, flat_sds, flat_sds],
        scratch_shapes=[pltpu.VMEM((SELECT_STRIPS, TOPK, LANES), F32),
                        pltpu.VMEM((SELECT_STRIPS, TOPK, LANES), F32)],
        compiler_params=_cparams(("arbitrary", "arbitrary")),
        name="peer_select",
    )(qp, keys, cflat)


def _peer_dense_kernel(h_ref, u_ref, vt_ref, e1_ref, cnt_ref, e2_ref, r2_ref, x_ref, mod_ref, o_ref,
                       acc_ref, s_ref, p_ref, e2b_ref, r2b_ref, ht_ref, *, na):
    j = pl.program_id(1)
    tt = s_ref.shape[1]

    @pl.when(j == 0)
    def _():
        acc_ref[...] = jnp.zeros_like(acc_ref)
        e2b_ref[...] = e2_ref[...].astype(BF16)
        r2b_ref[...] = r2_ref[...].astype(BF16)
        ht_ref[...] = h_ref[...].astype(F32).T.astype(BF16)

    s_ref[...] = jnp.dot(u_ref[...], ht_ref[...], preferred_element_type=F32)

    for aa in range(na):
        rows = slice(aa * N_KEYS, (aa + 1) * N_KEYS)
        for tg in range(tt // LANES):
            tsl = slice(tg * LANES, (tg + 1) * LANES)
            gate = None
            for hh in range(H_P):
                cnt_row = cnt_ref[hh, aa:aa + 1, tsl].astype(BF16)
                e1_row = e1_ref[hh, aa:aa + 1, tsl].astype(BF16)
                hsl = slice(hh * N_KEYS, (hh + 1) * N_KEYS)
                term = jnp.where(r2b_ref[hsl, tsl] < cnt_row, e2b_ref[hsl, tsl], 0.0) * e1_row
                gate = term if gate is None else gate + term
            p_ref[rows, tsl] = gate * _gelu_tanh(s_ref[rows, tsl].astype(BF16))

    acc_ref[...] += jnp.dot(vt_ref[...], p_ref[...], preferred_element_type=F32)

    @pl.when(j == pl.num_programs(1) - 1)
    def _():
        o_ref[...] = x_ref[...] + mod_ref[0][5:6] * acc_ref[...].T


def _peer_dense(st, h2, u, vt, e1, cnt, e2, r2, x, mod, tt, et):
    t, d = x.shape
    n_exp = u.shape[0]
    na = et // N_KEYS
    row_spec = pl.BlockSpec((H_P, na, tt), lambda i, j: (0, j, i))
    full_spec = pl.BlockSpec((H_P * N_KEYS, tt), lambda i, j: (0, i))
    return pl.pallas_call(
        functools.partial(_peer_dense_kernel, na=na),
        grid=(t // tt, n_exp // et),
        in_specs=[pl.BlockSpec((tt, d), lambda i, j: (i, 0)),
                  pl.BlockSpec((et, d), lambda i, j: (j, 0)),
                  pl.BlockSpec((d, et), lambda i, j: (0, j)),
                  row_spec, row_spec, full_spec, full_spec,
                  pl.BlockSpec((tt, d), lambda i, j: (i, 0)),
                  pl.BlockSpec((1, 6, d), lambda i, j: (st.mod_row(i, tt), 0, 0))],
        out_specs=pl.BlockSpec((tt, d), lambda i, j: (i, 0)),
        out_shape=jax.ShapeDtypeStruct((t, d), F32),
        scratch_shapes=[pltpu.VMEM((d, tt), F32), pltpu.VMEM((et, tt), F32), pltpu.VMEM((et, tt), BF16),
                        pltpu.VMEM((H_P * N_KEYS, tt), BF16), pltpu.VMEM((H_P * N_KEYS, tt), BF16),
                        pltpu.VMEM((d, tt), BF16)],
        compiler_params=_cparams(("arbitrary", "arbitrary")),
        name="peer_dense",
    )(h2, u, vt, e1, cnt, e2, r2, x, mod)


def _final_norm_kernel(x_ref, g_ref, o_ref):
    o_ref[...] = _rms(x_ref[...], g_ref[...])


def _final_norm(x, g, tm):
    t, d = x.shape
    return pl.pallas_call(
        _final_norm_kernel,
        grid=(t // tm,),
        in_specs=[pl.BlockSpec((tm, d), lambda i: (i, 0)), pl.BlockSpec((1, d), lambda i: (0, 0))],
        out_specs=pl.BlockSpec((tm, d), lambda i: (i, 0)),
        out_shape=jax.ShapeDtypeStruct((t, d), F32),
        compiler_params=_cparams(("arbitrary",)),
        name="final_norm",
    )(x, g)


def _rope_tables(seq_len):
    pos = jnp.arange(seq_len)
    rowp = (pos // GRID_W).astype(F32)
    colp = (pos % GRID_W).astype(F32)
    nf = DK_A // 4
    freqs = ROPE_BASE ** (-jnp.arange(nf, dtype=F32) / nf)
    ang = jnp.concatenate([rowp[:, None] * freqs, colp[:, None] * freqs], axis=-1)
    cos = jnp.cos(ang)
    sin = jnp.sin(ang)
    cos_t = jnp.tile(cos, (1, 4))
    sin_t = jnp.tile(jnp.concatenate([-sin, sin], axis=-1), (1, 2))
    return cos_t, sin_t


def _pad_rows(w, rows):
    return jnp.concatenate([w, jnp.zeros((rows - w.shape[0], w.shape[1]), w.dtype)], axis=0)


def _lane_row(vals):
    flat = vals.reshape(-1).astype(F32)
    return jnp.concatenate([flat, jnp.zeros((LANES - flat.shape[0],), F32)])[None, :]


def kernel(x_prompt, x_sample, state_ret, state_gdn, c, c_ctx, ada_w, ada_b, norm_mix_g, norm_ffn_g,
           final_norm_g, ev_w_in, ev_w_out, ret_gamma_logit, ret_norm_g, sc_conv_w, od_w_in, od_w_out,
           gdn_conv_w, gdn_a_log, gdn_dt_bias, gdn_norm_g, cf_dw_w, cf_dw_b, cf_ln_g, cf_ln_b,
           peer_wq, peer_keys, peer_u, peer_v):
    bp, lp, d = x_prompt.shape
    bs, ls, _ = x_sample.shape
    depth = ada_w.shape[0]
    st = _Streams(bp, lp, bs, ls)
    w_a = H_A * DV_A
    w_b = d - w_a
    w_c = H_C * DV_C
    w_d = d - w_c
    tm = TOKEN_TILE
    select_tt = 2 * TOKEN_TILE

    x = jnp.concatenate([x_prompt.reshape(st.tp, d), x_sample.reshape(st.ts, d)], axis=0)
    cvec = jnp.concatenate([c_ctx[None, :], c, jnp.zeros((N_MOD_ROWS - 1 - bs, d), F32)], axis=0)
    mods = _modulation(cvec, ada_w, ada_b).reshape(depth, N_MOD_ROWS, 6, d)
    cos_t, sin_t = _rope_tables(ls)
    zero_ret = jnp.zeros((bp, 2, H_A, DK_A, DV_A), F32)
    zero_gdn = jnp.zeros((bp, 2, H_C, DK_C, DV_C), F32)

    ret_new, gdn_new = [], []
    for l in range(depth):
        i = l // 2
        mod = mods[l]
        g1 = norm_mix_g[l][None, :]
        if l % 2 == 0:
            widths = (H_A * DK_A, H_A * DK_A, w_a, w_a, w_b, w_b, w_b)
            q, k, v, g, bg, cg, hb = _norm_proj(st, x, mod, g1, ev_w_in[i].astype(BF16), widths, tm)
            gam = jnp.broadcast_to(ret_gamma_logit[i][:, :, None, None], (2, H_A, CHUNK, DV_A))
            ng = ret_norm_g[i][:, None, :]
            ya_p, s_new = _retention(q, k, v, g, zero_ret, gam, ng, cos_t[:lp], sin_t[:lp],
                                     bp, lp, 0, False)
            ya_s, _ = _retention(q, k, v, g, state_ret[:, i], gam, ng, cos_t, sin_t,
                                 bs, ls, st.tp // ls, True)
            ret_new.append(s_new)
            ya = jnp.concatenate([ya_p, ya_s], axis=0)
            yb = _short_gated_conv(st, bg, cg, hb, _pad_rows(sc_conv_w[i], SUBLANES))
            w_out = ev_w_out[i]
        else:
            n_gate = 2 * 2 * H_C
            w_in = od_w_in[i]
            o_ab = 4 * w_c
            w_main = jnp.concatenate([w_in[:, :o_ab], w_in[:, o_ab + n_gate:],
                                      w_in[:, o_ab:o_ab + n_gate],
                                      jnp.zeros((d, LANES - n_gate), F32)], axis=1).astype(BF16)
            widths = (3 * w_c, w_c, 2 * w_d, LANES)
            qkv, z, glu, ab = _norm_proj(st, x, mod, g1, w_main, widths, tm)
            qkv_n = _qkv_conv(st, qkv, _pad_rows(gdn_conv_w[i], SUBLANES))
            alog = _lane_row(gdn_a_log[i])
            dtb = _lane_row(gdn_dt_bias[i])
            ng = gdn_norm_g[i][None, :]
            yc_p, s_new = _gdn(qkv_n, z, ab, alog, dtb, zero_gdn, ng, bp, lp, 0)
            yc_s, _ = _gdn(qkv_n, z, ab, alog, dtb, state_gdn[:, i], ng, bs, ls, st.tp // ls)
            gdn_new.append(s_new)
            ya = jnp.concatenate([yc_p, yc_s], axis=0)
            yb = _conformer(st, glu, _pad_rows(cf_dw_w[i], 2 * CF_HALO), cf_dw_b[i][None, :],
                            cf_ln_g[i][None, :], cf_ln_b[i][None, :])
            w_out = od_w_out[i]
        x, h2, qp = _out_proj(st, ya, yb, x, mod, norm_ffn_g[l][None, :], w_out.astype(BF16),
                              peer_wq[l].astype(BF16), tm)
        e1, cnt, e2, r2 = _peer_select(qp, peer_keys[l].astype(BF16), select_tt)
        x = _peer_dense(st, h2, peer_u[l].astype(BF16), peer_v[l].astype(BF16).T, e1, cnt, e2, r2,
                        x, mod, PEER_TOKEN_TILE, PEER_EXPERT_TILE)

    y = _final_norm(x, final_norm_g[None, :], tm)
    y_prompt = y[:st.tp].reshape(bp, lp, d)
    y_sample = y[st.tp:].reshape(bs, ls, d)
    new_state_ret = jnp.stack(ret_new, axis=1).astype(x_prompt.dtype)
    new_state_gdn = jnp.stack(gdn_new, axis=1).astype(x_prompt.dtype)
    return (y_prompt, y_sample, new_state_ret, new_state_gdn)
```

```python
import functools
import math

import jax
import jax.numpy as jnp
from jax import lax
from jax.experimental import pallas as pl
from jax.experimental.pallas import tpu as pltpu

F32 = jnp.float32
BF16 = jnp.bfloat16
HIGHEST = lax.Precision.HIGHEST

EPS = 1e-6
CHUNK = 64
GRID_W = 64
ROPE_BASE = 10000.0
H_A, DK_A, DV_A = 4, 64, 128
H_C, DK_C, DV_C = 4, 128, 128
SC_K, QKV_K, CF_K = 3, 3, 31
N_KEYS, H_P, TOPK = 128, 8, 16
N_MOD_ROWS = 16
LANES = 128
SUBLANES = 8
VMEM_LIMIT = 56 * 1024 * 1024
CONV_TILE = 256
TOKEN_TILE = 512
PEER_TOKEN_TILE = 512
PEER_EXPERT_TILE = 2048
NEG_INF = float("-inf")


def _cparams(sem):
    return pltpu.CompilerParams(dimension_semantics=sem, vmem_limit_bytes=VMEM_LIMIT)


def _bdot(a, b):
    return jnp.dot(a.astype(BF16), b.astype(BF16), preferred_element_type=F32)


def _bdot_nt(a, b):
    return lax.dot_general(a.astype(BF16), b.astype(BF16), (((1,), (1,)), ((), ())),
                           preferred_element_type=F32)


def _bdot_tn(a, b):
    return lax.dot_general(a.astype(BF16), b.astype(BF16), (((0,), (0,)), ((), ())),
                           preferred_element_type=F32)


def _hdot(a, b):
    return jnp.dot(a, b, precision=HIGHEST, preferred_element_type=F32)


def _sigmoid(x):
    return 1.0 / (1.0 + jnp.exp(-x))


def _silu(x):
    return x * _sigmoid(x)


def _softplus(x):
    return jnp.maximum(x, 0.0) + jnp.log1p(jnp.exp(-jnp.abs(x)))


def _log_sigmoid(x):
    return -_softplus(-x)


def _gelu_tanh(x):
    c = math.sqrt(2.0 / math.pi)
    return 0.5 * x * (1.0 + jnp.tanh(c * (x + 0.044715 * (x * x * x))))


def _rms(x, g):
    return x * lax.rsqrt(jnp.mean(x * x, axis=-1, keepdims=True) + EPS) * g


class _Streams:
    def __init__(self, bp, lp, bs, ls):
        self.bp, self.lp, self.bs, self.ls = bp, lp, bs, ls
        self.tp, self.ts = bp * lp, bs * ls
        self.t = self.tp + self.ts

    def mod_row(self, i, tile):
        tiles_p = self.tp // tile
        per_seq = self.ls // tile
        return jnp.where(i < tiles_p, 0, 1 + (i - tiles_p) // per_seq)

    def halo_flags(self, i, tile):
        tiles_p = self.tp // tile
        per_p = self.lp // tile
        per_s = self.ls // tile
        in_p = i < tiles_p
        jp = i % per_p
        js = (i - tiles_p) % per_s
        has_prev = jnp.where(in_p, jp > 0, js > 0)
        has_next = jnp.where(in_p, jp < per_p - 1, js < per_s - 1)
        return has_prev, has_next


def _mod_kernel(c_ref, w_ref, b_ref, o_ref):
    s = _silu(c_ref[...])
    o_ref[0] = _hdot(s, w_ref[0]) + b_ref[0]


def _modulation(cvec, ada_w, ada_b):
    depth, d, d6 = ada_w.shape
    nj = d6 // d
    return pl.pallas_call(
        _mod_kernel,
        grid=(depth, nj),
        in_specs=[pl.BlockSpec((N_MOD_ROWS, d), lambda l, j: (0, 0)),
                  pl.BlockSpec((1, d, d), lambda l, j: (l, 0, j)),
                  pl.BlockSpec((1, 1, d), lambda l, j: (l, 0, j))],
        out_specs=pl.BlockSpec((1, N_MOD_ROWS, d), lambda l, j: (l, 0, j)),
        out_shape=jax.ShapeDtypeStruct((depth, N_MOD_ROWS, d6), F32),
        compiler_params=_cparams(("arbitrary", "arbitrary")),
        name="modulation",
    )(cvec, ada_w, ada_b.reshape(depth, 1, d6))


def _norm_proj_kernel(x_ref, mod_ref, g_ref, w_ref, *o_refs, widths):
    m = mod_ref[0]
    h = _rms(x_ref[...], g_ref[...]) * (1.0 + m[1:2]) + m[0:1]
    p = jnp.dot(h.astype(BF16), w_ref[...], preferred_element_type=F32)
    off = 0
    for o_ref, wd in zip(o_refs, widths):
        o_ref[...] = p[:, off:off + wd]
        off += wd


def _norm_proj(st, x, mod, g, w, widths, tm):
    t, d = x.shape
    n = w.shape[1]
    return pl.pallas_call(
        functools.partial(_norm_proj_kernel, widths=widths),
        grid=(t // tm,),
        in_specs=[pl.BlockSpec((tm, d), lambda i: (i, 0)),
                  pl.BlockSpec((1, 6, d), lambda i: (st.mod_row(i, tm), 0, 0)),
                  pl.BlockSpec((1, d), lambda i: (0, 0)),
                  pl.BlockSpec((d, n), lambda i: (0, 0))],
        out_specs=[pl.BlockSpec((tm, wd), lambda i: (i, 0)) for wd in widths],
        out_shape=[jax.ShapeDtypeStruct((t, wd), F32) for wd in widths],
        compiler_params=_cparams(("arbitrary",)),
        name="norm_proj",
    )(x, mod, g, w)


def _ret_kernel(q_ref, k_ref, v_ref, g_ref, s0_ref, gam_ref, ng_ref, cos_ref, sin_ref,
                ya_ref, sout_ref, qs_ref, ks_ref, o_ref, sf_ref, sb_ref, *, seq_len, latent):
    nc = seq_len // CHUNK
    q = q_ref[...]
    k = k_ref[...] * (DK_A ** -0.5)
    if latent:
        lane = lax.broadcasted_iota(jnp.int32, q.shape, 1)
        first_half = (lane % DK_A) < (DK_A // 2)
        cos = cos_ref[...]
        sin = sin_ref[...]

        def rope(x):
            partner = jnp.where(first_half, pltpu.roll(x, LANES - DK_A // 2, 1),
                                pltpu.roll(x, DK_A // 2, 1))
            return x * cos + partner * sin

        q = rope(q)
        k = rope(k)
    qs_ref[...] = q
    ks_ref[...] = k

    cat = jnp.concatenate
    kw = 2 * DK_A
    vw = 2 * DV_A
    row = lax.broadcasted_iota(jnp.int32, (CHUNK, kw), 0).astype(F32)
    lane_k = lax.broadcasted_iota(jnp.int32, (CHUNK, kw), 1)
    lane_v = lax.broadcasted_iota(jnp.int32, (CHUNK, vw), 1)
    head0_k = lane_k < DK_A
    head0_v = lane_v < DV_A
    lg = [[_log_sigmoid(gam_ref[d, hh]) for hh in range(2)] for d in range(2)]
    lgf_k = jnp.where(head0_k, lg[0][0], lg[0][1])
    lgb_k = jnp.where(head0_k, lg[1][0], lg[1][1])
    diff = row - (lane_k % DK_A).astype(F32)
    dcomb = (jnp.where(diff >= 0, jnp.exp(lgf_k * jnp.maximum(diff, 0.0)), 0.0)
             + jnp.where(diff <= 0, jnp.exp(lgb_k * jnp.maximum(-diff, 0.0)), 0.0))
    wend_f = jnp.exp(lgf_k * (CHUNK - 1.0 - row))
    wstart_f = jnp.exp(lgf_k * (row + 1.0))
    wend_b = jnp.exp(lgb_k * row)
    wstart_b = jnp.exp(lgb_k * (CHUNK - row))

    def state_layout(h0, h1):
        zero = jnp.zeros_like(h0)
        return cat([cat([h0, zero], axis=1), cat([zero, h1], axis=1)], axis=0)

    on_diag = state_layout(jnp.ones((DK_A, DV_A), F32), jnp.ones((DK_A, DV_A), F32)) > 0.0
    gch_f = jnp.exp(state_layout(lg[0][0], lg[0][1]) * float(CHUNK))
    gch_b = jnp.exp(state_layout(lg[1][0], lg[1][1]) * float(CHUNK))
    sf_ref[...] = state_layout(s0_ref[0, 0, 0], s0_ref[0, 0, 1])
    sb_ref[...] = state_layout(s0_ref[0, 1, 0], s0_ref[0, 1, 1])
    o_ref[...] = jnp.zeros_like(o_ref)

    def step(i, carry):
        rows = pl.ds(pl.multiple_of(i * CHUNK, CHUNK), CHUNK)
        qn = qs_ref[rows, :]
        kn = ks_ref[rows, :]
        vn = v_ref[rows, :]
        k_bd = cat([jnp.where(head0_k, kn, 0.0), jnp.where(head0_k, 0.0, kn)], axis=0)
        v_bd = cat([jnp.where(head0_v, vn, 0.0), jnp.where(head0_v, 0.0, vn)], axis=0)
        a = _bdot_nt(qn, k_bd) * dcomb
        s_f = sf_ref[...]
        o_ref[rows, :] += _bdot(cat([a, qn * wstart_f], axis=1), cat([v_bd, s_f], axis=0))
        sf_ref[...] = s_f * gch_f + jnp.where(on_diag, _bdot_tn(kn * wend_f, vn), 0.0)
        rows = pl.ds(pl.multiple_of((nc - 1 - i) * CHUNK, CHUNK), CHUNK)
        qn = qs_ref[rows, :]
        kn = ks_ref[rows, :]
        vn = v_ref[rows, :]
        s_b = sb_ref[...]
        o_ref[rows, :] += _bdot(qn * wstart_b, s_b)
        sb_ref[...] = s_b * gch_b + jnp.where(on_diag, _bdot_tn(kn * wend_b, vn), 0.0)
        return carry

    lax.fori_loop(0, nc, step, 0)
    for d, s_ref in enumerate((sf_ref, sb_ref)):
        sout_ref[0, d, 0] = s_ref[:DK_A, :DV_A]
        sout_ref[0, d, 1] = s_ref[DK_A:, DV_A:]

    for hh in range(2):
        vsl = slice(hh * DV_A, (hh + 1) * DV_A)
        ya_ref[:, vsl] = _silu(g_ref[:, vsl]) * _rms(o_ref[:, vsl], ng_ref[hh])


def _retention(q, k, v, g, s0, gam, ng, cos_t, sin_t, nseq, seq_len, row_off, latent):
    w2 = 2 * DK_A
    v2 = 2 * DV_A
    return pl.pallas_call(
        functools.partial(_ret_kernel, seq_len=seq_len, latent=latent),
        grid=(nseq, H_A // 2),
        in_specs=[pl.BlockSpec((seq_len, w2), lambda b, p: (b + row_off, p)),
                  pl.BlockSpec((seq_len, w2), lambda b, p: (b + row_off, p)),
                  pl.BlockSpec((seq_len, v2), lambda b, p: (b + row_off, p)),
                  pl.BlockSpec((seq_len, v2), lambda b, p: (b + row_off, p)),
                  pl.BlockSpec((1, 2, 2, DK_A, DV_A), lambda b, p: (b, 0, p, 0, 0)),
                  pl.BlockSpec((2, 2, CHUNK, DV_A), lambda b, p: (0, p, 0, 0)),
                  pl.BlockSpec((2, 1, DV_A), lambda b, p: (p, 0, 0)),
                  pl.BlockSpec((seq_len, w2), lambda b, p: (0, 0)),
                  pl.BlockSpec((seq_len, w2), lambda b, p: (0, 0))],
        out_specs=[pl.BlockSpec((seq_len, v2), lambda b, p: (b, p)),
                   pl.BlockSpec((1, 2, 2, DK_A, DV_A), lambda b, p: (b, 0, p, 0, 0))],
        out_shape=[jax.ShapeDtypeStruct((nseq * seq_len, H_A * DV_A), F32),
                   jax.ShapeDtypeStruct((nseq, 2, H_A, DK_A, DV_A), F32)],
        scratch_shapes=[pltpu.VMEM((seq_len, w2), F32), pltpu.VMEM((seq_len, w2), F32),
                        pltpu.VMEM((seq_len, v2), F32), pltpu.VMEM((w2, v2), F32), pltpu.VMEM((w2, v2), F32)],
        compiler_params=_cparams(("arbitrary", "arbitrary")),
        name="retention",
    )(q, k, v, g, s0, gam, ng, cos_t, sin_t)


def _fill_padded(pad_ref, cur, prev, nxt, has_prev, has_next, halo):
    tile = cur.shape[0]
    pad_ref[0:halo, :] = jnp.where(has_prev, prev, 0.0)
    pad_ref[halo:halo + tile, :] = cur
    pad_ref[halo + tile:halo + tile + halo, :] = jnp.where(has_next, nxt, 0.0)


def _conv_taps(pad_ref, w_ref, ntaps, halo, tile):
    base = halo - ntaps // 2
    acc = w_ref[0:1, :] * pad_ref[base:base + tile, :]
    for kk in range(1, ntaps):
        acc = acc + w_ref[kk:kk + 1, :] * pad_ref[base + kk:base + kk + tile, :]
    return acc


def _halo_specs(width, col_map, halo):
    per = CONV_TILE // halo

    def cur(i, j):
        return (i, col_map(j))

    def prev(i, j):
        return (jnp.maximum(i * per - 1, 0), col_map(j))

    def make_next(nblk):
        def nxt(i, j):
            return (jnp.minimum((i + 1) * per, nblk - 1), col_map(j))
        return nxt

    return cur, prev, make_next


def _sconv_kernel(bg_ref, cg_ref, cgp_ref, cgn_ref, hb_ref, hbp_ref, hbn_ref, w_ref, o_ref, pad_ref, *, st):
    has_prev, has_next = st.halo_flags(pl.program_id(0), CONV_TILE)
    _fill_padded(pad_ref, cg_ref[...] * hb_ref[...], cgp_ref[...] * hbp_ref[...],
                 cgn_ref[...] * hbn_ref[...], has_prev, has_next, SUBLANES)
    o_ref[...] = bg_ref[...] * _conv_taps(pad_ref, w_ref, SC_K, SUBLANES, CONV_TILE)


CONV_COLS = 512


def _short_gated_conv(st, bg, cg, hb, w):
    t, c = bg.shape
    halo = SUBLANES
    cur, prev, make_next = _halo_specs(c, lambda j: j, halo)
    nxt = make_next(t // halo)
    tile_spec = pl.BlockSpec((CONV_TILE, CONV_COLS), cur)
    prev_spec = pl.BlockSpec((halo, CONV_COLS), prev)
    next_spec = pl.BlockSpec((halo, CONV_COLS), nxt)
    return pl.pallas_call(
        functools.partial(_sconv_kernel, st=st),
        grid=(t // CONV_TILE, c // CONV_COLS),
        in_specs=[tile_spec, tile_spec, prev_spec, next_spec, tile_spec, prev_spec, next_spec,
                  pl.BlockSpec((SUBLANES, CONV_COLS), lambda i, j: (0, j))],
        out_specs=tile_spec,
        out_shape=jax.ShapeDtypeStruct((t, c), F32),
        scratch_shapes=[pltpu.VMEM((CONV_TILE + 2 * halo, CONV_COLS), F32)],
        compiler_params=_cparams(("arbitrary", "arbitrary")),
        name="short_gated_conv",
    )(bg, cg, cg, cg, hb, hb, hb, w)


def _qkv_conv_kernel(x_ref, xp_ref, xn_ref, w_ref, o_ref, pad_ref, *, st):
    has_prev, has_next = st.halo_flags(pl.program_id(0), CONV_TILE)
    _fill_padded(pad_ref, x_ref[...], xp_ref[...], xn_ref[...], has_prev, has_next, SUBLANES)
    s = _silu(_conv_taps(pad_ref, w_ref, QKV_K, SUBLANES, CONV_TILE))
    j = pl.program_id(1)
    for hh in range(CONV_COLS // DK_C):
        hsl = slice(hh * DK_C, (hh + 1) * DK_C)
        sh = s[:, hsl]
        nrm = sh * lax.rsqrt(jnp.sum(sh * sh, axis=-1, keepdims=True) + EPS)
        o_ref[:, hsl] = jnp.where(j == 0, nrm * (DK_C ** -0.5), jnp.where(j == 1, nrm, sh))


def _qkv_conv(st, qkv, w):
    t, c = qkv.shape
    assert CONV_COLS == H_C * DK_C == H_C * DV_C
    halo = SUBLANES
    cur, prev, make_next = _halo_specs(c, lambda j: j, halo)
    nxt = make_next(t // halo)
    tile_spec = pl.BlockSpec((CONV_TILE, CONV_COLS), cur)
    return pl.pallas_call(
        functools.partial(_qkv_conv_kernel, st=st),
        grid=(t // CONV_TILE, c // CONV_COLS),
        in_specs=[tile_spec, pl.BlockSpec((halo, CONV_COLS), prev), pl.BlockSpec((halo, CONV_COLS), nxt),
                  pl.BlockSpec((SUBLANES, CONV_COLS), lambda i, j: (0, j))],
        out_specs=tile_spec,
        out_shape=jax.ShapeDtypeStruct((t, c), F32),
        scratch_shapes=[pltpu.VMEM((CONV_TILE + 2 * halo, CONV_COLS), F32)],
        compiler_params=_cparams(("arbitrary", "arbitrary")),
        name="qkv_conv",
    )(qkv, qkv, qkv, w)


CF_HALO = 16


def _conformer_kernel(ca_ref, cap_ref, can_ref, cg_ref, cgp_ref, cgn_ref, w_ref, b_ref, lg_ref, lb_ref,
                      o_ref, pad_ref, *, st):
    has_prev, has_next = st.halo_flags(pl.program_id(0), CONV_TILE)
    _fill_padded(pad_ref, ca_ref[...] * _sigmoid(cg_ref[...]), cap_ref[...] * _sigmoid(cgp_ref[...]),
                 can_ref[...] * _sigmoid(cgn_ref[...]), has_prev, has_next, CF_HALO)
    hc = _conv_taps(pad_ref, w_ref, CF_K, CF_HALO, CONV_TILE) + b_ref[...]
    mu = jnp.mean(hc, axis=-1, keepdims=True)
    xc = hc - mu
    y = xc * lax.rsqrt(jnp.mean(xc * xc, axis=-1, keepdims=True) + EPS) * lg_ref[...] + lb_ref[...]
    o_ref[...] = _silu(y)


def _conformer(st, glu, w, b, ln_g, ln_b):
    t, c2 = glu.shape
    c = c2 // 2
    nblk = t // CF_HALO
    per = CONV_TILE // CF_HALO
    vec = pl.BlockSpec((1, c), lambda i: (0, 0))
    return pl.pallas_call(
        functools.partial(_conformer_kernel, st=st),
        grid=(t // CONV_TILE,),
        in_specs=[pl.BlockSpec((CONV_TILE, c), lambda i: (i, 0)),
                  pl.BlockSpec((CF_HALO, c), lambda i: (jnp.maximum(i * per - 1, 0), 0)),
                  pl.BlockSpec((CF_HALO, c), lambda i: (jnp.minimum((i + 1) * per, nblk - 1), 0)),
                  pl.BlockSpec((CONV_TILE, c), lambda i: (i, 1)),
                  pl.BlockSpec((CF_HALO, c), lambda i: (jnp.maximum(i * per - 1, 0), 1)),
                  pl.BlockSpec((CF_HALO, c), lambda i: (jnp.minimum((i + 1) * per, nblk - 1), 1)),
                  pl.BlockSpec((2 * CF_HALO, c), lambda i: (0, 0)), vec, vec, vec],
        out_specs=pl.BlockSpec((CONV_TILE, c), lambda i: (i, 0)),
        out_shape=jax.ShapeDtypeStruct((t, c), F32),
        scratch_shapes=[pltpu.VMEM((CONV_TILE + 2 * CF_HALO, c), F32)],
        compiler_params=_cparams(("arbitrary",)),
        name="conformer_conv",
    )(glu, glu, glu, glu, glu, glu, w, b, ln_g, ln_b)


GDN_PACK = 4
GDN_GROUPS = 4


def _split_bf16(x):
    hi = x.astype(BF16)
    return hi, (x - hi.astype(F32)).astype(BF16)


def _packed_unit_inverses(groups):
    dot = functools.partial(jnp.dot, preferred_element_type=F32)
    cat = jnp.concatenate
    n = len(groups[0])
    c = CHUNK
    w = n * c
    eye = (lax.broadcasted_iota(jnp.int32, (c, c), 0) == lax.broadcasted_iota(jnp.int32, (c, c), 1)).astype(F32)
    lane_block = lax.broadcasted_iota(jnp.int32, (c, 2 * w), 1) % w // c

    def rhs_of(p):
        hi, lo = _split_bf16(p)
        both = cat([hi, lo], axis=1)
        return hi, lo, cat([jnp.where(lane_block == i, both, jnp.zeros_like(both)) for i in range(n)], axis=0)

    def product(m, r):
        return (r[:m, :w] + r[m:, :w]) + (r[:m, w:] + r[m:, w:])

    ps = [cat(a_list, axis=1) for a_list in groups]
    ts = [cat([eye - a for a in a_list], axis=1) for a_list in groups]
    for g, p in enumerate(ps):
        hi, lo, rhs = rhs_of(p)
        ps[g] = product(c, dot(cat([hi, lo], axis=0), rhs))
    steps = int(math.log2(c)) - 1
    for step in range(steps):
        for g in range(len(groups)):
            p_hi, p_lo, rhs = rhs_of(ps[g])
            t_hi, t_lo = _split_bf16(ts[g])
            if step < steps - 1:
                both = product(2 * c, dot(cat([t_hi, p_hi, t_lo, p_lo], axis=0), rhs))
                ts[g] = ts[g] + both[:c]
                ps[g] = both[c:]
            else:
                ts[g] = ts[g] + product(c, dot(cat([t_hi, t_lo], axis=0), rhs))
    return [[t[:, i * c:(i + 1) * c] for i in range(n)] for t in ts]


def _chunk_cumsum(x, reverse):
    row = lax.broadcasted_iota(jnp.int32, x.shape, 0)
    s = 1
    while s < CHUNK:
        if reverse:
            x = x + jnp.where(row < CHUNK - s, pltpu.roll(x, CHUNK - s, 0), 0.0)
        else:
            x = x + jnp.where(row >= s, pltpu.roll(x, s, 0), 0.0)
        s *= 2
    return x


GDN_HEADS = 2


def _gdn_kernel(q_ref, k_ref, v_ref, z_ref, ab_ref, alog_ref, dtb_ref, s0_ref, ng_ref,
                y_ref, sout_ref, g_ref, beta_ref, u_ref, wq_ref, ak_ref, egl_ref, o_ref, *, seq_len):
    nc = seq_len // CHUNK
    head0 = pl.program_id(1) * GDN_HEADS
    row = lax.broadcasted_iota(jnp.int32, (CHUNK, CHUNK), 0)
    col = lax.broadcasted_iota(jnp.int32, (CHUNK, CHUNK), 1)
    lane = lax.broadcasted_iota(jnp.int32, (CHUNK, LANES), 1)
    alog = alog_ref[...]
    dtb = dtb_ref[...]
    head_cols = [slice(hh * DK_C, (hh + 1) * DK_C) for hh in range(GDN_HEADS)]

    def gates(n, carry):
        rows = pl.ds(pl.multiple_of(n * CHUNK, CHUNK), CHUNK)
        ab = ab_ref[rows, :]
        g_all = -jnp.exp(alog) * _softplus(ab + dtb)
        b_all = _sigmoid(ab)
        for hh in range(GDN_HEADS):
            for d in range(2):
                gcol = d * H_C + head0 + hh
                gsel = jnp.sum(jnp.where(lane == gcol, g_all, 0.0), axis=-1, keepdims=True)
                bsel = jnp.sum(jnp.where(lane == 2 * H_C + gcol, b_all, 0.0), axis=-1, keepdims=True)
                g_ref[hh, d, rows, :] = jnp.broadcast_to(gsel, (CHUNK, LANES))
                beta_ref[hh, d, rows, :] = jnp.broadcast_to(bsel, (CHUNK, LANES))
        return carry

    lax.fori_loop(0, nc, gates, 0)

    pairs_per_iter = GDN_GROUPS // GDN_HEADS
    chunks_per_group = GDN_PACK // 2

    def precompute(it, carry):
        groups = [prepare_group(hh, it * pairs_per_iter + pr)
                  for hh in range(GDN_HEADS) for pr in range(pairs_per_iter)]
        inverses = _packed_unit_inverses([[sysm[0] for sysm in systems] for systems in groups])
        for t_invs, systems in zip(inverses, groups):
            for t_inv, (_, rhs, hh, d, n) in zip(t_invs, systems):
                uw = _bdot(t_inv, rhs)
                u_ref[hh, d, pl.ds(pl.multiple_of(n * CHUNK, CHUNK), CHUNK), :] = uw[:, :DV_C]
                wq_ref[hh, d, pl.ds(pl.multiple_of(n * 2 * CHUNK, CHUNK), CHUNK), :] = uw[:, DV_C:].astype(BF16)
        return carry

    def prepare_group(hh, pair):
        systems = []
        for jc in range(chunks_per_group):
            n = pair * chunks_per_group + jc
            rows = pl.ds(pl.multiple_of(n * CHUNK, CHUNK), CHUNK)
            qc = q_ref[rows, head_cols[hh]]
            kc = k_ref[rows, head_cols[hh]]
            vc = v_ref[rows, head_cols[hh]]
            gbs = [_chunk_cumsum(g_ref[hh, d, rows, :], d == 1) for d in range(2)]
            bbs = [beta_ref[hh, d, rows, :] for d in range(2)]
            kbs = [kc * bb for bb in bbs]
            prod = _bdot_nt(jnp.concatenate(kbs + [qc], axis=0), kc)
            for d in range(2):
                incl = (row >= col) if d == 0 else (row <= col)
                strict = (row > col) if d == 0 else (row < col)
                last = CHUNK - 1 if d == 0 else 0
                gb = gbs[d]
                gr = gb[:, :CHUNK].T
                gl = gb[last:last + 1, :]
                decay = jnp.where(incl, jnp.exp(jnp.where(incl, gb[:, :CHUNK] - gr, 0.0)), 0.0)
                eg = jnp.exp(gb)
                a_low = jnp.where(strict, prod[d * CHUNK:(d + 1) * CHUNK] * decay, 0.0)
                attn = jnp.where(incl, prod[2 * CHUNK:] * decay, 0.0)
                kg_t = (kc * jnp.exp(gl - gb)).T
                ak_ref[hh, d, pl.ds(pl.multiple_of(n * 3 * CHUNK, CHUNK), 3 * CHUNK), :] = jnp.concatenate(
                    [attn, kg_t], axis=0).astype(BF16)
                wq_ref[hh, d, pl.ds(pl.multiple_of(n * 2 * CHUNK + CHUNK, CHUNK), CHUNK), :] = (
                    qc * eg).astype(BF16)
                egl_ref[hh, d, pl.ds(n, 1), :] = jnp.exp(gl)
                systems.append((a_low, jnp.concatenate([vc * bbs[d], kbs[d] * eg], axis=1), hh, d, n))
        return systems

    lax.fori_loop(0, nc // (pairs_per_iter * chunks_per_group), precompute, 0)

    o_ref[...] = jnp.zeros_like(o_ref)

    def scan(i, carry):
        out = []
        for idx, s in enumerate(carry):
            hh, d = divmod(idx, 2)
            n = i if d == 0 else nc - 1 - i
            rows = pl.ds(pl.multiple_of(n * CHUNK, CHUNK), CHUNK)
            on_s = _bdot(wq_ref[hh, d, pl.ds(pl.multiple_of(n * 2 * CHUNK, CHUNK), 2 * CHUNK), :], s)
            v_new = u_ref[hh, d, rows, :] - on_s[:CHUNK]
            on_v = _bdot(ak_ref[hh, d, pl.ds(pl.multiple_of(n * 3 * CHUNK, CHUNK), 3 * CHUNK), :], v_new)
            o_ref[rows, head_cols[hh]] += on_s[CHUNK:] + on_v[:CHUNK]
            out.append(s * egl_ref[hh, d, pl.ds(n, 1), :] + on_v[CHUNK:])
        return tuple(out)

    init = tuple(s0_ref[0, d, hh] for hh in range(GDN_HEADS) for d in range(2))
    final = lax.fori_loop(0, nc, scan, init)
    for idx, s in enumerate(final):
        hh, d = divmod(idx, 2)
        sout_ref[0, d, hh] = s
    for hh in range(GDN_HEADS):
        y_ref[:, head_cols[hh]] = _rms(o_ref[:, head_cols[hh]], ng_ref[...]) * _silu(z_ref[:, head_cols[hh]])


def _gdn(qkv_n, z, ab, alog, dtb, s0, ng, nseq, seq_len, row_off):
    assert GDN_GROUPS % GDN_HEADS == 0 and seq_len % (CHUNK * (GDN_GROUPS // GDN_HEADS) * (GDN_PACK // 2)) == 0
    wide = GDN_HEADS * DK_C
    steps = H_C // GDN_HEADS
    blk = lambda off: pl.BlockSpec((seq_len, wide), lambda b, h: (b + row_off, h + off))
    vec = pl.BlockSpec((1, LANES), lambda b, h: (0, 0))
    st_spec = pl.BlockSpec((1, 2, GDN_HEADS, DK_C, DV_C), lambda b, h: (b, 0, h, 0, 0))
    nc = seq_len // CHUNK
    per = lambda rows, width, dtype: pltpu.VMEM((GDN_HEADS, 2, rows, width), dtype)
    return pl.pallas_call(
        functools.partial(_gdn_kernel, seq_len=seq_len),
        grid=(nseq, steps),
        in_specs=[blk(0), blk(steps), blk(2 * steps), blk(0),
                  pl.BlockSpec((seq_len, LANES), lambda b, h: (b + row_off, 0)),
                  vec, vec, st_spec, vec],
        out_specs=[pl.BlockSpec((seq_len, wide), lambda b, h: (b, h)), st_spec],
        out_shape=[jax.ShapeDtypeStruct((nseq * seq_len, H_C * DV_C), F32),
                   jax.ShapeDtypeStruct((nseq, 2, H_C, DK_C, DV_C), F32)],
        scratch_shapes=[per(seq_len, LANES, F32), per(seq_len, LANES, F32), per(seq_len, DV_C, F32),
                        per(2 * seq_len, DK_C, BF16), per(3 * seq_len, CHUNK, BF16),
                        per(max(nc, SUBLANES), LANES, F32), pltpu.VMEM((seq_len, wide), F32)],
        compiler_params=_cparams(("arbitrary", "arbitrary")),
        name="gated_deltanet",
    )(qkv_n, qkv_n, qkv_n, z, ab, alog, dtb, s0, ng)


def _out_proj_kernel(ya_ref, yb_ref, x_ref, mod_ref, g_ref, wo_ref, wq_ref, x1_ref, h2_ref, qp_ref):
    half = ya_ref.shape[1]
    m = mod_ref[0]
    y = (jnp.dot(ya_ref[...].astype(BF16), wo_ref[0:half, :], preferred_element_type=F32)
         + jnp.dot(yb_ref[...].astype(BF16), wo_ref[half:, :], preferred_element_type=F32))
    x1 = x_ref[...] + m[2:3] * y
    x1_ref[...] = x1
    h2 = (_rms(x1, g_ref[...]) * (1.0 + m[4:5]) + m[3:4]).astype(BF16)
    h2_ref[...] = h2
    qp_ref[...] = jnp.dot(h2, wq_ref[...], preferred_element_type=F32)


def _out_proj(st, ya, yb, x, mod, g, wo, wq, tm):
    t, d = x.shape
    half = ya.shape[1]
    nq = wq.shape[1]
    return pl.pallas_call(
        _out_proj_kernel,
        grid=(t // tm,),
        in_specs=[pl.BlockSpec((tm, half), lambda i: (i, 0)),
                  pl.BlockSpec((tm, half), lambda i: (i, 0)),
                  pl.BlockSpec((tm, d), lambda i: (i, 0)),
                  pl.BlockSpec((1, 6, d), lambda i: (st.mod_row(i, tm), 0, 0)),
                  pl.BlockSpec((1, d), lambda i: (0, 0)),
                  pl.BlockSpec((2 * half, d), lambda i: (0, 0)),
                  pl.BlockSpec((d, nq), lambda i: (0, 0))],
        out_specs=[pl.BlockSpec((tm, d), lambda i: (i, 0)),
                   pl.BlockSpec((tm, d), lambda i: (i, 0)),
                   pl.BlockSpec((tm, nq), lambda i: (i, 0))],
        out_shape=[jax.ShapeDtypeStruct((t, d), F32), jax.ShapeDtypeStruct((t, d), BF16),
                   jax.ShapeDtypeStruct((t, nq), F32)],
        compiler_params=_cparams(("arbitrary",)),
        name="out_proj",
    )(ya, yb, x, mod, g, wo, wq)


def _candidate_rows():
    groups = ([(0, k2) for k2 in range(16)], [(1, k2) for k2 in range(8)], [(2, k2) for k2 in range(8)],
              [(3, k2) for k2 in range(8)], [(k1, 0) for k1 in range(16)], [(k1, 1) for k1 in range(8)],
              [(k1, 2) for k1 in range(8)])
    rows, seen = [], set()
    for grp in groups:
        for k1, k2 in grp:
            ok = (k1 + 1) * (k2 + 1) <= TOPK and (k1, k2) not in seen
            if ok:
                seen.add((k1, k2))
            rows.append(float(k1 * TOPK + k2) if ok else CAND_INVALID)
    return rows


CAND_INVALID = float(TOPK * TOPK)
RANK_MARK = 2.0 ** 100
RANK_STEP = 2.0 ** 95
SELECT_STRIPS = 4


def _pack_candidates(a1, a2, op):
    return jnp.concatenate([op(a1[0:1], a2), op(a1[1:2], a2[0:8]), op(a1[2:3], a2[0:8]),
                            op(a1[3:4], a2[0:8]), op(a1, a2[0:1]), op(a1[0:8], a2[1:2]),
                            op(a1[0:8], a2[2:3])], axis=0)


def _top16(s, vals_ref, exact):
    n = s.shape[0]
    if not exact:
        for kk in range(TOPK):
            m = jnp.max(s, axis=0, keepdims=True)
            vals_ref[kk:kk + 1, :] = m
            s = jnp.where(s == m, -(RANK_MARK + kk * RANK_STEP), s)
        return jnp.where(s < -0.5 * RANK_MARK, (-s - RANK_MARK) * (1.0 / RANK_STEP), float(TOPK))
    iota = lax.broadcasted_iota(jnp.int32, s.shape, 0).astype(F32)
    rank = jnp.full(s.shape, float(TOPK), F32)
    for kk in range(TOPK):
        m = jnp.max(s, axis=0, keepdims=True)
        idx = jnp.min(jnp.where(s == m, iota, float(n)), axis=0, keepdims=True)
        hit = iota == idx
        rank = jnp.where(hit, float(kk), rank)
        vals_ref[kk:kk + 1, :] = m
        s = jnp.where(hit, NEG_INF, s)
    return rank


def _peer_select_kernel(q_ref, keys_ref, cflat_ref, e1_ref, cnt_ref, e2_ref, r2_ref, v1_ref, v2_ref, *, tt):
    nk = N_KEYS
    cflat = cflat_ref[...]
    valid = cflat < CAND_INVALID
    row16 = lax.broadcasted_iota(jnp.int32, (TOPK, LANES), 0)

    def compute(strip, exact):
        s1, s2, tsl, va_ref, vb_ref = strip
        rank1 = _top16(s1, va_ref, exact)
        rank2 = _top16(s2, vb_ref, exact)
        v1 = va_ref[...]
        v2 = vb_ref[...]
        cand = jnp.where(valid, _pack_candidates(v1, v2, jnp.add), NEG_INF)
        prod = _pack_candidates(jnp.exp(v1 - v1[0:1]), jnp.exp(v2 - v2[0:1]), jnp.multiply)
        sel = jnp.zeros(cand.shape, F32)
        for _ in range(TOPK):
            m = jnp.max(cand, axis=0, keepdims=True)
            if exact:
                idx = jnp.min(jnp.where(cand == m, cflat, CAND_INVALID), axis=0, keepdims=True)
                hit = cflat == idx
            else:
                hit = cand == m
            sel = jnp.where(hit, 1.0, sel)
            cand = jnp.where(hit, NEG_INF, cand)
        zsum = jnp.sum(sel * prod, axis=0, keepdims=True)
        c_all = sel[40:56] + jnp.concatenate([sel[56:64] + sel[64:72], jnp.zeros((8, LANES), F32)], axis=0)
        for k1, (lo, hi) in enumerate(((0, 16), (16, 24), (24, 32), (32, 40))):
            c_all = jnp.where(row16 == k1, jnp.sum(sel[lo:hi], axis=0, keepdims=True), c_all)
        rank1_b = rank1.astype(BF16)
        c_all_b = c_all.astype(BF16)
        cnt = jnp.zeros(rank1.shape, BF16)
        for k1 in range(TOPK):
            cnt = jnp.where(rank1_b == float(k1), c_all_b[k1:k1 + 1], cnt)
        cnt = cnt.astype(F32)
        in1 = rank1 < float(TOPK)
        in2 = rank2 < float(TOPK)
        e1_ref[0, :, tsl] = jnp.where(in1, jnp.exp(s1 - v1[0:1]), 0.0) / zsum
        cnt_ref[0, :, tsl] = cnt
        e2_ref[:, tsl] = jnp.where(in2, jnp.exp(s2 - v2[0:1]), 0.0).astype(e2_ref.dtype)
        r2_ref[:, tsl] = rank2.astype(r2_ref.dtype)
        n1 = jnp.sum(jnp.where(in1, 1.0, 0.0), axis=0, keepdims=True)
        n2 = jnp.sum(jnp.where(in2, 1.0, 0.0), axis=0, keepdims=True)
        nc = jnp.sum(sel, axis=0, keepdims=True)
        want = float(TOPK)
        return jnp.abs(n1 - want) + jnp.abs(n2 - want) + jnp.abs(nc - want)

    def strip_group(si, carry):
        strips = []
        for sub in range(SELECT_STRIPS):
            t0 = pl.multiple_of((si * SELECT_STRIPS + sub) * LANES, LANES)
            tsl = pl.ds(t0, LANES)
            s1 = _bdot_nt(keys_ref[0, 0], q_ref[tsl, 0:nk])
            s2 = _bdot_nt(keys_ref[0, 1], q_ref[tsl, nk:2 * nk])
            strips.append((s1, s2, tsl, v1_ref.at[sub], v2_ref.at[sub]))
        tied = [compute(strip, False) for strip in strips]
        for strip, bad in zip(strips, tied):
            @pl.when(jnp.max(bad) > 0.0)
            def _(strip=strip):
                compute(strip, True)
        return carry

    lax.fori_loop(0, tt // (SELECT_STRIPS * LANES), strip_group, 0)


def _peer_select(qp, keys, tt):
    t = qp.shape[0]
    rows = _candidate_rows()
    cflat = jnp.broadcast_to(jnp.asarray(rows, F32)[:, None], (len(rows), LANES))
    out_spec = pl.BlockSpec((1, N_KEYS, tt), lambda i, h: (h, 0, i))
    f32_sds = jax.ShapeDtypeStruct((H_P, N_KEYS, t), F32)
    flat_spec = pl.BlockSpec((N_KEYS, tt), lambda i, h: (h, i))
    flat_sds = jax.ShapeDtypeStruct((H_P * N_KEYS, t), BF16)
    return pl.pallas_call(
        functools.partial(_peer_select_kernel, tt=tt),
        grid=(t // tt, H_P),
        in_specs=[pl.BlockSpec((tt, 2 * N_KEYS), lambda i, h: (i, h)),
                  pl.BlockSpec((1, 2, N_KEYS, N_KEYS), lambda i, h: (h, 0, 0, 0)),
                  pl.BlockSpec(cflat.shape, lambda i, h: (0, 0))],
        out_specs=[out_spec, out_spec, flat_spec, flat_spec],
        out_shape=[f32_sds, f32_sds, flat_sds, flat_sds],
        scratch_shapes=[pltpu.VMEM((SELECT_STRIPS, TOPK, LANES), F32),
                        pltpu.VMEM((SELECT_STRIPS, TOPK, LANES), F32)],
        compiler_params=_cparams(("arbitrary", "arbitrary")),
        name="peer_select",
    )(qp, keys, cflat)


def _peer_dense_kernel(h_ref, u_ref, vt_ref, e1_ref, cnt_ref, e2_ref, r2_ref, x_ref, mod_ref, o_ref,
                       acc_ref, s_ref, p_ref, e2b_ref, r2b_ref, ht_ref, *, na):
    j = pl.program_id(1)
    tt = s_ref.shape[1]

    @pl.when(j == 0)
    def _():
        acc_ref[...] = jnp.zeros_like(acc_ref)
        e2b_ref[...] = e2_ref[...].astype(BF16)
        r2b_ref[...] = r2_ref[...].astype(BF16)
        ht_ref[...] = h_ref[...].astype(F32).T.astype(BF16)

    s_ref[...] = jnp.dot(u_ref[...], ht_ref[...], preferred_element_type=F32)

    for aa in range(na):
        rows = slice(aa * N_KEYS, (aa + 1) * N_KEYS)
        for tg in range(tt // LANES):
            tsl = slice(tg * LANES, (tg + 1) * LANES)
            gate = None
            for hh in range(H_P):
                cnt_row = cnt_ref[hh, aa:aa + 1, tsl].astype(BF16)
                e1_row = e1_ref[hh, aa:aa + 1, tsl].astype(BF16)
                hsl = slice(hh * N_KEYS, (hh + 1) * N_KEYS)
                term = jnp.where(r2b_ref[hsl, tsl] < cnt_row, e2b_ref[hsl, tsl], 0.0) * e1_row
                gate = term if gate is None else gate + term
            p_ref[rows, tsl] = gate * _gelu_tanh(s_ref[rows, tsl].astype(BF16))

    acc_ref[...] += jnp.dot(vt_ref[...], p_ref[...], preferred_element_type=F32)

    @pl.when(j == pl.num_programs(1) - 1)
    def _():
        o_ref[...] = x_ref[...] + mod_ref[0][5:6] * acc_ref[...].T


def _peer_dense(st, h2, u, vt, e1, cnt, e2, r2, x, mod, tt, et):
    t, d = x.shape
    n_exp = u.shape[0]
    na = et // N_KEYS
    row_spec = pl.BlockSpec((H_P, na, tt), lambda i, j: (0, j, i))
    full_spec = pl.BlockSpec((H_P * N_KEYS, tt), lambda i, j: (0, i))
    return pl.pallas_call(
        functools.partial(_peer_dense_kernel, na=na),
        grid=(t // tt, n_exp // et),
        in_specs=[pl.BlockSpec((tt, d), lambda i, j: (i, 0)),
                  pl.BlockSpec((et, d), lambda i, j: (j, 0)),
                  pl.BlockSpec((d, et), lambda i, j: (0, j)),
                  row_spec, row_spec, full_spec, full_spec,
                  pl.BlockSpec((tt, d), lambda i, j: (i, 0)),
                  pl.BlockSpec((1, 6, d), lambda i, j: (st.mod_row(i, tt), 0, 0))],
        out_specs=pl.BlockSpec((tt, d), lambda i, j: (i, 0)),
        out_shape=jax.ShapeDtypeStruct((t, d), F32),
        scratch_shapes=[pltpu.VMEM((d, tt), F32), pltpu.VMEM((et, tt), F32), pltpu.VMEM((et, tt), BF16),
                        pltpu.VMEM((H_P * N_KEYS, tt), BF16), pltpu.VMEM((H_P * N_KEYS, tt), BF16),
                        pltpu.VMEM((d, tt), BF16)],
        compiler_params=_cparams(("arbitrary", "arbitrary")),
        name="peer_dense",
    )(h2, u, vt, e1, cnt, e2, r2, x, mod)


def _final_norm_kernel(x_ref, g_ref, o_ref):
    o_ref[...] = _rms(x_ref[...], g_ref[...])


def _final_norm(x, g, tm):
    t, d = x.shape
    return pl.pallas_call(
        _final_norm_kernel,
        grid=(t // tm,),
        in_specs=[pl.BlockSpec((tm, d), lambda i: (i, 0)), pl.BlockSpec((1, d), lambda i: (0, 0))],
        out_specs=pl.BlockSpec((tm, d), lambda i: (i, 0)),
        out_shape=jax.ShapeDtypeStruct((t, d), F32),
        compiler_params=_cparams(("arbitrary",)),
        name="final_norm",
    )(x, g)


def _rope_tables(seq_len):
    pos = jnp.arange(seq_len)
    rowp = (pos // GRID_W).astype(F32)
    colp = (pos % GRID_W).astype(F32)
    nf = DK_A // 4
    freqs = ROPE_BASE ** (-jnp.arange(nf, dtype=F32) / nf)
    ang = jnp.concatenate([rowp[:, None] * freqs, colp[:, None] * freqs], axis=-1)
    cos = jnp.cos(ang)
    sin = jnp.sin(ang)
    cos_t = jnp.tile(cos, (1, 4))
    sin_t = jnp.tile(jnp.concatenate([-sin, sin], axis=-1), (1, 2))
    return cos_t, sin_t


def _pad_rows(w, rows):
    return jnp.concatenate([w, jnp.zeros((rows - w.shape[0], w.shape[1]), w.dtype)], axis=0)


def _lane_row(vals):
    flat = vals.reshape(-1).astype(F32)
    return jnp.concatenate([flat, jnp.zeros((LANES - flat.shape[0],), F32)])[None, :]


def kernel(x_prompt, x_sample, state_ret, state_gdn, c, c_ctx, ada_w, ada_b, norm_mix_g, norm_ffn_g,
           final_norm_g, ev_w_in, ev_w_out, ret_gamma_logit, ret_norm_g, sc_conv_w, od_w_in, od_w_out,
           gdn_conv_w, gdn_a_log, gdn_dt_bias, gdn_norm_g, cf_dw_w, cf_dw_b, cf_ln_g, cf_ln_b,
           peer_wq, peer_keys, peer_u, peer_v):
    bp, lp, d = x_prompt.shape
    bs, ls, _ = x_sample.shape
    depth = ada_w.shape[0]
    st = _Streams(bp, lp, bs, ls)
    w_a = H_A * DV_A
    w_b = d - w_a
    w_c = H_C * DV_C
    w_d = d - w_c
    tm = TOKEN_TILE
    select_tt = 2 * TOKEN_TILE

    x = jnp.concatenate([x_prompt.reshape(st.tp, d), x_sample.reshape(st.ts, d)], axis=0)
    cvec = jnp.concatenate([c_ctx[None, :], c, jnp.zeros((N_MOD_ROWS - 1 - bs, d), F32)], axis=0)
    mods = _modulation(cvec, ada_w, ada_b).reshape(depth, N_MOD_ROWS, 6, d)
    cos_t, sin_t = _rope_tables(ls)
    zero_ret = jnp.zeros((bp, 2, H_A, DK_A, DV_A), F32)
    zero_gdn = jnp.zeros((bp, 2, H_C, DK_C, DV_C), F32)

    ret_new, gdn_new = [], []
    for l in range(depth):
        i = l // 2
        mod = mods[l]
        g1 = norm_mix_g[l][None, :]
        if l % 2 == 0:
            widths = (H_A * DK_A, H_A * DK_A, w_a, w_a, w_b, w_b, w_b)
            q, k, v, g, bg, cg, hb = _norm_proj(st, x, mod, g1, ev_w_in[i].astype(BF16), widths, tm)
            gam = jnp.broadcast_to(ret_gamma_logit[i][:, :, None, None], (2, H_A, CHUNK, DV_A))
            ng = ret_norm_g[i][:, None, :]
            ya_p, s_new = _retention(q, k, v, g, zero_ret, gam, ng, cos_t[:lp], sin_t[:lp],
                                     bp, lp, 0, False)
            ya_s, _ = _retention(q, k, v, g, state_ret[:, i], gam, ng, cos_t, sin_t,
                                 bs, ls, st.tp // ls, True)
            ret_new.append(s_new)
            ya = jnp.concatenate([ya_p, ya_s], axis=0)
            yb = _short_gated_conv(st, bg, cg, hb, _pad_rows(sc_conv_w[i], SUBLANES))
            w_out = ev_w_out[i]
        else:
            n_gate = 2 * 2 * H_C
            w_in = od_w_in[i]
            o_ab = 4 * w_c
            w_main = jnp.concatenate([w_in[:, :o_ab], w_in[:, o_ab + n_gate:],
                                      w_in[:, o_ab:o_ab + n_gate],
                                      jnp.zeros((d, LANES - n_gate), F32)], axis=1).astype(BF16)
            widths = (3 * w_c, w_c, 2 * w_d, LANES)
            qkv, z, glu, ab = _norm_proj(st, x, mod, g1, w_main, widths, tm)
            qkv_n = _qkv_conv(st, qkv, _pad_rows(gdn_conv_w[i], SUBLANES))
            alog = _lane_row(gdn_a_log[i])
            dtb = _lane_row(gdn_dt_bias[i])
            ng = gdn_norm_g[i][None, :]
            yc_p, s_new = _gdn(qkv_n, z, ab, alog, dtb, zero_gdn, ng, bp, lp, 0)
            yc_s, _ = _gdn(qkv_n, z, ab, alog, dtb, state_gdn[:, i], ng, bs, ls, st.tp // ls)
            gdn_new.append(s_new)
            ya = jnp.concatenate([yc_p, yc_s], axis=0)
            yb = _conformer(st, glu, _pad_rows(cf_dw_w[i], 2 * CF_HALO), cf_dw_b[i][None, :],
                            cf_ln_g[i][None, :], cf_ln_b[i][None, :])
            w_out = od_w_out[i]
        x, h2, qp = _out_proj(st, ya, yb, x, mod, norm_ffn_g[l][None, :], w_out.astype(BF16),
                              peer_wq[l].astype(BF16), tm)
        e1, cnt, e2, r2 = _peer_select(qp, peer_keys[l].astype(BF16), select_tt)
        x = _peer_dense(st, h2, peer_u[l].astype(BF16), peer_v[l].astype(BF16).T, e1, cnt, e2, r2,
                        x, mod, PEER_TOKEN_TILE, PEER_EXPERT_TILE)

    y = _final_norm(x, final_norm_g[None, :], tm)
    y_prompt = y[:st.tp].reshape(bp, lp, d)
    y_sample = y[st.tp:].reshape(bs, ls, d)
    new_state_ret = jnp.stack(ret_new, axis=1).astype(x_prompt.dtype)
    new_state_gdn = jnp.stack(gdn_new, axis=1).astype(x_prompt.dtype)
    return (y_prompt, y_sample, new_state_ret, new_state_gdn)
```

```python
import functools
import math

import jax
import jax.numpy as jnp
from jax import lax
from jax.experimental import pallas as pl
from jax.experimental.pallas import tpu as pltpu

F32 = jnp.float32
BF16 = jnp.bfloat16
HIGHEST = lax.Precision.HIGHEST

EPS = 1e-6
CHUNK = 64
GRID_W = 64
ROPE_BASE = 10000.0
H_A, DK_A, DV_A = 4, 64, 128
H_C, DK_C, DV_C = 4, 128, 128
SC_K, QKV_K, CF_K = 3, 3, 31
N_KEYS, H_P, TOPK = 128, 8, 16
N_MOD_ROWS = 16
LANES = 128
SUBLANES = 8
VMEM_LIMIT = 56 * 1024 * 1024
CONV_TILE = 256
TOKEN_TILE = 512
PEER_TOKEN_TILE = 512
PEER_EXPERT_TILE = 2048
NEG_INF = float("-inf")


def _cparams(sem):
    return pltpu.CompilerParams(dimension_semantics=sem, vmem_limit_bytes=VMEM_LIMIT)


def _bdot(a, b):
    return jnp.dot(a.astype(BF16), b.astype(BF16), preferred_element_type=F32)


def _bdot_nt(a, b):
    return lax.dot_general(a.astype(BF16), b.astype(BF16), (((1,), (1,)), ((), ())),
                           preferred_element_type=F32)


def _bdot_tn(a, b):
    return lax.dot_general(a.astype(BF16), b.astype(BF16), (((0,), (0,)), ((), ())),
                           preferred_element_type=F32)


def _hdot(a, b):
    return jnp.dot(a, b, precision=HIGHEST, preferred_element_type=F32)


def _sigmoid(x):
    return 1.0 / (1.0 + jnp.exp(-x))


def _silu(x):
    return x * _sigmoid(x)


def _softplus(x):
    return jnp.maximum(x, 0.0) + jnp.log1p(jnp.exp(-jnp.abs(x)))


def _log_sigmoid(x):
    return -_softplus(-x)


def _gelu_tanh(x):
    c = math.sqrt(2.0 / math.pi)
    return 0.5 * x * (1.0 + jnp.tanh(c * (x + 0.044715 * (x * x * x))))


def _rms(x, g):
    return x * lax.rsqrt(jnp.mean(x * x, axis=-1, keepdims=True) + EPS) * g


class _Streams:
    def __init__(self, bp, lp, bs, ls):
        self.bp, self.lp, self.bs, self.ls = bp, lp, bs, ls
        self.tp, self.ts = bp * lp, bs * ls
        self.t = self.tp + self.ts

    def mod_row(self, i, tile):
        tiles_p = self.tp // tile
        per_seq = self.ls // tile
        return jnp.where(i < tiles_p, 0, 1 + (i - tiles_p) // per_seq)

    def halo_flags(self, i, tile):
        tiles_p = self.tp // tile
        per_p = self.lp // tile
        per_s = self.ls // tile
        in_p = i < tiles_p
        jp = i % per_p
        js = (i - tiles_p) % per_s
        has_prev = jnp.where(in_p, jp > 0, js > 0)
        has_next = jnp.where(in_p, jp < per_p - 1, js < per_s - 1)
        return has_prev, has_next


def _mod_kernel(c_ref, w_ref, b_ref, o_ref):
    s = _silu(c_ref[...])
    o_ref[0] = _hdot(s, w_ref[0]) + b_ref[0]


def _modulation(cvec, ada_w, ada_b):
    depth, d, d6 = ada_w.shape
    nj = d6 // d
    return pl.pallas_call(
        _mod_kernel,
        grid=(depth, nj),
        in_specs=[pl.BlockSpec((N_MOD_ROWS, d), lambda l, j: (0, 0)),
                  pl.BlockSpec((1, d, d), lambda l, j: (l, 0, j)),
                  pl.BlockSpec((1, 1, d), lambda l, j: (l, 0, j))],
        out_specs=pl.BlockSpec((1, N_MOD_ROWS, d), lambda l, j: (l, 0, j)),
        out_shape=jax.ShapeDtypeStruct((depth, N_MOD_ROWS, d6), F32),
        compiler_params=_cparams(("arbitrary", "arbitrary")),
        name="modulation",
    )(cvec, ada_w, ada_b.reshape(depth, 1, d6))


def _norm_proj_kernel(x_ref, mod_ref, g_ref, w_ref, *o_refs, widths):
    m = mod_ref[0]
    h = _rms(x_ref[...], g_ref[...]) * (1.0 + m[1:2]) + m[0:1]
    p = jnp.dot(h.astype(BF16), w_ref[...], preferred_element_type=F32)
    off = 0
    for o_ref, wd in zip(o_refs, widths):
        o_ref[...] = p[:, off:off + wd]
        off += wd


def _norm_proj(st, x, mod, g, w, widths, tm):
    t, d = x.shape
    n = w.shape[1]
    return pl.pallas_call(
        functools.partial(_norm_proj_kernel, widths=widths),
        grid=(t // tm,),
        in_specs=[pl.BlockSpec((tm, d), lambda i: (i, 0)),
                  pl.BlockSpec((1, 6, d), lambda i: (st.mod_row(i, tm), 0, 0)),
                  pl.BlockSpec((1, d), lambda i: (0, 0)),
                  pl.BlockSpec((d, n), lambda i: (0, 0))],
        out_specs=[pl.BlockSpec((tm, wd), lambda i: (i, 0)) for wd in widths],
        out_shape=[jax.ShapeDtypeStruct((t, wd), F32) for wd in widths],
        compiler_params=_cparams(("arbitrary",)),
        name="norm_proj",
    )(x, mod, g, w)


def _ret_kernel(q_ref, k_ref, v_ref, g_ref, s0_ref, gam_ref, ng_ref, cos_ref, sin_ref,
                ya_ref, sout_ref, qs_ref, ks_ref, o_ref, sf_ref, sb_ref, *, seq_len, latent):
    nc = seq_len // CHUNK
    q = q_ref[...]
    k = k_ref[...] * (DK_A ** -0.5)
    if latent:
        lane = lax.broadcasted_iota(jnp.int32, q.shape, 1)
        first_half = (lane % DK_A) < (DK_A // 2)
        cos = cos_ref[...]
        sin = sin_ref[...]

        def rope(x):
            partner = jnp.where(first_half, pltpu.roll(x, LANES - DK_A // 2, 1),
                                pltpu.roll(x, DK_A // 2, 1))
            return x * cos + partner * sin

        q = rope(q)
        k = rope(k)
    qs_ref[...] = q
    ks_ref[...] = k

    cat = jnp.concatenate
    kw = 2 * DK_A
    vw = 2 * DV_A
    row = lax.broadcasted_iota(jnp.int32, (CHUNK, kw), 0).astype(F32)
    lane_k = lax.broadcasted_iota(jnp.int32, (CHUNK, kw), 1)
    lane_v = lax.broadcasted_iota(jnp.int32, (CHUNK, vw), 1)
    head0_k = lane_k < DK_A
    head0_v = lane_v < DV_A
    lg = [[_log_sigmoid(gam_ref[d, hh]) for hh in range(2)] for d in range(2)]
    lgf_k = jnp.where(head0_k, lg[0][0], lg[0][1])
    lgb_k = jnp.where(head0_k, lg[1][0], lg[1][1])
    diff = row - (lane_k % DK_A).astype(F32)
    dcomb = (jnp.where(diff >= 0, jnp.exp(lgf_k * jnp.maximum(diff, 0.0)), 0.0)
             + jnp.where(diff <= 0, jnp.exp(lgb_k * jnp.maximum(-diff, 0.0)), 0.0))
    wend_f = jnp.exp(lgf_k * (CHUNK - 1.0 - row))
    wstart_f = jnp.exp(lgf_k * (row + 1.0))
    wend_b = jnp.exp(lgb_k * row)
    wstart_b = jnp.exp(lgb_k * (CHUNK - row))

    def state_layout(h0, h1):
        zero = jnp.zeros_like(h0)
        return cat([cat([h0, zero], axis=1), cat([zero, h1], axis=1)], axis=0)

    on_diag = state_layout(jnp.ones((DK_A, DV_A), F32), jnp.ones((DK_A, DV_A), F32)) > 0.0
    gch_f = jnp.exp(state_layout(lg[0][0], lg[0][1]) * float(CHUNK))
    gch_b = jnp.exp(state_layout(lg[1][0], lg[1][1]) * float(CHUNK))
    sf_ref[...] = state_layout(s0_ref[0, 0, 0], s0_ref[0, 0, 1])
    sb_ref[...] = state_layout(s0_ref[0, 1, 0], s0_ref[0, 1, 1])
    o_ref[...] = jnp.zeros_like(o_ref)

    def step(i, carry):
        rows = pl.ds(pl.multiple_of(i * CHUNK, CHUNK), CHUNK)
        qn = qs_ref[rows, :]
        kn = ks_ref[rows, :]
        vn = v_ref[rows, :]
        k_bd = cat([jnp.where(head0_k, kn, 0.0), jnp.where(head0_k, 0.0, kn)], axis=0)
        v_bd = cat([jnp.where(head0_v, vn, 0.0), jnp.where(head0_v, 0.0, vn)], axis=0)
        a = _bdot_nt(qn, k_bd) * dcomb
        s_f = sf_ref[...]
        o_ref[rows, :] += _bdot(cat([a, qn * wstart_f], axis=1), cat([v_bd, s_f], axis=0))
        sf_ref[...] = s_f * gch_f + jnp.where(on_diag, _bdot_tn(kn * wend_f, vn), 0.0)
        rows = pl.ds(pl.multiple_of((nc - 1 - i) * CHUNK, CHUNK), CHUNK)
        qn = qs_ref[rows, :]
        kn = ks_ref[rows, :]
        vn = v_ref[rows, :]
        s_b = sb_ref[...]
        o_ref[rows, :] += _bdot(qn * wstart_b, s_b)
        sb_ref[...] = s_b * gch_b + jnp.where(on_diag, _bdot_tn(kn * wend_b, vn), 0.0)
        return carry

    lax.fori_loop(0, nc, step, 0)
    for d, s_ref in enumerate((sf_ref, sb_ref)):
        sout_ref[0, d, 0] = s_ref[:DK_A, :DV_A]
        sout_ref[0, d, 1] = s_ref[DK_A:, DV_A:]

    for hh in range(2):
        vsl = slice(hh * DV_A, (hh + 1) * DV_A)
        ya_ref[:, vsl] = _silu(g_ref[:, vsl]) * _rms(o_ref[:, vsl], ng_ref[hh])


def _retention(q, k, v, g, s0, gam, ng, cos_t, sin_t, nseq, seq_len, row_off, latent):
    w2 = 2 * DK_A
    v2 = 2 * DV_A
    return pl.pallas_call(
        functools.partial(_ret_kernel, seq_len=seq_len, latent=latent),
        grid=(nseq, H_A // 2),
        in_specs=[pl.BlockSpec((seq_len, w2), lambda b, p: (b + row_off, p)),
                  pl.BlockSpec((seq_len, w2), lambda b, p: (b + row_off, p)),
                  pl.BlockSpec((seq_len, v2), lambda b, p: (b + row_off, p)),
                  pl.BlockSpec((seq_len, v2), lambda b, p: (b + row_off, p)),
                  pl.BlockSpec((1, 2, 2, DK_A, DV_A), lambda b, p: (b, 0, p, 0, 0)),
                  pl.BlockSpec((2, 2, CHUNK, DV_A), lambda b, p: (0, p, 0, 0)),
                  pl.BlockSpec((2, 1, DV_A), lambda b, p: (p, 0, 0)),
                  pl.BlockSpec((seq_len, w2), lambda b, p: (0, 0)),
                  pl.BlockSpec((seq_len, w2), lambda b, p: (0, 0))],
        out_specs=[pl.BlockSpec((seq_len, v2), lambda b, p: (b, p)),
                   pl.BlockSpec((1, 2, 2, DK_A, DV_A), lambda b, p: (b, 0, p, 0, 0))],
        out_shape=[jax.ShapeDtypeStruct((nseq * seq_len, H_A * DV_A), F32),
                   jax.ShapeDtypeStruct((nseq, 2, H_A, DK_A, DV_A), F32)],
        scratch_shapes=[pltpu.VMEM((seq_len, w2), F32), pltpu.VMEM((seq_len, w2), F32),
                        pltpu.VMEM((seq_len, v2), F32), pltpu.VMEM((w2, v2), F32), pltpu.VMEM((w2, v2), F32)],
        compiler_params=_cparams(("arbitrary", "arbitrary")),
        name="retention",
    )(q, k, v, g, s0, gam, ng, cos_t, sin_t)


def _fill_padded(pad_ref, cur, prev, nxt, has_prev, has_next, halo):
    tile = cur.shape[0]
    pad_ref[0:halo, :] = jnp.where(has_prev, prev, 0.0)
    pad_ref[halo:halo + tile, :] = cur
    pad_ref[halo + tile:halo + tile + halo, :] = jnp.where(has_next, nxt, 0.0)


def _conv_taps(pad_ref, w_ref, ntaps, halo, tile):
    base = halo - ntaps // 2
    acc = w_ref[0:1, :] * pad_ref[base:base + tile, :]
    for kk in range(1, ntaps):
        acc = acc + w_ref[kk:kk + 1, :] * pad_ref[base + kk:base + kk + tile, :]
    return acc


def _halo_specs(width, col_map, halo):
    per = CONV_TILE // halo

    def cur(i, j):
        return (i, col_map(j))

    def prev(i, j):
        return (jnp.maximum(i * per - 1, 0), col_map(j))

    def make_next(nblk):
        def nxt(i, j):
            return (jnp.minimum((i + 1) * per, nblk - 1), col_map(j))
        return nxt

    return cur, prev, make_next


def _sconv_kernel(bg_ref, cg_ref, cgp_ref, cgn_ref, hb_ref, hbp_ref, hbn_ref, w_ref, o_ref, pad_ref, *, st):
    has_prev, has_next = st.halo_flags(pl.program_id(0), CONV_TILE)
    _fill_padded(pad_ref, cg_ref[...] * hb_ref[...], cgp_ref[...] * hbp_ref[...],
                 cgn_ref[...] * hbn_ref[...], has_prev, has_next, SUBLANES)
    o_ref[...] = bg_ref[...] * _conv_taps(pad_ref, w_ref, SC_K, SUBLANES, CONV_TILE)


CONV_COLS = 512


def _short_gated_conv(st, bg, cg, hb, w):
    t, c = bg.shape
    halo = SUBLANES
    cur, prev, make_next = _halo_specs(c, lambda j: j, halo)
    nxt = make_next(t // halo)
    tile_spec = pl.BlockSpec((CONV_TILE, CONV_COLS), cur)
    prev_spec = pl.BlockSpec((halo, CONV_COLS), prev)
    next_spec = pl.BlockSpec((halo, CONV_COLS), nxt)
    return pl.pallas_call(
        functools.partial(_sconv_kernel, st=st),
        grid=(t // CONV_TILE, c // CONV_COLS),
        in_specs=[tile_spec, tile_spec, prev_spec, next_spec, tile_spec, prev_spec, next_spec,
                  pl.BlockSpec((SUBLANES, CONV_COLS), lambda i, j: (0, j))],
        out_specs=tile_spec,
        out_shape=jax.ShapeDtypeStruct((t, c), F32),
        scratch_shapes=[pltpu.VMEM((CONV_TILE + 2 * halo, CONV_COLS), F32)],
        compiler_params=_cparams(("arbitrary", "arbitrary")),
        name="short_gated_conv",
    )(bg, cg, cg, cg, hb, hb, hb, w)


def _qkv_conv_kernel(x_ref, xp_ref, xn_ref, w_ref, o_ref, pad_ref, *, st):
    has_prev, has_next = st.halo_flags(pl.program_id(0), CONV_TILE)
    _fill_padded(pad_ref, x_ref[...], xp_ref[...], xn_ref[...], has_prev, has_next, SUBLANES)
    s = _silu(_conv_taps(pad_ref, w_ref, QKV_K, SUBLANES, CONV_TILE))
    for hh in range(3 * H_C):
        hsl = slice(hh * DK_C, (hh + 1) * DK_C)
        sh = s[:, hsl]
        if hh < 2 * H_C:
            sh = sh * lax.rsqrt(jnp.sum(sh * sh, axis=-1, keepdims=True) + EPS)
        o_ref[:, hsl] = sh * (DK_C ** -0.5) if hh < H_C else sh


def _qkv_conv(st, qkv, w):
    t, c = qkv.shape
    assert c == 3 * H_C * DK_C and DK_C == DV_C
    halo = SUBLANES
    cur, prev, make_next = _halo_specs(c, lambda j: j, halo)
    nxt = make_next(t // halo)
    tile_spec = pl.BlockSpec((CONV_TILE, c), cur)
    return pl.pallas_call(
        functools.partial(_qkv_conv_kernel, st=st),
        grid=(t // CONV_TILE, 1),
        in_specs=[tile_spec, pl.BlockSpec((halo, c), prev), pl.BlockSpec((halo, c), nxt),
                  pl.BlockSpec((SUBLANES, c), lambda i, j: (0, j))],
        out_specs=tile_spec,
        out_shape=jax.ShapeDtypeStruct((t, c), F32),
        scratch_shapes=[pltpu.VMEM((CONV_TILE + 2 * halo, c), F32)],
        compiler_params=_cparams(("arbitrary", "arbitrary")),
        name="qkv_conv",
    )(qkv, qkv, qkv, w)


CF_HALO = 16


def _conformer_kernel(ca_ref, cap_ref, can_ref, cg_ref, cgp_ref, cgn_ref, w_ref, b_ref, lg_ref, lb_ref,
                      o_ref, pad_ref, *, st):
    has_prev, has_next = st.halo_flags(pl.program_id(0), CONV_TILE)
    _fill_padded(pad_ref, ca_ref[...] * _sigmoid(cg_ref[...]), cap_ref[...] * _sigmoid(cgp_ref[...]),
                 can_ref[...] * _sigmoid(cgn_ref[...]), has_prev, has_next, CF_HALO)
    hc = _conv_taps(pad_ref, w_ref, CF_K, CF_HALO, CONV_TILE) + b_ref[...]
    mu = jnp.mean(hc, axis=-1, keepdims=True)
    xc = hc - mu
    y = xc * lax.rsqrt(jnp.mean(xc * xc, axis=-1, keepdims=True) + EPS) * lg_ref[...] + lb_ref[...]
    o_ref[...] = _silu(y)


def _conformer(st, glu, w, b, ln_g, ln_b):
    t, c2 = glu.shape
    c = c2 // 2
    nblk = t // CF_HALO
    per = CONV_TILE // CF_HALO
    vec = pl.BlockSpec((1, c), lambda i: (0, 0))
    return pl.pallas_call(
        functools.partial(_conformer_kernel, st=st),
        grid=(t // CONV_TILE,),
        in_specs=[pl.BlockSpec((CONV_TILE, c), lambda i: (i, 0)),
                  pl.BlockSpec((CF_HALO, c), lambda i: (jnp.maximum(i * per - 1, 0), 0)),
                  pl.BlockSpec((CF_HALO, c), lambda i: (jnp.minimum((i + 1) * per, nblk - 1), 0)),
                  pl.BlockSpec((CONV_TILE, c), lambda i: (i, 1)),
                  pl.BlockSpec((CF_HALO, c), lambda i: (jnp.maximum(i * per - 1, 0), 1)),
                  pl.BlockSpec((CF_HALO, c), lambda i: (jnp.minimum((i + 1) * per, nblk - 1), 1)),
                  pl.BlockSpec((2 * CF_HALO, c), lambda i: (0, 0)), vec, vec, vec],
        out_specs=pl.BlockSpec((CONV_TILE, c), lambda i: (i, 0)),
        out_shape=jax.ShapeDtypeStruct((t, c), F32),
        scratch_shapes=[pltpu.VMEM((CONV_TILE + 2 * CF_HALO, c), F32)],
        compiler_params=_cparams(("arbitrary",)),
        name="conformer_conv",
    )(glu, glu, glu, glu, glu, glu, w, b, ln_g, ln_b)


GDN_PACK = 4
GDN_GROUPS = 4


def _split_bf16(x):
    hi = x.astype(BF16)
    return hi, (x - hi.astype(F32)).astype(BF16)


def _packed_unit_inverses(groups):
    dot = functools.partial(jnp.dot, preferred_element_type=F32)
    cat = jnp.concatenate
    n = len(groups[0])
    c = CHUNK
    w = n * c
    eye = (lax.broadcasted_iota(jnp.int32, (c, c), 0) == lax.broadcasted_iota(jnp.int32, (c, c), 1)).astype(F32)
    lane_block = lax.broadcasted_iota(jnp.int32, (c, 2 * w), 1) % w // c

    def rhs_of(p):
        hi, lo = _split_bf16(p)
        both = cat([hi, lo], axis=1)
        return hi, lo, cat([jnp.where(lane_block == i, both, jnp.zeros_like(both)) for i in range(n)], axis=0)

    def product(m, r):
        return (r[:m, :w] + r[m:, :w]) + (r[:m, w:] + r[m:, w:])

    ps = [cat(a_list, axis=1) for a_list in groups]
    ts = [cat([eye - a for a in a_list], axis=1) for a_list in groups]
    for g, p in enumerate(ps):
        hi, lo, rhs = rhs_of(p)
        ps[g] = product(c, dot(cat([hi, lo], axis=0), rhs))
    steps = int(math.log2(c)) - 1
    for step in range(steps):
        for g in range(len(groups)):
            p_hi, p_lo, rhs = rhs_of(ps[g])
            t_hi, t_lo = _split_bf16(ts[g])
            if step < steps - 1:
                both = product(2 * c, dot(cat([t_hi, p_hi, t_lo, p_lo], axis=0), rhs))
                ts[g] = ts[g] + both[:c]
                ps[g] = both[c:]
            else:
                ts[g] = ts[g] + product(c, dot(cat([t_hi, t_lo], axis=0), rhs))
    return [[t[:, i * c:(i + 1) * c] for i in range(n)] for t in ts]


def _chunk_cumsum(x, reverse):
    row = lax.broadcasted_iota(jnp.int32, x.shape, 0)
    s = 1
    while s < CHUNK:
        if reverse:
            x = x + jnp.where(row < CHUNK - s, pltpu.roll(x, CHUNK - s, 0), 0.0)
        else:
            x = x + jnp.where(row >= s, pltpu.roll(x, s, 0), 0.0)
        s *= 2
    return x


GDN_HEADS = 2


def _gdn_kernel(q_ref, k_ref, v_ref, z_ref, ab_ref, alog_ref, dtb_ref, s0_ref, ng_ref,
                y_ref, sout_ref, g_ref, beta_ref, u_ref, wq_ref, ak_ref, egl_ref, o_ref, *, seq_len):
    nc = seq_len // CHUNK
    head0 = pl.program_id(1) * GDN_HEADS
    row = lax.broadcasted_iota(jnp.int32, (CHUNK, CHUNK), 0)
    col = lax.broadcasted_iota(jnp.int32, (CHUNK, CHUNK), 1)
    lane = lax.broadcasted_iota(jnp.int32, (CHUNK, LANES), 1)
    alog = alog_ref[...]
    dtb = dtb_ref[...]
    head_cols = [slice(hh * DK_C, (hh + 1) * DK_C) for hh in range(GDN_HEADS)]

    def gates(n, carry):
        rows = pl.ds(pl.multiple_of(n * CHUNK, CHUNK), CHUNK)
        ab = ab_ref[rows, :]
        g_all = -jnp.exp(alog) * _softplus(ab + dtb)
        b_all = _sigmoid(ab)
        for hh in range(GDN_HEADS):
            for d in range(2):
                gcol = d * H_C + head0 + hh
                gsel = jnp.sum(jnp.where(lane == gcol, g_all, 0.0), axis=-1, keepdims=True)
                bsel = jnp.sum(jnp.where(lane == 2 * H_C + gcol, b_all, 0.0), axis=-1, keepdims=True)
                g_ref[hh, d, rows, :] = jnp.broadcast_to(gsel, (CHUNK, LANES))
                beta_ref[hh, d, rows, :] = jnp.broadcast_to(bsel, (CHUNK, LANES))
        return carry

    lax.fori_loop(0, nc, gates, 0)

    pairs_per_iter = GDN_GROUPS // GDN_HEADS
    chunks_per_group = GDN_PACK // 2

    def precompute(it, carry):
        groups = [prepare_group(hh, it * pairs_per_iter + pr)
                  for hh in range(GDN_HEADS) for pr in range(pairs_per_iter)]
        inverses = _packed_unit_inverses([[sysm[0] for sysm in systems] for systems in groups])
        for t_invs, systems in zip(inverses, groups):
            for t_inv, (_, rhs, hh, d, n) in zip(t_invs, systems):
                uw = _bdot(t_inv, rhs)
                u_ref[hh, d, pl.ds(pl.multiple_of(n * CHUNK, CHUNK), CHUNK), :] = uw[:, :DV_C]
                wq_ref[hh, d, pl.ds(pl.multiple_of(n * 2 * CHUNK, CHUNK), CHUNK), :] = uw[:, DV_C:].astype(BF16)
        return carry

    def prepare_group(hh, pair):
        systems = []
        for jc in range(chunks_per_group):
            n = pair * chunks_per_group + jc
            rows = pl.ds(pl.multiple_of(n * CHUNK, CHUNK), CHUNK)
            qc = q_ref[rows, head_cols[hh]]
            kc = k_ref[rows, head_cols[hh]]
            vc = v_ref[rows, head_cols[hh]]
            gbs = [_chunk_cumsum(g_ref[hh, d, rows, :], d == 1) for d in range(2)]
            bbs = [beta_ref[hh, d, rows, :] for d in range(2)]
            kbs = [kc * bb for bb in bbs]
            prod = _bdot_nt(jnp.concatenate(kbs + [qc], axis=0), kc)
            for d in range(2):
                incl = (row >= col) if d == 0 else (row <= col)
                strict = (row > col) if d == 0 else (row < col)
                last = CHUNK - 1 if d == 0 else 0
                gb = gbs[d]
                gr = gb[:, :CHUNK].T
                gl = gb[last:last + 1, :]
                decay = jnp.where(incl, jnp.exp(jnp.where(incl, gb[:, :CHUNK] - gr, 0.0)), 0.0)
                eg = jnp.exp(gb)
                a_low = jnp.where(strict, prod[d * CHUNK:(d + 1) * CHUNK] * decay, 0.0)
                attn = jnp.where(incl, prod[2 * CHUNK:] * decay, 0.0)
                kg_t = (kc * jnp.exp(gl - gb)).T
                ak_ref[hh, d, pl.ds(pl.multiple_of(n * 3 * CHUNK, CHUNK), 3 * CHUNK), :] = jnp.concatenate(
                    [attn, kg_t], axis=0).astype(BF16)
                wq_ref[hh, d, pl.ds(pl.multiple_of(n * 2 * CHUNK + CHUNK, CHUNK), CHUNK), :] = (
                    qc * eg).astype(BF16)
                egl_ref[hh, d, pl.ds(n, 1), :] = jnp.exp(gl)
                systems.append((a_low, jnp.concatenate([vc * bbs[d], kbs[d] * eg], axis=1), hh, d, n))
        return systems

    lax.fori_loop(0, nc // (pairs_per_iter * chunks_per_group), precompute, 0)

    o_ref[...] = jnp.zeros_like(o_ref)

    def scan(i, carry):
        out = []
        for idx, s in enumerate(carry):
            hh, d = divmod(idx, 2)
            n = i if d == 0 else nc - 1 - i
            rows = pl.ds(pl.multiple_of(n * CHUNK, CHUNK), CHUNK)
            on_s = _bdot(wq_ref[hh, d, pl.ds(pl.multiple_of(n * 2 * CHUNK, CHUNK), 2 * CHUNK), :], s)
            v_new = u_ref[hh, d, rows, :] - on_s[:CHUNK]
            on_v = _bdot(ak_ref[hh, d, pl.ds(pl.multiple_of(n * 3 * CHUNK, CHUNK), 3 * CHUNK), :], v_new)
            o_ref[rows, head_cols[hh]] += on_s[CHUNK:] + on_v[:CHUNK]
            out.append(s * egl_ref[hh, d, pl.ds(n, 1), :] + on_v[CHUNK:])
        return tuple(out)

    init = tuple(s0_ref[0, d, hh] for hh in range(GDN_HEADS) for d in range(2))
    final = lax.fori_loop(0, nc, scan, init)
    for idx, s in enumerate(final):
        hh, d = divmod(idx, 2)
        sout_ref[0, d, hh] = s
    for hh in range(GDN_HEADS):
        y_ref[:, head_cols[hh]] = _rms(o_ref[:, head_cols[hh]], ng_ref[...]) * _silu(z_ref[:, head_cols[hh]])


def _gdn(qkv_n, z, ab, alog, dtb, s0, ng, nseq, seq_len, row_off):
    assert GDN_GROUPS % GDN_HEADS == 0 and seq_len % (CHUNK * (GDN_GROUPS // GDN_HEADS) * (GDN_PACK // 2)) == 0
    wide = GDN_HEADS * DK_C
    steps = H_C // GDN_HEADS
    blk = lambda off: pl.BlockSpec((seq_len, wide), lambda b, h: (b + row_off, h + off))
    vec = pl.BlockSpec((1, LANES), lambda b, h: (0, 0))
    st_spec = pl.BlockSpec((1, 2, GDN_HEADS, DK_C, DV_C), lambda b, h: (b, 0, h, 0, 0))
    nc = seq_len // CHUNK
    per = lambda rows, width, dtype: pltpu.VMEM((GDN_HEADS, 2, rows, width), dtype)
    return pl.pallas_call(
        functools.partial(_gdn_kernel, seq_len=seq_len),
        grid=(nseq, steps),
        in_specs=[blk(0), blk(steps), blk(2 * steps), blk(0),
                  pl.BlockSpec((seq_len, LANES), lambda b, h: (b + row_off, 0)),
                  vec, vec, st_spec, vec],
        out_specs=[pl.BlockSpec((seq_len, wide), lambda b, h: (b, h)), st_spec],
        out_shape=[jax.ShapeDtypeStruct((nseq * seq_len, H_C * DV_C), F32),
                   jax.ShapeDtypeStruct((nseq, 2, H_C, DK_C, DV_C), F32)],
        scratch_shapes=[per(seq_len, LANES, F32), per(seq_len, LANES, F32), per(seq_len, DV_C, F32),
                        per(2 * seq_len, DK_C, BF16), per(3 * seq_len, CHUNK, BF16),
                        per(max(nc, SUBLANES), LANES, F32), pltpu.VMEM((seq_len, wide), F32)],
        compiler_params=_cparams(("arbitrary", "arbitrary")),
        name="gated_deltanet",
    )(qkv_n, qkv_n, qkv_n, z, ab, alog, dtb, s0, ng)


def _out_proj_kernel(ya_ref, yb_ref, x_ref, mod_ref, g_ref, wo_ref, wq_ref, x1_ref, h2_ref, qp_ref):
    half = ya_ref.shape[1]
    m = mod_ref[0]
    y = (jnp.dot(ya_ref[...].astype(BF16), wo_ref[0:half, :], preferred_element_type=F32)
         + jnp.dot(yb_ref[...].astype(BF16), wo_ref[half:, :], preferred_element_type=F32))
    x1 = x_ref[...] + m[2:3] * y
    x1_ref[...] = x1
    h2 = (_rms(x1, g_ref[...]) * (1.0 + m[4:5]) + m[3:4]).astype(BF16)
    h2_ref[...] = h2
    qp_ref[...] = jnp.dot(h2, wq_ref[...], preferred_element_type=F32)


def _out_proj(st, ya, yb, x, mod, g, wo, wq, tm):
    t, d = x.shape
    half = ya.shape[1]
    nq = wq.shape[1]
    return pl.pallas_call(
        _out_proj_kernel,
        grid=(t // tm,),
        in_specs=[pl.BlockSpec((tm, half), lambda i: (i, 0)),
                  pl.BlockSpec((tm, half), lambda i: (i, 0)),
                  pl.BlockSpec((tm, d), lambda i: (i, 0)),
                  pl.BlockSpec((1, 6, d), lambda i: (st.mod_row(i, tm), 0, 0)),
                  pl.BlockSpec((1, d), lambda i: (0, 0)),
                  pl.BlockSpec((2 * half, d), lambda i: (0, 0)),
                  pl.BlockSpec((d, nq), lambda i: (0, 0))],
        out_specs=[pl.BlockSpec((tm, d), lambda i: (i, 0)),
                   pl.BlockSpec((tm, d), lambda i: (i, 0)),
                   pl.BlockSpec((tm, nq), lambda i: (i, 0))],
        out_shape=[jax.ShapeDtypeStruct((t, d), F32), jax.ShapeDtypeStruct((t, d), BF16),
                   jax.ShapeDtypeStruct((t, nq), F32)],
        compiler_params=_cparams(("arbitrary",)),
        name="out_proj",
    )(ya, yb, x, mod, g, wo, wq)


def _candidate_rows():
    groups = ([(0, k2) for k2 in range(16)], [(1, k2) for k2 in range(8)], [(2, k2) for k2 in range(8)],
              [(3, k2) for k2 in range(8)], [(k1, 0) for k1 in range(16)], [(k1, 1) for k1 in range(8)],
              [(k1, 2) for k1 in range(8)])
    rows, seen = [], set()
    for grp in groups:
        for k1, k2 in grp:
            ok = (k1 + 1) * (k2 + 1) <= TOPK and (k1, k2) not in seen
            if ok:
                seen.add((k1, k2))
            rows.append(float(k1 * TOPK + k2) if ok else CAND_INVALID)
    return rows


CAND_INVALID = float(TOPK * TOPK)
RANK_MARK = 2.0 ** 100
RANK_STEP = 2.0 ** 95
SELECT_STRIPS = 4


def _pack_candidates(a1, a2, op):
    return jnp.concatenate([op(a1[0:1], a2), op(a1[1:2], a2[0:8]), op(a1[2:3], a2[0:8]),
                            op(a1[3:4], a2[0:8]), op(a1, a2[0:1]), op(a1[0:8], a2[1:2]),
                            op(a1[0:8], a2[2:3])], axis=0)


def _top16(s, vals_ref, exact):
    n = s.shape[0]
    if not exact:
        for kk in range(TOPK):
            m = jnp.max(s, axis=0, keepdims=True)
            vals_ref[kk:kk + 1, :] = m
            s = jnp.where(s == m, -(RANK_MARK + kk * RANK_STEP), s)
        return jnp.where(s < -0.5 * RANK_MARK, (-s - RANK_MARK) * (1.0 / RANK_STEP), float(TOPK))
    iota = lax.broadcasted_iota(jnp.int32, s.shape, 0).astype(F32)
    rank = jnp.full(s.shape, float(TOPK), F32)
    for kk in range(TOPK):
        m = jnp.max(s, axis=0, keepdims=True)
        idx = jnp.min(jnp.where(s == m, iota, float(n)), axis=0, keepdims=True)
        hit = iota == idx
        rank = jnp.where(hit, float(kk), rank)
        vals_ref[kk:kk + 1, :] = m
        s = jnp.where(hit, NEG_INF, s)
    return rank


def _peer_select_kernel(q_ref, keys_ref, cflat_ref, e1_ref, cnt_ref, e2_ref, r2_ref, v1_ref, v2_ref, *, tt):
    nk = N_KEYS
    cflat = cflat_ref[...]
    valid = cflat < CAND_INVALID
    row16 = lax.broadcasted_iota(jnp.int32, (TOPK, LANES), 0)

    def compute(strip, exact):
        s1, s2, tsl, va_ref, vb_ref = strip
        rank1 = _top16(s1, va_ref, exact)
        rank2 = _top16(s2, vb_ref, exact)
        v1 = va_ref[...]
        v2 = vb_ref[...]
        cand = jnp.where(valid, _pack_candidates(v1, v2, jnp.add), NEG_INF)
        prod = _pack_candidates(jnp.exp(v1 - v1[0:1]), jnp.exp(v2 - v2[0:1]), jnp.multiply)
        sel = jnp.zeros(cand.shape, F32)
        for _ in range(TOPK):
            m = jnp.max(cand, axis=0, keepdims=True)
            if exact:
                idx = jnp.min(jnp.where(cand == m, cflat, CAND_INVALID), axis=0, keepdims=True)
                hit = cflat == idx
            else:
                hit = cand == m
            sel = jnp.where(hit, 1.0, sel)
            cand = jnp.where(hit, NEG_INF, cand)
        zsum = jnp.sum(sel * prod, axis=0, keepdims=True)
        c_all = sel[40:56] + jnp.concatenate([sel[56:64] + sel[64:72], jnp.zeros((8, LANES), F32)], axis=0)
        for k1, (lo, hi) in enumerate(((0, 16), (16, 24), (24, 32), (32, 40))):
            c_all = jnp.where(row16 == k1, jnp.sum(sel[lo:hi], axis=0, keepdims=True), c_all)
        rank1_b = rank1.astype(BF16)
        c_all_b = c_all.astype(BF16)
        cnt = jnp.zeros(rank1.shape, BF16)
        for k1 in range(TOPK):
            cnt = jnp.where(rank1_b == float(k1), c_all_b[k1:k1 + 1], cnt)
        cnt = cnt.astype(F32)
        in1 = rank1 < float(TOPK)
        in2 = rank2 < float(TOPK)
        e1_ref[0, :, tsl] = jnp.where(in1, jnp.exp(s1 - v1[0:1]), 0.0) / zsum
        cnt_ref[0, :, tsl] = cnt
        e2_ref[:, tsl] = jnp.where(in2, jnp.exp(s2 - v2[0:1]), 0.0).astype(e2_ref.dtype)
        r2_ref[:, tsl] = rank2.astype(r2_ref.dtype)
        n1 = jnp.sum(jnp.where(in1, 1.0, 0.0), axis=0, keepdims=True)
        n2 = jnp.sum(jnp.where(in2, 1.0, 0.0), axis=0, keepdims=True)
        nc = jnp.sum(sel, axis=0, keepdims=True)
        want = float(TOPK)
        return jnp.abs(n1 - want) + jnp.abs(n2 - want) + jnp.abs(nc - want)

    def strip_group(si, carry):
        strips = []
        for sub in range(SELECT_STRIPS):
            t0 = pl.multiple_of((si * SELECT_STRIPS + sub) * LANES, LANES)
            tsl = pl.ds(t0, LANES)
            s1 = _bdot_nt(keys_ref[0, 0], q_ref[tsl, 0:nk])
            s2 = _bdot_nt(keys_ref[0, 1], q_ref[tsl, nk:2 * nk])
            strips.append((s1, s2, tsl, v1_ref.at[sub], v2_ref.at[sub]))
        tied = [compute(strip, False) for strip in strips]
        for strip, bad in zip(strips, tied):
            @pl.when(jnp.max(bad) > 0.0)
            def _(strip=strip):
                compute(strip, True)
        return carry

    lax.fori_loop(0, tt // (SELECT_STRIPS * LANES), strip_group, 0)


def _peer_select(qp, keys, tt):
    t = qp.shape[0]
    rows = _candidate_rows()
    cflat = jnp.broadcast_to(jnp.asarray(rows, F32)[:, None], (len(rows), LANES))
    out_spec = pl.BlockSpec((1, N_KEYS, tt), lambda i, h: (h, 0, i))
    f32_sds = jax.ShapeDtypeStruct((H_P, N_KEYS, t), F32)
    flat_spec = pl.BlockSpec((N_KEYS, tt), lambda i, h: (h, i))
    flat_sds = jax.ShapeDtypeStruct((H_P * N_KEYS, t), BF16)
    return pl.pallas_call(
        functools.partial(_peer_select_kernel, tt=tt),
        grid=(t // tt, H_P),
        in_specs=[pl.BlockSpec((tt, 2 * N_KEYS), lambda i, h: (i, h)),
                  pl.BlockSpec((1, 2, N_KEYS, N_KEYS), lambda i, h: (h, 0, 0, 0)),
                  pl.BlockSpec(cflat.shape, lambda i, h: (0, 0))],
        out_specs=[out_spec, out_spec, flat_spec, flat_spec],
        out_shape=[f32_sds, f32_sds, flat_sds, flat_sds],
        scratch_shapes=[pltpu.VMEM((SELECT_STRIPS, TOPK, LANES), F32),
                        pltpu.VMEM((SELECT_STRIPS, TOPK, LANES), F32)],
        compiler_params=_cparams(("arbitrary", "arbitrary")),
        name="peer_select",
    )(qp, keys, cflat)


def _peer_dense_kernel(h_ref, u_ref, vt_ref, e1_ref, cnt_ref, e2_ref, r2_ref, x_ref, mod_ref, o_ref,
                       acc_ref, s_ref, p_ref, e2b_ref, r2b_ref, ht_ref, *, na):
    j = pl.program_id(1)
    tt = s_ref.shape[1]

    @pl.when(j == 0)
    def _():
        acc_ref[...] = jnp.zeros_like(acc_ref)
        e2b_ref[...] = e2_ref[...].astype(BF16)
        r2b_ref[...] = r2_ref[...].astype(BF16)
        ht_ref[...] = h_ref[...].astype(F32).T.astype(BF16)

    s_ref[...] = jnp.dot(u_ref[...], ht_ref[...], preferred_element_type=F32)

    for aa in range(na):
        rows = slice(aa * N_KEYS, (aa + 1) * N_KEYS)
        for tg in range(tt // LANES):
            tsl = slice(tg * LANES, (tg + 1) * LANES)
            gate = None
            for hh in range(H_P):
                cnt_row = cnt_ref[hh, aa:aa + 1, tsl].astype(BF16)
                e1_row = e1_ref[hh, aa:aa + 1, tsl].astype(BF16)
                hsl = slice(hh * N_KEYS, (hh + 1) * N_KEYS)
                term = jnp.where(r2b_ref[hsl, tsl] < cnt_row, e2b_ref[hsl, tsl], 0.0) * e1_row
                gate = term if gate is None else gate + term
            p_ref[rows, tsl] = gate * _gelu_tanh(s_ref[rows, tsl].astype(BF16))

    acc_ref[...] += jnp.dot(vt_ref[...], p_ref[...], preferred_element_type=F32)

    @pl.when(j == pl.num_programs(1) - 1)
    def _():
        o_ref[...] = x_ref[...] + mod_ref[0][5:6] * acc_ref[...].T


def _peer_dense(st, h2, u, vt, e1, cnt, e2, r2, x, mod, tt, et):
    t, d = x.shape
    n_exp = u.shape[0]
    na = et // N_KEYS
    row_spec = pl.BlockSpec((H_P, na, tt), lambda i, j: (0, j, i))
    full_spec = pl.BlockSpec((H_P * N_KEYS, tt), lambda i, j: (0, i))
    return pl.pallas_call(
        functools.partial(_peer_dense_kernel, na=na),
        grid=(t // tt, n_exp // et),
        in_specs=[pl.BlockSpec((tt, d), lambda i, j: (i, 0)),
                  pl.BlockSpec((et, d), lambda i, j: (j, 0)),
                  pl.BlockSpec((d, et), lambda i, j: (0, j)),
                  row_spec, row_spec, full_spec, full_spec,
                  pl.BlockSpec((tt, d), lambda i, j: (i, 0)),
                  pl.BlockSpec((1, 6, d), lambda i, j: (st.mod_row(i, tt), 0, 0))],
        out_specs=pl.BlockSpec((tt, d), lambda i, j: (i, 0)),
        out_shape=jax.ShapeDtypeStruct((t, d), F32),
        scratch_shapes=[pltpu.VMEM((d, tt), F32), pltpu.VMEM((et, tt), F32), pltpu.VMEM((et, tt), BF16),
                        pltpu.VMEM((H_P * N_KEYS, tt), BF16), pltpu.VMEM((H_P * N_KEYS, tt), BF16),
                        pltpu.VMEM((d, tt), BF16)],
        compiler_params=_cparams(("arbitrary", "arbitrary")),
        name="peer_dense",
    )(h2, u, vt, e1, cnt, e2, r2, x, mod)


def _final_norm_kernel(x_ref, g_ref, o_ref):
    o_ref[...] = _rms(x_ref[...], g_ref[...])


def _final_norm(x, g, tm):
    t, d = x.shape
    return pl.pallas_call(
        _final_norm_kernel,
        grid=(t // tm,),
        in_specs=[pl.BlockSpec((tm, d), lambda i: (i, 0)), pl.BlockSpec((1, d), lambda i: (0, 0))],
        out_specs=pl.BlockSpec((tm, d), lambda i: (i, 0)),
        out_shape=jax.ShapeDtypeStruct((t, d), F32),
        compiler_params=_cparams(("arbitrary",)),
        name="final_norm",
    )(x, g)


def _rope_tables(seq_len):
    pos = jnp.arange(seq_len)
    rowp = (pos // GRID_W).astype(F32)
    colp = (pos % GRID_W).astype(F32)
    nf = DK_A // 4
    freqs = ROPE_BASE ** (-jnp.arange(nf, dtype=F32) / nf)
    ang = jnp.concatenate([rowp[:, None] * freqs, colp[:, None] * freqs], axis=-1)
    cos = jnp.cos(ang)
    sin = jnp.sin(ang)
    cos_t = jnp.tile(cos, (1, 4))
    sin_t = jnp.tile(jnp.concatenate([-sin, sin], axis=-1), (1, 2))
    return cos_t, sin_t


def _pad_rows(w, rows):
    return jnp.concatenate([w, jnp.zeros((rows - w.shape[0], w.shape[1]), w.dtype)], axis=0)


def _lane_row(vals):
    flat = vals.reshape(-1).astype(F32)
    return jnp.concatenate([flat, jnp.zeros((LANES - flat.shape[0],), F32)])[None, :]


def kernel(x_prompt, x_sample, state_ret, state_gdn, c, c_ctx, ada_w, ada_b, norm_mix_g, norm_ffn_g,
           final_norm_g, ev_w_in, ev_w_out, ret_gamma_logit, ret_norm_g, sc_conv_w, od_w_in, od_w_out,
           gdn_conv_w, gdn_a_log, gdn_dt_bias, gdn_norm_g, cf_dw_w, cf_dw_b, cf_ln_g, cf_ln_b,
           peer_wq, peer_keys, peer_u, peer_v):
    bp, lp, d = x_prompt.shape
    bs, ls, _ = x_sample.shape
    depth = ada_w.shape[0]
    st = _Streams(bp, lp, bs, ls)
    w_a = H_A * DV_A
    w_b = d - w_a
    w_c = H_C * DV_C
    w_d = d - w_c
    tm = TOKEN_TILE
    select_tt = 2 * TOKEN_TILE

    x = jnp.concatenate([x_prompt.reshape(st.tp, d), x_sample.reshape(st.ts, d)], axis=0)
    cvec = jnp.concatenate([c_ctx[None, :], c, jnp.zeros((N_MOD_ROWS - 1 - bs, d), F32)], axis=0)
    mods = _modulation(cvec, ada_w, ada_b).reshape(depth, N_MOD_ROWS, 6, d)
    cos_t, sin_t = _rope_tables(ls)
    zero_ret = jnp.zeros((bp, 2, H_A, DK_A, DV_A), F32)
    zero_gdn = jnp.zeros((bp, 2, H_C, DK_C, DV_C), F32)

    ret_new, gdn_new = [], []
    for l in range(depth):
        i = l // 2
        mod = mods[l]
        g1 = norm_mix_g[l][None, :]
        if l % 2 == 0:
            widths = (H_A * DK_A, H_A * DK_A, w_a, w_a, w_b, w_b, w_b)
            q, k, v, g, bg, cg, hb = _norm_proj(st, x, mod, g1, ev_w_in[i].astype(BF16), widths, tm)
            gam = jnp.broadcast_to(ret_gamma_logit[i][:, :, None, None], (2, H_A, CHUNK, DV_A))
            ng = ret_norm_g[i][:, None, :]
            ya_p, s_new = _retention(q, k, v, g, zero_ret, gam, ng, cos_t[:lp], sin_t[:lp],
                                     bp, lp, 0, False)
            ya_s, _ = _retention(q, k, v, g, state_ret[:, i], gam, ng, cos_t, sin_t,
                                 bs, ls, st.tp // ls, True)
            ret_new.append(s_new)
            ya = jnp.concatenate([ya_p, ya_s], axis=0)
            yb = _short_gated_conv(st, bg, cg, hb, _pad_rows(sc_conv_w[i], SUBLANES))
            w_out = ev_w_out[i]
        else:
            n_gate = 2 * 2 * H_C
            w_in = od_w_in[i]
            o_ab = 4 * w_c
            w_main = jnp.concatenate([w_in[:, :o_ab], w_in[:, o_ab + n_gate:],
                                      w_in[:, o_ab:o_ab + n_gate],
                                      jnp.zeros((d, LANES - n_gate), F32)], axis=1).astype(BF16)
            widths = (3 * w_c, w_c, 2 * w_d, LANES)
            qkv, z, glu, ab = _norm_proj(st, x, mod, g1, w_main, widths, tm)
            qkv_n = _qkv_conv(st, qkv, _pad_rows(gdn_conv_w[i], SUBLANES))
            alog = _lane_row(gdn_a_log[i])
            dtb = _lane_row(gdn_dt_bias[i])
            ng = gdn_norm_g[i][None, :]
            yc_p, s_new = _gdn(qkv_n, z, ab, alog, dtb, zero_gdn, ng, bp, lp, 0)
            yc_s, _ = _gdn(qkv_n, z, ab, alog, dtb, state_gdn[:, i], ng, bs, ls, st.tp // ls)
            gdn_new.append(s_new)
            ya = jnp.concatenate([yc_p, yc_s], axis=0)
            yb = _conformer(st, glu, _pad_rows(cf_dw_w[i], 2 * CF_HALO), cf_dw_b[i][None, :],
                            cf_ln_g[i][None, :], cf_ln_b[i][None, :])
            w_out = od_w_out[i]
        x, h2, qp = _out_proj(st, ya, yb, x, mod, norm_ffn_g[l][None, :], w_out.astype(BF16),
                              peer_wq[l].astype(BF16), tm)
        e1, cnt, e2, r2 = _peer_select(qp, peer_keys[l].astype(BF16), select_tt)
        x = _peer_dense(st, h2, peer_u[l].astype(BF16), peer_v[l].astype(BF16).T, e1, cnt, e2, r2,
                        x, mod, PEER_TOKEN_TILE, PEER_EXPERT_TILE)

    y = _final_norm(x, final_norm_g[None, :], tm)
    y_prompt = y[:st.tp].reshape(bp, lp, d)
    y_sample = y[st.tp:].reshape(bs, ls, d)
    new_state_ret = jnp.stack(ret_new, axis=1).astype(x_prompt.dtype)
    new_state_gdn = jnp.stack(gdn_new, axis=1).astype(x_prompt.dtype)
    return (y_prompt, y_sample, new_state_ret, new_state_gdn)
```
